```python
import jax, jax.numpy as jnp
from jax import lax
import numpy as np

D_MODEL = 1024
BATCH = 16
SEQ = 4096
DEPTH = 4

GRID_W = 64
NORM_EPS = 1e-6
N_DIR = 2
RW_HEADS = 8
RW_HEAD_DIM = 64
RW_WIDTH = RW_HEADS * RW_HEAD_DIM
RW_DECAY_LORA = 64
RW_ICLR_LORA = 64
RW_GATE_LORA = 128
RW_LN_EPS = 64e-5
CV_GROUPS = 8
CV_WIDTH = CV_GROUPS * 64
CV_KSIZE = 3
AT_Q_HEADS = 8
AT_KV_HEADS = 2
AT_HEAD_DIM = 64
AT_Q_BLOCK = 128
ROPE_THETA = 10000.0
N_BRANCH = 3
BRANCH_WIDTH = 512
D_FF = 4 * D_MODEL

RW_COLS = 3 * RW_WIDTH + N_DIR * RW_DECAY_LORA + N_DIR * RW_ICLR_LORA + RW_GATE_LORA
CV_COLS = 3 * CV_WIDTH
AT_COLS = (AT_Q_HEADS + 2 * AT_KV_HEADS) * AT_HEAD_DIM
GATE_COLS = N_BRANCH * D_MODEL
IN_COLS = RW_COLS + CV_COLS + AT_COLS + GATE_COLS

kernel_name = "hybrid_rwkv7_conv_gqa_gated_encoder"


def rmsnorm(x, g, eps=NORM_EPS):
    xf = x.astype(jnp.float32)
    y = xf * lax.rsqrt(jnp.mean(xf * xf, axis=-1, keepdims=True) + eps)
    return (y * g.astype(jnp.float32)).astype(x.dtype)


def split_cols(z, sizes):
    offs = np.concatenate([[0], np.cumsum(sizes)]).astype(int)
    return [z[..., int(offs[i]):int(offs[i + 1])] for i in range(len(sizes))]


def token_shift_centred(z, mu_prev, mu_next):
    zp = jnp.pad(z, ((0, 0), (1, 0), (0, 0)))[:, :-1]
    zn = jnp.pad(z, ((0, 0), (0, 1), (0, 0)))[:, 1:]
    return z + mu_prev * (zp - z) + mu_next * (zn - z)


def to_dir_scan(z):
    z = jnp.stack([z[:, :, 0], jnp.flip(z[:, :, 1], axis=1)], axis=0)
    return jnp.moveaxis(z, 2, 0)


def rwkv7_bidir(cols, mu_prev, mu_next, w0, w2, a0, a2, g2, k_k, k_a, r_k, ln_g, ln_b):
    B, S, _ = cols.shape
    H, N = RW_HEADS, RW_HEAD_DIM
    z = token_shift_centred(cols.astype(jnp.float32), mu_prev, mu_next)
    r, k, v, wd, ad, gd = split_cols(z, [RW_WIDTH] * 3 + [N_DIR * RW_DECAY_LORA, N_DIR * RW_ICLR_LORA, RW_GATE_LORA])
    heads = lambda t: t.reshape(t.shape[:-1] + (H, N))
    shared = lambda t: jnp.broadcast_to(t[:, :, None], (B, S, N_DIR, H, N))
    w_raw = w0 + jnp.einsum('bsdr,drc->bsdc', jnp.tanh(wd.reshape(B, S, N_DIR, RW_DECAY_LORA)), w2)
    decay = jnp.exp(-jnp.exp(-jax.nn.softplus(-w_raw) - 0.5))
    a = jax.nn.sigmoid(a0 + jnp.einsum('bsdr,drc->bsdc', ad.reshape(B, S, N_DIR, RW_ICLR_LORA), a2))
    g = jnp.einsum('bsr,rc->bsc', jax.nn.sigmoid(gd), g2)
    kk = heads(k * k_k)
    kk = kk / jnp.maximum(jnp.sqrt(jnp.sum(kk * kk, axis=-1, keepdims=True)), 1e-12)
    k_dir = k[:, :, None, :] * (1.0 + (a - 1.0) * k_a)
    kka = kk[:, :, None] * heads(a)
    xs = (to_dir_scan(shared(heads(r))), to_dir_scan(heads(decay)), to_dir_scan(heads(k_dir)),
          to_dir_scan(shared(heads(v))), to_dir_scan(shared(kk)), to_dir_scan(kka))

    def step(state, inp):
        r_t, w_t, k_t, v_t, kk_t, kka_t = inp
        sa = jnp.einsum('dbhvk,dbhk->dbhv', state, kk_t)
        state = state * w_t[..., None, :] - sa[..., :, None] * kka_t[..., None, :] + v_t[..., :, None] * k_t[..., None, :]
        y = jnp.einsum('dbhvk,dbhk->dbhv', state, r_t)
        return state, y

    state0 = jnp.zeros((N_DIR, B, H, N, N), jnp.float32)
    _, ys = lax.scan(step, state0, xs)
    y = jnp.moveaxis(ys[:, 0] + jnp.flip(ys[:, 1], axis=0), 0, 1)
    mu = jnp.mean(y, axis=-1, keepdims=True)
    var = jnp.mean(jnp.square(y - mu), axis=-1, keepdims=True)
    y = ((y - mu) * lax.rsqrt(var + RW_LN_EPS)).reshape(B, S, RW_WIDTH) * ln_g + ln_b
    k_bonus = heads(jnp.mean(k_dir, axis=2))
    bonus = jnp.sum(heads(r) * k_bonus * r_k, axis=-1, keepdims=True) * heads(v)
    return ((y + bonus.reshape(B, S, RW_WIDTH)) * g).astype(cols.dtype)


def short_gated_conv(cols, w, b):
    b_gate, c_gate, x_in = split_cols(cols, [CV_WIDTH] * 3)
    z = c_gate * x_in
    S = z.shape[1]
    zp = jnp.pad(z, ((0, 0), (1, 1), (0, 0)))
    conv = b + sum(w[j] * zp[:, j:j + S] for j in range(CV_KSIZE))
    return b_gate * conv


def axial_rope_tables(S):
    rows = S // GRID_W
    row = jnp.repeat(jnp.arange(rows), GRID_W).astype(jnp.float32)
    col = jnp.tile(jnp.arange(GRID_W), rows).astype(jnp.float32)
    half = AT_HEAD_DIM // 2
    inv = ROPE_THETA ** (-jnp.arange(0, half, 2, dtype=jnp.float32) / half)
    ang_r = row[:, None] * inv
    ang_c = col[:, None] * inv
    return jnp.cos(ang_r), jnp.sin(ang_r), jnp.cos(ang_c), jnp.sin(ang_c)


def rotate_half(x, cos, sin):
    d2 = x.shape[-1] // 2
    x1, x2 = x[..., :d2], x[..., d2:]
    c, s = cos[:, None, :], sin[:, None, :]
    return jnp.concatenate([x1 * c - x2 * s, x2 * c + x1 * s], axis=-1)


def apply_axial_rope(x, tabs):
    cr, sr, cc, sc = tabs
    half = AT_HEAD_DIM // 2
    xf = x.astype(jnp.float32)
    out = jnp.concatenate([rotate_half(xf[..., :half], cr, sr), rotate_half(xf[..., half:], cc, sc)], axis=-1)
    return out.astype(x.dtype)


def gqa_axial(cols, q_g, k_g, tabs):
    B, S, _ = cols.shape
    G = AT_Q_HEADS // AT_KV_HEADS
    q, k, v = split_cols(cols, [AT_Q_HEADS * AT_HEAD_DIM, AT_KV_HEADS * AT_HEAD_DIM, AT_KV_HEADS * AT_HEAD_DIM])
    q = apply_axial_rope(rmsnorm(q.reshape(B, S, AT_Q_HEADS, AT_HEAD_DIM), q_g), tabs)
    k = apply_axial_rope(rmsnorm(k.reshape(B, S, AT_KV_HEADS, AT_HEAD_DIM), k_g), tabs)
    v = v.reshape(B, S, AT_KV_HEADS, AT_HEAD_DIM)
    q = (q * (AT_HEAD_DIM ** -0.5)).reshape(B, S // AT_Q_BLOCK, AT_Q_BLOCK, AT_KV_HEADS, G, AT_HEAD_DIM)
    q = jnp.moveaxis(q, 1, 0)

    def block(qb):
        s = jnp.einsum('bqkgd,bskd->bkgqs', qb, k).astype(jnp.float32)
        p = jax.nn.softmax(s, axis=-1).astype(v.dtype)
        return jnp.einsum('bkgqs,bskd->bqkgd', p, v)

    o = lax.map(block, q)
    return jnp.moveaxis(o, 0, 1).reshape(B, S, AT_Q_HEADS * AT_HEAD_DIM)


def setup_inputs(seed: int = 0) -> dict:
    key = jax.random.key(seed)
    ks = jax.random.split(key, 32)
    L, D = DEPTH, D_MODEL
    nrm = lambda k, shape, s: jax.random.normal(k, shape, jnp.float32) * s
    uni = lambda k, shape, lo, hi: jax.random.uniform(k, shape, jnp.float32, lo, hi)
    return {
        "x": nrm(ks[0], (BATCH, SEQ, D), 1.0),
        "norm1_g": 1.0 + nrm(ks[1], (L, D), 0.02),
        "w_in": nrm(ks[2], (L, D, IN_COLS), D ** -0.5),
        "rw_mu_prev": uni(ks[3], (L, RW_COLS), 0.0, 0.5),
        "rw_mu_next": uni(ks[4], (L, RW_COLS), 0.0, 0.5),
        "rw_w0": uni(ks[5], (L, N_DIR, RW_WIDTH), -6.0, 1.0),
        "rw_w2": nrm(ks[6], (L, N_DIR, RW_DECAY_LORA, RW_WIDTH), 0.1),
        "rw_a0": nrm(ks[7], (L, N_DIR, RW_WIDTH), 0.5),
        "rw_a2": nrm(ks[8], (L, N_DIR, RW_ICLR_LORA, RW_WIDTH), 0.1),
        "rw_g2": nrm(ks[9], (L, RW_GATE_LORA, RW_WIDTH), RW_GATE_LORA ** -0.5),
        "rw_k_k": 0.85 + nrm(ks[10], (L, RW_WIDTH), 0.05),
        "rw_k_a": 1.0 + nrm(ks[11], (L, RW_WIDTH), 0.05),
        "rw_r_k": nrm(ks[12], (L, RW_HEADS, RW_HEAD_DIM), 0.1),
        "rw_ln_g": 1.0 + nrm(ks[13], (L, RW_WIDTH), 0.02),
        "rw_ln_b": nrm(ks[14], (L, RW_WIDTH), 0.01),
        "cv_w": nrm(ks[15], (L, CV_KSIZE, CV_WIDTH), 0.5),
        "cv_b": nrm(ks[16], (L, CV_WIDTH), 0.01),
        "at_q_g": 1.0 + nrm(ks[17], (L, AT_HEAD_DIM), 0.02),
        "at_k_g": 1.0 + nrm(ks[18], (L, AT_HEAD_DIM), 0.02),
        "w_branch": nrm(ks[19], (L, N_BRANCH, BRANCH_WIDTH, D), BRANCH_WIDTH ** -0.5),
        "w_out": nrm(ks[20], (L, D, D), D ** -0.5),
        "norm2_g": 1.0 + nrm(ks[21], (L, D), 0.02),
        "w_mlp1": nrm(ks[22], (L, D, D_FF), D ** -0.5),
        "w_mlp2": nrm(ks[23], (L, D_FF, D), D_FF ** -0.5),
    }


def reference(x, norm1_g, w_in, rw_mu_prev, rw_mu_next, rw_w0, rw_w2, rw_a0, rw_a2, rw_g2,
              rw_k_k, rw_k_a, rw_r_k, rw_ln_g, rw_ln_b, cv_w, cv_b, at_q_g, at_k_g,
              w_branch, w_out, norm2_g, w_mlp1, w_mlp2):
    B, S, D = x.shape
    tabs = axial_rope_tables(S)
    for l in range(DEPTH):
        h = rmsnorm(x, norm1_g[l])
        cols = h @ w_in[l]
        rw_c, cv_c, at_c, gate_c = split_cols(cols, [RW_COLS, CV_COLS, AT_COLS, GATE_COLS])
        y_rw = rwkv7_bidir(rw_c, rw_mu_prev[l], rw_mu_next[l], rw_w0[l], rw_w2[l], rw_a0[l], rw_a2[l],
                           rw_g2[l], rw_k_k[l], rw_k_a[l], rw_r_k[l], rw_ln_g[l], rw_ln_b[l])
        y_cv = short_gated_conv(cv_c, cv_w[l], cv_b[l])
        y_at = gqa_axial(at_c, at_q_g[l], at_k_g[l], tabs)
        gates = jax.nn.sigmoid(gate_c.reshape(B, S, N_BRANCH, D))
        merged = (gates[:, :, 0] * (y_rw @ w_branch[l, 0])
                  + gates[:, :, 1] * (y_cv @ w_branch[l, 1])
                  + gates[:, :, 2] * (y_at @ w_branch[l, 2]))
        x = x + (merged @ w_out[l]).astype(x.dtype)
        h = rmsnorm(x, norm2_g[l])
        u = jnp.square(jax.nn.relu(h @ w_mlp1[l]))
        x = x + (u @ w_mlp2[l]).astype(x.dtype)
    return x
```

```python
import functools
import math

import numpy as np
import jax
import jax.numpy as jnp
from jax import lax
from jax.experimental import pallas as pl
from jax.experimental.pallas import tpu as pltpu

F32 = jnp.float32
BF16 = jnp.bfloat16

D_MODEL = 1024
DEPTH = 4
GRID_W = 64
NORM_EPS = 1e-6
HEAD = 64
RW_WIDTH = 512
RW_LORA = 128
RW_LN_EPS = 64e-5
RW_COLS = 3 * RW_WIDTH + 3 * RW_LORA
CV_WIDTH = 512
CV_COLS = 3 * CV_WIDTH
AT_Q_HEADS = 8
AT_KV_HEADS = 2
AT_COLS = (AT_Q_HEADS + 2 * AT_KV_HEADS) * HEAD
GATE_COLS = 3 * D_MODEL
ROPE_THETA = 10000.0
D_FF = 4 * D_MODEL

LANES = 128
SUBLANES = 8
VMEM_LIMIT_BYTES = 56 * 1024 * 1024
CHUNK = 64
TM_INPROJ = 256
TS_PREP = 256
TB_SCAN = 256
TS_CONV = 512
TQ_ATTN = 64
TK_PREP = 512
TM_MERGE = 256
TM_MLP = 256
FF_CHUNK = 1024


def _cparams(*sem):
    return pltpu.CompilerParams(dimension_semantics=sem, vmem_limit_bytes=VMEM_LIMIT_BYTES)


def _dot(a, b):
    return jnp.dot(a, b, preferred_element_type=F32)


def _dot_nt(a, b):
    return lax.dot_general(a, b, (((1,), (1,)), ((), ())), preferred_element_type=F32)


def _split2(x):
    hi = x.astype(BF16)
    lo = (x - hi.astype(F32)).astype(BF16)
    return hi, lo


def _segsum(x, j):
    hi, lo = _split2(x)
    return _dot(hi, j) + _dot(lo, j)


def _sigmoid(x):
    return 1.0 / (1.0 + jnp.exp(-x))


def _resident(shape):
    nd = len(shape)
    return pl.BlockSpec(shape, lambda *_: (0,) * nd, pipeline_mode=pl.Buffered(1))


def _shift_rows(c, halo_prev, halo_next, first, last):
    n = c.shape[0]
    row = lax.broadcasted_iota(jnp.int32, c.shape, 0)
    lane_row = lax.broadcasted_iota(jnp.int32, (1, c.shape[1]), 1)
    is_first = (lane_row * 0 + first.astype(jnp.int32)) > 0
    is_last = (lane_row * 0 + last.astype(jnp.int32)) > 0
    prev_row = jnp.where(is_first, 0.0, halo_prev[SUBLANES - 1:SUBLANES, :])
    next_row = jnp.where(is_last, 0.0, halo_next[0:1, :])
    prev = jnp.where(row == 0, prev_row, pltpu.roll(c, 1, 0))
    nxt = jnp.where(row == n - 1, next_row, pltpu.roll(c, n - 1, 0))
    return prev, nxt


def _halo_specs(ts, width, n_seq_rows):
    r = ts // SUBLANES
    last_blk = n_seq_rows // SUBLANES - 1
    prev = pl.BlockSpec((None, SUBLANES, width), lambda b, i: (b, jnp.maximum(i * r - 1, 0), 0))
    nxt = pl.BlockSpec((None, SUBLANES, width), lambda b, i: (b, jnp.minimum((i + 1) * r, last_blk), 0))
    return prev, nxt


def _inproj_kernel(x_ref, g_ref, w_ref, rw_ref, cv_ref, at_ref, sg_ref):
    x = x_ref[...]
    h = (x * lax.rsqrt(jnp.mean(x * x, axis=-1, keepdims=True) + NORM_EPS) * g_ref[...]).astype(BF16)
    o0, o1, o2 = RW_COLS, RW_COLS + CV_COLS, RW_COLS + CV_COLS + AT_COLS
    rw_ref[...] = _dot(h, w_ref[:, 0:o0])
    cv_ref[...] = _dot(h, w_ref[:, o0:o1])
    at_ref[...] = _dot(h, w_ref[:, o1:o2])
    sg_ref[...] = _sigmoid(_dot(h, w_ref[:, o2:o2 + GATE_COLS]))


def _inproj(x, g, w):
    t = x.shape[0]
    tm = TM_INPROJ
    row = lambda n: pl.BlockSpec((tm, n), lambda i: (i, 0))
    return pl.pallas_call(
        _inproj_kernel,
        grid=(t // tm,),
        in_specs=[row(D_MODEL), _resident((1, D_MODEL)), _resident(w.shape)],
        out_specs=[row(RW_COLS), row(CV_COLS), row(AT_COLS), row(GATE_COLS)],
        out_shape=[jax.ShapeDtypeStruct((t, n), F32) for n in (RW_COLS, CV_COLS, AT_COLS, GATE_COLS)],
        compiler_params=_cparams("parallel"),
    )(x, g, w)


def _conv_kernel(c_ref, hp_ref, hn_ref, w_ref, b_ref, o_ref):
    i = pl.program_id(1)
    c = c_ref[...]
    w = CV_WIDTH
    z = c[:, w:2 * w] * c[:, 2 * w:3 * w]
    hp, hn = hp_ref[...], hn_ref[...]
    zp, zn = _shift_rows(z, hp[:, w:2 * w] * hp[:, 2 * w:3 * w], hn[:, w:2 * w] * hn[:, 2 * w:3 * w],
                         i == 0, i == pl.num_programs(1) - 1)
    conv = b_ref[...] + w_ref[0:1, :] * zp + w_ref[1:2, :] * z + w_ref[2:3, :] * zn
    o_ref[...] = (c[:, 0:w] * conv).astype(BF16)


def _conv(cv, w, b):
    bsz, s, _ = cv.shape
    ts = min(TS_CONV, s)
    hp, hn = _halo_specs(ts, CV_COLS, s)
    return pl.pallas_call(
        _conv_kernel,
        grid=(bsz, s // ts),
        in_specs=[pl.BlockSpec((None, ts, CV_COLS), lambda b_, i: (b_, i, 0)), hp, hn,
                  pl.BlockSpec((3, CV_WIDTH), lambda b_, i: (0, 0)),
                  pl.BlockSpec((1, CV_WIDTH), lambda b_, i: (0, 0))],
        out_specs=pl.BlockSpec((None, ts, CV_WIDTH), lambda b_, i: (b_, i, 0)),
        out_shape=jax.ShapeDtypeStruct((bsz, s, CV_WIDTH), BF16),
        compiler_params=_cparams("parallel", "arbitrary"),
    )(cv, cv, cv, w, b)


def _prep_kernel(c_ref, hp_ref, hn_ref, mup_ref, mun_ref, w0_ref, w2_ref, a0_ref, a2_ref, g2_ref,
                 kk_w_ref, ka_ref, rk_ref, j_ref,
                 r_ref, v_ref, kk_ref, g_ref, bonus_ref, lw_ref, kd_ref, kka_ref):
    i = pl.program_id(1)
    c = c_ref[...]
    prev, nxt = _shift_rows(c, hp_ref[...], hn_ref[...], i == 0, i == pl.num_programs(1) - 1)
    z = c + mup_ref[...] * (prev - c) + mun_ref[...] * (nxt - c)
    w = RW_WIDTH
    r, k, v = z[:, 0:w], z[:, w:2 * w], z[:, 2 * w:3 * w]
    wd = z[:, 3 * w:3 * w + RW_LORA]
    ad = z[:, 3 * w + RW_LORA:3 * w + 2 * RW_LORA]
    gd = z[:, 3 * w + 2 * RW_LORA:3 * w + 3 * RW_LORA]
    j = j_ref[...]
    w_raw = w0_ref[...] + _dot(jnp.tanh(wd).astype(BF16), w2_ref[...])
    lw = -math.exp(-0.5) * _sigmoid(w_raw)
    a = _sigmoid(a0_ref[...] + _dot(ad.astype(BF16), a2_ref[...]))
    g_ref[...] = _dot(_sigmoid(gd).astype(BF16), g2_ref[...])
    kkr = k * kk_w_ref[...]
    kk = kkr / jnp.maximum(jnp.sqrt(_segsum(kkr * kkr, j)), 1e-12)
    r_ref[...] = r
    v_ref[...] = v
    kk_ref[...] = kk
    ksum = jnp.zeros_like(k)
    for d in range(2):
        a_d = a[:, d * w:(d + 1) * w]
        k_d = k * (1.0 + (a_d - 1.0) * ka_ref[...])
        ksum = ksum + k_d
        lw_ref[d] = lw[:, d * w:(d + 1) * w]
        kd_ref[d] = k_d
        kka_ref[d] = kk * a_d
    bonus_ref[...] = _segsum(r * (0.5 * ksum) * rk_ref[...], j) * v


def _prep(rw, mup, mun, w0, w2bd, a0, a2bd, g2, k_k, k_a, r_k, jones):
    bsz, s, _ = rw.shape
    ts = min(TS_PREP, s)
    hp, hn = _halo_specs(ts, RW_COLS, s)
    tok = pl.BlockSpec((None, ts, RW_WIDTH), lambda b_, i: (b_, i, 0))
    tok2 = pl.BlockSpec((2, None, ts, RW_WIDTH), lambda b_, i: (0, b_, i, 0))
    cst = lambda a: pl.BlockSpec(a.shape, lambda b_, i: (0,) * a.ndim)
    consts = (mup, mun, w0, w2bd, a0, a2bd, g2, k_k, k_a, r_k, jones)
    one = jax.ShapeDtypeStruct((bsz, s, RW_WIDTH), F32)
    two = jax.ShapeDtypeStruct((2, bsz, s, RW_WIDTH), F32)
    return pl.pallas_call(
        _prep_kernel,
        grid=(bsz, s // ts),
        in_specs=[pl.BlockSpec((None, ts, RW_COLS), lambda b_, i: (b_, i, 0)), hp, hn] + [cst(a) for a in consts],
        out_specs=[tok] * 5 + [tok2] * 3,
        out_shape=[one] * 5 + [two] * 3,
        compiler_params=_cparams("parallel", "arbitrary"),
    )(rw, rw, rw, *consts)


def _scan_masks():
    c = CHUNK
    t = np.arange(c)
    lower = (t[None, :] <= t[:, None]).astype(np.float32)
    tri = np.stack([np.concatenate([lower, np.ones((c, c), np.float32)], 0),
                    np.concatenate([lower.T, np.ones((c, c), np.float32)], 0)])
    eye2 = np.kron(np.eye(2, dtype=np.float32), np.ones((c, c), np.float32))
    strict_l = np.kron(np.eye(2, dtype=np.float32), (t[None, :] < t[:, None]).astype(np.float32))
    incl_l = np.kron(np.eye(2, dtype=np.float32), lower)
    strict = np.stack([strict_l, strict_l.T]) * eye2
    incl = np.stack([incl_l, incl_l.T]) * eye2
    return jnp.asarray(tri, BF16), jnp.asarray(strict, F32), jnp.asarray(incl, F32)


def _scan_kernel(r_ref, v_ref, kk_ref, lw_ref, kd_ref, kka_ref, tri_ref, strict_ref, incl_ref, y_ref, st_ref):
    d = pl.program_id(1)
    c = CHUNK
    n_chunks = r_ref.shape[0] // c

    @pl.when(pl.program_id(2) == 0)
    def _():
        st_ref[...] = jnp.zeros_like(st_ref)

    tri = tri_ref[...]
    strict = strict_ref[...]
    incl = incl_ref[...]
    lane = lax.broadcasted_iota(jnp.int32, (c, LANES), 1)
    head0 = lane < HEAD
    ri = lax.broadcasted_iota(jnp.int32, (2 * c, 2 * c), 0)
    ci = lax.broadcasted_iota(jnp.int32, (2 * c, 2 * c), 1)
    eye = jnp.where(ri == ci, 1.0, 0.0).astype(F32)

    def stack(x):
        return jnp.concatenate([jnp.where(head0, x, 0.0), jnp.where(head0, 0.0, x)], axis=0)

    def chunk_step(step, carry):
        cidx = step + d * (n_chunks - 1 - 2 * step)
        rows = pl.ds(pl.multiple_of(cidx * c, c), c)
        lw = lw_ref[rows, :]
        hi = lw.astype(BF16)
        rem = lw - hi.astype(F32)
        mid = rem.astype(BF16)
        lo = (rem - mid.astype(F32)).astype(BF16)
        cum = _dot(tri, hi) + _dot(tri, mid) + _dot(tri, lo)
        lp = cum[0:c]
        e_end = jnp.exp(cum[c:2 * c])
        e_p = jnp.exp(lp)
        e_pe = jnp.exp(lp - lw)
        e_n = jnp.exp(-lp)
        rq = r_ref[rows, :] * e_p
        kq = kk_ref[rows, :] * e_pe
        ks = kd_ref[rows, :] * e_n
        bs = kka_ref[rows, :] * e_n
        kdec = ks * e_end
        bdec = bs * e_end
        v = v_ref[rows, :]
        for p in range(RW_WIDTH // LANES):
            sl = slice(p * LANES, (p + 1) * LANES)
            l_kq = stack(kq[:, sl]).astype(BF16)
            l_rq = stack(rq[:, sl]).astype(BF16)
            s_ks = stack(ks[:, sl]).astype(BF16)
            s_bs = stack(bs[:, sl]).astype(BF16)
            s_v = stack(v[:, sl])
            s_vb = s_v.astype(BF16)
            l2 = jnp.concatenate([l_kq, l_rq], axis=0)
            gk = _dot_nt(l2, s_ks)
            gb = _dot_nt(l2, s_bs)
            a_vk = (gk[0:2 * c] * strict).astype(BF16)
            a_rk = (gk[2 * c:4 * c] * incl).astype(BF16)
            m = -(gb[0:2 * c] * strict)
            a_rb = (gb[2 * c:4 * c] * incl).astype(BF16)
            mb = m.astype(BF16)
            t_inv = eye + m
            pw = _dot(mb, mb)
            n_levels = int(math.log2(c)) - 1
            for lvl in range(n_levels):
                pb = pw.astype(BF16)
                t_inv = t_inv + _dot(t_inv.astype(BF16), pb)
                if lvl + 1 < n_levels:
                    pw = _dot(pb, pb)
            st = st_ref[p]
            stb = st.astype(BF16)
            rhs = _dot_nt(l_kq, stb) + _dot(a_vk, s_vb)
            t_b = t_inv.astype(BF16)
            u0 = -_dot(t_b, rhs.astype(BF16))
            m_hi, m_lo = _split2(m)
            u_hi, u_lo = _split2(u0)
            mu = _dot(m_hi, u_hi) + _dot(m_hi, u_lo) + _dot(m_lo, u_hi)
            u = u0 + _dot(t_b, (mu - rhs - u0).astype(BF16))
            ub = u.astype(BF16)
            y = _dot_nt(l_rq, stb) + _dot(jnp.concatenate([a_rk, a_rb], axis=1),
                                          jnp.concatenate([s_vb, ub], axis=0))
            y_ref[rows, sl] = y[0:c] + y[c:2 * c]
            vu_t = jnp.concatenate([s_v, u], axis=0).T.astype(BF16)
            kb = jnp.concatenate([stack(kdec[:, sl]), stack(bdec[:, sl])], axis=0).astype(BF16)
            st_ref[p] = st * e_end[0:1, sl] + _dot(vu_t, kb)
        return carry

    lax.fori_loop(0, n_chunks, chunk_step, 0)


def _scan(r, v, kk, lw, kd, kka, masks):
    bsz, s, w = r.shape
    tb = min(TB_SCAN, s)
    nblk = s // tb
    blk = lambda d, j: j + d * (nblk - 1 - 2 * j)
    tok = pl.BlockSpec((None, tb, w), lambda b_, d, j: (b_, blk(d, j), 0))
    tok2 = pl.BlockSpec((None, None, tb, w), lambda b_, d, j: (d, b_, blk(d, j), 0))
    msk = lambda a: pl.BlockSpec((None,) + a.shape[1:], lambda b_, d, j: (d, 0, 0))
    tri, strict, incl = masks
    return pl.pallas_call(
        _scan_kernel,
        grid=(bsz, 2, nblk),
        in_specs=[tok, tok, tok, tok2, tok2, tok2, msk(tri), msk(strict), msk(incl)],
        out_specs=tok2,
        out_shape=jax.ShapeDtypeStruct((2, bsz, s, w), F32),
        scratch_shapes=[pltpu.VMEM((w // LANES, LANES, LANES), F32)],
        compiler_params=_cparams("parallel", "arbitrary", "arbitrary"),
    )(r, v, kk, lw, kd, kka, tri, strict, incl)


def _rope_partner(x):
    n = x.shape[1]
    lane = lax.broadcasted_iota(jnp.int32, x.shape, 1)
    return jnp.where((lane & 31) < 16, pltpu.roll(x, n - 16, 1), pltpu.roll(x, 16, 1))


def _attn_kernel(q_ref, k_ref, v_ref, cq_ref, sq_ref, ck_ref, sk_ref, qg_ref, kg_ref, jq_ref, jk_ref,
                 o_ref, kd_ref, vd_ref):
    s_len = k_ref.shape[0]
    tq = q_ref.shape[0]
    tk = min(TK_PREP, s_len)

    @pl.when(pl.program_id(1) == 0)
    def _():
        def body(i, carry):
            rows = pl.ds(pl.multiple_of(i * tk, tk), tk)
            k = k_ref[rows, :]
            kn = k * lax.rsqrt(_segsum(k * k, jk_ref[...]) * (1.0 / HEAD) + NORM_EPS) * kg_ref[...]
            kr = kn * ck_ref[rows, :] + _rope_partner(kn) * sk_ref[rows, :]
            v = v_ref[rows, :]
            low = lax.broadcasted_iota(jnp.int32, kr.shape, 1) < HEAD
            ksw = pltpu.roll(kr, HEAD, 1)
            vsw = pltpu.roll(v, HEAD, 1)
            kd_ref[0, rows, :] = jnp.where(low, kr, ksw).astype(BF16)
            kd_ref[1, rows, :] = jnp.where(low, ksw, kr).astype(BF16)
            vd_ref[0, rows, :] = jnp.where(low, v, vsw).astype(BF16)
            vd_ref[1, rows, :] = jnp.where(low, vsw, v).astype(BF16)
            return carry
        lax.fori_loop(0, s_len // tk, body, 0)

    q = q_ref[...]
    qn = q * lax.rsqrt(_segsum(q * q, jq_ref[...]) * (1.0 / HEAD) + NORM_EPS) * qg_ref[...]
    qr = qn * cq_ref[...] + _rope_partner(qn) * sq_ref[...]
    low = lax.broadcasted_iota(jnp.int32, (tq, LANES), 1) < HEAD
    pairs_per_kv = AT_Q_HEADS // AT_KV_HEADS // 2
    for g in range(AT_KV_HEADS):
        parts = []
        for jj in range(pairs_per_kv):
            qp = qr[:, (g * pairs_per_kv + jj) * LANES:(g * pairs_per_kv + jj + 1) * LANES]
            parts += [jnp.where(low, qp, 0.0), jnp.where(low, 0.0, qp)]
        qs = jnp.concatenate(parts, axis=0).astype(BF16)
        s = _dot_nt(qs, kd_ref[g])
        p = jnp.exp(s - jnp.max(s, axis=-1, keepdims=True))
        denom = jnp.sum(p, axis=-1, keepdims=True)
        o = _dot(p.astype(BF16), vd_ref[g]) / denom
        for jj in range(pairs_per_kv):
            pair = jnp.where(low, o[(2 * jj) * tq:(2 * jj + 1) * tq], o[(2 * jj + 1) * tq:(2 * jj + 2) * tq])
            col = (g * pairs_per_kv + jj) * LANES
            o_ref[:, col:col + LANES] = pair.astype(BF16)


def _attn(at, tabs, q_g, k_g, jones):
    bsz, s, _ = at.shape
    tq = min(TQ_ATTN, s)
    cq, sq, ck, sk = tabs
    qw = AT_Q_HEADS * HEAD
    kvw = AT_KV_HEADS * HEAD
    cst = lambda a: pl.BlockSpec(a.shape, lambda b_, i: (0,) * a.ndim)
    jk = jones[0:kvw, 0:kvw]
    return pl.pallas_call(
        _attn_kernel,
        grid=(bsz, s // tq),
        in_specs=[pl.BlockSpec((None, tq, qw), lambda b_, i: (b_, i, 0)),
                  pl.BlockSpec((None, s, kvw), lambda b_, i: (b_, 0, qw // kvw)),
                  pl.BlockSpec((None, s, kvw), lambda b_, i: (b_, 0, qw // kvw + 1)),
                  pl.BlockSpec((tq, qw), lambda b_, i: (i, 0)),
                  pl.BlockSpec((tq, qw), lambda b_, i: (i, 0)),
                  cst(ck), cst(sk), cst(q_g), cst(k_g), cst(jones), cst(jk)],
        out_specs=pl.BlockSpec((None, tq, qw), lambda b_, i: (b_, i, 0)),
        out_shape=jax.ShapeDtypeStruct((bsz, s, qw), BF16),
        scratch_shapes=[pltpu.VMEM((AT_KV_HEADS, s, LANES), BF16), pltpu.VMEM((AT_KV_HEADS, s, LANES), BF16)],
        compiler_params=_cparams("parallel", "arbitrary"),
    )(at, at, at, cq, sq, ck, sk, q_g, k_g, jones, jk)


def _rope_tables(s):
    rows = s // GRID_W
    pos_r = jnp.repeat(jnp.arange(rows), GRID_W).astype(F32)
    pos_c = jnp.tile(jnp.arange(GRID_W), rows).astype(F32)
    half = HEAD // 2
    inv = ROPE_THETA ** (-jnp.arange(0, half, 2, dtype=F32) / half)
    ang_r = pos_r[:, None] * inv
    ang_c = pos_c[:, None] * inv
    cos = jnp.concatenate([jnp.cos(ang_r)] * 2 + [jnp.cos(ang_c)] * 2, axis=-1)
    sin = jnp.concatenate([-jnp.sin(ang_r), jnp.sin(ang_r), -jnp.sin(ang_c), jnp.sin(ang_c)], axis=-1)
    scale = HEAD ** -0.5
    return (jnp.tile(cos, (1, AT_Q_HEADS)) * scale, jnp.tile(sin, (1, AT_Q_HEADS)) * scale,
            jnp.tile(cos, (1, AT_KV_HEADS)), jnp.tile(sin, (1, AT_KV_HEADS)))


def _merge_kernel(y0_ref, y1_ref, g_ref, bonus_ref, lng_ref, lnb_ref, j_ref, ycv_ref, yat_ref, sg_ref, x_ref,
                  wb_ref, wo_ref, o_ref):
    j = j_ref[...]
    y = y0_ref[...] + y1_ref[...]
    mu = _segsum(y, j) * (1.0 / HEAD)
    dev = y - mu
    var = _segsum(dev * dev, j) * (1.0 / HEAD)
    yn = dev * lax.rsqrt(var + RW_LN_EPS) * lng_ref[...] + lnb_ref[...]
    y_rw = ((yn + bonus_ref[...]) * g_ref[...]).astype(BF16)
    dm = D_MODEL
    merged = (sg_ref[:, 0:dm] * _dot(y_rw, wb_ref[0])
              + sg_ref[:, dm:2 * dm] * _dot(ycv_ref[...], wb_ref[1])
              + sg_ref[:, 2 * dm:3 * dm] * _dot(yat_ref[...], wb_ref[2]))
    o_ref[...] = x_ref[...] + _dot(merged.astype(BF16), wo_ref[...])


def _merge(y, g, bonus, ln_g, ln_b, jones, y_cv, y_at, sg, x, w_branch, w_out):
    t = x.shape[0]
    tm = TM_MERGE
    row = lambda n: pl.BlockSpec((tm, n), lambda i: (i, 0))
    return pl.pallas_call(
        _merge_kernel,
        grid=(t // tm,),
        in_specs=[pl.BlockSpec((None, tm, RW_WIDTH), lambda i: (0, i, 0)),
                  pl.BlockSpec((None, tm, RW_WIDTH), lambda i: (1, i, 0)),
                  row(RW_WIDTH), row(RW_WIDTH), _resident(ln_g.shape), _resident(ln_b.shape), _resident(jones.shape),
                  row(CV_WIDTH), row(AT_Q_HEADS * HEAD), row(GATE_COLS), row(D_MODEL),
                  _resident(w_branch.shape), _resident(w_out.shape)],
        out_specs=row(D_MODEL),
        out_shape=jax.ShapeDtypeStruct((t, D_MODEL), F32),
        compiler_params=_cparams("parallel"),
    )(y, y, g, bonus, ln_g, ln_b, jones, y_cv, y_at, sg, x, w_branch, w_out)


def _mlp_kernel(x_ref, g_ref, w1_ref, w2_ref, o_ref):
    x = x_ref[...]
    h = (x * lax.rsqrt(jnp.mean(x * x, axis=-1, keepdims=True) + NORM_EPS) * g_ref[...]).astype(BF16)
    acc = x
    for c in range(D_FF // FF_CHUNK):
        cols = slice(c * FF_CHUNK, (c + 1) * FF_CHUNK)
        u = jnp.maximum(_dot(h, w1_ref[:, cols]), 0.0)
        acc = acc + _dot((u * u).astype(BF16), w2_ref[cols, :])
    o_ref[...] = acc


def _mlp(x, g, w1, w2):
    t = x.shape[0]
    tm = TM_MLP
    row = pl.BlockSpec((tm, D_MODEL), lambda i: (i, 0))
    return pl.pallas_call(
        _mlp_kernel,
        grid=(t // tm,),
        in_specs=[row, _resident(g.shape), _resident(w1.shape), _resident(w2.shape)],
        out_specs=row,
        out_shape=jax.ShapeDtypeStruct((t, D_MODEL), F32),
        compiler_params=_cparams("parallel"),
    )(x, g, w1, w2)


def _block_diag2(w):
    z = jnp.zeros_like(w[0])
    return jnp.concatenate([jnp.concatenate([w[0], z], axis=1), jnp.concatenate([z, w[1]], axis=1)], axis=0)


def kernel(x, norm1_g, w_in, rw_mu_prev, rw_mu_next, rw_w0, rw_w2, rw_a0, rw_a2, rw_g2, rw_k_k, rw_k_a, rw_r_k,
           rw_ln_g, rw_ln_b, cv_w, cv_b, at_q_g, at_k_g, w_branch, w_out, norm2_g, w_mlp1, w_mlp2):
    bsz, s, dm = x.shape
    assert dm == D_MODEL and s % GRID_W == 0
    for tile in (TS_PREP, TB_SCAN, TS_CONV, TQ_ATTN, TK_PREP):
        assert s % min(tile, s) == 0 and min(tile, s) % CHUNK == 0
    t = bsz * s
    tabs = _rope_tables(s)
    masks = _scan_masks()
    jones = jnp.asarray(np.kron(np.eye(RW_WIDTH // HEAD, dtype=np.float32), np.ones((HEAD, HEAD), np.float32)), BF16)
    row = lambda a: a.reshape(1, -1)
    xf = x.reshape(t, dm)
    for l in range(DEPTH):
        rw, cv, at, sg = _inproj(xf, row(norm1_g[l]), w_in[l].astype(BF16))
        r, v, kk, g, bonus, lw, kd, kka = _prep(
            rw.reshape(bsz, s, RW_COLS), row(rw_mu_prev[l]), row(rw_mu_next[l]),
            row(rw_w0[l]), _block_diag2(rw_w2[l]).astype(BF16), row(rw_a0[l]), _block_diag2(rw_a2[l]).astype(BF16),
            rw_g2[l].astype(BF16), row(rw_k_k[l]), row(rw_k_a[l]), row(rw_r_k[l]), jones)
        y = _scan(r, v, kk, lw, kd, kka, masks)
        y_cv = _conv(cv.reshape(bsz, s, CV_COLS), cv_w[l], row(cv_b[l]))
        y_at = _attn(at.reshape(bsz, s, AT_COLS), tabs, row(jnp.tile(at_q_g[l], AT_Q_HEADS)),
                     row(jnp.tile(at_k_g[l], AT_KV_HEADS)), jones)
        xf = _merge(y.reshape(2, t, RW_WIDTH), g.reshape(t, RW_WIDTH), bonus.reshape(t, RW_WIDTH),
                    row(rw_ln_g[l]), row(rw_ln_b[l]), jones, y_cv.reshape(t, CV_WIDTH),
                    y_at.reshape(t, AT_Q_HEADS * HEAD), sg, xf, w_branch[l].astype(BF16), w_out[l].astype(BF16))
        xf = _mlp(xf, row(norm2_g[l]), w_mlp1[l].astype(BF16), w_mlp2[l].astype(BF16))
    return xf.reshape(bsz, s, dm)
```

```python
import functools
import math

import numpy as np
import jax
import jax.numpy as jnp
from jax import lax
from jax.experimental import pallas as pl
from jax.experimental.pallas import tpu as pltpu

F32 = jnp.float32
BF16 = jnp.bfloat16

D_MODEL = 1024
DEPTH = 4
GRID_W = 64
NORM_EPS = 1e-6
HEAD = 64
RW_WIDTH = 512
RW_LORA = 128
RW_LN_EPS = 64e-5
RW_COLS = 3 * RW_WIDTH + 3 * RW_LORA
CV_WIDTH = 512
CV_COLS = 3 * CV_WIDTH
AT_Q_HEADS = 8
AT_KV_HEADS = 2
AT_COLS = (AT_Q_HEADS + 2 * AT_KV_HEADS) * HEAD
GATE_COLS = 3 * D_MODEL
ROPE_THETA = 10000.0
D_FF = 4 * D_MODEL

LANES = 128
SUBLANES = 8
VMEM_LIMIT_BYTES = 56 * 1024 * 1024
CHUNK = 64
TM_INPROJ = 256
TS_PREP = 256
TB_SCAN = 256
TS_CONV = 512
TQ_ATTN = 64
TK_PREP = 512
TM_MERGE = 256
TM_MLP = 256
FF_CHUNK = 1024


def _cparams(*sem):
    return pltpu.CompilerParams(dimension_semantics=sem, vmem_limit_bytes=VMEM_LIMIT_BYTES)


def _dot(a, b):
    return jnp.dot(a, b, preferred_element_type=F32)


def _dot_nt(a, b):
    return lax.dot_general(a, b, (((1,), (1,)), ((), ())), preferred_element_type=F32)


def _split2(x):
    hi = x.astype(BF16)
    lo = (x - hi.astype(F32)).astype(BF16)
    return hi, lo


def _segsum(x, j):
    hi, lo = _split2(x)
    return _dot(hi, j) + _dot(lo, j)


def _sigmoid(x):
    return 1.0 / (1.0 + jnp.exp(-x))


def _resident(shape):
    nd = len(shape)
    return pl.BlockSpec(shape, lambda *_: (0,) * nd, pipeline_mode=pl.Buffered(1))


def _shift_rows(c, halo_prev, halo_next, first, last):
    n = c.shape[0]
    row = lax.broadcasted_iota(jnp.int32, c.shape, 0)
    lane_row = lax.broadcasted_iota(jnp.int32, (1, c.shape[1]), 1)
    is_first = (lane_row * 0 + first.astype(jnp.int32)) > 0
    is_last = (lane_row * 0 + last.astype(jnp.int32)) > 0
    prev_row = jnp.where(is_first, 0.0, halo_prev[SUBLANES - 1:SUBLANES, :])
    next_row = jnp.where(is_last, 0.0, halo_next[0:1, :])
    prev = jnp.where(row == 0, prev_row, pltpu.roll(c, 1, 0))
    nxt = jnp.where(row == n - 1, next_row, pltpu.roll(c, n - 1, 0))
    return prev, nxt


def _halo_specs(ts, width, n_seq_rows):
    r = ts // SUBLANES
    last_blk = n_seq_rows // SUBLANES - 1
    prev = pl.BlockSpec((None, SUBLANES, width), lambda b, i: (b, jnp.maximum(i * r - 1, 0), 0))
    nxt = pl.BlockSpec((None, SUBLANES, width), lambda b, i: (b, jnp.minimum((i + 1) * r, last_blk), 0))
    return prev, nxt


def _inproj_kernel(x_ref, g_ref, w_ref, rw_ref, cv_ref, at_ref, sg_ref):
    x = x_ref[...]
    h = (x * lax.rsqrt(jnp.mean(x * x, axis=-1, keepdims=True) + NORM_EPS) * g_ref[...]).astype(BF16)
    o0, o1, o2 = RW_COLS, RW_COLS + CV_COLS, RW_COLS + CV_COLS + AT_COLS
    rw_ref[...] = _dot(h, w_ref[:, 0:o0])
    cv_ref[...] = _dot(h, w_ref[:, o0:o1])
    at_ref[...] = _dot(h, w_ref[:, o1:o2])
    sg_ref[...] = _sigmoid(_dot(h, w_ref[:, o2:o2 + GATE_COLS]))


def _inproj(x, g, w):
    t = x.shape[0]
    tm = TM_INPROJ
    row = lambda n: pl.BlockSpec((tm, n), lambda i: (i, 0))
    return pl.pallas_call(
        _inproj_kernel,
        grid=(t // tm,),
        in_specs=[row(D_MODEL), _resident((1, D_MODEL)), _resident(w.shape)],
        out_specs=[row(RW_COLS), row(CV_COLS), row(AT_COLS), row(GATE_COLS)],
        out_shape=[jax.ShapeDtypeStruct((t, n), F32) for n in (RW_COLS, CV_COLS, AT_COLS, GATE_COLS)],
        compiler_params=_cparams("parallel"),
    )(x, g, w)


def _conv_kernel(c_ref, hp_ref, hn_ref, w_ref, b_ref, o_ref):
    i = pl.program_id(1)
    c = c_ref[...]
    w = CV_WIDTH
    z = c[:, w:2 * w] * c[:, 2 * w:3 * w]
    hp, hn = hp_ref[...], hn_ref[...]
    zp, zn = _shift_rows(z, hp[:, w:2 * w] * hp[:, 2 * w:3 * w], hn[:, w:2 * w] * hn[:, 2 * w:3 * w],
                         i == 0, i == pl.num_programs(1) - 1)
    conv = b_ref[...] + w_ref[0:1, :] * zp + w_ref[1:2, :] * z + w_ref[2:3, :] * zn
    o_ref[...] = (c[:, 0:w] * conv).astype(BF16)


def _conv(cv, w, b):
    bsz, s, _ = cv.shape
    ts = min(TS_CONV, s)
    hp, hn = _halo_specs(ts, CV_COLS, s)
    return pl.pallas_call(
        _conv_kernel,
        grid=(bsz, s // ts),
        in_specs=[pl.BlockSpec((None, ts, CV_COLS), lambda b_, i: (b_, i, 0)), hp, hn,
                  pl.BlockSpec((3, CV_WIDTH), lambda b_, i: (0, 0)),
                  pl.BlockSpec((1, CV_WIDTH), lambda b_, i: (0, 0))],
        out_specs=pl.BlockSpec((None, ts, CV_WIDTH), lambda b_, i: (b_, i, 0)),
        out_shape=jax.ShapeDtypeStruct((bsz, s, CV_WIDTH), BF16),
        compiler_params=_cparams("parallel", "arbitrary"),
    )(cv, cv, cv, w, b)


def _prep_kernel(c_ref, hp_ref, hn_ref, mup_ref, mun_ref, w0_ref, w2_ref, a0_ref, a2_ref, g2_ref,
                 kk_w_ref, ka_ref, rk_ref, j_ref,
                 r_ref, v_ref, kk_ref, g_ref, bonus_ref, lw_ref, kd_ref, kka_ref):
    i = pl.program_id(1)
    c = c_ref[...]
    prev, nxt = _shift_rows(c, hp_ref[...], hn_ref[...], i == 0, i == pl.num_programs(1) - 1)
    z = c + mup_ref[...] * (prev - c) + mun_ref[...] * (nxt - c)
    w = RW_WIDTH
    r, k, v = z[:, 0:w], z[:, w:2 * w], z[:, 2 * w:3 * w]
    wd = z[:, 3 * w:3 * w + RW_LORA]
    ad = z[:, 3 * w + RW_LORA:3 * w + 2 * RW_LORA]
    gd = z[:, 3 * w + 2 * RW_LORA:3 * w + 3 * RW_LORA]
    j = j_ref[...]
    w_raw = w0_ref[...] + _dot(jnp.tanh(wd).astype(BF16), w2_ref[...])
    lw = -math.exp(-0.5) * _sigmoid(w_raw)
    a = _sigmoid(a0_ref[...] + _dot(ad.astype(BF16), a2_ref[...]))
    g_ref[...] = _dot(_sigmoid(gd).astype(BF16), g2_ref[...])
    kkr = k * kk_w_ref[...]
    kk = kkr / jnp.maximum(jnp.sqrt(_segsum(kkr * kkr, j)), 1e-12)
    r_ref[...] = r
    v_ref[...] = v
    kk_ref[...] = kk
    ksum = jnp.zeros_like(k)
    for d in range(2):
        a_d = a[:, d * w:(d + 1) * w]
        k_d = k * (1.0 + (a_d - 1.0) * ka_ref[...])
        ksum = ksum + k_d
        lw_ref[d] = lw[:, d * w:(d + 1) * w]
        kd_ref[d] = k_d
        kka_ref[d] = kk * a_d
    bonus_ref[...] = _segsum(r * (0.5 * ksum) * rk_ref[...], j) * v


def _prep(rw, mup, mun, w0, w2bd, a0, a2bd, g2, k_k, k_a, r_k, jones):
    bsz, s, _ = rw.shape
    ts = min(TS_PREP, s)
    hp, hn = _halo_specs(ts, RW_COLS, s)
    tok = pl.BlockSpec((None, ts, RW_WIDTH), lambda b_, i: (b_, i, 0))
    tok2 = pl.BlockSpec((2, None, ts, RW_WIDTH), lambda b_, i: (0, b_, i, 0))
    cst = lambda a: pl.BlockSpec(a.shape, lambda b_, i: (0,) * a.ndim)
    consts = (mup, mun, w0, w2bd, a0, a2bd, g2, k_k, k_a, r_k, jones)
    one = jax.ShapeDtypeStruct((bsz, s, RW_WIDTH), F32)
    two = jax.ShapeDtypeStruct((2, bsz, s, RW_WIDTH), F32)
    return pl.pallas_call(
        _prep_kernel,
        grid=(bsz, s // ts),
        in_specs=[pl.BlockSpec((None, ts, RW_COLS), lambda b_, i: (b_, i, 0)), hp, hn] + [cst(a) for a in consts],
        out_specs=[tok] * 5 + [tok2] * 3,
        out_shape=[one] * 5 + [two] * 3,
        compiler_params=_cparams("parallel", "arbitrary"),
    )(rw, rw, rw, *consts)


def _scan_masks():
    c = CHUNK
    t = np.arange(c)
    lower = (t[None, :] <= t[:, None]).astype(np.float32)
    tri = np.stack([np.concatenate([lower, np.ones((c, c), np.float32)], 0),
                    np.concatenate([lower.T, np.ones((c, c), np.float32)], 0)])
    eye2 = np.kron(np.eye(2, dtype=np.float32), np.ones((c, c), np.float32))
    strict_l = np.kron(np.eye(2, dtype=np.float32), (t[None, :] < t[:, None]).astype(np.float32))
    incl_l = np.kron(np.eye(2, dtype=np.float32), lower)
    strict = np.stack([strict_l, strict_l.T]) * eye2
    incl = np.stack([incl_l, incl_l.T]) * eye2
    return jnp.asarray(tri, BF16), jnp.asarray(strict, F32), jnp.asarray(incl, F32)


def _scan_kernel(r_ref, v_ref, kk_ref, lw_ref, kd_ref, kka_ref, tri_ref, strict_ref, incl_ref, y_ref, st_ref):
    d = pl.program_id(1)
    c = CHUNK
    n_chunks = r_ref.shape[0] // c

    @pl.when(pl.program_id(2) == 0)
    def _():
        st_ref[...] = jnp.zeros_like(st_ref)

    tri = tri_ref[...]
    strict = strict_ref[...]
    incl = incl_ref[...]
    lane = lax.broadcasted_iota(jnp.int32, (c, LANES), 1)
    head0 = lane < HEAD
    ri = lax.broadcasted_iota(jnp.int32, (2 * c, 2 * c), 0)
    ci = lax.broadcasted_iota(jnp.int32, (2 * c, 2 * c), 1)
    eye = jnp.where(ri == ci, 1.0, 0.0).astype(F32)

    def stack(x):
        return jnp.concatenate([jnp.where(head0, x, 0.0), jnp.where(head0, 0.0, x)], axis=0)

    def chunk_step(step, carry):
        cidx = step + d * (n_chunks - 1 - 2 * step)
        rows = pl.ds(pl.multiple_of(cidx * c, c), c)
        lw = lw_ref[rows, :]
        hi = lw.astype(BF16)
        rem = lw - hi.astype(F32)
        mid = rem.astype(BF16)
        lo = (rem - mid.astype(F32)).astype(BF16)
        cum = _dot(tri, hi) + _dot(tri, mid) + _dot(tri, lo)
        lp = cum[0:c]
        e_end = jnp.exp(cum[c:2 * c])
        e_p = jnp.exp(lp)
        e_pe = jnp.exp(lp - lw)
        e_n = jnp.exp(-lp)
        rq = r_ref[rows, :] * e_p
        kq = kk_ref[rows, :] * e_pe
        ks = kd_ref[rows, :] * e_n
        bs = kka_ref[rows, :] * e_n
        kdec = ks * e_end
        bdec = bs * e_end
        v = v_ref[rows, :]
        pairs = range(RW_WIDTH // LANES)
        sls = [slice(p * LANES, (p + 1) * LANES) for p in pairs]
        l_kq = [stack(kq[:, sl]).astype(BF16) for sl in sls]
        l_rq = [stack(rq[:, sl]).astype(BF16) for sl in sls]
        s_ks = [stack(ks[:, sl]).astype(BF16) for sl in sls]
        s_bs = [stack(bs[:, sl]).astype(BF16) for sl in sls]
        s_v = [stack(v[:, sl]) for sl in sls]
        s_vb = [x.astype(BF16) for x in s_v]
        st = [st_ref[p] for p in pairs]
        stb = [x.astype(BF16) for x in st]
        l2 = [jnp.concatenate([l_kq[p], l_rq[p]], axis=0) for p in pairs]
        gk = [_dot_nt(l2[p], s_ks[p]) for p in pairs]
        gb = [_dot_nt(l2[p], s_bs[p]) for p in pairs]
        m = [-(gb[p][0:2 * c] * strict) for p in pairs]
        a_vk = [(gk[p][0:2 * c] * strict).astype(BF16) for p in pairs]
        a_rk = [(gk[p][2 * c:4 * c] * incl).astype(BF16) for p in pairs]
        a_rb = [(gb[p][2 * c:4 * c] * incl).astype(BF16) for p in pairs]
        rhs = [_dot_nt(l_kq[p], stb[p]) + _dot(a_vk[p], s_vb[p]) for p in pairs]
        mb = [x.astype(BF16) for x in m]
        t_inv = [eye + x for x in m]
        pw = [_dot(x, x) for x in mb]
        n_levels = int(math.log2(c)) - 1
        for lvl in range(n_levels):
            pb = [x.astype(BF16) for x in pw]
            t_inv = [t_inv[p] + _dot(t_inv[p].astype(BF16), pb[p]) for p in pairs]
            if lvl + 1 < n_levels:
                pw = [_dot(x, x) for x in pb]
        t_b = [x.astype(BF16) for x in t_inv]
        u0 = [-_dot(t_b[p], rhs[p].astype(BF16)) for p in pairs]
        m_sp = [_split2(x) for x in m]
        u_sp = [_split2(x) for x in u0]
        mu = [_dot(m_sp[p][0], u_sp[p][0]) + _dot(m_sp[p][0], u_sp[p][1]) + _dot(m_sp[p][1], u_sp[p][0])
              for p in pairs]
        u = [u0[p] + _dot(t_b[p], (mu[p] - rhs[p] - u0[p]).astype(BF16)) for p in pairs]
        ub = [x.astype(BF16) for x in u]
        y = [_dot_nt(l_rq[p], stb[p]) + _dot(jnp.concatenate([a_rk[p], a_rb[p]], axis=1),
                                             jnp.concatenate([s_vb[p], ub[p]], axis=0)) for p in pairs]
        vu_t = [jnp.concatenate([s_v[p], u[p]], axis=0).T.astype(BF16) for p in pairs]
        kb = [jnp.concatenate([stack(kdec[:, sl]), stack(bdec[:, sl])], axis=0).astype(BF16) for sl in sls]
        for p in pairs:
            y_ref[rows, sls[p]] = y[p][0:c] + y[p][c:2 * c]
            st_ref[p] = st[p] * e_end[0:1, sls[p]] + _dot(vu_t[p], kb[p])
        return carry

    lax.fori_loop(0, n_chunks, chunk_step, 0)


def _scan(r, v, kk, lw, kd, kka, masks):
    bsz, s, w = r.shape
    tb = min(TB_SCAN, s)
    nblk = s // tb
    blk = lambda d, j: j + d * (nblk - 1 - 2 * j)
    tok = pl.BlockSpec((None, tb, w), lambda b_, d, j: (b_, blk(d, j), 0))
    tok2 = pl.BlockSpec((None, None, tb, w), lambda b_, d, j: (d, b_, blk(d, j), 0))
    msk = lambda a: pl.BlockSpec((None,) + a.shape[1:], lambda b_, d, j: (d, 0, 0))
    tri, strict, incl = masks
    return pl.pallas_call(
        _scan_kernel,
        grid=(bsz, 2, nblk),
        in_specs=[tok, tok, tok, tok2, tok2, tok2, msk(tri), msk(strict), msk(incl)],
        out_specs=tok2,
        out_shape=jax.ShapeDtypeStruct((2, bsz, s, w), F32),
        scratch_shapes=[pltpu.VMEM((w // LANES, LANES, LANES), F32)],
        compiler_params=_cparams("parallel", "arbitrary", "arbitrary"),
    )(r, v, kk, lw, kd, kka, tri, strict, incl)


def _rope_partner(x):
    n = x.shape[1]
    lane = lax.broadcasted_iota(jnp.int32, x.shape, 1)
    return jnp.where((lane & 31) < 16, pltpu.roll(x, n - 16, 1), pltpu.roll(x, 16, 1))


def _attn_kernel(q_ref, k_ref, v_ref, cq_ref, sq_ref, ck_ref, sk_ref, qg_ref, kg_ref, jq_ref, jk_ref,
                 o_ref, kd_ref, vd_ref):
    s_len = k_ref.shape[0]
    tq = q_ref.shape[0]
    tk = min(TK_PREP, s_len)

    @pl.when(pl.program_id(1) == 0)
    def _():
        def body(i, carry):
            rows = pl.ds(pl.multiple_of(i * tk, tk), tk)
            k = k_ref[rows, :]
            kn = k * lax.rsqrt(_segsum(k * k, jk_ref[...]) * (1.0 / HEAD) + NORM_EPS) * kg_ref[...]
            kr = kn * ck_ref[rows, :] + _rope_partner(kn) * sk_ref[rows, :]
            v = v_ref[rows, :]
            low = lax.broadcasted_iota(jnp.int32, kr.shape, 1) < HEAD
            ksw = pltpu.roll(kr, HEAD, 1)
            vsw = pltpu.roll(v, HEAD, 1)
            kd_ref[0, rows, :] = jnp.where(low, kr, ksw).astype(BF16)
            kd_ref[1, rows, :] = jnp.where(low, ksw, kr).astype(BF16)
            vd_ref[0, rows, :] = jnp.where(low, v, vsw).astype(BF16)
            vd_ref[1, rows, :] = jnp.where(low, vsw, v).astype(BF16)
            return carry
        lax.fori_loop(0, s_len // tk, body, 0)

    q = q_ref[...]
    qn = q * lax.rsqrt(_segsum(q * q, jq_ref[...]) * (1.0 / HEAD) + NORM_EPS) * qg_ref[...]
    qr = qn * cq_ref[...] + _rope_partner(qn) * sq_ref[...]
    low = lax.broadcasted_iota(jnp.int32, (tq, LANES), 1) < HEAD
    pairs_per_kv = AT_Q_HEADS // AT_KV_HEADS // 2
    for g in range(AT_KV_HEADS):
        parts = []
        for jj in range(pairs_per_kv):
            qp = qr[:, (g * pairs_per_kv + jj) * LANES:(g * pairs_per_kv + jj + 1) * LANES]
            parts += [jnp.where(low, qp, 0.0), jnp.where(low, 0.0, qp)]
        qs = jnp.concatenate(parts, axis=0).astype(BF16)
        s = _dot_nt(qs, kd_ref[g])
        p = jnp.exp(s - jnp.max(s, axis=-1, keepdims=True))
        denom = jnp.sum(p, axis=-1, keepdims=True)
        o = _dot(p.astype(BF16), vd_ref[g]) / denom
        for jj in range(pairs_per_kv):
            pair = jnp.where(low, o[(2 * jj) * tq:(2 * jj + 1) * tq], o[(2 * jj + 1) * tq:(2 * jj + 2) * tq])
            col = (g * pairs_per_kv + jj) * LANES
            o_ref[:, col:col + LANES] = pair.astype(BF16)


def _attn(at, tabs, q_g, k_g, jones):
    bsz, s, _ = at.shape
    tq = min(TQ_ATTN, s)
    cq, sq, ck, sk = tabs
    qw = AT_Q_HEADS * HEAD
    kvw = AT_KV_HEADS * HEAD
    cst = lambda a: pl.BlockSpec(a.shape, lambda b_, i: (0,) * a.ndim)
    jk = jones[0:kvw, 0:kvw]
    return pl.pallas_call(
        _attn_kernel,
        grid=(bsz, s // tq),
        in_specs=[pl.BlockSpec((None, tq, qw), lambda b_, i: (b_, i, 0)),
                  pl.BlockSpec((None, s, kvw), lambda b_, i: (b_, 0, qw // kvw)),
                  pl.BlockSpec((None, s, kvw), lambda b_, i: (b_, 0, qw // kvw + 1)),
                  pl.BlockSpec((tq, qw), lambda b_, i: (i, 0)),
                  pl.BlockSpec((tq, qw), lambda b_, i: (i, 0)),
                  cst(ck), cst(sk), cst(q_g), cst(k_g), cst(jones), cst(jk)],
        out_specs=pl.BlockSpec((None, tq, qw), lambda b_, i: (b_, i, 0)),
        out_shape=jax.ShapeDtypeStruct((bsz, s, qw), BF16),
        scratch_shapes=[pltpu.VMEM((AT_KV_HEADS, s, LANES), BF16), pltpu.VMEM((AT_KV_HEADS, s, LANES), BF16)],
        compiler_params=_cparams("parallel", "arbitrary"),
    )(at, at, at, cq, sq, ck, sk, q_g, k_g, jones, jk)


def _rope_tables(s):
    rows = s // GRID_W
    pos_r = jnp.repeat(jnp.arange(rows), GRID_W).astype(F32)
    pos_c = jnp.tile(jnp.arange(GRID_W), rows).astype(F32)
    half = HEAD // 2
    inv = ROPE_THETA ** (-jnp.arange(0, half, 2, dtype=F32) / half)
    ang_r = pos_r[:, None] * inv
    ang_c = pos_c[:, None] * inv
    cos = jnp.concatenate([jnp.cos(ang_r)] * 2 + [jnp.cos(ang_c)] * 2, axis=-1)
    sin = jnp.concatenate([-jnp.sin(ang_r), jnp.sin(ang_r), -jnp.sin(ang_c), jnp.sin(ang_c)], axis=-1)
    scale = HEAD ** -0.5
    return (jnp.tile(cos, (1, AT_Q_HEADS)) * scale, jnp.tile(sin, (1, AT_Q_HEADS)) * scale,
            jnp.tile(cos, (1, AT_KV_HEADS)), jnp.tile(sin, (1, AT_KV_HEADS)))


def _merge_kernel(y0_ref, y1_ref, g_ref, bonus_ref, lng_ref, lnb_ref, j_ref, ycv_ref, yat_ref, sg_ref, x_ref,
                  wb_ref, wo_ref, o_ref):
    j = j_ref[...]
    y = y0_ref[...] + y1_ref[...]
    mu = _segsum(y, j) * (1.0 / HEAD)
    dev = y - mu
    var = _segsum(dev * dev, j) * (1.0 / HEAD)
    yn = dev * lax.rsqrt(var + RW_LN_EPS) * lng_ref[...] + lnb_ref[...]
    y_rw = ((yn + bonus_ref[...]) * g_ref[...]).astype(BF16)
    dm = D_MODEL
    merged = (sg_ref[:, 0:dm] * _dot(y_rw, wb_ref[0])
              + sg_ref[:, dm:2 * dm] * _dot(ycv_ref[...], wb_ref[1])
              + sg_ref[:, 2 * dm:3 * dm] * _dot(yat_ref[...], wb_ref[2]))
    o_ref[...] = x_ref[...] + _dot(merged.astype(BF16), wo_ref[...])


def _merge(y, g, bonus, ln_g, ln_b, jones, y_cv, y_at, sg, x, w_branch, w_out):
    t = x.shape[0]
    tm = TM_MERGE
    row = lambda n: pl.BlockSpec((tm, n), lambda i: (i, 0))
    return pl.pallas_call(
        _merge_kernel,
        grid=(t // tm,),
        in_specs=[pl.BlockSpec((None, tm, RW_WIDTH), lambda i: (0, i, 0)),
                  pl.BlockSpec((None, tm, RW_WIDTH), lambda i: (1, i, 0)),
                  row(RW_WIDTH), row(RW_WIDTH), _resident(ln_g.shape), _resident(ln_b.shape), _resident(jones.shape),
                  row(CV_WIDTH), row(AT_Q_HEADS * HEAD), row(GATE_COLS), row(D_MODEL),
                  _resident(w_branch.shape), _resident(w_out.shape)],
        out_specs=row(D_MODEL),
        out_shape=jax.ShapeDtypeStruct((t, D_MODEL), F32),
        compiler_params=_cparams("parallel"),
    )(y, y, g, bonus, ln_g, ln_b, jones, y_cv, y_at, sg, x, w_branch, w_out)


def _mlp_kernel(x_ref, g_ref, w1_ref, w2_ref, o_ref):
    x = x_ref[...]
    h = (x * lax.rsqrt(jnp.mean(x * x, axis=-1, keepdims=True) + NORM_EPS) * g_ref[...]).astype(BF16)
    acc = x
    for c in range(D_FF // FF_CHUNK):
        cols = slice(c * FF_CHUNK, (c + 1) * FF_CHUNK)
        u = jnp.maximum(_dot(h, w1_ref[:, cols]), 0.0)
        acc = acc + _dot((u * u).astype(BF16), w2_ref[cols, :])
    o_ref[...] = acc


def _mlp(x, g, w1, w2):
    t = x.shape[0]
    tm = TM_MLP
    row = pl.BlockSpec((tm, D_MODEL), lambda i: (i, 0))
    return pl.pallas_call(
        _mlp_kernel,
        grid=(t // tm,),
        in_specs=[row, _resident(g.shape), _resident(w1.shape), _resident(w2.shape)],
        out_specs=row,
        out_shape=jax.ShapeDtypeStruct((t, D_MODEL), F32),
        compiler_params=_cparams("parallel"),
    )(x, g, w1, w2)


def _block_diag2(w):
    z = jnp.zeros_like(w[0])
    return jnp.concatenate([jnp.concatenate([w[0], z], axis=1), jnp.concatenate([z, w[1]], axis=1)], axis=0)


def kernel(x, norm1_g, w_in, rw_mu_prev, rw_mu_next, rw_w0, rw_w2, rw_a0, rw_a2, rw_g2, rw_k_k, rw_k_a, rw_r_k,
           rw_ln_g, rw_ln_b, cv_w, cv_b, at_q_g, at_k_g, w_branch, w_out, norm2_g, w_mlp1, w_mlp2):
    bsz, s, dm = x.shape
    assert dm == D_MODEL and s % GRID_W == 0
    for tile in (TS_PREP, TB_SCAN, TS_CONV, TQ_ATTN, TK_PREP):
        assert s % min(tile, s) == 0 and min(tile, s) % CHUNK == 0
    t = bsz * s
    tabs = _rope_tables(s)
    masks = _scan_masks()
    jones = jnp.asarray(np.kron(np.eye(RW_WIDTH // HEAD, dtype=np.float32), np.ones((HEAD, HEAD), np.float32)), BF16)
    row = lambda a: a.reshape(1, -1)
    xf = x.reshape(t, dm)
    for l in range(DEPTH):
        rw, cv, at, sg = _inproj(xf, row(norm1_g[l]), w_in[l].astype(BF16))
        r, v, kk, g, bonus, lw, kd, kka = _prep(
            rw.reshape(bsz, s, RW_COLS), row(rw_mu_prev[l]), row(rw_mu_next[l]),
            row(rw_w0[l]), _block_diag2(rw_w2[l]).astype(BF16), row(rw_a0[l]), _block_diag2(rw_a2[l]).astype(BF16),
            rw_g2[l].astype(BF16), row(rw_k_k[l]), row(rw_k_a[l]), row(rw_r_k[l]), jones)
        y = _scan(r, v, kk, lw, kd, kka, masks)
        y_cv = _conv(cv.reshape(bsz, s, CV_COLS), cv_w[l], row(cv_b[l]))
        y_at = _attn(at.reshape(bsz, s, AT_COLS), tabs, row(jnp.tile(at_q_g[l], AT_Q_HEADS)),
                     row(jnp.tile(at_k_g[l], AT_KV_HEADS)), jones)
        xf = _merge(y.reshape(2, t, RW_WIDTH), g.reshape(t, RW_WIDTH), bonus.reshape(t, RW_WIDTH),
                    row(rw_ln_g[l]), row(rw_ln_b[l]), jones, y_cv.reshape(t, CV_WIDTH),
                    y_at.reshape(t, AT_Q_HEADS * HEAD), sg, xf, w_branch[l].astype(BF16), w_out[l].astype(BF16))
        xf = _mlp(xf, row(norm2_g[l]), w_mlp1[l].astype(BF16), w_mlp2[l].astype(BF16))
    return xf.reshape(bsz, s, dm)
```

```python
import functools
import math

import numpy as np
import jax
import jax.numpy as jnp
from jax import lax
from jax.experimental import pallas as pl
from jax.experimental.pallas import tpu as pltpu

F32 = jnp.float32
BF16 = jnp.bfloat16

D_MODEL = 1024
DEPTH = 4
GRID_W = 64
NORM_EPS = 1e-6
HEAD = 64
RW_WIDTH = 512
RW_LORA = 128
RW_LN_EPS = 64e-5
RW_COLS = 3 * RW_WIDTH + 3 * RW_LORA
CV_WIDTH = 512
CV_COLS = 3 * CV_WIDTH
AT_Q_HEADS = 8
AT_KV_HEADS = 2
AT_COLS = (AT_Q_HEADS + 2 * AT_KV_HEADS) * HEAD
GATE_COLS = 3 * D_MODEL
ROPE_THETA = 10000.0
D_FF = 4 * D_MODEL

LANES = 128
SUBLANES = 8
VMEM_LIMIT_BYTES = 56 * 1024 * 1024
CHUNK = 64
TM_INPROJ = 256
TS_PREP = 256
TB_SCAN = 256
TS_CONV = 512
TQ_ATTN = 64
TK_PREP = 512
TM_MERGE = 256
TM_MLP = 256
FF_CHUNK = 1024


def _cparams(*sem):
    return pltpu.CompilerParams(dimension_semantics=sem, vmem_limit_bytes=VMEM_LIMIT_BYTES)


def _dot(a, b):
    return jnp.dot(a, b, preferred_element_type=F32)


def _dot_nt(a, b):
    return lax.dot_general(a, b, (((1,), (1,)), ((), ())), preferred_element_type=F32)


def _split2(x):
    hi = x.astype(BF16)
    lo = (x - hi.astype(F32)).astype(BF16)
    return hi, lo


def _segsum(x, j):
    hi, lo = _split2(x)
    return _dot(hi, j) + _dot(lo, j)


def _sigmoid(x):
    return 1.0 / (1.0 + jnp.exp(-x))


def _resident(shape):
    nd = len(shape)
    return pl.BlockSpec(shape, lambda *_: (0,) * nd, pipeline_mode=pl.Buffered(1))


def _shift_rows(c, halo_prev, halo_next, first, last):
    n = c.shape[0]
    row = lax.broadcasted_iota(jnp.int32, c.shape, 0)
    lane_row = lax.broadcasted_iota(jnp.int32, (1, c.shape[1]), 1)
    is_first = (lane_row * 0 + first.astype(jnp.int32)) > 0
    is_last = (lane_row * 0 + last.astype(jnp.int32)) > 0
    prev_row = jnp.where(is_first, 0.0, halo_prev[SUBLANES - 1:SUBLANES, :])
    next_row = jnp.where(is_last, 0.0, halo_next[0:1, :])
    prev = jnp.where(row == 0, prev_row, pltpu.roll(c, 1, 0))
    nxt = jnp.where(row == n - 1, next_row, pltpu.roll(c, n - 1, 0))
    return prev, nxt


def _halo_specs(ts, width, n_seq_rows):
    r = ts // SUBLANES
    last_blk = n_seq_rows // SUBLANES - 1
    prev = pl.BlockSpec((None, SUBLANES, width), lambda b, i: (b, jnp.maximum(i * r - 1, 0), 0))
    nxt = pl.BlockSpec((None, SUBLANES, width), lambda b, i: (b, jnp.minimum((i + 1) * r, last_blk), 0))
    return prev, nxt


def _inproj_kernel(x_ref, g_ref, w_ref, rw_ref, cv_ref, at_ref, sg_ref):
    x = x_ref[...]
    h = (x * lax.rsqrt(jnp.mean(x * x, axis=-1, keepdims=True) + NORM_EPS) * g_ref[...]).astype(BF16)
    o0, o1, o2 = RW_COLS, RW_COLS + CV_COLS, RW_COLS + CV_COLS + AT_COLS
    rw_ref[...] = _dot(h, w_ref[:, 0:o0])
    cv_ref[...] = _dot(h, w_ref[:, o0:o1])
    at_ref[...] = _dot(h, w_ref[:, o1:o2])
    sg_ref[...] = _sigmoid(_dot(h, w_ref[:, o2:o2 + GATE_COLS]))


def _inproj(x, g, w):
    t = x.shape[0]
    tm = TM_INPROJ
    row = lambda n: pl.BlockSpec((tm, n), lambda i: (i, 0))
    return pl.pallas_call(
        _inproj_kernel,
        grid=(t // tm,),
        in_specs=[row(D_MODEL), _resident((1, D_MODEL)), _resident(w.shape)],
        out_specs=[row(RW_COLS), row(CV_COLS), row(AT_COLS), row(GATE_COLS)],
        out_shape=[jax.ShapeDtypeStruct((t, n), F32) for n in (RW_COLS, CV_COLS, AT_COLS, GATE_COLS)],
        compiler_params=_cparams("parallel"),
    )(x, g, w)


def _conv_kernel(c_ref, hp_ref, hn_ref, w_ref, b_ref, o_ref):
    i = pl.program_id(1)
    c = c_ref[...]
    w = CV_WIDTH
    z = c[:, w:2 * w] * c[:, 2 * w:3 * w]
    hp, hn = hp_ref[...], hn_ref[...]
    zp, zn = _shift_rows(z, hp[:, w:2 * w] * hp[:, 2 * w:3 * w], hn[:, w:2 * w] * hn[:, 2 * w:3 * w],
                         i == 0, i == pl.num_programs(1) - 1)
    conv = b_ref[...] + w_ref[0:1, :] * zp + w_ref[1:2, :] * z + w_ref[2:3, :] * zn
    o_ref[...] = (c[:, 0:w] * conv).astype(BF16)


def _conv(cv, w, b):
    bsz, s, _ = cv.shape
    ts = min(TS_CONV, s)
    hp, hn = _halo_specs(ts, CV_COLS, s)
    return pl.pallas_call(
        _conv_kernel,
        grid=(bsz, s // ts),
        in_specs=[pl.BlockSpec((None, ts, CV_COLS), lambda b_, i: (b_, i, 0)), hp, hn,
                  pl.BlockSpec((3, CV_WIDTH), lambda b_, i: (0, 0)),
                  pl.BlockSpec((1, CV_WIDTH), lambda b_, i: (0, 0))],
        out_specs=pl.BlockSpec((None, ts, CV_WIDTH), lambda b_, i: (b_, i, 0)),
        out_shape=jax.ShapeDtypeStruct((bsz, s, CV_WIDTH), BF16),
        compiler_params=_cparams("parallel", "arbitrary"),
    )(cv, cv, cv, w, b)


def _prep_kernel(c_ref, hp_ref, hn_ref, mup_ref, mun_ref, w0_ref, w2_ref, a0_ref, a2_ref, g2_ref,
                 kk_w_ref, ka_ref, rk_ref, j_ref,
                 r_ref, v_ref, kk_ref, g_ref, bonus_ref, lw_ref, kd_ref, kka_ref):
    i = pl.program_id(1)
    c = c_ref[...]
    prev, nxt = _shift_rows(c, hp_ref[...], hn_ref[...], i == 0, i == pl.num_programs(1) - 1)
    z = c + mup_ref[...] * (prev - c) + mun_ref[...] * (nxt - c)
    w = RW_WIDTH
    r, k, v = z[:, 0:w], z[:, w:2 * w], z[:, 2 * w:3 * w]
    wd = z[:, 3 * w:3 * w + RW_LORA]
    ad = z[:, 3 * w + RW_LORA:3 * w + 2 * RW_LORA]
    gd = z[:, 3 * w + 2 * RW_LORA:3 * w + 3 * RW_LORA]
    j = j_ref[...]
    w_raw = w0_ref[...] + _dot(jnp.tanh(wd).astype(BF16), w2_ref[...])
    lw = -math.exp(-0.5) * _sigmoid(w_raw)
    a = _sigmoid(a0_ref[...] + _dot(ad.astype(BF16), a2_ref[...]))
    g_ref[...] = _dot(_sigmoid(gd).astype(BF16), g2_ref[...])
    kkr = k * kk_w_ref[...]
    kk = kkr / jnp.maximum(jnp.sqrt(_segsum(kkr * kkr, j)), 1e-12)
    r_ref[...] = r
    v_ref[...] = v
    kk_ref[...] = kk
    ksum = jnp.zeros_like(k)
    for d in range(2):
        a_d = a[:, d * w:(d + 1) * w]
        k_d = k * (1.0 + (a_d - 1.0) * ka_ref[...])
        ksum = ksum + k_d
        lw_ref[d] = lw[:, d * w:(d + 1) * w]
        kd_ref[d] = k_d
        kka_ref[d] = kk * a_d
    bonus_ref[...] = _segsum(r * (0.5 * ksum) * rk_ref[...], j) * v


def _prep(rw, mup, mun, w0, w2bd, a0, a2bd, g2, k_k, k_a, r_k, jones):
    bsz, s, _ = rw.shape
    ts = min(TS_PREP, s)
    hp, hn = _halo_specs(ts, RW_COLS, s)
    tok = pl.BlockSpec((None, ts, RW_WIDTH), lambda b_, i: (b_, i, 0))
    tok2 = pl.BlockSpec((2, None, ts, RW_WIDTH), lambda b_, i: (0, b_, i, 0))
    cst = lambda a: pl.BlockSpec(a.shape, lambda b_, i: (0,) * a.ndim)
    consts = (mup, mun, w0, w2bd, a0, a2bd, g2, k_k, k_a, r_k, jones)
    one = jax.ShapeDtypeStruct((bsz, s, RW_WIDTH), F32)
    two = jax.ShapeDtypeStruct((2, bsz, s, RW_WIDTH), F32)
    return pl.pallas_call(
        _prep_kernel,
        grid=(bsz, s // ts),
        in_specs=[pl.BlockSpec((None, ts, RW_COLS), lambda b_, i: (b_, i, 0)), hp, hn] + [cst(a) for a in consts],
        out_specs=[tok] * 5 + [tok2] * 3,
        out_shape=[one] * 5 + [two] * 3,
        compiler_params=_cparams("parallel", "arbitrary"),
    )(rw, rw, rw, *consts)


def _scan_masks():
    c = CHUNK
    t = np.arange(c)
    lower = (t[None, :] <= t[:, None]).astype(np.float32)
    strict_l = (t[None, :] < t[:, None]).astype(np.float32)
    ones = np.ones((c, c), np.float32)
    tri = np.stack([np.concatenate([lower, ones], 0), np.concatenate([lower.T, ones], 0)])
    strict = np.stack([np.tile(strict_l, (1, 2)), np.tile(strict_l.T, (1, 2))])
    incl = np.stack([np.tile(lower, (1, 2)), np.tile(lower.T, (1, 2))])
    return jnp.asarray(tri, BF16), jnp.asarray(strict, F32), jnp.asarray(incl, F32)


def _scan_kernel(rf_ref, vf_ref, kkf_ref, rb_ref, vb_ref, kkb_ref,
                 lwf_ref, kdf_ref, kkaf_ref, lwb_ref, kdb_ref, kkab_ref,
                 tri_ref, strict_ref, incl_ref, yf_ref, yb_ref, st_ref):
    c = CHUNK
    n_chunks = rf_ref.shape[0] // c
    n_pairs = RW_WIDTH // LANES

    @pl.when(pl.program_id(1) == 0)
    def _():
        st_ref[...] = jnp.zeros_like(st_ref)

    lane = lax.broadcasted_iota(jnp.int32, (c, LANES), 1)
    row = lax.broadcasted_iota(jnp.int32, (c, LANES), 0)
    head0 = lane < HEAD
    eye = jnp.where((lane & (HEAD - 1)) == row, 1.0, 0.0).astype(F32)
    ri = lax.broadcasted_iota(jnp.int32, (LANES, LANES), 0)
    ci = lax.broadcasted_iota(jnp.int32, (LANES, LANES), 1)
    same_head = jnp.where((ri < HEAD) == (ci < HEAD), 1.0, 0.0).astype(F32)

    def stack(x):
        return jnp.concatenate([jnp.where(head0, x, 0.0), jnp.where(head0, 0.0, x)], axis=0)

    def stack_b(x):
        return stack(x).astype(BF16)

    dir_refs = ((rf_ref, vf_ref, kkf_ref, lwf_ref, kdf_ref, kkaf_ref, yf_ref),
                (rb_ref, vb_ref, kkb_ref, lwb_ref, kdb_ref, kkab_ref, yb_ref))

    def chunk_step(step, carry):
        ch = []
        for d, (r_ref, v_ref, kk_ref, lw_ref, kd_ref, kka_ref, y_ref) in enumerate(dir_refs):
            cidx = step if d == 0 else n_chunks - 1 - step
            rows = pl.ds(pl.multiple_of(cidx * c, c), c)
            lw = lw_ref[rows, :]
            hi = lw.astype(BF16)
            rem = lw - hi.astype(F32)
            mid = rem.astype(BF16)
            lo = (rem - mid.astype(F32)).astype(BF16)
            tri = tri_ref[d]
            cum = _dot(tri, hi) + _dot(tri, mid) + _dot(tri, lo)
            lp = cum[0:c]
            e_end = jnp.exp(cum[c:2 * c])
            e_n = jnp.exp(-lp)
            rq = r_ref[rows, :] * jnp.exp(lp)
            kq = kk_ref[rows, :] * jnp.exp(lp - lw)
            ks = kd_ref[rows, :] * e_n
            bs = kka_ref[rows, :] * e_n
            kdec = ks * e_end
            bdec = bs * e_end
            v = v_ref[rows, :]
            for p in range(n_pairs):
                sl = slice(p * LANES, (p + 1) * LANES)
                ch.append(dict(d=d, p=p, rows=rows, sl=sl, y_ref=y_ref, kq=kq[:, sl], rq=rq[:, sl], ks=ks[:, sl],
                               bs=bs[:, sl], v=v[:, sl], kdec=kdec[:, sl], bdec=bdec[:, sl], e_end=e_end[0:1, sl]))
        n = range(len(ch))
        strict = [strict_ref[x["d"]] for x in ch]
        incl = [incl_ref[x["d"]] for x in ch]
        kq_b = [x["kq"].astype(BF16) for x in ch]
        rq_b = [x["rq"].astype(BF16) for x in ch]
        lq = [jnp.concatenate([kq_b[i], rq_b[i]], axis=0) for i in n]
        s_ks = [stack_b(x["ks"]) for x in ch]
        s_bs = [stack_b(x["bs"]) for x in ch]
        s_vb = [stack_b(x["v"]) for x in ch]
        st = [st_ref[x["d"], x["p"]] for x in ch]
        stb = [x.astype(BF16) for x in st]
        gk = [_dot_nt(lq[i], s_ks[i]) for i in n]
        gb = [_dot_nt(lq[i], s_bs[i]) for i in n]
        m = [-(gb[i][0:c] * strict[i]) for i in n]
        a_vk = [(gk[i][0:c] * strict[i]).astype(BF16) for i in n]
        a_rk = [(gk[i][c:2 * c] * incl[i]).astype(BF16) for i in n]
        a_rb = [(gb[i][c:2 * c] * incl[i]).astype(BF16) for i in n]
        rhs = [_dot_nt(kq_b[i], stb[i]) + _dot(a_vk[i], s_vb[i]) for i in n]
        t_inv = [eye + x for x in m]
        pw = [_dot(x.astype(BF16), stack_b(x)) for x in m]
        n_levels = int(math.log2(c)) - 1
        for lvl in range(n_levels):
            pw_bd = [stack_b(x) for x in pw]
            if lvl + 1 < n_levels:
                both = [_dot(jnp.concatenate([t_inv[i].astype(BF16), pw[i].astype(BF16)], axis=0), pw_bd[i])
                        for i in n]
                t_inv = [t_inv[i] + both[i][0:c] for i in n]
                pw = [both[i][c:2 * c] for i in n]
            else:
                t_inv = [t_inv[i] + _dot(t_inv[i].astype(BF16), pw_bd[i]) for i in n]
        t_b = [x.astype(BF16) for x in t_inv]
        u0 = [-_dot(t_b[i], stack_b(rhs[i])) for i in n]
        m_sp = [_split2(x) for x in m]
        u_sp = [_split2(x) for x in u0]
        mu_a = [_dot(jnp.concatenate(m_sp[i], axis=0), stack(u_sp[i][0].astype(F32)).astype(BF16)) for i in n]
        mu_b = [_dot(m_sp[i][0], stack(u_sp[i][1].astype(F32)).astype(BF16)) for i in n]
        res = [mu_a[i][0:c] + mu_a[i][c:2 * c] + mu_b[i] - rhs[i] - u0[i] for i in n]
        u = [u0[i] + _dot(t_b[i], stack_b(res[i])) for i in n]
        y = [_dot_nt(rq_b[i], stb[i]) + _dot(jnp.concatenate([a_rk[i], a_rb[i]], axis=1),
                                             jnp.concatenate([s_vb[i], stack_b(u[i])], axis=0)) for i in n]
        vu_t = [jnp.concatenate([ch[i]["v"], u[i]], axis=0).T.astype(BF16) for i in n]
        kb = [jnp.concatenate([x["kdec"], x["bdec"]], axis=0).astype(BF16) for x in ch]
        upd = [_dot(vu_t[i], kb[i]) * same_head for i in n]
        for i, x in enumerate(ch):
            x["y_ref"][x["rows"], x["sl"]] = y[i]
            st_ref[x["d"], x["p"]] = st[i] * x["e_end"] + upd[i]
        return carry

    lax.fori_loop(0, n_chunks, chunk_step, 0)


def _scan(r, v, kk, lw, kd, kka, masks):
    bsz, s, w = r.shape
    tb = min(TB_SCAN, s)
    nblk = s // tb
    fwd = pl.BlockSpec((None, tb, w), lambda b_, j: (b_, j, 0))
    bwd = pl.BlockSpec((None, tb, w), lambda b_, j: (b_, nblk - 1 - j, 0))
    fwd2 = pl.BlockSpec((None, None, tb, w), lambda b_, j: (0, b_, j, 0))
    bwd2 = pl.BlockSpec((None, None, tb, w), lambda b_, j: (1, b_, nblk - 1 - j, 0))
    out = jax.ShapeDtypeStruct((bsz, s, w), F32)
    return pl.pallas_call(
        _scan_kernel,
        grid=(bsz, nblk),
        in_specs=[fwd, fwd, fwd, bwd, bwd, bwd, fwd2, fwd2, fwd2, bwd2, bwd2, bwd2]
        + [_resident(a.shape) for a in masks],
        out_specs=[fwd, bwd],
        out_shape=[out, out],
        scratch_shapes=[pltpu.VMEM((2, w // LANES, LANES, LANES), F32)],
        compiler_params=_cparams("parallel", "arbitrary"),
    )(r, v, kk, r, v, kk, lw, kd, kka, lw, kd, kka, *masks)


def _rope_partner(x):
    n = x.shape[1]
    lane = lax.broadcasted_iota(jnp.int32, x.shape, 1)
    return jnp.where((lane & 31) < 16, pltpu.roll(x, n - 16, 1), pltpu.roll(x, 16, 1))


def _attn_kernel(q_ref, k_ref, v_ref, cq_ref, sq_ref, ck_ref, sk_ref, qg_ref, kg_ref, jq_ref, jk_ref,
                 o_ref, kd_ref, vd_ref):
    s_len = k_ref.shape[0]
    tq = q_ref.shape[0]
    tk = min(TK_PREP, s_len)

    @pl.when(pl.program_id(1) == 0)
    def _():
        def body(i, carry):
            rows = pl.ds(pl.multiple_of(i * tk, tk), tk)
            k = k_ref[rows, :]
            kn = k * lax.rsqrt(_segsum(k * k, jk_ref[...]) * (1.0 / HEAD) + NORM_EPS) * kg_ref[...]
            kr = kn * ck_ref[rows, :] + _rope_partner(kn) * sk_ref[rows, :]
            v = v_ref[rows, :]
            low = lax.broadcasted_iota(jnp.int32, kr.shape, 1) < HEAD
            ksw = pltpu.roll(kr, HEAD, 1)
            vsw = pltpu.roll(v, HEAD, 1)
            kd_ref[0, rows, :] = jnp.where(low, kr, ksw).astype(BF16)
            kd_ref[1, rows, :] = jnp.where(low, ksw, kr).astype(BF16)
            vd_ref[0, rows, :] = jnp.where(low, v, vsw).astype(BF16)
            vd_ref[1, rows, :] = jnp.where(low, vsw, v).astype(BF16)
            return carry
        lax.fori_loop(0, s_len // tk, body, 0)

    q = q_ref[...]
    qn = q * lax.rsqrt(_segsum(q * q, jq_ref[...]) * (1.0 / HEAD) + NORM_EPS) * qg_ref[...]
    qr = qn * cq_ref[...] + _rope_partner(qn) * sq_ref[...]
    low = lax.broadcasted_iota(jnp.int32, (tq, LANES), 1) < HEAD
    pairs_per_kv = AT_Q_HEADS // AT_KV_HEADS // 2
    for g in range(AT_KV_HEADS):
        parts = []
        for jj in range(pairs_per_kv):
            qp = qr[:, (g * pairs_per_kv + jj) * LANES:(g * pairs_per_kv + jj + 1) * LANES]
            parts += [jnp.where(low, qp, 0.0), jnp.where(low, 0.0, qp)]
        qs = jnp.concatenate(parts, axis=0).astype(BF16)
        s = _dot_nt(qs, kd_ref[g])
        p = jnp.exp(s - jnp.max(s, axis=-1, keepdims=True))
        denom = jnp.sum(p, axis=-1, keepdims=True)
        o = _dot(p.astype(BF16), vd_ref[g]) / denom
        for jj in range(pairs_per_kv):
            pair = jnp.where(low, o[(2 * jj) * tq:(2 * jj + 1) * tq], o[(2 * jj + 1) * tq:(2 * jj + 2) * tq])
            col = (g * pairs_per_kv + jj) * LANES
            o_ref[:, col:col + LANES] = pair.astype(BF16)


def _attn(at, tabs, q_g, k_g, jones):
    bsz, s, _ = at.shape
    tq = min(TQ_ATTN, s)
    cq, sq, ck, sk = tabs
    qw = AT_Q_HEADS * HEAD
    kvw = AT_KV_HEADS * HEAD
    cst = lambda a: pl.BlockSpec(a.shape, lambda b_, i: (0,) * a.ndim)
    jk = jones[0:kvw, 0:kvw]
    return pl.pallas_call(
        _attn_kernel,
        grid=(bsz, s // tq),
        in_specs=[pl.BlockSpec((None, tq, qw), lambda b_, i: (b_, i, 0)),
                  pl.BlockSpec((None, s, kvw), lambda b_, i: (b_, 0, qw // kvw)),
                  pl.BlockSpec((None, s, kvw), lambda b_, i: (b_, 0, qw // kvw + 1)),
                  pl.BlockSpec((tq, qw), lambda b_, i: (i, 0)),
                  pl.BlockSpec((tq, qw), lambda b_, i: (i, 0)),
                  cst(ck), cst(sk), cst(q_g), cst(k_g), cst(jones), cst(jk)],
        out_specs=pl.BlockSpec((None, tq, qw), lambda b_, i: (b_, i, 0)),
        out_shape=jax.ShapeDtypeStruct((bsz, s, qw), BF16),
        scratch_shapes=[pltpu.VMEM((AT_KV_HEADS, s, LANES), BF16), pltpu.VMEM((AT_KV_HEADS, s, LANES), BF16)],
        compiler_params=_cparams("parallel", "arbitrary"),
    )(at, at, at, cq, sq, ck, sk, q_g, k_g, jones, jk)


def _rope_tables(s):
    rows = s // GRID_W
    pos_r = jnp.repeat(jnp.arange(rows), GRID_W).astype(F32)
    pos_c = jnp.tile(jnp.arange(GRID_W), rows).astype(F32)
    half = HEAD // 2
    inv = ROPE_THETA ** (-jnp.arange(0, half, 2, dtype=F32) / half)
    ang_r = pos_r[:, None] * inv
    ang_c = pos_c[:, None] * inv
    cos = jnp.concatenate([jnp.cos(ang_r)] * 2 + [jnp.cos(ang_c)] * 2, axis=-1)
    sin = jnp.concatenate([-jnp.sin(ang_r), jnp.sin(ang_r), -jnp.sin(ang_c), jnp.sin(ang_c)], axis=-1)
    scale = HEAD ** -0.5
    return (jnp.tile(cos, (1, AT_Q_HEADS)) * scale, jnp.tile(sin, (1, AT_Q_HEADS)) * scale,
            jnp.tile(cos, (1, AT_KV_HEADS)), jnp.tile(sin, (1, AT_KV_HEADS)))


def _merge_kernel(y0_ref, y1_ref, g_ref, bonus_ref, lng_ref, lnb_ref, j_ref, ycv_ref, yat_ref, sg_ref, x_ref,
                  wb_ref, wo_ref, o_ref):
    j = j_ref[...]
    y = y0_ref[...] + y1_ref[...]
    mu = _segsum(y, j) * (1.0 / HEAD)
    dev = y - mu
    var = _segsum(dev * dev, j) * (1.0 / HEAD)
    yn = dev * lax.rsqrt(var + RW_LN_EPS) * lng_ref[...] + lnb_ref[...]
    y_rw = ((yn + bonus_ref[...]) * g_ref[...]).astype(BF16)
    dm = D_MODEL
    merged = (sg_ref[:, 0:dm] * _dot(y_rw, wb_ref[0])
              + sg_ref[:, dm:2 * dm] * _dot(ycv_ref[...], wb_ref[1])
              + sg_ref[:, 2 * dm:3 * dm] * _dot(yat_ref[...], wb_ref[2]))
    o_ref[...] = x_ref[...] + _dot(merged.astype(BF16), wo_ref[...])


def _merge(yf, yb, g, bonus, ln_g, ln_b, jones, y_cv, y_at, sg, x, w_branch, w_out):
    t = x.shape[0]
    tm = TM_MERGE
    row = lambda n: pl.BlockSpec((tm, n), lambda i: (i, 0))
    return pl.pallas_call(
        _merge_kernel,
        grid=(t // tm,),
        in_specs=[row(RW_WIDTH), row(RW_WIDTH), row(RW_WIDTH), row(RW_WIDTH),
                  _resident(ln_g.shape), _resident(ln_b.shape), _resident(jones.shape),
                  row(CV_WIDTH), row(AT_Q_HEADS * HEAD), row(GATE_COLS), row(D_MODEL),
                  _resident(w_branch.shape), _resident(w_out.shape)],
        out_specs=row(D_MODEL),
        out_shape=jax.ShapeDtypeStruct((t, D_MODEL), F32),
        compiler_params=_cparams("parallel"),
    )(yf, yb, g, bonus, ln_g, ln_b, jones, y_cv, y_at, sg, x, w_branch, w_out)


def _mlp_kernel(x_ref, g_ref, w1_ref, w2_ref, o_ref):
    x = x_ref[...]
    h = (x * lax.rsqrt(jnp.mean(x * x, axis=-1, keepdims=True) + NORM_EPS) * g_ref[...]).astype(BF16)
    acc = x
    for c in range(D_FF // FF_CHUNK):
        cols = slice(c * FF_CHUNK, (c + 1) * FF_CHUNK)
        u = jnp.maximum(_dot(h, w1_ref[:, cols]), 0.0)
        acc = acc + _dot((u * u).astype(BF16), w2_ref[cols, :])
    o_ref[...] = acc


def _mlp(x, g, w1, w2):
    t = x.shape[0]
    tm = TM_MLP
    row = pl.BlockSpec((tm, D_MODEL), lambda i: (i, 0))
    return pl.pallas_call(
        _mlp_kernel,
        grid=(t // tm,),
        in_specs=[row, _resident(g.shape), _resident(w1.shape), _resident(w2.shape)],
        out_specs=row,
        out_shape=jax.ShapeDtypeStruct((t, D_MODEL), F32),
        compiler_params=_cparams("parallel"),
    )(x, g, w1, w2)


def _block_diag2(w):
    z = jnp.zeros_like(w[0])
    return jnp.concatenate([jnp.concatenate([w[0], z], axis=1), jnp.concatenate([z, w[1]], axis=1)], axis=0)


def kernel(x, norm1_g, w_in, rw_mu_prev, rw_mu_next, rw_w0, rw_w2, rw_a0, rw_a2, rw_g2, rw_k_k, rw_k_a, rw_r_k,
           rw_ln_g, rw_ln_b, cv_w, cv_b, at_q_g, at_k_g, w_branch, w_out, norm2_g, w_mlp1, w_mlp2):
    bsz, s, dm = x.shape
    assert dm == D_MODEL and s % GRID_W == 0
    for tile in (TS_PREP, TB_SCAN, TS_CONV, TQ_ATTN, TK_PREP):
        assert s % min(tile, s) == 0 and min(tile, s) % CHUNK == 0
    t = bsz * s
    tabs = _rope_tables(s)
    masks = _scan_masks()
    jones = jnp.asarray(np.kron(np.eye(RW_WIDTH // HEAD, dtype=np.float32), np.ones((HEAD, HEAD), np.float32)), BF16)
    row = lambda a: a.reshape(1, -1)
    xf = x.reshape(t, dm)
    for l in range(DEPTH):
        rw, cv, at, sg = _inproj(xf, row(norm1_g[l]), w_in[l].astype(BF16))
        r, v, kk, g, bonus, lw, kd, kka = _prep(
            rw.reshape(bsz, s, RW_COLS), row(rw_mu_prev[l]), row(rw_mu_next[l]),
            row(rw_w0[l]), _block_diag2(rw_w2[l]).astype(BF16), row(rw_a0[l]), _block_diag2(rw_a2[l]).astype(BF16),
            rw_g2[l].astype(BF16), row(rw_k_k[l]), row(rw_k_a[l]), row(rw_r_k[l]), jones)
        yf, yb = _scan(r, v, kk, lw, kd, kka, masks)
        y_cv = _conv(cv.reshape(bsz, s, CV_COLS), cv_w[l], row(cv_b[l]))
        y_at = _attn(at.reshape(bsz, s, AT_COLS), tabs, row(jnp.tile(at_q_g[l], AT_Q_HEADS)),
                     row(jnp.tile(at_k_g[l], AT_KV_HEADS)), jones)
        xf = _merge(yf.reshape(t, RW_WIDTH), yb.reshape(t, RW_WIDTH), g.reshape(t, RW_WIDTH),
                    bonus.reshape(t, RW_WIDTH),
                    row(rw_ln_g[l]), row(rw_ln_b[l]), jones, y_cv.reshape(t, CV_WIDTH),
                    y_at.reshape(t, AT_Q_HEADS * HEAD), sg, xf, w_branch[l].astype(BF16), w_out[l].astype(BF16))
        xf = _mlp(xf, row(norm2_g[l]), w_mlp1[l].astype(BF16), w_mlp2[l].astype(BF16))
    return xf.reshape(bsz, s, dm)
```

```python
import functools
import math

import numpy as np
import jax
import jax.numpy as jnp
from jax import lax
from jax.experimental import pallas as pl
from jax.experimental.pallas import tpu as pltpu

F32 = jnp.float32
BF16 = jnp.bfloat16

D_MODEL = 1024
DEPTH = 4
GRID_W = 64
NORM_EPS = 1e-6
HEAD = 64
RW_WIDTH = 512
RW_LORA = 128
RW_LN_EPS = 64e-5
RW_COLS = 3 * RW_WIDTH + 3 * RW_LORA
CV_WIDTH = 512
CV_COLS = 3 * CV_WIDTH
AT_Q_HEADS = 8
AT_KV_HEADS = 2
AT_COLS = (AT_Q_HEADS + 2 * AT_KV_HEADS) * HEAD
GATE_COLS = 3 * D_MODEL
ROPE_THETA = 10000.0
D_FF = 4 * D_MODEL

LANES = 128
SUBLANES = 8
BF16_SUBLANES = 16
VMEM_LIMIT_BYTES = 56 * 1024 * 1024
CHUNK = 64
TM_INPROJ = 256
TS_PREP = 256
TB_SCAN = 256
PASS_CHUNKS = 2
TS_CONV = 512
TQ_ATTN = 128
TK_ATTN = 1024
TK_PREP = 512
TM_MERGE = 256
TM_MLP = 256
FF_CHUNK = 1024


def _cparams(*sem):
    return pltpu.CompilerParams(dimension_semantics=sem, vmem_limit_bytes=VMEM_LIMIT_BYTES)


def _dot(a, b):
    return jnp.dot(a, b, preferred_element_type=F32)


def _dot_nt(a, b):
    return lax.dot_general(a, b, (((1,), (1,)), ((), ())), preferred_element_type=F32)


def _split2(x):
    hi = x.astype(BF16)
    lo = (x - hi.astype(F32)).astype(BF16)
    return hi, lo


def _segsum(x, j):
    hi, lo = _split2(x)
    return _dot(hi, j) + _dot(lo, j)


def _sigmoid(x):
    return 1.0 / (1.0 + jnp.exp(-x))


def _resident(shape):
    nd = len(shape)
    return pl.BlockSpec(shape, lambda *_: (0,) * nd, pipeline_mode=pl.Buffered(1))


def _shift_rows(c, halo_prev, halo_next, first, last):
    n = c.shape[0]
    row = lax.broadcasted_iota(jnp.int32, c.shape, 0)
    lane_row = lax.broadcasted_iota(jnp.int32, (1, c.shape[1]), 1)
    is_first = (lane_row * 0 + first.astype(jnp.int32)) > 0
    is_last = (lane_row * 0 + last.astype(jnp.int32)) > 0
    prev_row = jnp.where(is_first, 0.0, halo_prev[halo_prev.shape[0] - 1:, :])
    next_row = jnp.where(is_last, 0.0, halo_next[0:1, :])
    prev = jnp.where(row == 0, prev_row, pltpu.roll(c, 1, 0))
    nxt = jnp.where(row == n - 1, next_row, pltpu.roll(c, n - 1, 0))
    return prev, nxt


def _halo_specs(ts, width, n_seq_rows, rows):
    r = ts // rows
    last_blk = n_seq_rows // rows - 1
    prev = pl.BlockSpec((None, rows, width), lambda b, i: (b, jnp.maximum(i * r - 1, 0), 0))
    nxt = pl.BlockSpec((None, rows, width), lambda b, i: (b, jnp.minimum((i + 1) * r, last_blk), 0))
    return prev, nxt


def _inproj_kernel(x_ref, g_ref, w_ref, rw_ref, cv_ref, at_ref, sg_ref):
    x = x_ref[...]
    h = (x * lax.rsqrt(jnp.mean(x * x, axis=-1, keepdims=True) + NORM_EPS) * g_ref[...]).astype(BF16)
    o0, o1, o2 = RW_COLS, RW_COLS + CV_COLS, RW_COLS + CV_COLS + AT_COLS
    rw_ref[...] = _dot(h, w_ref[:, 0:o0])
    cv_ref[...] = _dot(h, w_ref[:, o0:o1]).astype(BF16)
    at_ref[...] = _dot(h, w_ref[:, o1:o2]).astype(BF16)
    sg_ref[...] = _sigmoid(_dot(h, w_ref[:, o2:o2 + GATE_COLS])).astype(BF16)


def _inproj(x, g, w):
    t = x.shape[0]
    tm = TM_INPROJ
    row = lambda n: pl.BlockSpec((tm, n), lambda i: (i, 0))
    return pl.pallas_call(
        _inproj_kernel,
        grid=(t // tm,),
        in_specs=[row(D_MODEL), _resident((1, D_MODEL)), _resident(w.shape)],
        out_specs=[row(RW_COLS), row(CV_COLS), row(AT_COLS), row(GATE_COLS)],
        out_shape=[jax.ShapeDtypeStruct((t, n), dt)
                   for n, dt in ((RW_COLS, F32), (CV_COLS, BF16), (AT_COLS, BF16), (GATE_COLS, BF16))],
        compiler_params=_cparams("parallel"),
    )(x, g, w)


def _conv_kernel(c_ref, hp_ref, hn_ref, w_ref, b_ref, o_ref):
    i = pl.program_id(1)
    c = c_ref[...].astype(F32)
    w = CV_WIDTH
    z = c[:, w:2 * w] * c[:, 2 * w:3 * w]
    hp, hn = hp_ref[...].astype(F32), hn_ref[...].astype(F32)
    zp, zn = _shift_rows(z, hp[:, w:2 * w] * hp[:, 2 * w:3 * w], hn[:, w:2 * w] * hn[:, 2 * w:3 * w],
                         i == 0, i == pl.num_programs(1) - 1)
    conv = b_ref[...] + w_ref[0:1, :] * zp + w_ref[1:2, :] * z + w_ref[2:3, :] * zn
    o_ref[...] = (c[:, 0:w] * conv).astype(BF16)


def _conv(cv, w, b):
    bsz, s, _ = cv.shape
    ts = min(TS_CONV, s)
    hp, hn = _halo_specs(ts, CV_COLS, s, BF16_SUBLANES)
    return pl.pallas_call(
        _conv_kernel,
        grid=(bsz, s // ts),
        in_specs=[pl.BlockSpec((None, ts, CV_COLS), lambda b_, i: (b_, i, 0)), hp, hn,
                  pl.BlockSpec((3, CV_WIDTH), lambda b_, i: (0, 0)),
                  pl.BlockSpec((1, CV_WIDTH), lambda b_, i: (0, 0))],
        out_specs=pl.BlockSpec((None, ts, CV_WIDTH), lambda b_, i: (b_, i, 0)),
        out_shape=jax.ShapeDtypeStruct((bsz, s, CV_WIDTH), BF16),
        compiler_params=_cparams("parallel", "arbitrary"),
    )(cv, cv, cv, w, b)


def _prep_kernel(c_ref, hp_ref, hn_ref, mup_ref, mun_ref, w0_ref, w2_ref, a0_ref, a2_ref, g2_ref,
                 kk_w_ref, ka_ref, rk_ref, j_ref,
                 r_ref, v_ref, kk_ref, g_ref, bonus_ref, lw_ref, kd_ref, kka_ref):
    i = pl.program_id(1)
    c = c_ref[...]
    prev, nxt = _shift_rows(c, hp_ref[...], hn_ref[...], i == 0, i == pl.num_programs(1) - 1)
    z = c + mup_ref[...] * (prev - c) + mun_ref[...] * (nxt - c)
    w = RW_WIDTH
    r, k, v = z[:, 0:w], z[:, w:2 * w], z[:, 2 * w:3 * w]
    wd = z[:, 3 * w:3 * w + RW_LORA]
    ad = z[:, 3 * w + RW_LORA:3 * w + 2 * RW_LORA]
    gd = z[:, 3 * w + 2 * RW_LORA:3 * w + 3 * RW_LORA]
    j = j_ref[...]
    w_raw = w0_ref[...] + _dot(jnp.tanh(wd).astype(BF16), w2_ref[...])
    lw = -math.exp(-0.5) * _sigmoid(w_raw)
    a = _sigmoid(a0_ref[...] + _dot(ad.astype(BF16), a2_ref[...]))
    g_ref[...] = _dot(_sigmoid(gd).astype(BF16), g2_ref[...])
    kkr = k * kk_w_ref[...]
    kk = kkr / jnp.maximum(jnp.sqrt(_segsum(kkr * kkr, j)), 1e-12)
    r_ref[...] = r
    v_ref[...] = v
    kk_ref[...] = kk
    ksum = jnp.zeros_like(k)
    for d in range(2):
        a_d = a[:, d * w:(d + 1) * w]
        k_d = k * (1.0 + (a_d - 1.0) * ka_ref[...])
        ksum = ksum + k_d
        lw_ref[d] = lw[:, d * w:(d + 1) * w]
        kd_ref[d] = k_d
        kka_ref[d] = kk * a_d
    bonus_ref[...] = _segsum(r * (0.5 * ksum) * rk_ref[...], j) * v


def _prep(rw, mup, mun, w0, w2bd, a0, a2bd, g2, k_k, k_a, r_k, jones):
    bsz, s, _ = rw.shape
    ts = min(TS_PREP, s)
    hp, hn = _halo_specs(ts, RW_COLS, s, SUBLANES)
    tok = pl.BlockSpec((None, ts, RW_WIDTH), lambda b_, i: (b_, i, 0))
    tok2 = pl.BlockSpec((2, None, ts, RW_WIDTH), lambda b_, i: (0, b_, i, 0))
    cst = lambda a: pl.BlockSpec(a.shape, lambda b_, i: (0,) * a.ndim)
    consts = (mup, mun, w0, w2bd, a0, a2bd, g2, k_k, k_a, r_k, jones)
    one = jax.ShapeDtypeStruct((bsz, s, RW_WIDTH), F32)
    two = jax.ShapeDtypeStruct((2, bsz, s, RW_WIDTH), F32)
    return pl.pallas_call(
        _prep_kernel,
        grid=(bsz, s // ts),
        in_specs=[pl.BlockSpec((None, ts, RW_COLS), lambda b_, i: (b_, i, 0)), hp, hn] + [cst(a) for a in consts],
        out_specs=[tok] * 5 + [tok2] * 3,
        out_shape=[one] * 5 + [two] * 3,
        compiler_params=_cparams("parallel", "arbitrary"),
    )(rw, rw, rw, *consts)


def _scan_masks():
    c = CHUNK
    t = np.arange(c)
    lower = (t[None, :] <= t[:, None]).astype(np.float32)
    strict_l = (t[None, :] < t[:, None]).astype(np.float32)
    ones = np.ones((c, c), np.float32)
    tri = np.stack([np.concatenate([lower, ones], 0), np.concatenate([lower.T, ones], 0)])
    strict = np.stack([np.tile(strict_l, (1, 2)), np.tile(strict_l.T, (1, 2))])
    incl = np.stack([np.tile(lower, (1, 2)), np.tile(lower.T, (1, 2))])
    return jnp.asarray(tri, BF16), jnp.asarray(strict, F32), jnp.asarray(incl, F32)


def _scan_kernel(rf_ref, vf_ref, kkf_ref, rb_ref, vb_ref, kkb_ref,
                 lwf_ref, kdf_ref, kkaf_ref, lwb_ref, kdb_ref, kkab_ref,
                 tri_ref, strict_ref, incl_ref, yf_ref, yb_ref, st_ref,
                 kq_ref, rq_ref, kdec_ref, bdec_ref, eend_ref, avk_ref, ark_ref, arb_ref, mhi_ref, mlo_ref, tinv_ref):
    c = CHUNK
    n_chunks = rf_ref.shape[0] // c
    n_pairs = RW_WIDTH // LANES

    @pl.when(pl.program_id(1) == 0)
    def _():
        st_ref[...] = jnp.zeros_like(st_ref)

    lane = lax.broadcasted_iota(jnp.int32, (c, LANES), 1)
    row = lax.broadcasted_iota(jnp.int32, (c, LANES), 0)
    head0 = lane < HEAD
    eye = jnp.where((lane & (HEAD - 1)) == row, 1.0, 0.0).astype(F32)
    ri = lax.broadcasted_iota(jnp.int32, (LANES, LANES), 0)
    ci = lax.broadcasted_iota(jnp.int32, (LANES, LANES), 1)
    same_head = jnp.where((ri < HEAD) == (ci < HEAD), 1.0, 0.0).astype(F32)

    def stack(x):
        return jnp.concatenate([jnp.where(head0, x, 0.0), jnp.where(head0, 0.0, x)], axis=0)

    def stack_b(x):
        return stack(x).astype(BF16)

    dir_refs = ((rf_ref, vf_ref, kkf_ref, lwf_ref, kdf_ref, kkaf_ref, yf_ref),
                (rb_ref, vb_ref, kkb_ref, lwb_ref, kdb_ref, kkab_ref, yb_ref))

    def local_pass(blk, carry):
        ch = []
        for d, (r_ref, v_ref, kk_ref, lw_ref, kd_ref, kka_ref, y_ref) in enumerate(dir_refs):
            for sub in range(PASS_CHUNKS):
                step = blk * PASS_CHUNKS + sub
                cidx = step if d == 0 else n_chunks - 1 - step
                rows = pl.ds(pl.multiple_of(cidx * c, c), c)
                lw = lw_ref[rows, :]
                hi = lw.astype(BF16)
                rem = lw - hi.astype(F32)
                mid = rem.astype(BF16)
                lo = (rem - mid.astype(F32)).astype(BF16)
                tri = tri_ref[d]
                cum = _dot(tri, hi) + _dot(tri, mid) + _dot(tri, lo)
                lp = cum[0:c]
                e_end = jnp.exp(cum[c:c + SUBLANES])
                e_n = jnp.exp(-lp)
                rq = (r_ref[rows, :] * jnp.exp(lp)).astype(BF16)
                kq = (kk_ref[rows, :] * jnp.exp(lp - lw)).astype(BF16)
                ks = kd_ref[rows, :] * e_n
                bs = kka_ref[rows, :] * e_n
                kq_ref[d, step] = kq
                rq_ref[d, step] = rq
                kdec_ref[d, step] = (ks * e_end[0:1]).astype(BF16)
                bdec_ref[d, step] = (bs * e_end[0:1]).astype(BF16)
                eend_ref[d, step] = e_end
                for p in range(n_pairs):
                    sl = slice(p * LANES, (p + 1) * LANES)
                    ch.append(dict(d=d, step=step, sl=sl, kq=kq[:, sl], rq=rq[:, sl], ks=ks[:, sl], bs=bs[:, sl]))
        n = range(len(ch))
        strict = [strict_ref[x["d"]] for x in ch]
        incl = [incl_ref[x["d"]] for x in ch]
        lq = [jnp.concatenate([x["kq"], x["rq"]], axis=0) for x in ch]
        s_ks = [stack_b(x["ks"]) for x in ch]
        s_bs = [stack_b(x["bs"]) for x in ch]
        gk = [_dot_nt(lq[i], s_ks[i]) for i in n]
        gb = [_dot_nt(lq[i], s_bs[i]) for i in n]
        m = [-(gb[i][0:c] * strict[i]) for i in n]
        for i, x in enumerate(ch):
            avk_ref[x["d"], x["step"], :, x["sl"]] = (gk[i][0:c] * strict[i]).astype(BF16)
            ark_ref[x["d"], x["step"], :, x["sl"]] = (gk[i][c:2 * c] * incl[i]).astype(BF16)
            arb_ref[x["d"], x["step"], :, x["sl"]] = (gb[i][c:2 * c] * incl[i]).astype(BF16)
            m_hi, m_lo = _split2(m[i])
            mhi_ref[x["d"], x["step"], :, x["sl"]] = m_hi
            mlo_ref[x["d"], x["step"], :, x["sl"]] = m_lo
        t_inv = [eye + x for x in m]
        pw = [_dot(x.astype(BF16), stack_b(x)) for x in m]
        n_levels = int(math.log2(c)) - 1
        for lvl in range(n_levels):
            pw_bd = [stack_b(x) for x in pw]
            if lvl + 1 < n_levels:
                both = [_dot(jnp.concatenate([t_inv[i].astype(BF16), pw[i].astype(BF16)], axis=0), pw_bd[i])
                        for i in n]
                t_inv = [t_inv[i] + both[i][0:c] for i in n]
                pw = [both[i][c:2 * c] for i in n]
            else:
                t_inv = [t_inv[i] + _dot(t_inv[i].astype(BF16), pw_bd[i]) for i in n]
        for i, x in enumerate(ch):
            tinv_ref[x["d"], x["step"], :, x["sl"]] = t_inv[i].astype(BF16)
        return carry

    def state_pass(step, carry):
        ch = []
        for d, (r_ref, v_ref, kk_ref, lw_ref, kd_ref, kka_ref, y_ref) in enumerate(dir_refs):
            cidx = step if d == 0 else n_chunks - 1 - step
            rows = pl.ds(pl.multiple_of(cidx * c, c), c)
            for p in range(n_pairs):
                ch.append(dict(d=d, p=p, rows=rows, sl=slice(p * LANES, (p + 1) * LANES), y_ref=y_ref, v_ref=v_ref))
        n = range(len(ch))
        get = lambda ref: [ref[x["d"], step, :, x["sl"]] for x in ch]
        v = [x["v_ref"][x["rows"], x["sl"]] for x in ch]
        s_vb = [stack_b(x) for x in v]
        st = [st_ref[x["d"], x["p"]] for x in ch]
        stb = [x.astype(BF16) for x in st]
        kq_b, rq_b, a_vk, a_rk, a_rb, t_b, m_hi, m_lo = (get(r) for r in (
            kq_ref, rq_ref, avk_ref, ark_ref, arb_ref, tinv_ref, mhi_ref, mlo_ref))
        from_state = [_dot_nt(jnp.concatenate([kq_b[i], rq_b[i]], axis=0), stb[i]) for i in n]
        from_v = [_dot(jnp.concatenate([a_vk[i], a_rk[i]], axis=0), s_vb[i]) for i in n]
        rhs = [from_state[i][0:c] + from_v[i][0:c] for i in n]
        u0 = [-_dot(t_b[i], stack_b(rhs[i])) for i in n]
        u_sp = [_split2(x) for x in u0]
        mu_a = [_dot(jnp.concatenate([m_hi[i], m_lo[i]], axis=0), stack(u_sp[i][0].astype(F32)).astype(BF16))
                for i in n]
        mu_b = [_dot(m_hi[i], stack(u_sp[i][1].astype(F32)).astype(BF16)) for i in n]
        res = [mu_a[i][0:c] + mu_a[i][c:2 * c] + mu_b[i] - rhs[i] - u0[i] for i in n]
        u = [u0[i] + _dot(t_b[i], stack_b(res[i])) for i in n]
        y = [from_state[i][c:2 * c] + from_v[i][c:2 * c] + _dot(a_rb[i], stack_b(u[i])) for i in n]
        vu_t = [jnp.concatenate([v[i], u[i]], axis=0).T.astype(BF16) for i in n]
        kb = [jnp.concatenate([kdec_ref[x["d"], step, :, x["sl"]], bdec_ref[x["d"], step, :, x["sl"]]], axis=0)
              for x in ch]
        upd = [_dot(vu_t[i], kb[i]) * same_head for i in n]
        for i, x in enumerate(ch):
            x["y_ref"][x["rows"], x["sl"]] = y[i]
            st_ref[x["d"], x["p"]] = st[i] * eend_ref[x["d"], step, 0:1, x["sl"]] + upd[i]
        return carry

    lax.fori_loop(0, n_chunks // PASS_CHUNKS, local_pass, 0)
    lax.fori_loop(0, n_chunks, state_pass, 0)


def _scan(r, v, kk, lw, kd, kka, masks):
    bsz, s, w = r.shape
    tb = min(TB_SCAN, s)
    nblk = s // tb
    fwd = pl.BlockSpec((None, tb, w), lambda b_, j: (b_, j, 0))
    bwd = pl.BlockSpec((None, tb, w), lambda b_, j: (b_, nblk - 1 - j, 0))
    fwd2 = pl.BlockSpec((None, None, tb, w), lambda b_, j: (0, b_, j, 0))
    bwd2 = pl.BlockSpec((None, None, tb, w), lambda b_, j: (1, b_, nblk - 1 - j, 0))
    out = jax.ShapeDtypeStruct((bsz, s, w), F32)
    per_chunk = lambda dt: pltpu.VMEM((2, tb // CHUNK, CHUNK, w), dt)
    return pl.pallas_call(
        _scan_kernel,
        grid=(bsz, nblk),
        in_specs=[fwd, fwd, fwd, bwd, bwd, bwd, fwd2, fwd2, fwd2, bwd2, bwd2, bwd2]
        + [_resident(a.shape) for a in masks],
        out_specs=[fwd, bwd],
        out_shape=[out, out],
        scratch_shapes=[pltpu.VMEM((2, w // LANES, LANES, LANES), F32), per_chunk(BF16), per_chunk(BF16),
                        per_chunk(BF16), per_chunk(BF16), pltpu.VMEM((2, tb // CHUNK, SUBLANES, w), F32)]
        + [per_chunk(BF16)] * 6,
        compiler_params=_cparams("parallel", "arbitrary"),
    )(r, v, kk, r, v, kk, lw, kd, kka, lw, kd, kka, *masks)


def _rope_partner(x):
    n = x.shape[1]
    lane = lax.broadcasted_iota(jnp.int32, x.shape, 1)
    return jnp.where((lane & 31) < 16, pltpu.roll(x, n - 16, 1), pltpu.roll(x, 16, 1))


def _attn_kernel(q_ref, k_ref, v_ref, cq_ref, sq_ref, ck_ref, sk_ref, qg_ref, kg_ref, jq_ref, jk_ref,
                 o_ref, kd_ref, vd_ref):
    s_len = k_ref.shape[0]
    tq = q_ref.shape[0]
    tk = min(TK_PREP, s_len)

    @pl.when(pl.program_id(1) == 0)
    def _():
        def body(i, carry):
            rows = pl.ds(pl.multiple_of(i * tk, tk), tk)
            k = k_ref[rows, :].astype(F32)
            kn = k * lax.rsqrt(_segsum(k * k, jk_ref[...]) * (1.0 / HEAD) + NORM_EPS) * kg_ref[...]
            kr = kn * ck_ref[rows, :] + _rope_partner(kn) * sk_ref[rows, :]
            v = v_ref[rows, :].astype(F32)
            low = lax.broadcasted_iota(jnp.int32, kr.shape, 1) < HEAD
            ksw = pltpu.roll(kr, HEAD, 1)
            kd_ref[0, rows, :] = jnp.where(low, kr, ksw).astype(BF16)
            kd_ref[1, rows, :] = jnp.where(low, ksw, kr).astype(BF16)
            vd_ref[0, rows, :] = jnp.where(low, v, 1.0).astype(BF16)
            vd_ref[1, rows, :] = jnp.where(low, pltpu.roll(v, HEAD, 1), 1.0).astype(BF16)
            return carry
        lax.fori_loop(0, s_len // tk, body, 0)

    q = q_ref[...].astype(F32)
    qn = q * lax.rsqrt(_segsum(q * q, jq_ref[...]) * (1.0 / HEAD) + NORM_EPS) * qg_ref[...]
    qr = qn * cq_ref[...] + _rope_partner(qn) * sq_ref[...]
    low = lax.broadcasted_iota(jnp.int32, (tq, LANES), 1) < HEAD
    pairs_per_kv = AT_Q_HEADS // AT_KV_HEADS // 2
    tkc = min(TK_ATTN, s_len)
    n_kc = s_len // tkc
    qs = []
    for g in range(AT_KV_HEADS):
        parts = []
        for jj in range(pairs_per_kv):
            qp = qr[:, (g * pairs_per_kv + jj) * LANES:(g * pairs_per_kv + jj + 1) * LANES]
            parts += [jnp.where(low, qp, 0.0), jnp.where(low, 0.0, qp)]
        qs.append(jnp.concatenate(parts, axis=0).astype(BF16))
    streams = [(g, kc) for g in range(AT_KV_HEADS) for kc in range(n_kc)]
    scores = lambda g, kc: _dot_nt(qs[g], kd_ref[g, kc * tkc:(kc + 1) * tkc, :])
    partial, s_next = {}, scores(*streams[0])
    for i, (g, kc) in enumerate(streams):
        s = s_next
        if i + 1 < len(streams):
            s_next = scores(*streams[i + 1])
        m = jnp.max(s, axis=-1, keepdims=True)
        p = jnp.exp2(s - m).astype(BF16)
        partial[g, kc] = (m, _dot(p, vd_ref[g, kc * tkc:(kc + 1) * tkc, :]))
    for g in range(AT_KV_HEADS):
        m_all = partial[g, 0][0]
        for kc in range(1, n_kc):
            m_all = jnp.maximum(m_all, partial[g, kc][0])
        o = sum(jnp.exp2(partial[g, kc][0] - m_all) * partial[g, kc][1] for kc in range(n_kc))
        o = o / pltpu.roll(o, HEAD, 1)
        for jj in range(pairs_per_kv):
            even = o[(2 * jj) * tq:(2 * jj + 1) * tq]
            odd = pltpu.roll(o[(2 * jj + 1) * tq:(2 * jj + 2) * tq], HEAD, 1)
            pair = jnp.where(low, even, odd)
            col = (g * pairs_per_kv + jj) * LANES
            o_ref[:, col:col + LANES] = pair.astype(BF16)


def _attn(at, tabs, q_g, k_g, jones):
    bsz, s, _ = at.shape
    tq = min(TQ_ATTN, s)
    cq, sq, ck, sk = tabs
    qw = AT_Q_HEADS * HEAD
    kvw = AT_KV_HEADS * HEAD
    cst = lambda a: pl.BlockSpec(a.shape, lambda b_, i: (0,) * a.ndim)
    jk = jones[0:kvw, 0:kvw]
    return pl.pallas_call(
        _attn_kernel,
        grid=(bsz, s // tq),
        in_specs=[pl.BlockSpec((None, tq, qw), lambda b_, i: (b_, i, 0)),
                  pl.BlockSpec((None, s, kvw), lambda b_, i: (b_, 0, qw // kvw)),
                  pl.BlockSpec((None, s, kvw), lambda b_, i: (b_, 0, qw // kvw + 1)),
                  pl.BlockSpec((tq, qw), lambda b_, i: (i, 0)),
                  pl.BlockSpec((tq, qw), lambda b_, i: (i, 0)),
                  cst(ck), cst(sk), cst(q_g), cst(k_g), cst(jones), cst(jk)],
        out_specs=pl.BlockSpec((None, tq, qw), lambda b_, i: (b_, i, 0)),
        out_shape=jax.ShapeDtypeStruct((bsz, s, qw), BF16),
        scratch_shapes=[pltpu.VMEM((AT_KV_HEADS, s, LANES), BF16), pltpu.VMEM((AT_KV_HEADS, s, LANES), BF16)],
        compiler_params=_cparams("parallel", "arbitrary"),
    )(at, at, at, cq, sq, ck, sk, q_g, k_g, jones, jk)


def _rope_tables(s):
    rows = s // GRID_W
    pos_r = jnp.repeat(jnp.arange(rows), GRID_W).astype(F32)
    pos_c = jnp.tile(jnp.arange(GRID_W), rows).astype(F32)
    half = HEAD // 2
    inv = ROPE_THETA ** (-jnp.arange(0, half, 2, dtype=F32) / half)
    ang_r = pos_r[:, None] * inv
    ang_c = pos_c[:, None] * inv
    cos = jnp.concatenate([jnp.cos(ang_r)] * 2 + [jnp.cos(ang_c)] * 2, axis=-1)
    sin = jnp.concatenate([-jnp.sin(ang_r), jnp.sin(ang_r), -jnp.sin(ang_c), jnp.sin(ang_c)], axis=-1)
    scale = HEAD ** -0.5 * math.log2(math.e)
    return (jnp.tile(cos, (1, AT_Q_HEADS)) * scale, jnp.tile(sin, (1, AT_Q_HEADS)) * scale,
            jnp.tile(cos, (1, AT_KV_HEADS)), jnp.tile(sin, (1, AT_KV_HEADS)))


def _merge_kernel(y0_ref, y1_ref, g_ref, bonus_ref, lng_ref, lnb_ref, j_ref, ycv_ref, yat_ref, sg_ref, x_ref,
                  wb_ref, wo_ref, o_ref):
    j = j_ref[...]
    y = y0_ref[...] + y1_ref[...]
    mu = _segsum(y, j) * (1.0 / HEAD)
    dev = y - mu
    var = _segsum(dev * dev, j) * (1.0 / HEAD)
    yn = dev * lax.rsqrt(var + RW_LN_EPS) * lng_ref[...] + lnb_ref[...]
    y_rw = ((yn + bonus_ref[...]) * g_ref[...]).astype(BF16)
    dm = D_MODEL
    merged = (sg_ref[:, 0:dm].astype(F32) * _dot(y_rw, wb_ref[0])
              + sg_ref[:, dm:2 * dm].astype(F32) * _dot(ycv_ref[...], wb_ref[1])
              + sg_ref[:, 2 * dm:3 * dm].astype(F32) * _dot(yat_ref[...], wb_ref[2]))
    o_ref[...] = x_ref[...] + _dot(merged.astype(BF16), wo_ref[...])


def _merge(yf, yb, g, bonus, ln_g, ln_b, jones, y_cv, y_at, sg, x, w_branch, w_out):
    t = x.shape[0]
    tm = TM_MERGE
    row = lambda n: pl.BlockSpec((tm, n), lambda i: (i, 0))
    return pl.pallas_call(
        _merge_kernel,
        grid=(t // tm,),
        in_specs=[row(RW_WIDTH), row(RW_WIDTH), row(RW_WIDTH), row(RW_WIDTH),
                  _resident(ln_g.shape), _resident(ln_b.shape), _resident(jones.shape),
                  row(CV_WIDTH), row(AT_Q_HEADS * HEAD), row(GATE_COLS), row(D_MODEL),
                  _resident(w_branch.shape), _resident(w_out.shape)],
        out_specs=row(D_MODEL),
        out_shape=jax.ShapeDtypeStruct((t, D_MODEL), F32),
        compiler_params=_cparams("parallel"),
    )(yf, yb, g, bonus, ln_g, ln_b, jones, y_cv, y_at, sg, x, w_branch, w_out)


def _mlp_kernel(x_ref, g_ref, w1_ref, w2_ref, o_ref):
    x = x_ref[...]
    h = (x * lax.rsqrt(jnp.mean(x * x, axis=-1, keepdims=True) + NORM_EPS) * g_ref[...]).astype(BF16)
    acc = x
    for c in range(D_FF // FF_CHUNK):
        cols = slice(c * FF_CHUNK, (c + 1) * FF_CHUNK)
        u = jnp.maximum(_dot(h, w1_ref[:, cols]), 0.0)
        acc = acc + _dot((u * u).astype(BF16), w2_ref[cols, :])
    o_ref[...] = acc


def _mlp(x, g, w1, w2):
    t = x.shape[0]
    tm = TM_MLP
    row = pl.BlockSpec((tm, D_MODEL), lambda i: (i, 0))
    return pl.pallas_call(
        _mlp_kernel,
        grid=(t // tm,),
        in_specs=[row, _resident(g.shape), _resident(w1.shape), _resident(w2.shape)],
        out_specs=row,
        out_shape=jax.ShapeDtypeStruct((t, D_MODEL), F32),
        compiler_params=_cparams("parallel"),
    )(x, g, w1, w2)


def _block_diag2(w):
    z = jnp.zeros_like(w[0])
    return jnp.concatenate([jnp.concatenate([w[0], z], axis=1), jnp.concatenate([z, w[1]], axis=1)], axis=0)


def kernel(x, norm1_g, w_in, rw_mu_prev, rw_mu_next, rw_w0, rw_w2, rw_a0, rw_a2, rw_g2, rw_k_k, rw_k_a, rw_r_k,
           rw_ln_g, rw_ln_b, cv_w, cv_b, at_q_g, at_k_g, w_branch, w_out, norm2_g, w_mlp1, w_mlp2):
    bsz, s, dm = x.shape
    assert dm == D_MODEL and s % GRID_W == 0
    for tile in (TS_PREP, TB_SCAN, TS_CONV, TQ_ATTN, TK_PREP):
        assert s % min(tile, s) == 0 and min(tile, s) % CHUNK == 0
    t = bsz * s
    tabs = _rope_tables(s)
    masks = _scan_masks()
    jones = jnp.asarray(np.kron(np.eye(RW_WIDTH // HEAD, dtype=np.float32), np.ones((HEAD, HEAD), np.float32)), BF16)
    row = lambda a: a.reshape(1, -1)
    xf = x.reshape(t, dm)
    for l in range(DEPTH):
        rw, cv, at, sg = _inproj(xf, row(norm1_g[l]), w_in[l].astype(BF16))
        r, v, kk, g, bonus, lw, kd, kka = _prep(
            rw.reshape(bsz, s, RW_COLS), row(rw_mu_prev[l]), row(rw_mu_next[l]),
            row(rw_w0[l]), _block_diag2(rw_w2[l]).astype(BF16), row(rw_a0[l]), _block_diag2(rw_a2[l]).astype(BF16),
            rw_g2[l].astype(BF16), row(rw_k_k[l]), row(rw_k_a[l]), row(rw_r_k[l]), jones)
        yf, yb = _scan(r, v, kk, lw, kd, kka, masks)
        y_cv = _conv(cv.reshape(bsz, s, CV_COLS), cv_w[l], row(cv_b[l]))
        y_at = _attn(at.reshape(bsz, s, AT_COLS), tabs, row(jnp.tile(at_q_g[l], AT_Q_HEADS)),
                     row(jnp.tile(at_k_g[l], AT_KV_HEADS)), jones)
        xf = _merge(yf.reshape(t, RW_WIDTH), yb.reshape(t, RW_WIDTH), g.reshape(t, RW_WIDTH),
                    bonus.reshape(t, RW_WIDTH),
                    row(rw_ln_g[l]), row(rw_ln_b[l]), jones, y_cv.reshape(t, CV_WIDTH),
                    y_at.reshape(t, AT_Q_HEADS * HEAD), sg, xf, w_branch[l].astype(BF16), w_out[l].astype(BF16))
        xf = _mlp(xf, row(norm2_g[l]), w_mlp1[l].astype(BF16), w_mlp2[l].astype(BF16))
    return xf.reshape(bsz, s, dm)
```

```python
import functools
import math

import numpy as np
import jax
import jax.numpy as jnp
from jax import lax
from jax.experimental import pallas as pl
from jax.experimental.pallas import tpu as pltpu

F32 = jnp.float32
BF16 = jnp.bfloat16

D_MODEL = 1024
DEPTH = 4
GRID_W = 64
NORM_EPS = 1e-6
HEAD = 64
RW_WIDTH = 512
RW_LORA = 128
RW_LN_EPS = 64e-5
RW_COLS = 3 * RW_WIDTH + 3 * RW_LORA
CV_WIDTH = 512
CV_COLS = 3 * CV_WIDTH
AT_Q_HEADS = 8
AT_KV_HEADS = 2
AT_COLS = (AT_Q_HEADS + 2 * AT_KV_HEADS) * HEAD
GATE_COLS = 3 * D_MODEL
ROPE_THETA = 10000.0
D_FF = 4 * D_MODEL

LANES = 128
SUBLANES = 8
BF16_SUBLANES = 16
VMEM_LIMIT_BYTES = 56 * 1024 * 1024
CHUNK = 64
TM_INPROJ = 256
TS_PREP = 256
TB_SCAN = 256
PASS_CHUNKS = 2
TS_CONV = 512
TQ_ATTN = 128
TK_ATTN = 2048
TK_PREP = 512
TM_MERGE = 512
TM_MLP = 512
FF_CHUNK = 1024


def _cparams(*sem):
    return pltpu.CompilerParams(dimension_semantics=sem, vmem_limit_bytes=VMEM_LIMIT_BYTES)


def _dot(a, b):
    return jnp.dot(a, b, preferred_element_type=F32)


def _dot_nt(a, b):
    return lax.dot_general(a, b, (((1,), (1,)), ((), ())), preferred_element_type=F32)


def _split2(x):
    hi = x.astype(BF16)
    lo = (x - hi.astype(F32)).astype(BF16)
    return hi, lo


def _segsum(x, j):
    hi, lo = _split2(x)
    return _dot(hi, j) + _dot(lo, j)


def _sigmoid(x):
    return 1.0 / (1.0 + jnp.exp(-x))


def _resident(shape):
    nd = len(shape)
    return pl.BlockSpec(shape, lambda *_: (0,) * nd, pipeline_mode=pl.Buffered(1))


def _shift_rows(c, halo_prev, halo_next, first, last):
    n = c.shape[0]
    row = lax.broadcasted_iota(jnp.int32, c.shape, 0)
    lane_row = lax.broadcasted_iota(jnp.int32, (1, c.shape[1]), 1)
    is_first = (lane_row * 0 + first.astype(jnp.int32)) > 0
    is_last = (lane_row * 0 + last.astype(jnp.int32)) > 0
    prev_row = jnp.where(is_first, 0.0, halo_prev[halo_prev.shape[0] - 1:, :])
    next_row = jnp.where(is_last, 0.0, halo_next[0:1, :])
    prev = jnp.where(row == 0, prev_row, pltpu.roll(c, 1, 0))
    nxt = jnp.where(row == n - 1, next_row, pltpu.roll(c, n - 1, 0))
    return prev, nxt


def _halo_specs(ts, width, n_seq_rows, rows):
    r = ts // rows
    last_blk = n_seq_rows // rows - 1
    prev = pl.BlockSpec((None, rows, width), lambda b, i: (b, jnp.maximum(i * r - 1, 0), 0))
    nxt = pl.BlockSpec((None, rows, width), lambda b, i: (b, jnp.minimum((i + 1) * r, last_blk), 0))
    return prev, nxt


def _inproj_kernel(x_ref, g_ref, w_ref, rw_ref, cv_ref, at_ref, sg_ref):
    x = x_ref[...]
    h = (x * lax.rsqrt(jnp.mean(x * x, axis=-1, keepdims=True) + NORM_EPS) * g_ref[...]).astype(BF16)
    o0, o1, o2 = RW_COLS, RW_COLS + CV_COLS, RW_COLS + CV_COLS + AT_COLS
    rw_ref[...] = _dot(h, w_ref[:, 0:o0])
    cv_ref[...] = _dot(h, w_ref[:, o0:o1]).astype(BF16)
    at_ref[...] = _dot(h, w_ref[:, o1:o2]).astype(BF16)
    sg_ref[...] = _sigmoid(_dot(h, w_ref[:, o2:o2 + GATE_COLS])).astype(BF16)


def _inproj(x, g, w):
    t = x.shape[0]
    tm = TM_INPROJ
    row = lambda n: pl.BlockSpec((tm, n), lambda i: (i, 0))
    return pl.pallas_call(
        _inproj_kernel,
        grid=(t // tm,),
        in_specs=[row(D_MODEL), _resident((1, D_MODEL)), _resident(w.shape)],
        out_specs=[row(RW_COLS), row(CV_COLS), row(AT_COLS), row(GATE_COLS)],
        out_shape=[jax.ShapeDtypeStruct((t, n), dt)
                   for n, dt in ((RW_COLS, F32), (CV_COLS, BF16), (AT_COLS, BF16), (GATE_COLS, BF16))],
        compiler_params=_cparams("parallel"),
    )(x, g, w)


def _conv_kernel(c_ref, hp_ref, hn_ref, w_ref, b_ref, o_ref):
    i = pl.program_id(1)
    c = c_ref[...].astype(F32)
    w = CV_WIDTH
    z = c[:, w:2 * w] * c[:, 2 * w:3 * w]
    hp, hn = hp_ref[...].astype(F32), hn_ref[...].astype(F32)
    zp, zn = _shift_rows(z, hp[:, w:2 * w] * hp[:, 2 * w:3 * w], hn[:, w:2 * w] * hn[:, 2 * w:3 * w],
                         i == 0, i == pl.num_programs(1) - 1)
    conv = b_ref[...] + w_ref[0:1, :] * zp + w_ref[1:2, :] * z + w_ref[2:3, :] * zn
    o_ref[...] = (c[:, 0:w] * conv).astype(BF16)


def _conv(cv, w, b):
    bsz, s, _ = cv.shape
    ts = min(TS_CONV, s)
    hp, hn = _halo_specs(ts, CV_COLS, s, BF16_SUBLANES)
    return pl.pallas_call(
        _conv_kernel,
        grid=(bsz, s // ts),
        in_specs=[pl.BlockSpec((None, ts, CV_COLS), lambda b_, i: (b_, i, 0)), hp, hn,
                  pl.BlockSpec((3, CV_WIDTH), lambda b_, i: (0, 0)),
                  pl.BlockSpec((1, CV_WIDTH), lambda b_, i: (0, 0))],
        out_specs=pl.BlockSpec((None, ts, CV_WIDTH), lambda b_, i: (b_, i, 0)),
        out_shape=jax.ShapeDtypeStruct((bsz, s, CV_WIDTH), BF16),
        compiler_params=_cparams("parallel", "arbitrary"),
    )(cv, cv, cv, w, b)


def _prep_kernel(c_ref, hp_ref, hn_ref, mup_ref, mun_ref, w0_ref, w2_ref, a0_ref, a2_ref, g2_ref,
                 kk_w_ref, ka_ref, rk_ref, j_ref,
                 r_ref, v_ref, kk_ref, g_ref, bonus_ref, lw_ref, kd_ref, kka_ref):
    i = pl.program_id(1)
    c = c_ref[...]
    prev, nxt = _shift_rows(c, hp_ref[...], hn_ref[...], i == 0, i == pl.num_programs(1) - 1)
    z = c + mup_ref[...] * (prev - c) + mun_ref[...] * (nxt - c)
    w = RW_WIDTH
    r, k, v = z[:, 0:w], z[:, w:2 * w], z[:, 2 * w:3 * w]
    wd = z[:, 3 * w:3 * w + RW_LORA]
    ad = z[:, 3 * w + RW_LORA:3 * w + 2 * RW_LORA]
    gd = z[:, 3 * w + 2 * RW_LORA:3 * w + 3 * RW_LORA]
    j = j_ref[...]
    w_raw = w0_ref[...] + _dot(jnp.tanh(wd).astype(BF16), w2_ref[...])
    lw = -math.exp(-0.5) * _sigmoid(w_raw)
    a = _sigmoid(a0_ref[...] + _dot(ad.astype(BF16), a2_ref[...]))
    g_ref[...] = _dot(_sigmoid(gd).astype(BF16), g2_ref[...]).astype(BF16)
    kkr = k * kk_w_ref[...]
    kk = kkr / jnp.maximum(jnp.sqrt(_segsum(kkr * kkr, j)), 1e-12)
    r_ref[...] = r.astype(BF16)
    v_ref[...] = v.astype(BF16)
    kk_ref[...] = kk.astype(BF16)
    ksum = jnp.zeros_like(k)
    for d in range(2):
        a_d = a[:, d * w:(d + 1) * w]
        k_d = k * (1.0 + (a_d - 1.0) * ka_ref[...])
        ksum = ksum + k_d
        lw_ref[d] = lw[:, d * w:(d + 1) * w]
        kd_ref[d] = k_d.astype(BF16)
        kka_ref[d] = (kk * a_d).astype(BF16)
    bonus_ref[...] = (_segsum(r * (0.5 * ksum) * rk_ref[...], j) * v).astype(BF16)


def _prep(rw, mup, mun, w0, w2bd, a0, a2bd, g2, k_k, k_a, r_k, jones):
    bsz, s, _ = rw.shape
    ts = min(TS_PREP, s)
    hp, hn = _halo_specs(ts, RW_COLS, s, SUBLANES)
    tok = pl.BlockSpec((None, ts, RW_WIDTH), lambda b_, i: (b_, i, 0))
    tok2 = pl.BlockSpec((2, None, ts, RW_WIDTH), lambda b_, i: (0, b_, i, 0))
    cst = lambda a: pl.BlockSpec(a.shape, lambda b_, i: (0,) * a.ndim)
    consts = (mup, mun, w0, w2bd, a0, a2bd, g2, k_k, k_a, r_k, jones)
    one = jax.ShapeDtypeStruct((bsz, s, RW_WIDTH), BF16)
    two = lambda dt: jax.ShapeDtypeStruct((2, bsz, s, RW_WIDTH), dt)
    return pl.pallas_call(
        _prep_kernel,
        grid=(bsz, s // ts),
        in_specs=[pl.BlockSpec((None, ts, RW_COLS), lambda b_, i: (b_, i, 0)), hp, hn] + [cst(a) for a in consts],
        out_specs=[tok] * 5 + [tok2] * 3,
        out_shape=[one] * 5 + [two(F32), two(BF16), two(BF16)],
        compiler_params=_cparams("parallel", "arbitrary"),
    )(rw, rw, rw, *consts)


def _scan_masks():
    c = CHUNK
    t = np.arange(c)
    lower = (t[None, :] <= t[:, None]).astype(np.float32)
    strict_l = (t[None, :] < t[:, None]).astype(np.float32)
    ones = np.ones((c, c), np.float32)
    tri = np.stack([np.concatenate([lower, ones], 0), np.concatenate([lower.T, ones], 0)])
    strict = np.stack([np.tile(strict_l, (1, 2)), np.tile(strict_l.T, (1, 2))])
    incl = np.stack([np.tile(lower, (1, 2)), np.tile(lower.T, (1, 2))])
    return jnp.asarray(tri, BF16), jnp.asarray(strict, F32), jnp.asarray(incl, F32)


def _scan_kernel(rf_ref, vf_ref, kkf_ref, rb_ref, vb_ref, kkb_ref,
                 lwf_ref, kdf_ref, kkaf_ref, lwb_ref, kdb_ref, kkab_ref,
                 tri_ref, strict_ref, incl_ref, yf_ref, yb_ref, st_ref,
                 kq_ref, rq_ref, kdec_ref, bdec_ref, eend_ref, avk_ref, ark_ref, arb_ref, mhi_ref, mlo_ref, tinv_ref):
    c = CHUNK
    n_chunks = rf_ref.shape[0] // c
    n_pairs = RW_WIDTH // LANES

    @pl.when(pl.program_id(1) == 0)
    def _():
        st_ref[...] = jnp.zeros_like(st_ref)

    lane = lax.broadcasted_iota(jnp.int32, (c, LANES), 1)
    row = lax.broadcasted_iota(jnp.int32, (c, LANES), 0)
    head0 = lane < HEAD
    eye = jnp.where((lane & (HEAD - 1)) == row, 1.0, 0.0).astype(F32)
    ri = lax.broadcasted_iota(jnp.int32, (LANES, LANES), 0)
    ci = lax.broadcasted_iota(jnp.int32, (LANES, LANES), 1)
    same_head = jnp.where((ri < HEAD) == (ci < HEAD), 1.0, 0.0).astype(F32)

    def stack(x):
        return jnp.concatenate([jnp.where(head0, x, 0.0), jnp.where(head0, 0.0, x)], axis=0)

    def stack_b(x):
        return stack(x).astype(BF16)

    dir_refs = ((rf_ref, vf_ref, kkf_ref, lwf_ref, kdf_ref, kkaf_ref, yf_ref),
                (rb_ref, vb_ref, kkb_ref, lwb_ref, kdb_ref, kkab_ref, yb_ref))

    def local_pass(blk, carry):
        ch = []
        for d, (r_ref, v_ref, kk_ref, lw_ref, kd_ref, kka_ref, y_ref) in enumerate(dir_refs):
            for sub in range(PASS_CHUNKS):
                step = blk * PASS_CHUNKS + sub
                cidx = step if d == 0 else n_chunks - 1 - step
                rows = pl.ds(pl.multiple_of(cidx * c, c), c)
                lw = lw_ref[rows, :]
                hi = lw.astype(BF16)
                rem = lw - hi.astype(F32)
                mid = rem.astype(BF16)
                lo = (rem - mid.astype(F32)).astype(BF16)
                tri = tri_ref[d]
                cum = _dot(tri, hi) + _dot(tri, mid) + _dot(tri, lo)
                lp = cum[0:c]
                e_end = jnp.exp(cum[c:c + SUBLANES])
                e_n = jnp.exp(-lp)
                rq = (r_ref[rows, :] * jnp.exp(lp)).astype(BF16)
                kq = (kk_ref[rows, :] * jnp.exp(lp - lw)).astype(BF16)
                ks = kd_ref[rows, :] * e_n
                bs = kka_ref[rows, :] * e_n
                kq_ref[d, step] = kq
                rq_ref[d, step] = rq
                kdec_ref[d, step] = (ks * e_end[0:1]).astype(BF16)
                bdec_ref[d, step] = (bs * e_end[0:1]).astype(BF16)
                eend_ref[d, step] = e_end
                for p in range(n_pairs):
                    sl = slice(p * LANES, (p + 1) * LANES)
                    ch.append(dict(d=d, step=step, sl=sl, kq=kq[:, sl], rq=rq[:, sl], ks=ks[:, sl], bs=bs[:, sl]))
        n = range(len(ch))
        strict = [strict_ref[x["d"]] for x in ch]
        incl = [incl_ref[x["d"]] for x in ch]
        lq = [jnp.concatenate([x["kq"], x["rq"]], axis=0) for x in ch]
        s_ks = [stack_b(x["ks"]) for x in ch]
        s_bs = [stack_b(x["bs"]) for x in ch]
        gk = [_dot_nt(lq[i], s_ks[i]) for i in n]
        gb = [_dot_nt(lq[i], s_bs[i]) for i in n]
        m = [-(gb[i][0:c] * strict[i]) for i in n]
        for i, x in enumerate(ch):
            avk_ref[x["d"], x["step"], :, x["sl"]] = (gk[i][0:c] * strict[i]).astype(BF16)
            ark_ref[x["d"], x["step"], :, x["sl"]] = (gk[i][c:2 * c] * incl[i]).astype(BF16)
            arb_ref[x["d"], x["step"], :, x["sl"]] = (gb[i][c:2 * c] * incl[i]).astype(BF16)
            m_hi, m_lo = _split2(m[i])
            mhi_ref[x["d"], x["step"], :, x["sl"]] = m_hi
            mlo_ref[x["d"], x["step"], :, x["sl"]] = m_lo
        t_inv = [eye + x for x in m]
        pw = [_dot(x.astype(BF16), stack_b(x)) for x in m]
        n_levels = int(math.log2(c)) - 1
        for lvl in range(n_levels):
            pw_bd = [stack_b(x) for x in pw]
            if lvl + 1 < n_levels:
                both = [_dot(jnp.concatenate([t_inv[i].astype(BF16), pw[i].astype(BF16)], axis=0), pw_bd[i])
                        for i in n]
                t_inv = [t_inv[i] + both[i][0:c] for i in n]
                pw = [both[i][c:2 * c] for i in n]
            else:
                t_inv = [t_inv[i] + _dot(t_inv[i].astype(BF16), pw_bd[i]) for i in n]
        for i, x in enumerate(ch):
            tinv_ref[x["d"], x["step"], :, x["sl"]] = t_inv[i].astype(BF16)
        return carry

    def state_pass(step, carry):
        ch = []
        for d, (r_ref, v_ref, kk_ref, lw_ref, kd_ref, kka_ref, y_ref) in enumerate(dir_refs):
            cidx = step if d == 0 else n_chunks - 1 - step
            rows = pl.ds(pl.multiple_of(cidx * c, c), c)
            for p in range(n_pairs):
                ch.append(dict(d=d, p=p, rows=rows, sl=slice(p * LANES, (p + 1) * LANES), y_ref=y_ref, v_ref=v_ref))
        n = range(len(ch))
        get = lambda ref: [ref[x["d"], step, :, x["sl"]] for x in ch]
        v = [x["v_ref"][x["rows"], x["sl"]].astype(F32) for x in ch]
        s_vb = [stack_b(x) for x in v]
        st = [st_ref[x["d"], x["p"]] for x in ch]
        stb = [x.astype(BF16) for x in st]
        kq_b, rq_b, a_vk, a_rk, a_rb, t_b, m_hi, m_lo = (get(r) for r in (
            kq_ref, rq_ref, avk_ref, ark_ref, arb_ref, tinv_ref, mhi_ref, mlo_ref))
        from_state = [_dot_nt(jnp.concatenate([kq_b[i], rq_b[i]], axis=0), stb[i]) for i in n]
        from_v = [_dot(jnp.concatenate([a_vk[i], a_rk[i]], axis=0), s_vb[i]) for i in n]
        rhs = [from_state[i][0:c] + from_v[i][0:c] for i in n]
        u0 = [-_dot(t_b[i], stack_b(rhs[i])) for i in n]
        u_sp = [_split2(x) for x in u0]
        mu_a = [_dot(jnp.concatenate([m_hi[i], m_lo[i]], axis=0), stack(u_sp[i][0].astype(F32)).astype(BF16))
                for i in n]
        mu_b = [_dot(m_hi[i], stack(u_sp[i][1].astype(F32)).astype(BF16)) for i in n]
        res = [mu_a[i][0:c] + mu_a[i][c:2 * c] + mu_b[i] - rhs[i] - u0[i] for i in n]
        u = [u0[i] + _dot(t_b[i], stack_b(res[i])) for i in n]
        y = [from_state[i][c:2 * c] + from_v[i][c:2 * c] + _dot(a_rb[i], stack_b(u[i])) for i in n]
        vu_t = [jnp.concatenate([v[i], u[i]], axis=0).T.astype(BF16) for i in n]
        kb = [jnp.concatenate([kdec_ref[x["d"], step, :, x["sl"]], bdec_ref[x["d"], step, :, x["sl"]]], axis=0)
              for x in ch]
        upd = [_dot(vu_t[i], kb[i]) * same_head for i in n]
        for i, x in enumerate(ch):
            x["y_ref"][x["rows"], x["sl"]] = y[i]
            st_ref[x["d"], x["p"]] = st[i] * eend_ref[x["d"], step, 0:1, x["sl"]] + upd[i]
        return carry

    lax.fori_loop(0, n_chunks // PASS_CHUNKS, local_pass, 0)
    lax.fori_loop(0, n_chunks, state_pass, 0)


def _scan(r, v, kk, lw, kd, kka, masks):
    bsz, s, w = r.shape
    tb = min(TB_SCAN, s)
    nblk = s // tb
    fwd = pl.BlockSpec((None, tb, w), lambda b_, j: (b_, j, 0))
    bwd = pl.BlockSpec((None, tb, w), lambda b_, j: (b_, nblk - 1 - j, 0))
    fwd2 = pl.BlockSpec((None, None, tb, w), lambda b_, j: (0, b_, j, 0))
    bwd2 = pl.BlockSpec((None, None, tb, w), lambda b_, j: (1, b_, nblk - 1 - j, 0))
    out = jax.ShapeDtypeStruct((bsz, s, w), F32)
    per_chunk = lambda dt: pltpu.VMEM((2, tb // CHUNK, CHUNK, w), dt)
    return pl.pallas_call(
        _scan_kernel,
        grid=(bsz, nblk),
        in_specs=[fwd, fwd, fwd, bwd, bwd, bwd, fwd2, fwd2, fwd2, bwd2, bwd2, bwd2]
        + [_resident(a.shape) for a in masks],
        out_specs=[fwd, bwd],
        out_shape=[out, out],
        scratch_shapes=[pltpu.VMEM((2, w // LANES, LANES, LANES), F32), per_chunk(BF16), per_chunk(BF16),
                        per_chunk(BF16), per_chunk(BF16), pltpu.VMEM((2, tb // CHUNK, SUBLANES, w), F32)]
        + [per_chunk(BF16)] * 6,
        compiler_params=_cparams("parallel", "arbitrary"),
    )(r, v, kk, r, v, kk, lw, kd, kka, lw, kd, kka, *masks)


def _rope_partner(x):
    n = x.shape[1]
    lane = lax.broadcasted_iota(jnp.int32, x.shape, 1)
    return jnp.where((lane & 31) < 16, pltpu.roll(x, n - 16, 1), pltpu.roll(x, 16, 1))


def _attn_kernel(q_ref, k_ref, v_ref, cq_ref, sq_ref, ck_ref, sk_ref, qg_ref, kg_ref, jq_ref, jk_ref,
                 o_ref, kd_ref, vd_ref):
    s_len = k_ref.shape[0]
    tq = q_ref.shape[0]
    tk = min(TK_PREP, s_len)

    @pl.when(pl.program_id(1) == 0)
    def _():
        def body(i, carry):
            rows = pl.ds(pl.multiple_of(i * tk, tk), tk)
            k = k_ref[rows, :].astype(F32)
            kn = k * lax.rsqrt(_segsum(k * k, jk_ref[...]) * (1.0 / HEAD) + NORM_EPS) * kg_ref[...]
            kr = kn * ck_ref[rows, :] + _rope_partner(kn) * sk_ref[rows, :]
            v = v_ref[rows, :].astype(F32)
            low = lax.broadcasted_iota(jnp.int32, kr.shape, 1) < HEAD
            ksw = pltpu.roll(kr, HEAD, 1)
            kd_ref[0, rows, :] = jnp.where(low, kr, ksw).astype(BF16)
            kd_ref[1, rows, :] = jnp.where(low, ksw, kr).astype(BF16)
            vd_ref[0, rows, :] = jnp.where(low, v, 1.0).astype(BF16)
            vd_ref[1, rows, :] = jnp.where(low, pltpu.roll(v, HEAD, 1), 1.0).astype(BF16)
            return carry
        lax.fori_loop(0, s_len // tk, body, 0)

    q = q_ref[...].astype(F32)
    qn = q * lax.rsqrt(_segsum(q * q, jq_ref[...]) * (1.0 / HEAD) + NORM_EPS) * qg_ref[...]
    qr = qn * cq_ref[...] + _rope_partner(qn) * sq_ref[...]
    low = lax.broadcasted_iota(jnp.int32, (tq, LANES), 1) < HEAD
    pairs_per_kv = AT_Q_HEADS // AT_KV_HEADS // 2
    tkc = min(TK_ATTN, s_len)
    n_kc = s_len // tkc
    qs = []
    for g in range(AT_KV_HEADS):
        parts = []
        for jj in range(pairs_per_kv):
            qp = qr[:, (g * pairs_per_kv + jj) * LANES:(g * pairs_per_kv + jj + 1) * LANES]
            parts += [jnp.where(low, qp, 0.0), jnp.where(low, 0.0, qp)]
        qs.append(jnp.concatenate(parts, axis=0).astype(BF16))
    streams = [(g, kc) for g in range(AT_KV_HEADS) for kc in range(n_kc)]
    scores = lambda g, kc: _dot_nt(qs[g], kd_ref[g, kc * tkc:(kc + 1) * tkc, :])
    partial, s_next = {}, scores(*streams[0])
    for i, (g, kc) in enumerate(streams):
        s = s_next
        if i + 1 < len(streams):
            s_next = scores(*streams[i + 1])
        m = jnp.max(s, axis=-1, keepdims=True)
        p = jnp.exp2(s - m).astype(BF16)
        partial[g, kc] = (m, _dot(p, vd_ref[g, kc * tkc:(kc + 1) * tkc, :]))
    for g in range(AT_KV_HEADS):
        m_all = partial[g, 0][0]
        for kc in range(1, n_kc):
            m_all = jnp.maximum(m_all, partial[g, kc][0])
        o = sum(jnp.exp2(partial[g, kc][0] - m_all) * partial[g, kc][1] for kc in range(n_kc))
        o = o / pltpu.roll(o, HEAD, 1)
        for jj in range(pairs_per_kv):
            even = o[(2 * jj) * tq:(2 * jj + 1) * tq]
            odd = pltpu.roll(o[(2 * jj + 1) * tq:(2 * jj + 2) * tq], HEAD, 1)
            pair = jnp.where(low, even, odd)
            col = (g * pairs_per_kv + jj) * LANES
            o_ref[:, col:col + LANES] = pair.astype(BF16)


def _attn(at, tabs, q_g, k_g, jones):
    bsz, s, _ = at.shape
    tq = min(TQ_ATTN, s)
    cq, sq, ck, sk = tabs
    qw = AT_Q_HEADS * HEAD
    kvw = AT_KV_HEADS * HEAD
    cst = lambda a: pl.BlockSpec(a.shape, lambda b_, i: (0,) * a.ndim)
    jk = jones[0:kvw, 0:kvw]
    return pl.pallas_call(
        _attn_kernel,
        grid=(bsz, s // tq),
        in_specs=[pl.BlockSpec((None, tq, qw), lambda b_, i: (b_, i, 0)),
                  pl.BlockSpec((None, s, kvw), lambda b_, i: (b_, 0, qw // kvw)),
                  pl.BlockSpec((None, s, kvw), lambda b_, i: (b_, 0, qw // kvw + 1)),
                  pl.BlockSpec((tq, qw), lambda b_, i: (i, 0)),
                  pl.BlockSpec((tq, qw), lambda b_, i: (i, 0)),
                  cst(ck), cst(sk), cst(q_g), cst(k_g), cst(jones), cst(jk)],
        out_specs=pl.BlockSpec((None, tq, qw), lambda b_, i: (b_, i, 0)),
        out_shape=jax.ShapeDtypeStruct((bsz, s, qw), BF16),
        scratch_shapes=[pltpu.VMEM((AT_KV_HEADS, s, LANES), BF16), pltpu.VMEM((AT_KV_HEADS, s, LANES), BF16)],
        compiler_params=_cparams("parallel", "arbitrary"),
    )(at, at, at, cq, sq, ck, sk, q_g, k_g, jones, jk)


def _rope_tables(s):
    rows = s // GRID_W
    pos_r = jnp.repeat(jnp.arange(rows), GRID_W).astype(F32)
    pos_c = jnp.tile(jnp.arange(GRID_W), rows).astype(F32)
    half = HEAD // 2
    inv = ROPE_THETA ** (-jnp.arange(0, half, 2, dtype=F32) / half)
    ang_r = pos_r[:, None] * inv
    ang_c = pos_c[:, None] * inv
    cos = jnp.concatenate([jnp.cos(ang_r)] * 2 + [jnp.cos(ang_c)] * 2, axis=-1)
    sin = jnp.concatenate([-jnp.sin(ang_r), jnp.sin(ang_r), -jnp.sin(ang_c), jnp.sin(ang_c)], axis=-1)
    scale = HEAD ** -0.5 * math.log2(math.e)
    return (jnp.tile(cos, (1, AT_Q_HEADS)) * scale, jnp.tile(sin, (1, AT_Q_HEADS)) * scale,
            jnp.tile(cos, (1, AT_KV_HEADS)), jnp.tile(sin, (1, AT_KV_HEADS)))


def _merge_kernel(y0_ref, y1_ref, g_ref, bonus_ref, lng_ref, lnb_ref, j_ref, ycv_ref, yat_ref, sg_ref, x_ref,
                  wb_ref, wo_ref, o_ref):
    j = j_ref[...]
    y = y0_ref[...] + y1_ref[...]
    mu = _segsum(y, j) * (1.0 / HEAD)
    dev = y - mu
    var = _segsum(dev * dev, j) * (1.0 / HEAD)
    yn = dev * lax.rsqrt(var + RW_LN_EPS) * lng_ref[...] + lnb_ref[...]
    y_rw = ((yn + bonus_ref[...]) * g_ref[...]).astype(BF16)
    dm = D_MODEL
    merged = (sg_ref[:, 0:dm].astype(F32) * _dot(y_rw, wb_ref[0])
              + sg_ref[:, dm:2 * dm].astype(F32) * _dot(ycv_ref[...], wb_ref[1])
              + sg_ref[:, 2 * dm:3 * dm].astype(F32) * _dot(yat_ref[...], wb_ref[2]))
    o_ref[...] = x_ref[...] + _dot(merged.astype(BF16), wo_ref[...])


def _merge(yf, yb, g, bonus, ln_g, ln_b, jones, y_cv, y_at, sg, x, w_branch, w_out):
    t = x.shape[0]
    tm = TM_MERGE
    row = lambda n: pl.BlockSpec((tm, n), lambda i: (i, 0))
    return pl.pallas_call(
        _merge_kernel,
        grid=(t // tm,),
        in_specs=[row(RW_WIDTH), row(RW_WIDTH), row(RW_WIDTH), row(RW_WIDTH),
                  _resident(ln_g.shape), _resident(ln_b.shape), _resident(jones.shape),
                  row(CV_WIDTH), row(AT_Q_HEADS * HEAD), row(GATE_COLS), row(D_MODEL),
                  _resident(w_branch.shape), _resident(w_out.shape)],
        out_specs=row(D_MODEL),
        out_shape=jax.ShapeDtypeStruct((t, D_MODEL), F32),
        compiler_params=_cparams("parallel"),
    )(yf, yb, g, bonus, ln_g, ln_b, jones, y_cv, y_at, sg, x, w_branch, w_out)


def _mlp_kernel(x_ref, g_ref, w1_ref, w2_ref, o_ref):
    x = x_ref[...]
    h = (x * lax.rsqrt(jnp.mean(x * x, axis=-1, keepdims=True) + NORM_EPS) * g_ref[...]).astype(BF16)
    acc = x
    for c in range(D_FF // FF_CHUNK):
        cols = slice(c * FF_CHUNK, (c + 1) * FF_CHUNK)
        u = jnp.maximum(_dot(h, w1_ref[:, cols]), 0.0)
        acc = acc + _dot((u * u).astype(BF16), w2_ref[cols, :])
    o_ref[...] = acc


def _mlp(x, g, w1, w2):
    t = x.shape[0]
    tm = TM_MLP
    row = pl.BlockSpec((tm, D_MODEL), lambda i: (i, 0))
    return pl.pallas_call(
        _mlp_kernel,
        grid=(t // tm,),
        in_specs=[row, _resident(g.shape), _resident(w1.shape), _resident(w2.shape)],
        out_specs=row,
        out_shape=jax.ShapeDtypeStruct((t, D_MODEL), F32),
        compiler_params=_cparams("parallel"),
    )(x, g, w1, w2)


def _block_diag2(w):
    z = jnp.zeros_like(w[0])
    return jnp.concatenate([jnp.concatenate([w[0], z], axis=1), jnp.concatenate([z, w[1]], axis=1)], axis=0)


def kernel(x, norm1_g, w_in, rw_mu_prev, rw_mu_next, rw_w0, rw_w2, rw_a0, rw_a2, rw_g2, rw_k_k, rw_k_a, rw_r_k,
           rw_ln_g, rw_ln_b, cv_w, cv_b, at_q_g, at_k_g, w_branch, w_out, norm2_g, w_mlp1, w_mlp2):
    bsz, s, dm = x.shape
    assert dm == D_MODEL and s % GRID_W == 0
    for tile in (TS_PREP, TB_SCAN, TS_CONV, TQ_ATTN, TK_PREP):
        assert s % min(tile, s) == 0 and min(tile, s) % CHUNK == 0
    t = bsz * s
    tabs = _rope_tables(s)
    masks = _scan_masks()
    jones = jnp.asarray(np.kron(np.eye(RW_WIDTH // HEAD, dtype=np.float32), np.ones((HEAD, HEAD), np.float32)), BF16)
    row = lambda a: a.reshape(1, -1)
    xf = x.reshape(t, dm)
    for l in range(DEPTH):
        rw, cv, at, sg = _inproj(xf, row(norm1_g[l]), w_in[l].astype(BF16))
        r, v, kk, g, bonus, lw, kd, kka = _prep(
            rw.reshape(bsz, s, RW_COLS), row(rw_mu_prev[l]), row(rw_mu_next[l]),
            row(rw_w0[l]), _block_diag2(rw_w2[l]).astype(BF16), row(rw_a0[l]), _block_diag2(rw_a2[l]).astype(BF16),
            rw_g2[l].astype(BF16), row(rw_k_k[l]), row(rw_k_a[l]), row(rw_r_k[l]), jones)
        yf, yb = _scan(r, v, kk, lw, kd, kka, masks)
        y_cv = _conv(cv.reshape(bsz, s, CV_COLS), cv_w[l], row(cv_b[l]))
        y_at = _attn(at.reshape(bsz, s, AT_COLS), tabs, row(jnp.tile(at_q_g[l], AT_Q_HEADS)),
                     row(jnp.tile(at_k_g[l], AT_KV_HEADS)), jones)
        xf = _merge(yf.reshape(t, RW_WIDTH), yb.reshape(t, RW_WIDTH), g.reshape(t, RW_WIDTH),
                    bonus.reshape(t, RW_WIDTH),
                    row(rw_ln_g[l]), row(rw_ln_b[l]), jones, y_cv.reshape(t, CV_WIDTH),
                    y_at.reshape(t, AT_Q_HEADS * HEAD), sg, xf, w_branch[l].astype(BF16), w_out[l].astype(BF16))
        xf = _mlp(xf, row(norm2_g[l]), w_mlp1[l].astype(BF16), w_mlp2[l].astype(BF16))
    return xf.reshape(bsz, s, dm)
```

```python
import functools
import math

import numpy as np
import jax
import jax.numpy as jnp
from jax import lax
from jax.experimental import pallas as pl
from jax.experimental.pallas import tpu as pltpu

F32 = jnp.float32
BF16 = jnp.bfloat16

D_MODEL = 1024
DEPTH = 4
GRID_W = 64
NORM_EPS = 1e-6
HEAD = 64
RW_WIDTH = 512
RW_LORA = 128
RW_LN_EPS = 64e-5
RW_COLS = 3 * RW_WIDTH + 3 * RW_LORA
CV_WIDTH = 512
CV_COLS = 3 * CV_WIDTH
AT_Q_HEADS = 8
AT_KV_HEADS = 2
AT_COLS = (AT_Q_HEADS + 2 * AT_KV_HEADS) * HEAD
GATE_COLS = 3 * D_MODEL
ROPE_THETA = 10000.0
D_FF = 4 * D_MODEL

LANES = 128
SUBLANES = 8
BF16_SUBLANES = 16
VMEM_LIMIT_BYTES = 56 * 1024 * 1024
CHUNK = 64
TM_INPROJ = 256
TB_SCAN = 256
PASS_CHUNKS = 2
TS_CONV = 512
TQ_ATTN = 128
TK_ATTN = 2048
TK_PREP = 512
TM_MERGE = 512
TM_MLP = 512
FF_CHUNK = 1024


def _cparams(*sem):
    return pltpu.CompilerParams(dimension_semantics=sem, vmem_limit_bytes=VMEM_LIMIT_BYTES)


def _dot(a, b):
    return jnp.dot(a, b, preferred_element_type=F32)


def _dot_nt(a, b):
    return lax.dot_general(a, b, (((1,), (1,)), ((), ())), preferred_element_type=F32)


def _split2(x):
    hi = x.astype(BF16)
    lo = (x - hi.astype(F32)).astype(BF16)
    return hi, lo


def _segsum(x, j):
    hi, lo = _split2(x)
    return _dot(hi, j) + _dot(lo, j)


def _sigmoid(x):
    return 1.0 / (1.0 + jnp.exp(-x))


def _resident(shape):
    nd = len(shape)
    return pl.BlockSpec(shape, lambda *_: (0,) * nd, pipeline_mode=pl.Buffered(1))


def _shift_rows(c, halo_prev, halo_next, first, last):
    n = c.shape[0]
    row = lax.broadcasted_iota(jnp.int32, c.shape, 0)
    lane_row = lax.broadcasted_iota(jnp.int32, (1, c.shape[1]), 1)
    is_first = (lane_row * 0 + first.astype(jnp.int32)) > 0
    is_last = (lane_row * 0 + last.astype(jnp.int32)) > 0
    prev_row = jnp.where(is_first, 0.0, halo_prev[halo_prev.shape[0] - 1:, :])
    next_row = jnp.where(is_last, 0.0, halo_next[0:1, :])
    prev = jnp.where(row == 0, prev_row, pltpu.roll(c, 1, 0))
    nxt = jnp.where(row == n - 1, next_row, pltpu.roll(c, n - 1, 0))
    return prev, nxt


def _halo_specs(ts, width, n_seq_rows, rows):
    r = ts // rows
    last_blk = n_seq_rows // rows - 1
    prev = pl.BlockSpec((None, rows, width), lambda b, i: (b, jnp.maximum(i * r - 1, 0), 0))
    nxt = pl.BlockSpec((None, rows, width), lambda b, i: (b, jnp.minimum((i + 1) * r, last_blk), 0))
    return prev, nxt


def _inproj_kernel(x_ref, xp_ref, xn_ref, g_ref, w_ref, mup_ref, mun_ref, w0_ref, w2_ref, a0_ref, a2_ref, g2_ref,
                   kk_w_ref, ka_ref, rk_ref, j_ref,
                   r_ref, v_ref, kk_ref, gate_ref, bonus_ref, lw_ref, kd_ref, kka_ref, cv_ref, at_ref, sg_ref):
    i = pl.program_id(1)

    def normed(x):
        return (x * lax.rsqrt(jnp.mean(x * x, axis=-1, keepdims=True) + NORM_EPS) * g_ref[...]).astype(BF16)

    h = normed(x_ref[...])
    o0, o1, o2 = RW_COLS, RW_COLS + CV_COLS, RW_COLS + CV_COLS + AT_COLS
    dm = D_MODEL

    def gate_cols(n):
        sg_ref[:, n * dm:(n + 1) * dm] = _sigmoid(_dot(h, w_ref[:, o2 + n * dm:o2 + (n + 1) * dm])).astype(BF16)

    c = _dot(h, w_ref[:, 0:o0])
    c_halo = _dot(normed(jnp.concatenate([xp_ref[...], xn_ref[...]], axis=0)), w_ref[:, 0:o0])
    cv_ref[...] = _dot(h, w_ref[:, o0:o1]).astype(BF16)
    prev, nxt = _shift_rows(c, c_halo[0:SUBLANES], c_halo[SUBLANES:2 * SUBLANES], i == 0, i == pl.num_programs(1) - 1)
    z = c + mup_ref[...] * (prev - c) + mun_ref[...] * (nxt - c)
    w = RW_WIDTH
    r, k, v = z[:, 0:w], z[:, w:2 * w], z[:, 2 * w:3 * w]
    wd = z[:, 3 * w:3 * w + RW_LORA]
    ad = z[:, 3 * w + RW_LORA:3 * w + 2 * RW_LORA]
    gd = z[:, 3 * w + 2 * RW_LORA:3 * w + 3 * RW_LORA]
    j = j_ref[...]
    w_raw = w0_ref[...] + _dot(jnp.tanh(wd).astype(BF16), w2_ref[...])
    lw = -math.exp(-0.5) * _sigmoid(w_raw)
    a = _sigmoid(a0_ref[...] + _dot(ad.astype(BF16), a2_ref[...]))
    gate_ref[...] = _dot(_sigmoid(gd).astype(BF16), g2_ref[...]).astype(BF16)
    gate_cols(0)
    kkr = k * kk_w_ref[...]
    kk = kkr / jnp.maximum(jnp.sqrt(_segsum(kkr * kkr, j)), 1e-12)
    r_ref[...] = r.astype(BF16)
    v_ref[...] = v.astype(BF16)
    kk_ref[...] = kk.astype(BF16)
    at_ref[...] = _dot(h, w_ref[:, o1:o2]).astype(BF16)
    gate_cols(1)
    ksum = jnp.zeros_like(k)
    for d in range(2):
        a_d = a[:, d * w:(d + 1) * w]
        k_d = k * (1.0 + (a_d - 1.0) * ka_ref[...])
        ksum = ksum + k_d
        lw_ref[d] = lw[:, d * w:(d + 1) * w]
        kd_ref[d] = k_d.astype(BF16)
        kka_ref[d] = (kk * a_d).astype(BF16)
    gate_cols(2)
    bonus_ref[...] = (_segsum(r * (0.5 * ksum) * rk_ref[...], j) * v).astype(BF16)


def _inproj(x, g, w, mup, mun, w0, w2bd, a0, a2bd, g2, k_k, k_a, r_k, jones):
    bsz, s, _ = x.shape
    tm = min(TM_INPROJ, s)
    hp, hn = _halo_specs(tm, D_MODEL, s, SUBLANES)
    tok = lambda n: pl.BlockSpec((None, tm, n), lambda b_, i: (b_, i, 0))
    tok2 = pl.BlockSpec((2, None, tm, RW_WIDTH), lambda b_, i: (0, b_, i, 0))
    consts = (g, w, mup, mun, w0, w2bd, a0, a2bd, g2, k_k, k_a, r_k, jones)
    one = lambda n: jax.ShapeDtypeStruct((bsz, s, n), BF16)
    two = lambda dt: jax.ShapeDtypeStruct((2, bsz, s, RW_WIDTH), dt)
    return pl.pallas_call(
        _inproj_kernel,
        grid=(bsz, s // tm),
        in_specs=[tok(D_MODEL), hp, hn] + [_resident(a.shape) for a in consts],
        out_specs=[tok(RW_WIDTH)] * 5 + [tok2] * 3 + [tok(CV_COLS), tok(AT_COLS), tok(GATE_COLS)],
        out_shape=[one(RW_WIDTH)] * 5 + [two(F32), two(BF16), two(BF16)] + [one(CV_COLS), one(AT_COLS), one(GATE_COLS)],
        compiler_params=_cparams("parallel", "arbitrary"),
    )(x, x, x, *consts)


def _conv_kernel(c_ref, hp_ref, hn_ref, w_ref, b_ref, o_ref):
    i = pl.program_id(1)
    c = c_ref[...].astype(F32)
    w = CV_WIDTH
    z = c[:, w:2 * w] * c[:, 2 * w:3 * w]
    hp, hn = hp_ref[...].astype(F32), hn_ref[...].astype(F32)
    zp, zn = _shift_rows(z, hp[:, w:2 * w] * hp[:, 2 * w:3 * w], hn[:, w:2 * w] * hn[:, 2 * w:3 * w],
                         i == 0, i == pl.num_programs(1) - 1)
    conv = b_ref[...] + w_ref[0:1, :] * zp + w_ref[1:2, :] * z + w_ref[2:3, :] * zn
    o_ref[...] = (c[:, 0:w] * conv).astype(BF16)


def _conv(cv, w, b):
    bsz, s, _ = cv.shape
    ts = min(TS_CONV, s)
    hp, hn = _halo_specs(ts, CV_COLS, s, BF16_SUBLANES)
    return pl.pallas_call(
        _conv_kernel,
        grid=(bsz, s // ts),
        in_specs=[pl.BlockSpec((None, ts, CV_COLS), lambda b_, i: (b_, i, 0)), hp, hn,
                  pl.BlockSpec((3, CV_WIDTH), lambda b_, i: (0, 0)),
                  pl.BlockSpec((1, CV_WIDTH), lambda b_, i: (0, 0))],
        out_specs=pl.BlockSpec((None, ts, CV_WIDTH), lambda b_, i: (b_, i, 0)),
        out_shape=jax.ShapeDtypeStruct((bsz, s, CV_WIDTH), BF16),
        compiler_params=_cparams("parallel", "arbitrary"),
    )(cv, cv, cv, w, b)


def _scan_masks():
    c = CHUNK
    t = np.arange(c)
    lower = (t[None, :] <= t[:, None]).astype(np.float32)
    strict_l = (t[None, :] < t[:, None]).astype(np.float32)
    ones = np.ones((c, c), np.float32)
    tri = np.stack([np.concatenate([lower, ones], 0), np.concatenate([lower.T, ones], 0)])
    strict = np.stack([np.tile(strict_l, (1, 2)), np.tile(strict_l.T, (1, 2))])
    incl = np.stack([np.tile(lower, (1, 2)), np.tile(lower.T, (1, 2))])
    return jnp.asarray(tri, BF16), jnp.asarray(strict, F32), jnp.asarray(incl, F32)


def _scan_kernel(rf_ref, vf_ref, kkf_ref, rb_ref, vb_ref, kkb_ref,
                 lwf_ref, kdf_ref, kkaf_ref, lwb_ref, kdb_ref, kkab_ref,
                 tri_ref, strict_ref, incl_ref, yf_ref, yb_ref, st_ref,
                 kq_ref, rq_ref, kdec_ref, bdec_ref, eend_ref, avk_ref, ark_ref, arb_ref, mhi_ref, mlo_ref, tinv_ref):
    c = CHUNK
    n_chunks = rf_ref.shape[0] // c
    n_pairs = RW_WIDTH // LANES

    @pl.when(pl.program_id(1) == 0)
    def _():
        st_ref[...] = jnp.zeros_like(st_ref)

    lane = lax.broadcasted_iota(jnp.int32, (c, LANES), 1)
    row = lax.broadcasted_iota(jnp.int32, (c, LANES), 0)
    head0 = lane < HEAD
    eye = jnp.where((lane & (HEAD - 1)) == row, 1.0, 0.0).astype(F32)
    ri = lax.broadcasted_iota(jnp.int32, (LANES, LANES), 0)
    ci = lax.broadcasted_iota(jnp.int32, (LANES, LANES), 1)
    same_head = jnp.where((ri < HEAD) == (ci < HEAD), 1.0, 0.0).astype(F32)

    def stack(x):
        return jnp.concatenate([jnp.where(head0, x, 0.0), jnp.where(head0, 0.0, x)], axis=0)

    def stack_b(x):
        return stack(x).astype(BF16)

    dir_refs = ((rf_ref, vf_ref, kkf_ref, lwf_ref, kdf_ref, kkaf_ref, yf_ref),
                (rb_ref, vb_ref, kkb_ref, lwb_ref, kdb_ref, kkab_ref, yb_ref))

    def local_pass(blk, carry):
        ch = []
        for d, (r_ref, v_ref, kk_ref, lw_ref, kd_ref, kka_ref, y_ref) in enumerate(dir_refs):
            for sub in range(PASS_CHUNKS):
                step = blk * PASS_CHUNKS + sub
                cidx = step if d == 0 else n_chunks - 1 - step
                rows = pl.ds(pl.multiple_of(cidx * c, c), c)
                lw = lw_ref[rows, :]
                hi = lw.astype(BF16)
                rem = lw - hi.astype(F32)
                mid = rem.astype(BF16)
                lo = (rem - mid.astype(F32)).astype(BF16)
                tri = tri_ref[d]
                cum = _dot(tri, hi) + _dot(tri, mid) + _dot(tri, lo)
                lp = cum[0:c]
                e_end = jnp.exp(cum[c:c + SUBLANES])
                e_n = jnp.exp(-lp)
                rq = (r_ref[rows, :] * jnp.exp(lp)).astype(BF16)
                kq = (kk_ref[rows, :] * jnp.exp(lp - lw)).astype(BF16)
                ks = kd_ref[rows, :] * e_n
                bs = kka_ref[rows, :] * e_n
                kq_ref[d, step] = kq
                rq_ref[d, step] = rq
                kdec_ref[d, step] = (ks * e_end[0:1]).astype(BF16)
                bdec_ref[d, step] = (bs * e_end[0:1]).astype(BF16)
                eend_ref[d, step] = e_end
                for p in range(n_pairs):
                    sl = slice(p * LANES, (p + 1) * LANES)
                    ch.append(dict(d=d, step=step, sl=sl, kq=kq[:, sl], rq=rq[:, sl], ks=ks[:, sl], bs=bs[:, sl]))
        n = range(len(ch))
        strict = [strict_ref[x["d"]] for x in ch]
        incl = [incl_ref[x["d"]] for x in ch]
        lq = [jnp.concatenate([x["kq"], x["rq"]], axis=0) for x in ch]
        s_ks = [stack_b(x["ks"]) for x in ch]
        s_bs = [stack_b(x["bs"]) for x in ch]
        gk = [_dot_nt(lq[i], s_ks[i]) for i in n]
        gb = [_dot_nt(lq[i], s_bs[i]) for i in n]
        m = [-(gb[i][0:c] * strict[i]) for i in n]
        for i, x in enumerate(ch):
            avk_ref[x["d"], x["step"], :, x["sl"]] = (gk[i][0:c] * strict[i]).astype(BF16)
            ark_ref[x["d"], x["step"], :, x["sl"]] = (gk[i][c:2 * c] * incl[i]).astype(BF16)
            arb_ref[x["d"], x["step"], :, x["sl"]] = (gb[i][c:2 * c] * incl[i]).astype(BF16)
            m_hi, m_lo = _split2(m[i])
            mhi_ref[x["d"], x["step"], :, x["sl"]] = m_hi
            mlo_ref[x["d"], x["step"], :, x["sl"]] = m_lo
        t_inv = [eye + x for x in m]
        pw = [_dot(x.astype(BF16), stack_b(x)) for x in m]
        n_levels = int(math.log2(c)) - 1
        for lvl in range(n_levels):
            pw_bd = [stack_b(x) for x in pw]
            if lvl + 1 < n_levels:
                both = [_dot(jnp.concatenate([t_inv[i].astype(BF16), pw[i].astype(BF16)], axis=0), pw_bd[i])
                        for i in n]
                t_inv = [t_inv[i] + both[i][0:c] for i in n]
                pw = [both[i][c:2 * c] for i in n]
            else:
                t_inv = [t_inv[i] + _dot(t_inv[i].astype(BF16), pw_bd[i]) for i in n]
        for i, x in enumerate(ch):
            tinv_ref[x["d"], x["step"], :, x["sl"]] = t_inv[i].astype(BF16)
        return carry

    def state_pass(step, carry):
        ch = []
        for d, (r_ref, v_ref, kk_ref, lw_ref, kd_ref, kka_ref, y_ref) in enumerate(dir_refs):
            cidx = step if d == 0 else n_chunks - 1 - step
            rows = pl.ds(pl.multiple_of(cidx * c, c), c)
            for p in range(n_pairs):
                ch.append(dict(d=d, p=p, rows=rows, sl=slice(p * LANES, (p + 1) * LANES), y_ref=y_ref, v_ref=v_ref))
        n = range(len(ch))
        get = lambda ref: [ref[x["d"], step, :, x["sl"]] for x in ch]
        v = [x["v_ref"][x["rows"], x["sl"]].astype(F32) for x in ch]
        s_vb = [stack_b(x) for x in v]
        st = [st_ref[x["d"], x["p"]] for x in ch]
        stb = [x.astype(BF16) for x in st]
        kq_b, rq_b, a_vk, a_rk, a_rb, t_b, m_hi, m_lo = (get(r) for r in (
            kq_ref, rq_ref, avk_ref, ark_ref, arb_ref, tinv_ref, mhi_ref, mlo_ref))
        from_state = [_dot_nt(jnp.concatenate([kq_b[i], rq_b[i]], axis=0), stb[i]) for i in n]
        from_v = [_dot(jnp.concatenate([a_vk[i], a_rk[i]], axis=0), s_vb[i]) for i in n]
        rhs = [from_state[i][0:c] + from_v[i][0:c] for i in n]
        u0 = [-_dot(t_b[i], stack_b(rhs[i])) for i in n]
        u_sp = [_split2(x) for x in u0]
        mu_a = [_dot(jnp.concatenate([m_hi[i], m_lo[i]], axis=0), stack(u_sp[i][0].astype(F32)).astype(BF16))
                for i in n]
        mu_b = [_dot(m_hi[i], stack(u_sp[i][1].astype(F32)).astype(BF16)) for i in n]
        res = [mu_a[i][0:c] + mu_a[i][c:2 * c] + mu_b[i] - rhs[i] - u0[i] for i in n]
        u = [u0[i] + _dot(t_b[i], stack_b(res[i])) for i in n]
        y = [from_state[i][c:2 * c] + from_v[i][c:2 * c] + _dot(a_rb[i], stack_b(u[i])) for i in n]
        vu_t = [jnp.concatenate([v[i], u[i]], axis=0).T.astype(BF16) for i in n]
        kb = [jnp.concatenate([kdec_ref[x["d"], step, :, x["sl"]], bdec_ref[x["d"], step, :, x["sl"]]], axis=0)
              for x in ch]
        upd = [_dot(vu_t[i], kb[i]) * same_head for i in n]
        for i, x in enumerate(ch):
            x["y_ref"][x["rows"], x["sl"]] = y[i]
            st_ref[x["d"], x["p"]] = st[i] * eend_ref[x["d"], step, 0:1, x["sl"]] + upd[i]
        return carry

    lax.fori_loop(0, n_chunks // PASS_CHUNKS, local_pass, 0)
    lax.fori_loop(0, n_chunks, state_pass, 0)


def _scan(r, v, kk, lw, kd, kka, masks):
    bsz, s, w = r.shape
    tb = min(TB_SCAN, s)
    nblk = s // tb
    fwd = pl.BlockSpec((None, tb, w), lambda b_, j: (b_, j, 0))
    bwd = pl.BlockSpec((None, tb, w), lambda b_, j: (b_, nblk - 1 - j, 0))
    fwd2 = pl.BlockSpec((None, None, tb, w), lambda b_, j: (0, b_, j, 0))
    bwd2 = pl.BlockSpec((None, None, tb, w), lambda b_, j: (1, b_, nblk - 1 - j, 0))
    out = jax.ShapeDtypeStruct((bsz, s, w), F32)
    per_chunk = lambda dt: pltpu.VMEM((2, tb // CHUNK, CHUNK, w), dt)
    return pl.pallas_call(
        _scan_kernel,
        grid=(bsz, nblk),
        in_specs=[fwd, fwd, fwd, bwd, bwd, bwd, fwd2, fwd2, fwd2, bwd2, bwd2, bwd2]
        + [_resident(a.shape) for a in masks],
        out_specs=[fwd, bwd],
        out_shape=[out, out],
        scratch_shapes=[pltpu.VMEM((2, w // LANES, LANES, LANES), F32), per_chunk(BF16), per_chunk(BF16),
                        per_chunk(BF16), per_chunk(BF16), pltpu.VMEM((2, tb // CHUNK, SUBLANES, w), F32)]
        + [per_chunk(BF16)] * 6,
        compiler_params=_cparams("parallel", "arbitrary"),
    )(r, v, kk, r, v, kk, lw, kd, kka, lw, kd, kka, *masks)


def _rope_partner(x):
    n = x.shape[1]
    lane = lax.broadcasted_iota(jnp.int32, x.shape, 1)
    return jnp.where((lane & 31) < 16, pltpu.roll(x, n - 16, 1), pltpu.roll(x, 16, 1))


def _attn_kernel(q_ref, k_ref, v_ref, cq_ref, sq_ref, ck_ref, sk_ref, qg_ref, kg_ref, jq_ref, jk_ref,
                 o_ref, kd_ref, vd_ref):
    s_len = k_ref.shape[0]
    tq = q_ref.shape[0]
    tk = min(TK_PREP, s_len)

    @pl.when(pl.program_id(1) == 0)
    def _():
        def body(i, carry):
            rows = pl.ds(pl.multiple_of(i * tk, tk), tk)
            k = k_ref[rows, :].astype(F32)
            kn = k * lax.rsqrt(_segsum(k * k, jk_ref[...]) * (1.0 / HEAD) + NORM_EPS) * kg_ref[...]
            kr = kn * ck_ref[rows, :] + _rope_partner(kn) * sk_ref[rows, :]
            v = v_ref[rows, :].astype(F32)
            low = lax.broadcasted_iota(jnp.int32, kr.shape, 1) < HEAD
            ksw = pltpu.roll(kr, HEAD, 1)
            kd_ref[0, rows, :] = jnp.where(low, kr, ksw).astype(BF16)
            kd_ref[1, rows, :] = jnp.where(low, ksw, kr).astype(BF16)
            vd_ref[0, rows, :] = jnp.where(low, v, 1.0).astype(BF16)
            vd_ref[1, rows, :] = jnp.where(low, pltpu.roll(v, HEAD, 1), 1.0).astype(BF16)
            return carry
        lax.fori_loop(0, s_len // tk, body, 0)

    q = q_ref[...].astype(F32)
    qn = q * lax.rsqrt(_segsum(q * q, jq_ref[...]) * (1.0 / HEAD) + NORM_EPS) * qg_ref[...]
    qr = qn * cq_ref[...] + _rope_partner(qn) * sq_ref[...]
    low = lax.broadcasted_iota(jnp.int32, (tq, LANES), 1) < HEAD
    pairs_per_kv = AT_Q_HEADS // AT_KV_HEADS // 2
    tkc = min(TK_ATTN, s_len)
    n_kc = s_len // tkc
    qs = []
    for g in range(AT_KV_HEADS):
        parts = []
        for jj in range(pairs_per_kv):
            qp = qr[:, (g * pairs_per_kv + jj) * LANES:(g * pairs_per_kv + jj + 1) * LANES]
            parts += [jnp.where(low, qp, 0.0), jnp.where(low, 0.0, qp)]
        qs.append(jnp.concatenate(parts, axis=0).astype(BF16))
    streams = [(g, kc) for g in range(AT_KV_HEADS) for kc in range(n_kc)]
    scores = lambda g, kc: _dot_nt(qs[g], kd_ref[g, kc * tkc:(kc + 1) * tkc, :])
    partial, s_next = {}, scores(*streams[0])
    for i, (g, kc) in enumerate(streams):
        s = s_next
        if i + 1 < len(streams):
            s_next = scores(*streams[i + 1])
        m = jnp.max(s, axis=-1, keepdims=True)
        p = jnp.exp2(s - m).astype(BF16)
        partial[g, kc] = (m, _dot(p, vd_ref[g, kc * tkc:(kc + 1) * tkc, :]))
    for g in range(AT_KV_HEADS):
        m_all = partial[g, 0][0]
        for kc in range(1, n_kc):
            m_all = jnp.maximum(m_all, partial[g, kc][0])
        o = sum(jnp.exp2(partial[g, kc][0] - m_all) * partial[g, kc][1] for kc in range(n_kc))
        o = o / pltpu.roll(o, HEAD, 1)
        for jj in range(pairs_per_kv):
            even = o[(2 * jj) * tq:(2 * jj + 1) * tq]
            odd = pltpu.roll(o[(2 * jj + 1) * tq:(2 * jj + 2) * tq], HEAD, 1)
            pair = jnp.where(low, even, odd)
            col = (g * pairs_per_kv + jj) * LANES
            o_ref[:, col:col + LANES] = pair.astype(BF16)


def _attn(at, tabs, q_g, k_g, jones):
    bsz, s, _ = at.shape
    tq = min(TQ_ATTN, s)
    cq, sq, ck, sk = tabs
    qw = AT_Q_HEADS * HEAD
    kvw = AT_KV_HEADS * HEAD
    cst = lambda a: pl.BlockSpec(a.shape, lambda b_, i: (0,) * a.ndim)
    jk = jones[0:kvw, 0:kvw]
    return pl.pallas_call(
        _attn_kernel,
        grid=(bsz, s // tq),
        in_specs=[pl.BlockSpec((None, tq, qw), lambda b_, i: (b_, i, 0)),
                  pl.BlockSpec((None, s, kvw), lambda b_, i: (b_, 0, qw // kvw)),
                  pl.BlockSpec((None, s, kvw), lambda b_, i: (b_, 0, qw // kvw + 1)),
                  pl.BlockSpec((tq, qw), lambda b_, i: (i, 0)),
                  pl.BlockSpec((tq, qw), lambda b_, i: (i, 0)),
                  cst(ck), cst(sk), cst(q_g), cst(k_g), cst(jones), cst(jk)],
        out_specs=pl.BlockSpec((None, tq, qw), lambda b_, i: (b_, i, 0)),
        out_shape=jax.ShapeDtypeStruct((bsz, s, qw), BF16),
        scratch_shapes=[pltpu.VMEM((AT_KV_HEADS, s, LANES), BF16), pltpu.VMEM((AT_KV_HEADS, s, LANES), BF16)],
        compiler_params=_cparams("parallel", "arbitrary"),
    )(at, at, at, cq, sq, ck, sk, q_g, k_g, jones, jk)


def _rope_tables(s):
    rows = s // GRID_W
    pos_r = jnp.repeat(jnp.arange(rows), GRID_W).astype(F32)
    pos_c = jnp.tile(jnp.arange(GRID_W), rows).astype(F32)
    half = HEAD // 2
    inv = ROPE_THETA ** (-jnp.arange(0, half, 2, dtype=F32) / half)
    ang_r = pos_r[:, None] * inv
    ang_c = pos_c[:, None] * inv
    cos = jnp.concatenate([jnp.cos(ang_r)] * 2 + [jnp.cos(ang_c)] * 2, axis=-1)
    sin = jnp.concatenate([-jnp.sin(ang_r), jnp.sin(ang_r), -jnp.sin(ang_c), jnp.sin(ang_c)], axis=-1)
    scale = HEAD ** -0.5 * math.log2(math.e)
    return (jnp.tile(cos, (1, AT_Q_HEADS)) * scale, jnp.tile(sin, (1, AT_Q_HEADS)) * scale,
            jnp.tile(cos, (1, AT_KV_HEADS)), jnp.tile(sin, (1, AT_KV_HEADS)))


def _merge_kernel(y0_ref, y1_ref, g_ref, bonus_ref, lng_ref, lnb_ref, j_ref, ycv_ref, yat_ref, sg_ref, x_ref,
                  wb_ref, wo_ref, o_ref):
    j = j_ref[...]
    y = y0_ref[...] + y1_ref[...]
    mu = _segsum(y, j) * (1.0 / HEAD)
    dev = y - mu
    var = _segsum(dev * dev, j) * (1.0 / HEAD)
    yn = dev * lax.rsqrt(var + RW_LN_EPS) * lng_ref[...] + lnb_ref[...]
    y_rw = ((yn + bonus_ref[...]) * g_ref[...]).astype(BF16)
    dm = D_MODEL
    merged = (sg_ref[:, 0:dm].astype(F32) * _dot(y_rw, wb_ref[0])
              + sg_ref[:, dm:2 * dm].astype(F32) * _dot(ycv_ref[...], wb_ref[1])
              + sg_ref[:, 2 * dm:3 * dm].astype(F32) * _dot(yat_ref[...], wb_ref[2]))
    o_ref[...] = x_ref[...] + _dot(merged.astype(BF16), wo_ref[...])


def _merge(yf, yb, g, bonus, ln_g, ln_b, jones, y_cv, y_at, sg, x, w_branch, w_out):
    t = x.shape[0]
    tm = TM_MERGE
    row = lambda n: pl.BlockSpec((tm, n), lambda i: (i, 0))
    return pl.pallas_call(
        _merge_kernel,
        grid=(t // tm,),
        in_specs=[row(RW_WIDTH), row(RW_WIDTH), row(RW_WIDTH), row(RW_WIDTH),
                  _resident(ln_g.shape), _resident(ln_b.shape), _resident(jones.shape),
                  row(CV_WIDTH), row(AT_Q_HEADS * HEAD), row(GATE_COLS), row(D_MODEL),
                  _resident(w_branch.shape), _resident(w_out.shape)],
        out_specs=row(D_MODEL),
        out_shape=jax.ShapeDtypeStruct((t, D_MODEL), F32),
        compiler_params=_cparams("parallel"),
    )(yf, yb, g, bonus, ln_g, ln_b, jones, y_cv, y_at, sg, x, w_branch, w_out)


def _mlp_kernel(x_ref, g_ref, w1_ref, w2_ref, o_ref):
    x = x_ref[...]
    h = (x * lax.rsqrt(jnp.mean(x * x, axis=-1, keepdims=True) + NORM_EPS) * g_ref[...]).astype(BF16)
    acc = x
    for c in range(D_FF // FF_CHUNK):
        cols = slice(c * FF_CHUNK, (c + 1) * FF_CHUNK)
        u = jnp.maximum(_dot(h, w1_ref[:, cols]), 0.0)
        acc = acc + _dot((u * u).astype(BF16), w2_ref[cols, :])
    o_ref[...] = acc


def _mlp(x, g, w1, w2):
    t = x.shape[0]
    tm = TM_MLP
    row = pl.BlockSpec((tm, D_MODEL), lambda i: (i, 0))
    return pl.pallas_call(
        _mlp_kernel,
        grid=(t // tm,),
        in_specs=[row, _resident(g.shape), _resident(w1.shape), _resident(w2.shape)],
        out_specs=row,
        out_shape=jax.ShapeDtypeStruct((t, D_MODEL), F32),
        compiler_params=_cparams("parallel"),
    )(x, g, w1, w2)


def _block_diag2(w):
    z = jnp.zeros_like(w[0])
    return jnp.concatenate([jnp.concatenate([w[0], z], axis=1), jnp.concatenate([z, w[1]], axis=1)], axis=0)


def kernel(x, norm1_g, w_in, rw_mu_prev, rw_mu_next, rw_w0, rw_w2, rw_a0, rw_a2, rw_g2, rw_k_k, rw_k_a, rw_r_k,
           rw_ln_g, rw_ln_b, cv_w, cv_b, at_q_g, at_k_g, w_branch, w_out, norm2_g, w_mlp1, w_mlp2):
    bsz, s, dm = x.shape
    assert dm == D_MODEL and s % GRID_W == 0
    for tile in (TM_INPROJ, TB_SCAN, TS_CONV, TQ_ATTN, TK_PREP):
        assert s % min(tile, s) == 0 and min(tile, s) % CHUNK == 0
    t = bsz * s
    tabs = _rope_tables(s)
    masks = _scan_masks()
    jones = jnp.asarray(np.kron(np.eye(RW_WIDTH // HEAD, dtype=np.float32), np.ones((HEAD, HEAD), np.float32)), BF16)
    row = lambda a: a.reshape(1, -1)
    xf = x.reshape(t, dm)
    for l in range(DEPTH):
        r, v, kk, g, bonus, lw, kd, kka, cv, at, sg = _inproj(
            xf.reshape(bsz, s, dm), row(norm1_g[l]), w_in[l].astype(BF16), row(rw_mu_prev[l]), row(rw_mu_next[l]),
            row(rw_w0[l]), _block_diag2(rw_w2[l]).astype(BF16), row(rw_a0[l]), _block_diag2(rw_a2[l]).astype(BF16),
            rw_g2[l].astype(BF16), row(rw_k_k[l]), row(rw_k_a[l]), row(rw_r_k[l]), jones)
        yf, yb = _scan(r, v, kk, lw, kd, kka, masks)
        y_cv = _conv(cv, cv_w[l], row(cv_b[l]))
        y_at = _attn(at, tabs, row(jnp.tile(at_q_g[l], AT_Q_HEADS)),
                     row(jnp.tile(at_k_g[l], AT_KV_HEADS)), jones)
        xf = _merge(yf.reshape(t, RW_WIDTH), yb.reshape(t, RW_WIDTH), g.reshape(t, RW_WIDTH),
                    bonus.reshape(t, RW_WIDTH),
                    row(rw_ln_g[l]), row(rw_ln_b[l]), jones, y_cv.reshape(t, CV_WIDTH),
                    y_at.reshape(t, AT_Q_HEADS * HEAD), sg.reshape(t, GATE_COLS), xf, w_branch[l].astype(BF16), w_out[l].astype(BF16))
        xf = _mlp(xf, row(norm2_g[l]), w_mlp1[l].astype(BF16), w_mlp2[l].astype(BF16))
    return xf.reshape(bsz, s, dm)
```

```python
import functools
import math

import numpy as np
import jax
import jax.numpy as jnp
from jax import lax
from jax.experimental import pallas as pl
from jax.experimental.pallas import tpu as pltpu

F32 = jnp.float32
BF16 = jnp.bfloat16

D_MODEL = 1024
DEPTH = 4
GRID_W = 64
NORM_EPS = 1e-6
HEAD = 64
RW_WIDTH = 512
RW_LORA = 128
RW_LN_EPS = 64e-5
RW_COLS = 3 * RW_WIDTH + 3 * RW_LORA
CV_WIDTH = 512
CV_COLS = 3 * CV_WIDTH
AT_Q_HEADS = 8
AT_KV_HEADS = 2
AT_COLS = (AT_Q_HEADS + 2 * AT_KV_HEADS) * HEAD
GATE_COLS = 3 * D_MODEL
ROPE_THETA = 10000.0
D_FF = 4 * D_MODEL

LANES = 128
SUBLANES = 8
BF16_SUBLANES = 16
VMEM_LIMIT_BYTES = 56 * 1024 * 1024
CHUNK = 64
TM_INPROJ = 256
TB_SCAN = 512
TS_CONV = 512
TQ_ATTN = 128
TK_ATTN = 2048
TK_PREP = 512
TM_MERGE = 512
TM_MLP = 512
FF_CHUNK = 1024


def _cparams(*sem):
    return pltpu.CompilerParams(dimension_semantics=sem, vmem_limit_bytes=VMEM_LIMIT_BYTES)


def _dot(a, b):
    return jnp.dot(a, b, preferred_element_type=F32)


def _dot_nt(a, b):
    return lax.dot_general(a, b, (((1,), (1,)), ((), ())), preferred_element_type=F32)


def _split2(x):
    hi = x.astype(BF16)
    lo = (x - hi.astype(F32)).astype(BF16)
    return hi, lo


def _segsum(x, j):
    hi, lo = _split2(x)
    return _dot(hi, j) + _dot(lo, j)


def _sigmoid(x):
    return 1.0 / (1.0 + jnp.exp(-x))


def _resident(shape):
    nd = len(shape)
    return pl.BlockSpec(shape, lambda *_: (0,) * nd, pipeline_mode=pl.Buffered(1))


def _shift_rows(c, halo_prev, halo_next, first, last):
    n = c.shape[0]
    row = lax.broadcasted_iota(jnp.int32, c.shape, 0)
    lane_row = lax.broadcasted_iota(jnp.int32, (1, c.shape[1]), 1)
    is_first = (lane_row * 0 + first.astype(jnp.int32)) > 0
    is_last = (lane_row * 0 + last.astype(jnp.int32)) > 0
    prev_row = jnp.where(is_first, 0.0, halo_prev[halo_prev.shape[0] - 1:, :])
    next_row = jnp.where(is_last, 0.0, halo_next[0:1, :])
    prev = jnp.where(row == 0, prev_row, pltpu.roll(c, 1, 0))
    nxt = jnp.where(row == n - 1, next_row, pltpu.roll(c, n - 1, 0))
    return prev, nxt


def _halo_specs(ts, width, n_seq_rows, rows):
    r = ts // rows
    last_blk = n_seq_rows // rows - 1
    prev = pl.BlockSpec((None, rows, width), lambda b, i: (b, jnp.maximum(i * r - 1, 0), 0))
    nxt = pl.BlockSpec((None, rows, width), lambda b, i: (b, jnp.minimum((i + 1) * r, last_blk), 0))
    return prev, nxt


def _inproj_kernel(x_ref, xp_ref, xn_ref, g_ref, w_ref, mup_ref, mun_ref, w0_ref, w2_ref, a0_ref, a2_ref, g2_ref,
                   kk_w_ref, ka_ref, rk_ref, j_ref,
                   r_ref, v_ref, kk_ref, gate_ref, bonus_ref, lw_ref, kd_ref, kka_ref, cv_ref, at_ref, sg_ref):
    i = pl.program_id(1)

    def normed(x):
        return (x * lax.rsqrt(jnp.mean(x * x, axis=-1, keepdims=True) + NORM_EPS) * g_ref[...]).astype(BF16)

    h = normed(x_ref[...])
    o0, o1, o2 = RW_COLS, RW_COLS + CV_COLS, RW_COLS + CV_COLS + AT_COLS
    dm = D_MODEL

    def gate_cols(n):
        sg_ref[:, n * dm:(n + 1) * dm] = _sigmoid(_dot(h, w_ref[:, o2 + n * dm:o2 + (n + 1) * dm])).astype(BF16)

    c = _dot(h, w_ref[:, 0:o0])
    c_halo = _dot(normed(jnp.concatenate([xp_ref[...], xn_ref[...]], axis=0)), w_ref[:, 0:o0])
    cv_ref[...] = _dot(h, w_ref[:, o0:o1]).astype(BF16)
    prev, nxt = _shift_rows(c, c_halo[0:SUBLANES], c_halo[SUBLANES:2 * SUBLANES], i == 0, i == pl.num_programs(1) - 1)
    z = c + mup_ref[...] * (prev - c) + mun_ref[...] * (nxt - c)
    w = RW_WIDTH
    r, k, v = z[:, 0:w], z[:, w:2 * w], z[:, 2 * w:3 * w]
    wd = z[:, 3 * w:3 * w + RW_LORA]
    ad = z[:, 3 * w + RW_LORA:3 * w + 2 * RW_LORA]
    gd = z[:, 3 * w + 2 * RW_LORA:3 * w + 3 * RW_LORA]
    j = j_ref[...]
    w_raw = w0_ref[...] + _dot(jnp.tanh(wd).astype(BF16), w2_ref[...])
    lw = -math.exp(-0.5) * _sigmoid(w_raw)
    a = _sigmoid(a0_ref[...] + _dot(ad.astype(BF16), a2_ref[...]))
    gate_ref[...] = _dot(_sigmoid(gd).astype(BF16), g2_ref[...]).astype(BF16)
    gate_cols(0)
    kkr = k * kk_w_ref[...]
    kk = kkr / jnp.maximum(jnp.sqrt(_segsum(kkr * kkr, j)), 1e-12)
    r_ref[...] = r.astype(BF16)
    v_ref[...] = v.astype(BF16)
    kk_ref[...] = kk.astype(BF16)
    at_ref[...] = _dot(h, w_ref[:, o1:o2]).astype(BF16)
    gate_cols(1)
    ksum = jnp.zeros_like(k)
    for d in range(2):
        a_d = a[:, d * w:(d + 1) * w]
        k_d = k * (1.0 + (a_d - 1.0) * ka_ref[...])
        ksum = ksum + k_d
        lw_ref[d] = lw[:, d * w:(d + 1) * w]
        kd_ref[d] = k_d.astype(BF16)
        kka_ref[d] = (kk * a_d).astype(BF16)
    gate_cols(2)
    bonus_ref[...] = (_segsum(r * (0.5 * ksum) * rk_ref[...], j) * v).astype(BF16)


def _inproj(x, g, w, mup, mun, w0, w2bd, a0, a2bd, g2, k_k, k_a, r_k, jones):
    bsz, s, _ = x.shape
    tm = min(TM_INPROJ, s)
    hp, hn = _halo_specs(tm, D_MODEL, s, SUBLANES)
    tok = lambda n: pl.BlockSpec((None, tm, n), lambda b_, i: (b_, i, 0))
    tok2 = pl.BlockSpec((2, None, tm, RW_WIDTH), lambda b_, i: (0, b_, i, 0))
    consts = (g, w, mup, mun, w0, w2bd, a0, a2bd, g2, k_k, k_a, r_k, jones)
    one = lambda n: jax.ShapeDtypeStruct((bsz, s, n), BF16)
    two = lambda dt: jax.ShapeDtypeStruct((2, bsz, s, RW_WIDTH), dt)
    return pl.pallas_call(
        _inproj_kernel,
        grid=(bsz, s // tm),
        in_specs=[tok(D_MODEL), hp, hn] + [_resident(a.shape) for a in consts],
        out_specs=[tok(RW_WIDTH)] * 5 + [tok2] * 3 + [tok(CV_COLS), tok(AT_COLS), tok(GATE_COLS)],
        out_shape=[one(RW_WIDTH)] * 5 + [two(F32), two(BF16), two(BF16)] + [one(CV_COLS), one(AT_COLS), one(GATE_COLS)],
        compiler_params=_cparams("parallel", "arbitrary"),
    )(x, x, x, *consts)


def _conv_kernel(c_ref, hp_ref, hn_ref, w_ref, b_ref, o_ref):
    i = pl.program_id(1)
    c = c_ref[...].astype(F32)
    w = CV_WIDTH
    z = c[:, w:2 * w] * c[:, 2 * w:3 * w]
    hp, hn = hp_ref[...].astype(F32), hn_ref[...].astype(F32)
    zp, zn = _shift_rows(z, hp[:, w:2 * w] * hp[:, 2 * w:3 * w], hn[:, w:2 * w] * hn[:, 2 * w:3 * w],
                         i == 0, i == pl.num_programs(1) - 1)
    conv = b_ref[...] + w_ref[0:1, :] * zp + w_ref[1:2, :] * z + w_ref[2:3, :] * zn
    o_ref[...] = (c[:, 0:w] * conv).astype(BF16)


def _conv(cv, w, b):
    bsz, s, _ = cv.shape
    ts = min(TS_CONV, s)
    hp, hn = _halo_specs(ts, CV_COLS, s, BF16_SUBLANES)
    return pl.pallas_call(
        _conv_kernel,
        grid=(bsz, s // ts),
        in_specs=[pl.BlockSpec((None, ts, CV_COLS), lambda b_, i: (b_, i, 0)), hp, hn,
                  pl.BlockSpec((3, CV_WIDTH), lambda b_, i: (0, 0)),
                  pl.BlockSpec((1, CV_WIDTH), lambda b_, i: (0, 0))],
        out_specs=pl.BlockSpec((None, ts, CV_WIDTH), lambda b_, i: (b_, i, 0)),
        out_shape=jax.ShapeDtypeStruct((bsz, s, CV_WIDTH), BF16),
        compiler_params=_cparams("parallel", "arbitrary"),
    )(cv, cv, cv, w, b)


def _scan_masks():
    c = CHUNK
    t = np.arange(c)
    lower = (t[None, :] <= t[:, None]).astype(np.float32)
    strict_l = (t[None, :] < t[:, None]).astype(np.float32)
    ones = np.ones((c, c), np.float32)
    tri = np.stack([np.concatenate([lower, ones], 0), np.concatenate([lower.T, ones], 0)])
    strict = np.stack([np.tile(strict_l, (1, 2)), np.tile(strict_l.T, (1, 2))])
    incl = np.stack([np.tile(lower, (1, 2)), np.tile(lower.T, (1, 2))])
    return jnp.asarray(tri, BF16), jnp.asarray(strict, F32), jnp.asarray(incl, F32)


def _scan_kernel(rf_ref, vf_ref, kkf_ref, rb_ref, vb_ref, kkb_ref,
                 lwf_ref, kdf_ref, kkaf_ref, lwb_ref, kdb_ref, kkab_ref,
                 tri_ref, strict_ref, incl_ref, yf_ref, yb_ref, st_ref,
                 kq_ref, rq_ref, kdec_ref, bdec_ref, eend_ref, avk_ref, ark_ref, arb_ref, m_ref, tinv_ref):
    c = CHUNK
    n_chunks = rf_ref.shape[0] // c
    n_pairs = RW_WIDTH // LANES

    @pl.when(pl.program_id(1) == 0)
    def _():
        st_ref[...] = jnp.zeros_like(st_ref)

    lane = lax.broadcasted_iota(jnp.int32, (c, LANES), 1)
    row = lax.broadcasted_iota(jnp.int32, (c, LANES), 0)
    head0 = lane < HEAD
    eye = jnp.where((lane & (HEAD - 1)) == row, 1.0, 0.0).astype(F32)
    ri = lax.broadcasted_iota(jnp.int32, (LANES, LANES), 0)
    ci = lax.broadcasted_iota(jnp.int32, (LANES, LANES), 1)
    same_head = jnp.where((ri < HEAD) == (ci < HEAD), 1.0, 0.0).astype(F32)

    def stack(x):
        return jnp.concatenate([jnp.where(head0, x, 0.0), jnp.where(head0, 0.0, x)], axis=0)

    def stack_b(x):
        return stack(x).astype(BF16)

    dir_refs = ((rf_ref, vf_ref, kkf_ref, lwf_ref, kdf_ref, kkaf_ref, yf_ref),
                (rb_ref, vb_ref, kkb_ref, lwb_ref, kdb_ref, kkab_ref, yb_ref))

    def chunk_rows(d, step):
        cidx = step if d == 0 else n_chunks - 1 - step
        start = cidx * c
        return pl.ds(start if isinstance(start, int) else pl.multiple_of(start, c), c)

    def local_stages(step):
        ch = []
        for d, (r_ref, v_ref, kk_ref, lw_ref, kd_ref, kka_ref, y_ref) in enumerate(dir_refs):
            rows = chunk_rows(d, step)
            lw = lw_ref[rows, :]
            hi = lw.astype(BF16)
            rem = lw - hi.astype(F32)
            mid = rem.astype(BF16)
            lo = (rem - mid.astype(F32)).astype(BF16)
            tri = tri_ref[d]
            cum = _dot(tri, hi) + _dot(tri, mid) + _dot(tri, lo)
            lp = cum[0:c]
            e_end = jnp.exp(cum[c:c + SUBLANES])
            e_n = jnp.exp(-lp)
            rq = (r_ref[rows, :] * jnp.exp(lp)).astype(BF16)
            kq = (kk_ref[rows, :] * jnp.exp(lp - lw)).astype(BF16)
            ks = kd_ref[rows, :] * e_n
            bs = kka_ref[rows, :] * e_n
            kq_ref[d, step] = kq
            rq_ref[d, step] = rq
            kdec_ref[d, step] = (ks * e_end[0:1]).astype(BF16)
            bdec_ref[d, step] = (bs * e_end[0:1]).astype(BF16)
            eend_ref[d, step] = e_end
            for p in range(n_pairs):
                sl = slice(p * LANES, (p + 1) * LANES)
                ch.append(dict(d=d, sl=sl, kq=kq[:, sl], rq=rq[:, sl], ks=ks[:, sl], bs=bs[:, sl]))
        n = range(len(ch))
        strict = [strict_ref[x["d"]] for x in ch]
        incl = [incl_ref[x["d"]] for x in ch]
        yield
        lq = [jnp.concatenate([x["kq"], x["rq"]], axis=0) for x in ch]
        s_ks = [stack_b(x["ks"]) for x in ch]
        s_bs = [stack_b(x["bs"]) for x in ch]
        gkb = [_dot_nt(lq[i], jnp.concatenate([s_ks[i], s_bs[i]], axis=0)) for i in n]
        yield
        gk = [x[:, 0:LANES] for x in gkb]
        gb = [x[:, LANES:2 * LANES] for x in gkb]
        m = [-(gb[i][0:c] * strict[i]) for i in n]
        for i, x in enumerate(ch):
            avk_ref[x["d"], step, :, x["sl"]] = (gk[i][0:c] * strict[i]).astype(BF16)
            ark_ref[x["d"], step, :, x["sl"]] = (gk[i][c:2 * c] * incl[i]).astype(BF16)
            arb_ref[x["d"], step, :, x["sl"]] = (gb[i][c:2 * c] * incl[i]).astype(BF16)
            m_ref[x["d"], step, :, x["sl"]] = m[i].astype(BF16)
        t_inv = [eye + x for x in m]
        pw = [_dot(x.astype(BF16), stack_b(x)) for x in m]
        yield
        n_levels = int(math.log2(c)) - 1
        for lvl in range(n_levels):
            pw_bd = [stack_b(x) for x in pw]
            if lvl + 1 < n_levels:
                both = [_dot(jnp.concatenate([t_inv[i].astype(BF16), pw[i].astype(BF16)], axis=0), pw_bd[i])
                        for i in n]
                t_inv = [t_inv[i] + both[i][0:c] for i in n]
                pw = [both[i][c:2 * c] for i in n]
            else:
                t_inv = [t_inv[i] + _dot(t_inv[i].astype(BF16), pw_bd[i]) for i in n]
            yield
        for i, x in enumerate(ch):
            tinv_ref[x["d"], step, :, x["sl"]] = t_inv[i].astype(BF16)

    def state_stages(step):
        ch = []
        for d, (r_ref, v_ref, kk_ref, lw_ref, kd_ref, kka_ref, y_ref) in enumerate(dir_refs):
            rows = chunk_rows(d, step)
            for p in range(n_pairs):
                ch.append(dict(d=d, p=p, rows=rows, sl=slice(p * LANES, (p + 1) * LANES), y_ref=y_ref, v_ref=v_ref))
        n = range(len(ch))
        get = lambda ref: [ref[x["d"], step, :, x["sl"]] for x in ch]
        v = [x["v_ref"][x["rows"], x["sl"]].astype(F32) for x in ch]
        s_vb = [stack_b(x) for x in v]
        st = [st_ref[x["d"], x["p"]] for x in ch]
        stb = [x.astype(BF16) for x in st]
        kq_b, rq_b, a_vk, a_rk, a_rb, t_b, m_b = (get(r) for r in (
            kq_ref, rq_ref, avk_ref, ark_ref, arb_ref, tinv_ref, m_ref))
        from_state = [_dot_nt(jnp.concatenate([kq_b[i], rq_b[i]], axis=0), stb[i]) for i in n]
        from_v = [_dot(jnp.concatenate([a_vk[i], a_rk[i]], axis=0), s_vb[i]) for i in n]
        yield
        rhs = [from_state[i][0:c] + from_v[i][0:c] for i in n]
        u0 = [-_dot(t_b[i], stack_b(rhs[i])) for i in n]
        yield
        res = [_dot(m_b[i], stack_b(u0[i])) - rhs[i] - u0[i] for i in n]
        yield
        u = [u0[i] + _dot(t_b[i], stack_b(res[i])) for i in n]
        yield
        y = [from_state[i][c:2 * c] + from_v[i][c:2 * c] + _dot(a_rb[i], stack_b(u[i])) for i in n]
        vu_t = [jnp.concatenate([v[i], u[i]], axis=0).T.astype(BF16) for i in n]
        kb = [jnp.concatenate([kdec_ref[x["d"], step, :, x["sl"]], bdec_ref[x["d"], step, :, x["sl"]]], axis=0)
              for x in ch]
        upd = [_dot(vu_t[i], kb[i]) * same_head for i in n]
        for i, x in enumerate(ch):
            x["y_ref"][x["rows"], x["sl"]] = y[i]
            st_ref[x["d"], x["p"]] = st[i] * eend_ref[x["d"], step, 0:1, x["sl"]] + upd[i]

    def emit(*gens, order=None):
        live = {i: g for i, g in enumerate(gens)}
        for i in list(order or ()) + [None]:
            while i is None and live:
                for k in list(live):
                    if next(live[k], live) is live:
                        del live[k]
            if i in live and next(live[i], live) is live:
                del live[i]

    emit(local_stages(0))

    def body(i, carry):
        emit(local_stages(i + 1), state_stages(i), order=(0, 1, 0, 0, 1, 0, 0, 1, 0, 1, 0, 1, 0))
        return carry

    lax.fori_loop(0, n_chunks - 1, body, 0)
    emit(state_stages(n_chunks - 1))


def _scan(r, v, kk, lw, kd, kka, masks):
    bsz, s, w = r.shape
    tb = min(TB_SCAN, s)
    nblk = s // tb
    fwd = pl.BlockSpec((None, tb, w), lambda b_, j: (b_, j, 0))
    bwd = pl.BlockSpec((None, tb, w), lambda b_, j: (b_, nblk - 1 - j, 0))
    fwd2 = pl.BlockSpec((None, None, tb, w), lambda b_, j: (0, b_, j, 0))
    bwd2 = pl.BlockSpec((None, None, tb, w), lambda b_, j: (1, b_, nblk - 1 - j, 0))
    out = jax.ShapeDtypeStruct((bsz, s, w), F32)
    per_chunk = lambda dt: pltpu.VMEM((2, tb // CHUNK, CHUNK, w), dt)
    return pl.pallas_call(
        _scan_kernel,
        grid=(bsz, nblk),
        in_specs=[fwd, fwd, fwd, bwd, bwd, bwd, fwd2, fwd2, fwd2, bwd2, bwd2, bwd2]
        + [_resident(a.shape) for a in masks],
        out_specs=[fwd, bwd],
        out_shape=[out, out],
        scratch_shapes=[pltpu.VMEM((2, w // LANES, LANES, LANES), F32), per_chunk(BF16), per_chunk(BF16),
                        per_chunk(BF16), per_chunk(BF16), pltpu.VMEM((2, tb // CHUNK, SUBLANES, w), F32)]
        + [per_chunk(BF16)] * 5,
        compiler_params=_cparams("parallel", "arbitrary"),
    )(r, v, kk, r, v, kk, lw, kd, kka, lw, kd, kka, *masks)


def _rope_partner(x):
    n = x.shape[1]
    lane = lax.broadcasted_iota(jnp.int32, x.shape, 1)
    return jnp.where((lane & 31) < 16, pltpu.roll(x, n - 16, 1), pltpu.roll(x, 16, 1))


def _attn_kernel(q_ref, k_ref, v_ref, cq_ref, sq_ref, ck_ref, sk_ref, qg_ref, kg_ref, jq_ref, jk_ref,
                 o_ref, kd_ref, vd_ref):
    s_len = k_ref.shape[0]
    tq = q_ref.shape[0]
    tk = min(TK_PREP, s_len)

    @pl.when(pl.program_id(1) == 0)
    def _():
        def body(i, carry):
            rows = pl.ds(pl.multiple_of(i * tk, tk), tk)
            k = k_ref[rows, :].astype(F32)
            kn = k * lax.rsqrt(_segsum(k * k, jk_ref[...]) * (1.0 / HEAD) + NORM_EPS) * kg_ref[...]
            kr = kn * ck_ref[rows, :] + _rope_partner(kn) * sk_ref[rows, :]
            v = v_ref[rows, :].astype(F32)
            low = lax.broadcasted_iota(jnp.int32, kr.shape, 1) < HEAD
            ksw = pltpu.roll(kr, HEAD, 1)
            kd_ref[0, rows, :] = jnp.where(low, kr, ksw).astype(BF16)
            kd_ref[1, rows, :] = jnp.where(low, ksw, kr).astype(BF16)
            vd_ref[0, rows, :] = jnp.where(low, v, 1.0).astype(BF16)
            vd_ref[1, rows, :] = jnp.where(low, pltpu.roll(v, HEAD, 1), 1.0).astype(BF16)
            return carry
        lax.fori_loop(0, s_len // tk, body, 0)

    q = q_ref[...].astype(F32)
    qn = q * lax.rsqrt(_segsum(q * q, jq_ref[...]) * (1.0 / HEAD) + NORM_EPS) * qg_ref[...]
    qr = qn * cq_ref[...] + _rope_partner(qn) * sq_ref[...]
    low = lax.broadcasted_iota(jnp.int32, (tq, LANES), 1) < HEAD
    pairs_per_kv = AT_Q_HEADS // AT_KV_HEADS // 2
    tkc = min(TK_ATTN, s_len)
    n_kc = s_len // tkc
    qs = []
    for g in range(AT_KV_HEADS):
        parts = []
        for jj in range(pairs_per_kv):
            qp = qr[:, (g * pairs_per_kv + jj) * LANES:(g * pairs_per_kv + jj + 1) * LANES]
            parts += [jnp.where(low, qp, 0.0), jnp.where(low, 0.0, qp)]
        qs.append(jnp.concatenate(parts, axis=0).astype(BF16))
    streams = [(g, kc) for g in range(AT_KV_HEADS) for kc in range(n_kc)]
    scores = lambda g, kc: _dot_nt(qs[g], kd_ref[g, kc * tkc:(kc + 1) * tkc, :])
    partial, s_next = {}, scores(*streams[0])
    for i, (g, kc) in enumerate(streams):
        s = s_next
        if i + 1 < len(streams):
            s_next = scores(*streams[i + 1])
        m = jnp.max(s, axis=-1, keepdims=True)
        p = jnp.exp2(s - m).astype(BF16)
        partial[g, kc] = (m, _dot(p, vd_ref[g, kc * tkc:(kc + 1) * tkc, :]))
    for g in range(AT_KV_HEADS):
        m_all = partial[g, 0][0]
        for kc in range(1, n_kc):
            m_all = jnp.maximum(m_all, partial[g, kc][0])
        o = sum(jnp.exp2(partial[g, kc][0] - m_all) * partial[g, kc][1] for kc in range(n_kc))
        o = o / pltpu.roll(o, HEAD, 1)
        for jj in range(pairs_per_kv):
            even = o[(2 * jj) * tq:(2 * jj + 1) * tq]
            odd = pltpu.roll(o[(2 * jj + 1) * tq:(2 * jj + 2) * tq], HEAD, 1)
            pair = jnp.where(low, even, odd)
            col = (g * pairs_per_kv + jj) * LANES
            o_ref[:, col:col + LANES] = pair.astype(BF16)


def _attn(at, tabs, q_g, k_g, jones):
    bsz, s, _ = at.shape
    tq = min(TQ_ATTN, s)
    cq, sq, ck, sk = tabs
    qw = AT_Q_HEADS * HEAD
    kvw = AT_KV_HEADS * HEAD
    cst = lambda a: pl.BlockSpec(a.shape, lambda b_, i: (0,) * a.ndim)
    jk = jones[0:kvw, 0:kvw]
    return pl.pallas_call(
        _attn_kernel,
        grid=(bsz, s // tq),
        in_specs=[pl.BlockSpec((None, tq, qw), lambda b_, i: (b_, i, 0)),
                  pl.BlockSpec((None, s, kvw), lambda b_, i: (b_, 0, qw // kvw)),
                  pl.BlockSpec((None, s, kvw), lambda b_, i: (b_, 0, qw // kvw + 1)),
                  pl.BlockSpec((tq, qw), lambda b_, i: (i, 0)),
                  pl.BlockSpec((tq, qw), lambda b_, i: (i, 0)),
                  cst(ck), cst(sk), cst(q_g), cst(k_g), cst(jones), cst(jk)],
        out_specs=pl.BlockSpec((None, tq, qw), lambda b_, i: (b_, i, 0)),
        out_shape=jax.ShapeDtypeStruct((bsz, s, qw), BF16),
        scratch_shapes=[pltpu.VMEM((AT_KV_HEADS, s, LANES), BF16), pltpu.VMEM((AT_KV_HEADS, s, LANES), BF16)],
        compiler_params=_cparams("parallel", "arbitrary"),
    )(at, at, at, cq, sq, ck, sk, q_g, k_g, jones, jk)


def _rope_tables(s):
    rows = s // GRID_W
    pos_r = jnp.repeat(jnp.arange(rows), GRID_W).astype(F32)
    pos_c = jnp.tile(jnp.arange(GRID_W), rows).astype(F32)
    half = HEAD // 2
    inv = ROPE_THETA ** (-jnp.arange(0, half, 2, dtype=F32) / half)
    ang_r = pos_r[:, None] * inv
    ang_c = pos_c[:, None] * inv
    cos = jnp.concatenate([jnp.cos(ang_r)] * 2 + [jnp.cos(ang_c)] * 2, axis=-1)
    sin = jnp.concatenate([-jnp.sin(ang_r), jnp.sin(ang_r), -jnp.sin(ang_c), jnp.sin(ang_c)], axis=-1)
    scale = HEAD ** -0.5 * math.log2(math.e)
    return (jnp.tile(cos, (1, AT_Q_HEADS)) * scale, jnp.tile(sin, (1, AT_Q_HEADS)) * scale,
            jnp.tile(cos, (1, AT_KV_HEADS)), jnp.tile(sin, (1, AT_KV_HEADS)))


def _merge_kernel(y0_ref, y1_ref, g_ref, bonus_ref, lng_ref, lnb_ref, j_ref, ycv_ref, yat_ref, sg_ref, x_ref,
                  wb_ref, wo_ref, o_ref):
    j = j_ref[...]
    y = y0_ref[...] + y1_ref[...]
    mu = _segsum(y, j) * (1.0 / HEAD)
    dev = y - mu
    var = _segsum(dev * dev, j) * (1.0 / HEAD)
    yn = dev * lax.rsqrt(var + RW_LN_EPS) * lng_ref[...] + lnb_ref[...]
    y_rw = ((yn + bonus_ref[...]) * g_ref[...]).astype(BF16)
    dm = D_MODEL
    merged = (sg_ref[:, 0:dm].astype(F32) * _dot(y_rw, wb_ref[0])
              + sg_ref[:, dm:2 * dm].astype(F32) * _dot(ycv_ref[...], wb_ref[1])
              + sg_ref[:, 2 * dm:3 * dm].astype(F32) * _dot(yat_ref[...], wb_ref[2]))
    o_ref[...] = x_ref[...] + _dot(merged.astype(BF16), wo_ref[...])


def _merge(yf, yb, g, bonus, ln_g, ln_b, jones, y_cv, y_at, sg, x, w_branch, w_out):
    t = x.shape[0]
    tm = TM_MERGE
    row = lambda n: pl.BlockSpec((tm, n), lambda i: (i, 0))
    return pl.pallas_call(
        _merge_kernel,
        grid=(t // tm,),
        in_specs=[row(RW_WIDTH), row(RW_WIDTH), row(RW_WIDTH), row(RW_WIDTH),
                  _resident(ln_g.shape), _resident(ln_b.shape), _resident(jones.shape),
                  row(CV_WIDTH), row(AT_Q_HEADS * HEAD), row(GATE_COLS), row(D_MODEL),
                  _resident(w_branch.shape), _resident(w_out.shape)],
        out_specs=row(D_MODEL),
        out_shape=jax.ShapeDtypeStruct((t, D_MODEL), F32),
        compiler_params=_cparams("parallel"),
    )(yf, yb, g, bonus, ln_g, ln_b, jones, y_cv, y_at, sg, x, w_branch, w_out)


def _mlp_kernel(x_ref, g_ref, w1_ref, w2_ref, o_ref):
    x = x_ref[...]
    h = (x * lax.rsqrt(jnp.mean(x * x, axis=-1, keepdims=True) + NORM_EPS) * g_ref[...]).astype(BF16)
    acc = x
    for c in range(D_FF // FF_CHUNK):
        cols = slice(c * FF_CHUNK, (c + 1) * FF_CHUNK)
        u = jnp.maximum(_dot(h, w1_ref[:, cols]), 0.0)
        acc = acc + _dot((u * u).astype(BF16), w2_ref[cols, :])
    o_ref[...] = acc


def _mlp(x, g, w1, w2):
    t = x.shape[0]
    tm = TM_MLP
    row = pl.BlockSpec((tm, D_MODEL), lambda i: (i, 0))
    return pl.pallas_call(
        _mlp_kernel,
        grid=(t // tm,),
        in_specs=[row, _resident(g.shape), _resident(w1.shape), _resident(w2.shape)],
        out_specs=row,
        out_shape=jax.ShapeDtypeStruct((t, D_MODEL), F32),
        compiler_params=_cparams("parallel"),
    )(x, g, w1, w2)


def _block_diag2(w):
    z = jnp.zeros_like(w[0])
    return jnp.concatenate([jnp.concatenate([w[0], z], axis=1), jnp.concatenate([z, w[1]], axis=1)], axis=0)


def kernel(x, norm1_g, w_in, rw_mu_prev, rw_mu_next, rw_w0, rw_w2, rw_a0, rw_a2, rw_g2, rw_k_k, rw_k_a, rw_r_k,
           rw_ln_g, rw_ln_b, cv_w, cv_b, at_q_g, at_k_g, w_branch, w_out, norm2_g, w_mlp1, w_mlp2):
    bsz, s, dm = x.shape
    assert dm == D_MODEL and s % GRID_W == 0
    for tile in (TM_INPROJ, TB_SCAN, TS_CONV, TQ_ATTN, TK_PREP):
        assert s % min(tile, s) == 0 and min(tile, s) % CHUNK == 0
    t = bsz * s
    tabs = _rope_tables(s)
    masks = _scan_masks()
    jones = jnp.asarray(np.kron(np.eye(RW_WIDTH // HEAD, dtype=np.float32), np.ones((HEAD, HEAD), np.float32)), BF16)
    row = lambda a: a.reshape(1, -1)
    xf = x.reshape(t, dm)
    for l in range(DEPTH):
        r, v, kk, g, bonus, lw, kd, kka, cv, at, sg = _inproj(
            xf.reshape(bsz, s, dm), row(norm1_g[l]), w_in[l].astype(BF16), row(rw_mu_prev[l]), row(rw_mu_next[l]),
            row(rw_w0[l]), _block_diag2(rw_w2[l]).astype(BF16), row(rw_a0[l]), _block_diag2(rw_a2[l]).astype(BF16),
            rw_g2[l].astype(BF16), row(rw_k_k[l]), row(rw_k_a[l]), row(rw_r_k[l]), jones)
        yf, yb = _scan(r, v, kk, lw, kd, kka, masks)
        y_cv = _conv(cv, cv_w[l], row(cv_b[l]))
        y_at = _attn(at, tabs, row(jnp.tile(at_q_g[l], AT_Q_HEADS)),
                     row(jnp.tile(at_k_g[l], AT_KV_HEADS)), jones)
        xf = _merge(yf.reshape(t, RW_WIDTH), yb.reshape(t, RW_WIDTH), g.reshape(t, RW_WIDTH),
                    bonus.reshape(t, RW_WIDTH),
                    row(rw_ln_g[l]), row(rw_ln_b[l]), jones, y_cv.reshape(t, CV_WIDTH),
                    y_at.reshape(t, AT_Q_HEADS * HEAD), sg.reshape(t, GATE_COLS), xf, w_branch[l].astype(BF16), w_out[l].astype(BF16))
        xf = _mlp(xf, row(norm2_g[l]), w_mlp1[l].astype(BF16), w_mlp2[l].astype(BF16))
    return xf.reshape(bsz, s, dm)
```

```python
import math

import numpy as np
import jax
import jax.numpy as jnp
from jax import lax
from jax.experimental import pallas as pl
from jax.experimental.pallas import tpu as pltpu

F32 = jnp.float32
BF16 = jnp.bfloat16

D_MODEL = 1024
DEPTH = 4
GRID_W = 64
NORM_EPS = 1e-6
HEAD = 64
RW_WIDTH = 512
RW_LORA = 128
RW_LN_EPS = 64e-5
RW_COLS = 3 * RW_WIDTH + 3 * RW_LORA
CV_WIDTH = 512
CV_COLS = 3 * CV_WIDTH
AT_Q_HEADS = 8
AT_KV_HEADS = 2
AT_COLS = (AT_Q_HEADS + 2 * AT_KV_HEADS) * HEAD
GATE_COLS = 3 * D_MODEL
ROPE_THETA = 10000.0
D_FF = 4 * D_MODEL

LANES = 128
SUBLANES = 8
VMEM_LIMIT_BYTES = 56 * 1024 * 1024
CHUNK = 64
TM_INPROJ = 256
TB_SCAN = 512
TQ_ATTN = 128
TK_ATTN = 2048
TK_PREP = 512
TM_MERGE = 512
TM_MLP = 512
FF_CHUNK = 1024


def _cparams(*sem):
    return pltpu.CompilerParams(dimension_semantics=sem, vmem_limit_bytes=VMEM_LIMIT_BYTES)


def _dot(a, b):
    return jnp.dot(a, b, preferred_element_type=F32)


def _dot_nt(a, b):
    return lax.dot_general(a, b, (((1,), (1,)), ((), ())), preferred_element_type=F32)


def _split2(x):
    hi = x.astype(BF16)
    lo = (x - hi.astype(F32)).astype(BF16)
    return hi, lo


def _segsum(x, j):
    hi, lo = _split2(x)
    return _dot(hi, j) + _dot(lo, j)


def _sigmoid(x):
    return 1.0 / (1.0 + jnp.exp(-x))


def _resident(shape):
    nd = len(shape)
    return pl.BlockSpec(shape, lambda *_: (0,) * nd, pipeline_mode=pl.Buffered(1))


def _shift_rows(c, halo_prev, halo_next, first, last):
    n = c.shape[0]
    row = lax.broadcasted_iota(jnp.int32, c.shape, 0)
    lane_row = lax.broadcasted_iota(jnp.int32, (1, c.shape[1]), 1)
    is_first = (lane_row * 0 + first.astype(jnp.int32)) > 0
    is_last = (lane_row * 0 + last.astype(jnp.int32)) > 0
    prev_row = jnp.where(is_first, 0.0, halo_prev[halo_prev.shape[0] - 1:, :])
    next_row = jnp.where(is_last, 0.0, halo_next[0:1, :])
    prev = jnp.where(row == 0, prev_row, pltpu.roll(c, 1, 0))
    nxt = jnp.where(row == n - 1, next_row, pltpu.roll(c, n - 1, 0))
    return prev, nxt


def _halo_specs(ts, width, n_seq_rows, rows):
    r = ts // rows
    last_blk = n_seq_rows // rows - 1
    prev = pl.BlockSpec((None, rows, width), lambda b, i: (b, jnp.maximum(i * r - 1, 0), 0))
    nxt = pl.BlockSpec((None, rows, width), lambda b, i: (b, jnp.minimum((i + 1) * r, last_blk), 0))
    return prev, nxt


def _inproj_kernel(x_ref, xp_ref, xn_ref, g_ref, w_ref, mup_ref, mun_ref, w0_ref, w2_ref, a0_ref, a2_ref, g2_ref,
                   kk_w_ref, ka_ref, rk_ref, j_ref, cvw_ref, cvb_ref,
                   r_ref, v_ref, kk_ref, gate_ref, bonus_ref, lw_ref, kd_ref, kka_ref, ycv_ref, at_ref, sg_ref):
    i = pl.program_id(1)

    def normed(x):
        return (x * lax.rsqrt(jnp.mean(x * x, axis=-1, keepdims=True) + NORM_EPS) * g_ref[...]).astype(BF16)

    h = normed(x_ref[...])
    o0, o1, o2 = RW_COLS, RW_COLS + CV_COLS, RW_COLS + CV_COLS + AT_COLS
    dm = D_MODEL

    def gate_cols(n):
        sg_ref[:, n * dm:(n + 1) * dm] = _sigmoid(_dot(h, w_ref[:, o2 + n * dm:o2 + (n + 1) * dm])).astype(BF16)

    c = _dot(h, w_ref[:, 0:o0])
    h_halo = normed(jnp.concatenate([xp_ref[...], xn_ref[...]], axis=0))
    c_halo = _dot(h_halo, w_ref[:, 0:o0])
    first, last = i == 0, i == pl.num_programs(1) - 1
    cw = CV_WIDTH
    cv = _dot(h, w_ref[:, o0:o1])
    cv_halo = _dot(h_halo, w_ref[:, o0 + cw:o1])
    zc = cv[:, cw:2 * cw] * cv[:, 2 * cw:3 * cw]
    zc_halo = cv_halo[:, 0:cw] * cv_halo[:, cw:2 * cw]
    zc_prev, zc_next = _shift_rows(zc, zc_halo[0:SUBLANES], zc_halo[SUBLANES:2 * SUBLANES], first, last)
    conv = cvb_ref[...] + cvw_ref[0:1, :] * zc_prev + cvw_ref[1:2, :] * zc + cvw_ref[2:3, :] * zc_next
    ycv_ref[...] = (cv[:, 0:cw] * conv).astype(BF16)
    prev, nxt = _shift_rows(c, c_halo[0:SUBLANES], c_halo[SUBLANES:2 * SUBLANES], first, last)
    z = c + mup_ref[...] * (prev - c) + mun_ref[...] * (nxt - c)
    w = RW_WIDTH
    r, k, v = z[:, 0:w], z[:, w:2 * w], z[:, 2 * w:3 * w]
    wd = z[:, 3 * w:3 * w + RW_LORA]
    ad = z[:, 3 * w + RW_LORA:3 * w + 2 * RW_LORA]
    gd = z[:, 3 * w + 2 * RW_LORA:3 * w + 3 * RW_LORA]
    j = j_ref[...]
    w_raw = w0_ref[...] + _dot(jnp.tanh(wd).astype(BF16), w2_ref[...])
    lw = -math.exp(-0.5) * _sigmoid(w_raw)
    a = _sigmoid(a0_ref[...] + _dot(ad.astype(BF16), a2_ref[...]))
    gate_ref[...] = _dot(_sigmoid(gd).astype(BF16), g2_ref[...]).astype(BF16)
    gate_cols(0)
    kkr = k * kk_w_ref[...]
    kk = kkr / jnp.maximum(jnp.sqrt(_segsum(kkr * kkr, j)), 1e-12)
    r_ref[...] = r.astype(BF16)
    v_ref[...] = v.astype(BF16)
    kk_ref[...] = kk.astype(BF16)
    at_ref[...] = _dot(h, w_ref[:, o1:o2]).astype(BF16)
    gate_cols(1)
    ksum = jnp.zeros_like(k)
    for d in range(2):
        a_d = a[:, d * w:(d + 1) * w]
        k_d = k * (1.0 + (a_d - 1.0) * ka_ref[...])
        ksum = ksum + k_d
        lw_ref[d] = lw[:, d * w:(d + 1) * w]
        kd_ref[d] = k_d.astype(BF16)
        kka_ref[d] = (kk * a_d).astype(BF16)
    gate_cols(2)
    bonus_ref[...] = (_segsum(r * (0.5 * ksum) * rk_ref[...], j) * v).astype(BF16)


def _inproj(x, g, w, mup, mun, w0, w2bd, a0, a2bd, g2, k_k, k_a, r_k, jones, cv_w, cv_b):
    bsz, s, _ = x.shape
    tm = min(TM_INPROJ, s)
    hp, hn = _halo_specs(tm, D_MODEL, s, SUBLANES)
    tok = lambda n: pl.BlockSpec((None, tm, n), lambda b_, i: (b_, i, 0))
    tok2 = pl.BlockSpec((2, None, tm, RW_WIDTH), lambda b_, i: (0, b_, i, 0))
    consts = (g, w, mup, mun, w0, w2bd, a0, a2bd, g2, k_k, k_a, r_k, jones, cv_w, cv_b)
    one = lambda n: jax.ShapeDtypeStruct((bsz, s, n), BF16)
    two = lambda dt: jax.ShapeDtypeStruct((2, bsz, s, RW_WIDTH), dt)
    return pl.pallas_call(
        _inproj_kernel,
        grid=(bsz, s // tm),
        in_specs=[tok(D_MODEL), hp, hn] + [_resident(a.shape) for a in consts],
        out_specs=[tok(RW_WIDTH)] * 5 + [tok2] * 3 + [tok(CV_WIDTH), tok(AT_COLS), tok(GATE_COLS)],
        out_shape=[one(RW_WIDTH)] * 5 + [two(F32), two(BF16), two(BF16)] + [one(CV_WIDTH), one(AT_COLS), one(GATE_COLS)],
        compiler_params=_cparams("parallel", "arbitrary"),
    )(x, x, x, *consts)


def _scan_masks():
    c = CHUNK
    t = np.arange(c)
    lower = (t[None, :] <= t[:, None]).astype(np.float32)
    strict_l = (t[None, :] < t[:, None]).astype(np.float32)
    ones = np.ones((c, c), np.float32)
    tri = np.stack([np.concatenate([lower, ones], 0), np.concatenate([lower.T, ones], 0)])
    strict = np.stack([np.tile(strict_l, (1, 2)), np.tile(strict_l.T, (1, 2))])
    incl = np.stack([np.tile(lower, (1, 2)), np.tile(lower.T, (1, 2))])
    return jnp.asarray(tri, BF16), jnp.asarray(strict, F32), jnp.asarray(incl, F32)


def _scan_kernel(rf_ref, vf_ref, kkf_ref, rb_ref, vb_ref, kkb_ref,
                 lwf_ref, kdf_ref, kkaf_ref, lwb_ref, kdb_ref, kkab_ref,
                 tri_ref, strict_ref, incl_ref, yf_ref, yb_ref, st_ref,
                 kq_ref, rq_ref, kdec_ref, bdec_ref, eend_ref, avk_ref, ark_ref, arb_ref, m_ref, tinv_ref):
    c = CHUNK
    n_chunks = rf_ref.shape[0] // c
    n_pairs = RW_WIDTH // LANES

    @pl.when(pl.program_id(1) == 0)
    def _():
        st_ref[...] = jnp.zeros_like(st_ref)

    lane = lax.broadcasted_iota(jnp.int32, (c, LANES), 1)
    row = lax.broadcasted_iota(jnp.int32, (c, LANES), 0)
    head0 = lane < HEAD
    eye = jnp.where((lane & (HEAD - 1)) == row, 1.0, 0.0).astype(F32)
    ri = lax.broadcasted_iota(jnp.int32, (LANES, LANES), 0)
    ci = lax.broadcasted_iota(jnp.int32, (LANES, LANES), 1)
    same_head = jnp.where((ri < HEAD) == (ci < HEAD), 1.0, 0.0).astype(F32)

    def stack(x):
        return jnp.concatenate([jnp.where(head0, x, 0.0), jnp.where(head0, 0.0, x)], axis=0)

    def stack_b(x):
        return stack(x).astype(BF16)

    dir_refs = ((rf_ref, vf_ref, kkf_ref, lwf_ref, kdf_ref, kkaf_ref, yf_ref),
                (rb_ref, vb_ref, kkb_ref, lwb_ref, kdb_ref, kkab_ref, yb_ref))

    def chunk_rows(d, step):
        cidx = step if d == 0 else n_chunks - 1 - step
        start = cidx * c
        return pl.ds(start if isinstance(start, int) else pl.multiple_of(start, c), c)

    def local_stages(step):
        ch = []
        for d, (r_ref, v_ref, kk_ref, lw_ref, kd_ref, kka_ref, y_ref) in enumerate(dir_refs):
            rows = chunk_rows(d, step)
            lw = lw_ref[rows, :]
            hi = lw.astype(BF16)
            rem = lw - hi.astype(F32)
            mid = rem.astype(BF16)
            lo = (rem - mid.astype(F32)).astype(BF16)
            tri = tri_ref[d]
            cum = _dot(tri, hi) + _dot(tri, mid) + _dot(tri, lo)
            lp = cum[0:c]
            e_end = jnp.exp(cum[c:c + SUBLANES])
            e_n = jnp.exp(-lp)
            rq = (r_ref[rows, :] * jnp.exp(lp)).astype(BF16)
            kq = (kk_ref[rows, :] * jnp.exp(lp - lw)).astype(BF16)
            ks = kd_ref[rows, :] * e_n
            bs = kka_ref[rows, :] * e_n
            kq_ref[d, step] = kq
            rq_ref[d, step] = rq
            kdec_ref[d, step] = (ks * e_end[0:1]).astype(BF16)
            bdec_ref[d, step] = (bs * e_end[0:1]).astype(BF16)
            eend_ref[d, step] = e_end
            for p in range(n_pairs):
                sl = slice(p * LANES, (p + 1) * LANES)
                ch.append(dict(d=d, sl=sl, kq=kq[:, sl], rq=rq[:, sl], ks=ks[:, sl], bs=bs[:, sl]))
        n = range(len(ch))
        strict = [strict_ref[x["d"]] for x in ch]
        incl = [incl_ref[x["d"]] for x in ch]
        yield
        lq = [jnp.concatenate([x["kq"], x["rq"]], axis=0) for x in ch]
        s_ks = [stack_b(x["ks"]) for x in ch]
        s_bs = [stack_b(x["bs"]) for x in ch]
        gkb = [_dot_nt(lq[i], jnp.concatenate([s_ks[i], s_bs[i]], axis=0)) for i in n]
        yield
        gk = [x[:, 0:LANES] for x in gkb]
        gb = [x[:, LANES:2 * LANES] for x in gkb]
        m = [-(gb[i][0:c] * strict[i]) for i in n]
        for i, x in enumerate(ch):
            avk_ref[x["d"], step, :, x["sl"]] = (gk[i][0:c] * strict[i]).astype(BF16)
            ark_ref[x["d"], step, :, x["sl"]] = (gk[i][c:2 * c] * incl[i]).astype(BF16)
            arb_ref[x["d"], step, :, x["sl"]] = (gb[i][c:2 * c] * incl[i]).astype(BF16)
            m_ref[x["d"], step, :, x["sl"]] = m[i].astype(BF16)
        t_inv = [eye + x for x in m]
        pw = [_dot(x.astype(BF16), stack_b(x)) for x in m]
        yield
        n_levels = int(math.log2(c)) - 1
        for lvl in range(n_levels):
            pw_bd = [stack_b(x) for x in pw]
            if lvl + 1 < n_levels:
                both = [_dot(jnp.concatenate([t_inv[i].astype(BF16), pw[i].astype(BF16)], axis=0), pw_bd[i])
                        for i in n]
                t_inv = [t_inv[i] + both[i][0:c] for i in n]
                pw = [both[i][c:2 * c] for i in n]
            else:
                t_inv = [t_inv[i] + _dot(t_inv[i].astype(BF16), pw_bd[i]) for i in n]
            yield
        for i, x in enumerate(ch):
            tinv_ref[x["d"], step, :, x["sl"]] = t_inv[i].astype(BF16)

    def state_stages(step):
        ch = []
        for d, (r_ref, v_ref, kk_ref, lw_ref, kd_ref, kka_ref, y_ref) in enumerate(dir_refs):
            rows = chunk_rows(d, step)
            for p in range(n_pairs):
                ch.append(dict(d=d, p=p, rows=rows, sl=slice(p * LANES, (p + 1) * LANES), y_ref=y_ref, v_ref=v_ref))
        n = range(len(ch))
        get = lambda ref: [ref[x["d"], step, :, x["sl"]] for x in ch]
        v = [x["v_ref"][x["rows"], x["sl"]].astype(F32) for x in ch]
        s_vb = [stack_b(x) for x in v]
        st = [st_ref[x["d"], x["p"]] for x in ch]
        stb = [x.astype(BF16) for x in st]
        kq_b, rq_b, a_vk, a_rk, a_rb, t_b, m_b = (get(r) for r in (
            kq_ref, rq_ref, avk_ref, ark_ref, arb_ref, tinv_ref, m_ref))
        from_state = [_dot_nt(jnp.concatenate([kq_b[i], rq_b[i]], axis=0), stb[i]) for i in n]
        from_v = [_dot(jnp.concatenate([a_vk[i], a_rk[i]], axis=0), s_vb[i]) for i in n]
        yield
        rhs = [from_state[i][0:c] + from_v[i][0:c] for i in n]
        u0 = [-_dot(t_b[i], stack_b(rhs[i])) for i in n]
        yield
        res = [_dot(m_b[i], stack_b(u0[i])) - rhs[i] - u0[i] for i in n]
        yield
        u = [u0[i] + _dot(t_b[i], stack_b(res[i])) for i in n]
        yield
        y = [from_state[i][c:2 * c] + from_v[i][c:2 * c] + _dot(a_rb[i], stack_b(u[i])) for i in n]
        vu_t = [jnp.concatenate([v[i], u[i]], axis=0).T.astype(BF16) for i in n]
        kb = [jnp.concatenate([kdec_ref[x["d"], step, :, x["sl"]], bdec_ref[x["d"], step, :, x["sl"]]], axis=0)
              for x in ch]
        upd = [_dot(vu_t[i], kb[i]) * same_head for i in n]
        for i, x in enumerate(ch):
            x["y_ref"][x["rows"], x["sl"]] = y[i]
            st_ref[x["d"], x["p"]] = st[i] * eend_ref[x["d"], step, 0:1, x["sl"]] + upd[i]

    def emit(*gens, order=None):
        live = {i: g for i, g in enumerate(gens)}
        for i in list(order or ()) + [None]:
            while i is None and live:
                for k in list(live):
                    if next(live[k], live) is live:
                        del live[k]
            if i in live and next(live[i], live) is live:
                del live[i]

    emit(local_stages(0))

    def body(i, carry):
        emit(local_stages(i + 1), state_stages(i), order=(0, 1, 0, 0, 1, 0, 0, 1, 0, 1, 0, 1, 0))
        return carry

    lax.fori_loop(0, n_chunks - 1, body, 0)
    emit(state_stages(n_chunks - 1))


def _scan(r, v, kk, lw, kd, kka, masks):
    bsz, s, w = r.shape
    tb = min(TB_SCAN, s)
    nblk = s // tb
    fwd = pl.BlockSpec((None, tb, w), lambda b_, j: (b_, j, 0))
    bwd = pl.BlockSpec((None, tb, w), lambda b_, j: (b_, nblk - 1 - j, 0))
    fwd2 = pl.BlockSpec((None, None, tb, w), lambda b_, j: (0, b_, j, 0))
    bwd2 = pl.BlockSpec((None, None, tb, w), lambda b_, j: (1, b_, nblk - 1 - j, 0))
    out = jax.ShapeDtypeStruct((bsz, s, w), F32)
    per_chunk = lambda dt: pltpu.VMEM((2, tb // CHUNK, CHUNK, w), dt)
    return pl.pallas_call(
        _scan_kernel,
        grid=(bsz, nblk),
        in_specs=[fwd, fwd, fwd, bwd, bwd, bwd, fwd2, fwd2, fwd2, bwd2, bwd2, bwd2]
        + [_resident(a.shape) for a in masks],
        out_specs=[fwd, bwd],
        out_shape=[out, out],
        scratch_shapes=[pltpu.VMEM((2, w // LANES, LANES, LANES), F32), per_chunk(BF16), per_chunk(BF16),
                        per_chunk(BF16), per_chunk(BF16), pltpu.VMEM((2, tb // CHUNK, SUBLANES, w), F32)]
        + [per_chunk(BF16)] * 5,
        compiler_params=_cparams("parallel", "arbitrary"),
    )(r, v, kk, r, v, kk, lw, kd, kka, lw, kd, kka, *masks)


def _rope_partner(x):
    n = x.shape[1]
    lane = lax.broadcasted_iota(jnp.int32, x.shape, 1)
    return jnp.where((lane & 31) < 16, pltpu.roll(x, n - 16, 1), pltpu.roll(x, 16, 1))


def _attn_kernel(q_ref, k_ref, v_ref, cq_ref, sq_ref, ck_ref, sk_ref, qg_ref, kg_ref, jq_ref, jk_ref,
                 o_ref, kd_ref, vd_ref):
    s_len = k_ref.shape[0]
    tq = q_ref.shape[0]
    tk = min(TK_PREP, s_len)

    @pl.when(pl.program_id(1) == 0)
    def _():
        def body(i, carry):
            rows = pl.ds(pl.multiple_of(i * tk, tk), tk)
            k = k_ref[rows, :].astype(F32)
            kn = k * lax.rsqrt(_segsum(k * k, jk_ref[...]) * (1.0 / HEAD) + NORM_EPS) * kg_ref[...]
            kr = kn * ck_ref[rows, :] + _rope_partner(kn) * sk_ref[rows, :]
            v = v_ref[rows, :].astype(F32)
            low = lax.broadcasted_iota(jnp.int32, kr.shape, 1) < HEAD
            ksw = pltpu.roll(kr, HEAD, 1)
            kd_ref[0, rows, :] = jnp.where(low, kr, ksw).astype(BF16)
            kd_ref[1, rows, :] = jnp.where(low, ksw, kr).astype(BF16)
            vd_ref[0, rows, :] = jnp.where(low, v, 1.0).astype(BF16)
            vd_ref[1, rows, :] = jnp.where(low, pltpu.roll(v, HEAD, 1), 1.0).astype(BF16)
            return carry
        lax.fori_loop(0, s_len // tk, body, 0)

    q = q_ref[...].astype(F32)
    qn = q * lax.rsqrt(_segsum(q * q, jq_ref[...]) * (1.0 / HEAD) + NORM_EPS) * qg_ref[...]
    qr = qn * cq_ref[...] + _rope_partner(qn) * sq_ref[...]
    low = lax.broadcasted_iota(jnp.int32, (tq, LANES), 1) < HEAD
    pairs_per_kv = AT_Q_HEADS // AT_KV_HEADS // 2
    tkc = min(TK_ATTN, s_len)
    n_kc = s_len // tkc
    qs = []
    for g in range(AT_KV_HEADS):
        parts = []
        for jj in range(pairs_per_kv):
            qp = qr[:, (g * pairs_per_kv + jj) * LANES:(g * pairs_per_kv + jj + 1) * LANES]
            parts += [jnp.where(low, qp, 0.0), jnp.where(low, 0.0, qp)]
        qs.append(jnp.concatenate(parts, axis=0).astype(BF16))
    streams = [(g, kc) for g in range(AT_KV_HEADS) for kc in range(n_kc)]
    scores = lambda g, kc: _dot_nt(qs[g], kd_ref[g, kc * tkc:(kc + 1) * tkc, :])
    partial, s_next = {}, scores(*streams[0])
    for i, (g, kc) in enumerate(streams):
        s = s_next
        if i + 1 < len(streams):
            s_next = scores(*streams[i + 1])
        m = jnp.max(s, axis=-1, keepdims=True)
        p = jnp.exp2(s - m).astype(BF16)
        partial[g, kc] = (m, _dot(p, vd_ref[g, kc * tkc:(kc + 1) * tkc, :]))
    for g in range(AT_KV_HEADS):
        m_all = partial[g, 0][0]
        for kc in range(1, n_kc):
            m_all = jnp.maximum(m_all, partial[g, kc][0])
        o = sum(jnp.exp2(partial[g, kc][0] - m_all) * partial[g, kc][1] for kc in range(n_kc))
        o = o / pltpu.roll(o, HEAD, 1)
        for jj in range(pairs_per_kv):
            even = o[(2 * jj) * tq:(2 * jj + 1) * tq]
            odd = pltpu.roll(o[(2 * jj + 1) * tq:(2 * jj + 2) * tq], HEAD, 1)
            pair = jnp.where(low, even, odd)
            col = (g * pairs_per_kv + jj) * LANES
            o_ref[:, col:col + LANES] = pair.astype(BF16)


def _attn(at, tabs, q_g, k_g, jones):
    bsz, s, _ = at.shape
    tq = min(TQ_ATTN, s)
    cq, sq, ck, sk = tabs
    qw = AT_Q_HEADS * HEAD
    kvw = AT_KV_HEADS * HEAD
    cst = lambda a: pl.BlockSpec(a.shape, lambda b_, i: (0,) * a.ndim)
    jk = jones[0:kvw, 0:kvw]
    return pl.pallas_call(
        _attn_kernel,
        grid=(bsz, s // tq),
        in_specs=[pl.BlockSpec((None, tq, qw), lambda b_, i: (b_, i, 0)),
                  pl.BlockSpec((None, s, kvw), lambda b_, i: (b_, 0, qw // kvw)),
                  pl.BlockSpec((None, s, kvw), lambda b_, i: (b_, 0, qw // kvw + 1)),
                  pl.BlockSpec((tq, qw), lambda b_, i: (i, 0)),
                  pl.BlockSpec((tq, qw), lambda b_, i: (i, 0)),
                  cst(ck), cst(sk), cst(q_g), cst(k_g), cst(jones), cst(jk)],
        out_specs=pl.BlockSpec((None, tq, qw), lambda b_, i: (b_, i, 0)),
        out_shape=jax.ShapeDtypeStruct((bsz, s, qw), BF16),
        scratch_shapes=[pltpu.VMEM((AT_KV_HEADS, s, LANES), BF16), pltpu.VMEM((AT_KV_HEADS, s, LANES), BF16)],
        compiler_params=_cparams("parallel", "arbitrary"),
    )(at, at, at, cq, sq, ck, sk, q_g, k_g, jones, jk)


def _rope_tables(s):
    rows = s // GRID_W
    pos_r = jnp.repeat(jnp.arange(rows), GRID_W).astype(F32)
    pos_c = jnp.tile(jnp.arange(GRID_W), rows).astype(F32)
    half = HEAD // 2
    inv = ROPE_THETA ** (-jnp.arange(0, half, 2, dtype=F32) / half)
    ang_r = pos_r[:, None] * inv
    ang_c = pos_c[:, None] * inv
    cos = jnp.concatenate([jnp.cos(ang_r)] * 2 + [jnp.cos(ang_c)] * 2, axis=-1)
    sin = jnp.concatenate([-jnp.sin(ang_r), jnp.sin(ang_r), -jnp.sin(ang_c), jnp.sin(ang_c)], axis=-1)
    scale = HEAD ** -0.5 * math.log2(math.e)
    return (jnp.tile(cos, (1, AT_Q_HEADS)) * scale, jnp.tile(sin, (1, AT_Q_HEADS)) * scale,
            jnp.tile(cos, (1, AT_KV_HEADS)), jnp.tile(sin, (1, AT_KV_HEADS)))


def _merge_kernel(y0_ref, y1_ref, g_ref, bonus_ref, lng_ref, lnb_ref, j_ref, ycv_ref, yat_ref, sg_ref, x_ref,
                  wb_ref, wo_ref, o_ref):
    j = j_ref[...]
    y = y0_ref[...] + y1_ref[...]
    mu = _segsum(y, j) * (1.0 / HEAD)
    dev = y - mu
    var = _segsum(dev * dev, j) * (1.0 / HEAD)
    yn = dev * lax.rsqrt(var + RW_LN_EPS) * lng_ref[...] + lnb_ref[...]
    y_rw = ((yn + bonus_ref[...]) * g_ref[...]).astype(BF16)
    dm = D_MODEL
    merged = (sg_ref[:, 0:dm].astype(F32) * _dot(y_rw, wb_ref[0])
              + sg_ref[:, dm:2 * dm].astype(F32) * _dot(ycv_ref[...], wb_ref[1])
              + sg_ref[:, 2 * dm:3 * dm].astype(F32) * _dot(yat_ref[...], wb_ref[2]))
    o_ref[...] = x_ref[...] + _dot(merged.astype(BF16), wo_ref[...])


def _merge(yf, yb, g, bonus, ln_g, ln_b, jones, y_cv, y_at, sg, x, w_branch, w_out):
    t = x.shape[0]
    tm = TM_MERGE
    row = lambda n: pl.BlockSpec((tm, n), lambda i: (i, 0))
    return pl.pallas_call(
        _merge_kernel,
        grid=(t // tm,),
        in_specs=[row(RW_WIDTH), row(RW_WIDTH), row(RW_WIDTH), row(RW_WIDTH),
                  _resident(ln_g.shape), _resident(ln_b.shape), _resident(jones.shape),
                  row(CV_WIDTH), row(AT_Q_HEADS * HEAD), row(GATE_COLS), row(D_MODEL),
                  _resident(w_branch.shape), _resident(w_out.shape)],
        out_specs=row(D_MODEL),
        out_shape=jax.ShapeDtypeStruct((t, D_MODEL), F32),
        compiler_params=_cparams("parallel"),
    )(yf, yb, g, bonus, ln_g, ln_b, jones, y_cv, y_at, sg, x, w_branch, w_out)


def _mlp_kernel(x_ref, g_ref, w1_ref, w2_ref, o_ref):
    x = x_ref[...]
    h = (x * lax.rsqrt(jnp.mean(x * x, axis=-1, keepdims=True) + NORM_EPS) * g_ref[...]).astype(BF16)
    acc = x
    for c in range(D_FF // FF_CHUNK):
        cols = slice(c * FF_CHUNK, (c + 1) * FF_CHUNK)
        u = jnp.maximum(_dot(h, w1_ref[:, cols]), 0.0)
        acc = acc + _dot((u * u).astype(BF16), w2_ref[cols, :])
    o_ref[...] = acc


def _mlp(x, g, w1, w2):
    t = x.shape[0]
    tm = TM_MLP
    row = pl.BlockSpec((tm, D_MODEL), lambda i: (i, 0))
    return pl.pallas_call(
        _mlp_kernel,
        grid=(t // tm,),
        in_specs=[row, _resident(g.shape), _resident(w1.shape), _resident(w2.shape)],
        out_specs=row,
        out_shape=jax.ShapeDtypeStruct((t, D_MODEL), F32),
        compiler_params=_cparams("parallel"),
    )(x, g, w1, w2)


def _block_diag2(w):
    z = jnp.zeros_like(w[0])
    return jnp.concatenate([jnp.concatenate([w[0], z], axis=1), jnp.concatenate([z, w[1]], axis=1)], axis=0)


def kernel(x, norm1_g, w_in, rw_mu_prev, rw_mu_next, rw_w0, rw_w2, rw_a0, rw_a2, rw_g2, rw_k_k, rw_k_a, rw_r_k,
           rw_ln_g, rw_ln_b, cv_w, cv_b, at_q_g, at_k_g, w_branch, w_out, norm2_g, w_mlp1, w_mlp2):
    bsz, s, dm = x.shape
    assert dm == D_MODEL and s % GRID_W == 0
    for tile in (TM_INPROJ, TB_SCAN, TQ_ATTN, TK_PREP):
        assert s % min(tile, s) == 0 and min(tile, s) % CHUNK == 0
    t = bsz * s
    tabs = _rope_tables(s)
    masks = _scan_masks()
    jones = jnp.asarray(np.kron(np.eye(RW_WIDTH // HEAD, dtype=np.float32), np.ones((HEAD, HEAD), np.float32)), BF16)
    row = lambda a: a.reshape(1, -1)
    xf = x.reshape(t, dm)
    for l in range(DEPTH):
        r, v, kk, g, bonus, lw, kd, kka, y_cv, at, sg = _inproj(
            xf.reshape(bsz, s, dm), row(norm1_g[l]), w_in[l].astype(BF16), row(rw_mu_prev[l]), row(rw_mu_next[l]),
            row(rw_w0[l]), _block_diag2(rw_w2[l]).astype(BF16), row(rw_a0[l]), _block_diag2(rw_a2[l]).astype(BF16),
            rw_g2[l].astype(BF16), row(rw_k_k[l]), row(rw_k_a[l]), row(rw_r_k[l]), jones, cv_w[l], row(cv_b[l]))
        yf, yb = _scan(r, v, kk, lw, kd, kka, masks)
        y_at = _attn(at, tabs, row(jnp.tile(at_q_g[l], AT_Q_HEADS)),
                     row(jnp.tile(at_k_g[l], AT_KV_HEADS)), jones)
        xf = _merge(yf.reshape(t, RW_WIDTH), yb.reshape(t, RW_WIDTH), g.reshape(t, RW_WIDTH),
                    bonus.reshape(t, RW_WIDTH),
                    row(rw_ln_g[l]), row(rw_ln_b[l]), jones, y_cv.reshape(t, CV_WIDTH),
                    y_at.reshape(t, AT_Q_HEADS * HEAD), sg.reshape(t, GATE_COLS), xf, w_branch[l].astype(BF16), w_out[l].astype(BF16))
        xf = _mlp(xf, row(norm2_g[l]), w_mlp1[l].astype(BF16), w_mlp2[l].astype(BF16))
    return xf.reshape(bsz, s, dm)
```

```python
import math

import numpy as np
import jax
import jax.numpy as jnp
from jax import lax
from jax.experimental import pallas as pl
from jax.experimental.pallas import tpu as pltpu

F32 = jnp.float32
BF16 = jnp.bfloat16

D_MODEL = 1024
DEPTH = 4
GRID_W = 64
NORM_EPS = 1e-6
HEAD = 64
RW_WIDTH = 512
RW_LORA = 128
RW_LN_EPS = 64e-5
RW_COLS = 3 * RW_WIDTH + 3 * RW_LORA
CV_WIDTH = 512
CV_COLS = 3 * CV_WIDTH
AT_Q_HEADS = 8
AT_KV_HEADS = 2
AT_COLS = (AT_Q_HEADS + 2 * AT_KV_HEADS) * HEAD
GATE_COLS = 3 * D_MODEL
ROPE_THETA = 10000.0
D_FF = 4 * D_MODEL

LANES = 128
SUBLANES = 8
VMEM_LIMIT_BYTES = 56 * 1024 * 1024
CHUNK = 64
TM_INPROJ = 256
TB_SCAN = 1024
SCAN_SLOTS = 2
TQ_ATTN = 128
TK_ATTN = 2048
TK_PREP = 512
TM_MERGE = 512
TM_MLP = 512
FF_CHUNK = 1024


def _cparams(*sem):
    return pltpu.CompilerParams(dimension_semantics=sem, vmem_limit_bytes=VMEM_LIMIT_BYTES)


def _dot(a, b):
    return jnp.dot(a, b, preferred_element_type=F32)


def _dot_nt(a, b):
    return lax.dot_general(a, b, (((1,), (1,)), ((), ())), preferred_element_type=F32)


def _segsum(x, j):
    return _dot(x.astype(BF16), j)


def _sigmoid(x):
    return 1.0 / (1.0 + jnp.exp(-x))


def _resident(shape):
    nd = len(shape)
    return pl.BlockSpec(shape, lambda *_: (0,) * nd, pipeline_mode=pl.Buffered(1))


def _shift_rows(c, halo_prev, halo_next, first, last):
    n = c.shape[0]
    row = lax.broadcasted_iota(jnp.int32, c.shape, 0)
    lane_row = lax.broadcasted_iota(jnp.int32, (1, c.shape[1]), 1)
    is_first = (lane_row * 0 + first.astype(jnp.int32)) > 0
    is_last = (lane_row * 0 + last.astype(jnp.int32)) > 0
    prev_row = jnp.where(is_first, 0.0, halo_prev[halo_prev.shape[0] - 1:, :])
    next_row = jnp.where(is_last, 0.0, halo_next[0:1, :])
    prev = jnp.where(row == 0, prev_row, pltpu.roll(c, 1, 0))
    nxt = jnp.where(row == n - 1, next_row, pltpu.roll(c, n - 1, 0))
    return prev, nxt


def _halo_specs(ts, width, n_seq_rows, rows):
    r = ts // rows
    last_blk = n_seq_rows // rows - 1
    prev = pl.BlockSpec((None, rows, width), lambda b, i: (b, jnp.maximum(i * r - 1, 0), 0))
    nxt = pl.BlockSpec((None, rows, width), lambda b, i: (b, jnp.minimum((i + 1) * r, last_blk), 0))
    return prev, nxt


def _inproj_kernel(x_ref, xp_ref, xn_ref, g_ref, w_ref, mup_ref, mun_ref, w0_ref, w2_ref, a0_ref, a2_ref, g2_ref,
                   kk_w_ref, ka_ref, rk_ref, j_ref, cvw_ref, cvb_ref,
                   r_ref, v_ref, kk_ref, gate_ref, bonus_ref, lw_ref, kd_ref, kka_ref, ycv_ref, at_ref, sg_ref):
    i = pl.program_id(1)

    def normed(x):
        return (x * lax.rsqrt(jnp.mean(x * x, axis=-1, keepdims=True) + NORM_EPS) * g_ref[...]).astype(BF16)

    h = normed(x_ref[...])
    o0, o1, o2 = RW_COLS, RW_COLS + CV_COLS, RW_COLS + CV_COLS + AT_COLS
    dm = D_MODEL

    def gate_cols(n):
        sg_ref[:, n * dm:(n + 1) * dm] = _sigmoid(_dot(h, w_ref[:, o2 + n * dm:o2 + (n + 1) * dm])).astype(BF16)

    c = _dot(h, w_ref[:, 0:o0])
    h_halo = normed(jnp.concatenate([xp_ref[...], xn_ref[...]], axis=0))
    c_halo = _dot(h_halo, w_ref[:, 0:o0])
    first, last = i == 0, i == pl.num_programs(1) - 1
    cw = CV_WIDTH
    cv = _dot(h, w_ref[:, o0:o1])
    cv_halo = _dot(h_halo, w_ref[:, o0 + cw:o1])
    zc = cv[:, cw:2 * cw] * cv[:, 2 * cw:3 * cw]
    zc_halo = cv_halo[:, 0:cw] * cv_halo[:, cw:2 * cw]
    zc_prev, zc_next = _shift_rows(zc, zc_halo[0:SUBLANES], zc_halo[SUBLANES:2 * SUBLANES], first, last)
    conv = cvb_ref[...] + cvw_ref[0:1, :] * zc_prev + cvw_ref[1:2, :] * zc + cvw_ref[2:3, :] * zc_next
    ycv_ref[...] = (cv[:, 0:cw] * conv).astype(BF16)
    prev, nxt = _shift_rows(c, c_halo[0:SUBLANES], c_halo[SUBLANES:2 * SUBLANES], first, last)
    z = c + mup_ref[...] * (prev - c) + mun_ref[...] * (nxt - c)
    w = RW_WIDTH
    r, k, v = z[:, 0:w], z[:, w:2 * w], z[:, 2 * w:3 * w]
    wd = z[:, 3 * w:3 * w + RW_LORA]
    ad = z[:, 3 * w + RW_LORA:3 * w + 2 * RW_LORA]
    gd = z[:, 3 * w + 2 * RW_LORA:3 * w + 3 * RW_LORA]
    j = j_ref[...]
    w_raw = w0_ref[...] + _dot(jnp.tanh(wd).astype(BF16), w2_ref[...])
    lw = -math.exp(-0.5) * _sigmoid(w_raw)
    a = _sigmoid(a0_ref[...] + _dot(ad.astype(BF16), a2_ref[...]))
    gate_ref[...] = _dot(_sigmoid(gd).astype(BF16), g2_ref[...]).astype(BF16)
    gate_cols(0)
    kkr = k * kk_w_ref[...]
    kk = kkr / jnp.maximum(jnp.sqrt(_segsum(kkr * kkr, j)), 1e-12)
    r_ref[...] = r.astype(BF16)
    v_ref[...] = v.astype(BF16)
    kk_ref[...] = kk.astype(BF16)
    at_ref[...] = _dot(h, w_ref[:, o1:o2]).astype(BF16)
    gate_cols(1)
    ksum = jnp.zeros_like(k)
    for d in range(2):
        a_d = a[:, d * w:(d + 1) * w]
        k_d = k * (1.0 + (a_d - 1.0) * ka_ref[...])
        ksum = ksum + k_d
        lw_ref[d] = lw[:, d * w:(d + 1) * w]
        kd_ref[d] = k_d.astype(BF16)
        kka_ref[d] = (kk * a_d).astype(BF16)
    gate_cols(2)
    bonus_ref[...] = (_segsum(r * (0.5 * ksum) * rk_ref[...], j) * v).astype(BF16)


def _inproj(x, g, w, mup, mun, w0, w2bd, a0, a2bd, g2, k_k, k_a, r_k, jones, cv_w, cv_b):
    bsz, s, _ = x.shape
    tm = min(TM_INPROJ, s)
    hp, hn = _halo_specs(tm, D_MODEL, s, SUBLANES)
    tok = lambda n: pl.BlockSpec((None, tm, n), lambda b_, i: (b_, i, 0))
    tok2 = pl.BlockSpec((2, None, tm, RW_WIDTH), lambda b_, i: (0, b_, i, 0))
    consts = (g, w, mup, mun, w0, w2bd, a0, a2bd, g2, k_k, k_a, r_k, jones, cv_w, cv_b)
    one = lambda n: jax.ShapeDtypeStruct((bsz, s, n), BF16)
    two = lambda dt: jax.ShapeDtypeStruct((2, bsz, s, RW_WIDTH), dt)
    return pl.pallas_call(
        _inproj_kernel,
        grid=(bsz, s // tm),
        in_specs=[tok(D_MODEL), hp, hn] + [_resident(a.shape) for a in consts],
        out_specs=[tok(RW_WIDTH)] * 5 + [tok2] * 3 + [tok(CV_WIDTH), tok(AT_COLS), tok(GATE_COLS)],
        out_shape=[one(RW_WIDTH)] * 5 + [two(F32), two(BF16), two(BF16)] + [one(CV_WIDTH), one(AT_COLS), one(GATE_COLS)],
        compiler_params=_cparams("parallel", "arbitrary"),
    )(x, x, x, *consts)


def _scan_masks():
    c = CHUNK
    t = np.arange(c)
    lower = (t[None, :] <= t[:, None]).astype(np.float32)
    strict_l = (t[None, :] < t[:, None]).astype(np.float32)
    ones = np.ones((c, c), np.float32)
    tri = np.stack([np.concatenate([lower, ones], 0), np.concatenate([lower.T, ones], 0)])
    strict = np.stack([np.tile(strict_l, (1, 2)), np.tile(strict_l.T, (1, 2))])
    incl = np.stack([np.tile(lower, (1, 2)), np.tile(lower.T, (1, 2))])
    return jnp.asarray(tri, BF16), jnp.asarray(strict, F32), jnp.asarray(incl, F32)


def _scan_kernel(rf_ref, vf_ref, kkf_ref, rb_ref, vb_ref, kkb_ref,
                 lwf_ref, kdf_ref, kkaf_ref, lwb_ref, kdb_ref, kkab_ref,
                 tri_ref, strict_ref, incl_ref, yf_ref, yb_ref, st_ref,
                 kq_ref, rq_ref, kdec_ref, bdec_ref, eend_ref, avk_ref, ark_ref, arb_ref, m_ref, tinv_ref):
    c = CHUNK
    n_chunks = rf_ref.shape[0] // c
    n_pairs = RW_WIDTH // LANES

    @pl.when(pl.program_id(1) == 0)
    def _():
        st_ref[...] = jnp.zeros_like(st_ref)

    lane = lax.broadcasted_iota(jnp.int32, (c, LANES), 1)
    row = lax.broadcasted_iota(jnp.int32, (c, LANES), 0)
    head0 = lane < HEAD
    eye = jnp.where((lane & (HEAD - 1)) == row, 1.0, 0.0).astype(F32)
    ri = lax.broadcasted_iota(jnp.int32, (LANES, LANES), 0)
    ci = lax.broadcasted_iota(jnp.int32, (LANES, LANES), 1)
    same_head = jnp.where((ri < HEAD) == (ci < HEAD), 1.0, 0.0).astype(F32)

    def stack(x):
        return jnp.concatenate([jnp.where(head0, x, 0.0), jnp.where(head0, 0.0, x)], axis=0)

    def stack_b(x):
        return stack(x).astype(BF16)

    dir_refs = ((rf_ref, vf_ref, kkf_ref, lwf_ref, kdf_ref, kkaf_ref, yf_ref),
                (rb_ref, vb_ref, kkb_ref, lwb_ref, kdb_ref, kkab_ref, yb_ref))

    def chunk_rows(d, step):
        cidx = step if d == 0 else n_chunks - 1 - step
        start = cidx * c
        return pl.ds(start if isinstance(start, int) else pl.multiple_of(start, c), c)

    def local_stages(step):
        slot = step % SCAN_SLOTS
        ch = []
        for d, (r_ref, v_ref, kk_ref, lw_ref, kd_ref, kka_ref, y_ref) in enumerate(dir_refs):
            rows = chunk_rows(d, step)
            lw = lw_ref[rows, :]
            hi = lw.astype(BF16)
            rem = lw - hi.astype(F32)
            mid = rem.astype(BF16)
            lo = (rem - mid.astype(F32)).astype(BF16)
            tri = tri_ref[d]
            cum = _dot(tri, hi) + _dot(tri, mid) + _dot(tri, lo)
            lp = cum[0:c]
            e_end = jnp.exp(cum[c:c + SUBLANES])
            e_n = jnp.exp(-lp)
            rq = (r_ref[rows, :] * jnp.exp(lp)).astype(BF16)
            kq = (kk_ref[rows, :] * jnp.exp(lp - lw)).astype(BF16)
            ks = kd_ref[rows, :] * e_n
            bs = kka_ref[rows, :] * e_n
            kq_ref[d, slot] = kq
            rq_ref[d, slot] = rq
            kdec_ref[d, slot] = (ks * e_end[0:1]).astype(BF16)
            bdec_ref[d, slot] = (bs * e_end[0:1]).astype(BF16)
            eend_ref[d, slot] = e_end
            for p in range(n_pairs):
                sl = slice(p * LANES, (p + 1) * LANES)
                ch.append(dict(d=d, sl=sl, kq=kq[:, sl], rq=rq[:, sl], ks=ks[:, sl], bs=bs[:, sl]))
        n = range(len(ch))
        strict = [strict_ref[x["d"]] for x in ch]
        incl = [incl_ref[x["d"]] for x in ch]
        yield
        lq = [jnp.concatenate([x["kq"], x["rq"]], axis=0) for x in ch]
        s_ks = [stack_b(x["ks"]) for x in ch]
        s_bs = [stack_b(x["bs"]) for x in ch]
        gkb = [_dot_nt(lq[i], jnp.concatenate([s_ks[i], s_bs[i]], axis=0)) for i in n]
        yield
        gk = [x[:, 0:LANES] for x in gkb]
        gb = [x[:, LANES:2 * LANES] for x in gkb]
        m = [-(gb[i][0:c] * strict[i]) for i in n]
        for i, x in enumerate(ch):
            avk_ref[x["d"], slot, :, x["sl"]] = (gk[i][0:c] * strict[i]).astype(BF16)
            ark_ref[x["d"], slot, :, x["sl"]] = (gk[i][c:2 * c] * incl[i]).astype(BF16)
            arb_ref[x["d"], slot, :, x["sl"]] = (gb[i][c:2 * c] * incl[i]).astype(BF16)
            m_ref[x["d"], slot, :, x["sl"]] = m[i].astype(BF16)
        t_inv = [eye + x for x in m]
        pw = [_dot(x.astype(BF16), stack_b(x)) for x in m]
        yield
        n_levels = int(math.log2(c)) - 1
        for lvl in range(n_levels):
            pw_bd = [stack_b(x) for x in pw]
            if lvl + 1 < n_levels:
                both = [_dot(jnp.concatenate([t_inv[i].astype(BF16), pw[i].astype(BF16)], axis=0), pw_bd[i])
                        for i in n]
                t_inv = [t_inv[i] + both[i][0:c] for i in n]
                pw = [both[i][c:2 * c] for i in n]
            else:
                t_inv = [t_inv[i] + _dot(t_inv[i].astype(BF16), pw_bd[i]) for i in n]
            yield
        for i, x in enumerate(ch):
            tinv_ref[x["d"], slot, :, x["sl"]] = t_inv[i].astype(BF16)

    def state_stages(step):
        slot = step % SCAN_SLOTS
        ch = []
        for d, (r_ref, v_ref, kk_ref, lw_ref, kd_ref, kka_ref, y_ref) in enumerate(dir_refs):
            rows = chunk_rows(d, step)
            for p in range(n_pairs):
                ch.append(dict(d=d, p=p, rows=rows, sl=slice(p * LANES, (p + 1) * LANES), y_ref=y_ref, v_ref=v_ref))
        n = range(len(ch))
        get = lambda ref: [ref[x["d"], slot, :, x["sl"]] for x in ch]
        v = [x["v_ref"][x["rows"], x["sl"]].astype(F32) for x in ch]
        s_vb = [stack_b(x) for x in v]
        st = [st_ref[x["d"], x["p"]] for x in ch]
        stb = [x.astype(BF16) for x in st]
        kq_b, rq_b, a_vk, a_rk, a_rb, t_b, m_b = (get(r) for r in (
            kq_ref, rq_ref, avk_ref, ark_ref, arb_ref, tinv_ref, m_ref))
        from_state = [_dot_nt(jnp.concatenate([kq_b[i], rq_b[i]], axis=0), stb[i]) for i in n]
        from_v = [_dot(jnp.concatenate([a_vk[i], a_rk[i]], axis=0), s_vb[i]) for i in n]
        yield
        rhs = [from_state[i][0:c] + from_v[i][0:c] for i in n]
        u0 = [-_dot(t_b[i], stack_b(rhs[i])) for i in n]
        yield
        res = [_dot(m_b[i], stack_b(u0[i])) - rhs[i] - u0[i] for i in n]
        yield
        u = [u0[i] + _dot(t_b[i], stack_b(res[i])) for i in n]
        yield
        y = [from_state[i][c:2 * c] + from_v[i][c:2 * c] + _dot(a_rb[i], stack_b(u[i])) for i in n]
        vu_t = [jnp.concatenate([v[i], u[i]], axis=0).T.astype(BF16) for i in n]
        kb = [jnp.concatenate([kdec_ref[x["d"], slot, :, x["sl"]], bdec_ref[x["d"], slot, :, x["sl"]]], axis=0)
              for x in ch]
        upd = [_dot(vu_t[i], kb[i]) * same_head for i in n]
        for i, x in enumerate(ch):
            x["y_ref"][x["rows"], x["sl"]] = y[i]
            st_ref[x["d"], x["p"]] = st[i] * eend_ref[x["d"], slot, 0:1, x["sl"]] + upd[i]

    def emit(*gens, order=None):
        live = {i: g for i, g in enumerate(gens)}
        for i in list(order or ()) + [None]:
            while i is None and live:
                for k in list(live):
                    if next(live[k], live) is live:
                        del live[k]
            if i in live and next(live[i], live) is live:
                del live[i]

    emit(local_stages(0))

    def body(i, carry):
        emit(local_stages(i + 1), state_stages(i), order=(0, 1, 0, 0, 1, 0, 0, 1, 0, 1, 0, 1, 0))
        return carry

    lax.fori_loop(0, n_chunks - 1, body, 0)
    emit(state_stages(n_chunks - 1))


def _scan(r, v, kk, lw, kd, kka, masks):
    bsz, s, w = r.shape
    tb = min(TB_SCAN, s)
    nblk = s // tb
    fwd = pl.BlockSpec((None, tb, w), lambda b_, j: (b_, j, 0))
    bwd = pl.BlockSpec((None, tb, w), lambda b_, j: (b_, nblk - 1 - j, 0))
    fwd2 = pl.BlockSpec((None, None, tb, w), lambda b_, j: (0, b_, j, 0))
    bwd2 = pl.BlockSpec((None, None, tb, w), lambda b_, j: (1, b_, nblk - 1 - j, 0))
    out = jax.ShapeDtypeStruct((bsz, s, w), F32)
    per_chunk = lambda dt: pltpu.VMEM((2, SCAN_SLOTS, CHUNK, w), dt)
    return pl.pallas_call(
        _scan_kernel,
        grid=(bsz, nblk),
        in_specs=[fwd, fwd, fwd, bwd, bwd, bwd, fwd2, fwd2, fwd2, bwd2, bwd2, bwd2]
        + [_resident(a.shape) for a in masks],
        out_specs=[fwd, bwd],
        out_shape=[out, out],
        scratch_shapes=[pltpu.VMEM((2, w // LANES, LANES, LANES), F32), per_chunk(BF16), per_chunk(BF16),
                        per_chunk(BF16), per_chunk(BF16), pltpu.VMEM((2, SCAN_SLOTS, SUBLANES, w), F32)]
        + [per_chunk(BF16)] * 5,
        compiler_params=_cparams("parallel", "arbitrary"),
    )(r, v, kk, r, v, kk, lw, kd, kka, lw, kd, kka, *masks)


def _rope_partner(x):
    n = x.shape[1]
    lane = lax.broadcasted_iota(jnp.int32, x.shape, 1)
    return jnp.where((lane & 31) < 16, pltpu.roll(x, n - 16, 1), pltpu.roll(x, 16, 1))


def _attn_kernel(q_ref, k_ref, v_ref, cq_ref, sq_ref, ck_ref, sk_ref, qg_ref, kg_ref, jq_ref, jk_ref,
                 o_ref, kd_ref, vd_ref):
    s_len = k_ref.shape[0]
    tq = q_ref.shape[0]
    tk = min(TK_PREP, s_len)

    @pl.when(pl.program_id(1) == 0)
    def _():
        def body(i, carry):
            rows = pl.ds(pl.multiple_of(i * tk, tk), tk)
            k = k_ref[rows, :].astype(F32)
            kn = k * lax.rsqrt(_segsum(k * k, jk_ref[...]) * (1.0 / HEAD) + NORM_EPS) * kg_ref[...]
            kr = kn * ck_ref[rows, :] + _rope_partner(kn) * sk_ref[rows, :]
            v = v_ref[rows, :].astype(F32)
            low = lax.broadcasted_iota(jnp.int32, kr.shape, 1) < HEAD
            ksw = pltpu.roll(kr, HEAD, 1)
            kd_ref[0, rows, :] = jnp.where(low, kr, ksw).astype(BF16)
            kd_ref[1, rows, :] = jnp.where(low, ksw, kr).astype(BF16)
            vd_ref[0, rows, :] = jnp.where(low, v, 1.0).astype(BF16)
            vd_ref[1, rows, :] = jnp.where(low, pltpu.roll(v, HEAD, 1), 1.0).astype(BF16)
            return carry
        lax.fori_loop(0, s_len // tk, body, 0)

    q = q_ref[...].astype(F32)
    qn = q * lax.rsqrt(_segsum(q * q, jq_ref[...]) * (1.0 / HEAD) + NORM_EPS) * qg_ref[...]
    qr = qn * cq_ref[...] + _rope_partner(qn) * sq_ref[...]
    low = lax.broadcasted_iota(jnp.int32, (tq, LANES), 1) < HEAD
    pairs_per_kv = AT_Q_HEADS // AT_KV_HEADS // 2
    tkc = min(TK_ATTN, s_len)
    n_kc = s_len // tkc
    qs = []
    for g in range(AT_KV_HEADS):
        parts = []
        for jj in range(pairs_per_kv):
            qp = qr[:, (g * pairs_per_kv + jj) * LANES:(g * pairs_per_kv + jj + 1) * LANES]
            parts += [jnp.where(low, qp, 0.0), jnp.where(low, 0.0, qp)]
        qs.append(jnp.concatenate(parts, axis=0).astype(BF16))
    streams = [(g, kc) for g in range(AT_KV_HEADS) for kc in range(n_kc)]
    scores = lambda g, kc: _dot_nt(qs[g], kd_ref[g, kc * tkc:(kc + 1) * tkc, :])
    partial, s_next = {}, scores(*streams[0])
    for i, (g, kc) in enumerate(streams):
        s = s_next
        if i + 1 < len(streams):
            s_next = scores(*streams[i + 1])
        m = jnp.max(s, axis=-1, keepdims=True)
        p = jnp.exp2(s - m).astype(BF16)
        partial[g, kc] = (m, _dot(p, vd_ref[g, kc * tkc:(kc + 1) * tkc, :]))
    for g in range(AT_KV_HEADS):
        m_all = partial[g, 0][0]
        for kc in range(1, n_kc):
            m_all = jnp.maximum(m_all, partial[g, kc][0])
        o = sum(jnp.exp2(partial[g, kc][0] - m_all) * partial[g, kc][1] for kc in range(n_kc))
        o = o / pltpu.roll(o, HEAD, 1)
        for jj in range(pairs_per_kv):
            even = o[(2 * jj) * tq:(2 * jj + 1) * tq]
            odd = pltpu.roll(o[(2 * jj + 1) * tq:(2 * jj + 2) * tq], HEAD, 1)
            pair = jnp.where(low, even, odd)
            col = (g * pairs_per_kv + jj) * LANES
            o_ref[:, col:col + LANES] = pair.astype(BF16)


def _attn(at, tabs, q_g, k_g, jones):
    bsz, s, _ = at.shape
    tq = min(TQ_ATTN, s)
    cq, sq, ck, sk = tabs
    qw = AT_Q_HEADS * HEAD
    kvw = AT_KV_HEADS * HEAD
    cst = lambda a: pl.BlockSpec(a.shape, lambda b_, i: (0,) * a.ndim)
    jk = jones[0:kvw, 0:kvw]
    return pl.pallas_call(
        _attn_kernel,
        grid=(bsz, s // tq),
        in_specs=[pl.BlockSpec((None, tq, qw), lambda b_, i: (b_, i, 0)),
                  pl.BlockSpec((None, s, kvw), lambda b_, i: (b_, 0, qw // kvw)),
                  pl.BlockSpec((None, s, kvw), lambda b_, i: (b_, 0, qw // kvw + 1)),
                  pl.BlockSpec((tq, qw), lambda b_, i: (i, 0)),
                  pl.BlockSpec((tq, qw), lambda b_, i: (i, 0)),
                  cst(ck), cst(sk), cst(q_g), cst(k_g), cst(jones), cst(jk)],
        out_specs=pl.BlockSpec((None, tq, qw), lambda b_, i: (b_, i, 0)),
        out_shape=jax.ShapeDtypeStruct((bsz, s, qw), BF16),
        scratch_shapes=[pltpu.VMEM((AT_KV_HEADS, s, LANES), BF16), pltpu.VMEM((AT_KV_HEADS, s, LANES), BF16)],
        compiler_params=_cparams("parallel", "arbitrary"),
    )(at, at, at, cq, sq, ck, sk, q_g, k_g, jones, jk)


def _rope_tables(s):
    rows = s // GRID_W
    pos_r = jnp.repeat(jnp.arange(rows), GRID_W).astype(F32)
    pos_c = jnp.tile(jnp.arange(GRID_W), rows).astype(F32)
    half = HEAD // 2
    inv = ROPE_THETA ** (-jnp.arange(0, half, 2, dtype=F32) / half)
    ang_r = pos_r[:, None] * inv
    ang_c = pos_c[:, None] * inv
    cos = jnp.concatenate([jnp.cos(ang_r)] * 2 + [jnp.cos(ang_c)] * 2, axis=-1)
    sin = jnp.concatenate([-jnp.sin(ang_r), jnp.sin(ang_r), -jnp.sin(ang_c), jnp.sin(ang_c)], axis=-1)
    scale = HEAD ** -0.5 * math.log2(math.e)
    return (jnp.tile(cos, (1, AT_Q_HEADS)) * scale, jnp.tile(sin, (1, AT_Q_HEADS)) * scale,
            jnp.tile(cos, (1, AT_KV_HEADS)), jnp.tile(sin, (1, AT_KV_HEADS)))


def _merge_kernel(y0_ref, y1_ref, g_ref, bonus_ref, lng_ref, lnb_ref, j_ref, ycv_ref, yat_ref, sg_ref, x_ref,
                  wb_ref, wo_ref, o_ref):
    j = j_ref[...]
    y = y0_ref[...] + y1_ref[...]
    mu = _segsum(y, j) * (1.0 / HEAD)
    dev = y - mu
    var = _segsum(dev * dev, j) * (1.0 / HEAD)
    yn = dev * lax.rsqrt(var + RW_LN_EPS) * lng_ref[...] + lnb_ref[...]
    y_rw = ((yn + bonus_ref[...]) * g_ref[...]).astype(BF16)
    dm = D_MODEL
    merged = (sg_ref[:, 0:dm].astype(F32) * _dot(y_rw, wb_ref[0])
              + sg_ref[:, dm:2 * dm].astype(F32) * _dot(ycv_ref[...], wb_ref[1])
              + sg_ref[:, 2 * dm:3 * dm].astype(F32) * _dot(yat_ref[...], wb_ref[2]))
    o_ref[...] = x_ref[...] + _dot(merged.astype(BF16), wo_ref[...])


def _merge(yf, yb, g, bonus, ln_g, ln_b, jones, y_cv, y_at, sg, x, w_branch, w_out):
    t = x.shape[0]
    tm = TM_MERGE
    row = lambda n: pl.BlockSpec((tm, n), lambda i: (i, 0))
    return pl.pallas_call(
        _merge_kernel,
        grid=(t // tm,),
        in_specs=[row(RW_WIDTH), row(RW_WIDTH), row(RW_WIDTH), row(RW_WIDTH),
                  _resident(ln_g.shape), _resident(ln_b.shape), _resident(jones.shape),
                  row(CV_WIDTH), row(AT_Q_HEADS * HEAD), row(GATE_COLS), row(D_MODEL),
                  _resident(w_branch.shape), _resident(w_out.shape)],
        out_specs=row(D_MODEL),
        out_shape=jax.ShapeDtypeStruct((t, D_MODEL), F32),
        compiler_params=_cparams("parallel"),
    )(yf, yb, g, bonus, ln_g, ln_b, jones, y_cv, y_at, sg, x, w_branch, w_out)


def _mlp_kernel(x_ref, g_ref, w1_ref, w2_ref, o_ref):
    x = x_ref[...]
    h = (x * lax.rsqrt(jnp.mean(x * x, axis=-1, keepdims=True) + NORM_EPS) * g_ref[...]).astype(BF16)
    acc = x
    for c in range(D_FF // FF_CHUNK):
        cols = slice(c * FF_CHUNK, (c + 1) * FF_CHUNK)
        u = jnp.maximum(_dot(h, w1_ref[:, cols]), 0.0)
        acc = acc + _dot((u * u).astype(BF16), w2_ref[cols, :])
    o_ref[...] = acc


def _mlp(x, g, w1, w2):
    t = x.shape[0]
    tm = TM_MLP
    row = pl.BlockSpec((tm, D_MODEL), lambda i: (i, 0))
    return pl.pallas_call(
        _mlp_kernel,
        grid=(t // tm,),
        in_specs=[row, _resident(g.shape), _resident(w1.shape), _resident(w2.shape)],
        out_specs=row,
        out_shape=jax.ShapeDtypeStruct((t, D_MODEL), F32),
        compiler_params=_cparams("parallel"),
    )(x, g, w1, w2)


def _block_diag2(w):
    z = jnp.zeros_like(w[0])
    return jnp.concatenate([jnp.concatenate([w[0], z], axis=1), jnp.concatenate([z, w[1]], axis=1)], axis=0)


def kernel(x, norm1_g, w_in, rw_mu_prev, rw_mu_next, rw_w0, rw_w2, rw_a0, rw_a2, rw_g2, rw_k_k, rw_k_a, rw_r_k,
           rw_ln_g, rw_ln_b, cv_w, cv_b, at_q_g, at_k_g, w_branch, w_out, norm2_g, w_mlp1, w_mlp2):
    bsz, s, dm = x.shape
    assert dm == D_MODEL and s % GRID_W == 0
    for tile in (TM_INPROJ, TB_SCAN, TQ_ATTN, TK_PREP):
        assert s % min(tile, s) == 0 and min(tile, s) % CHUNK == 0
    t = bsz * s
    tabs = _rope_tables(s)
    masks = _scan_masks()
    jones = jnp.asarray(np.kron(np.eye(RW_WIDTH // HEAD, dtype=np.float32), np.ones((HEAD, HEAD), np.float32)), BF16)
    row = lambda a: a.reshape(1, -1)
    xf = x.reshape(t, dm)
    for l in range(DEPTH):
        r, v, kk, g, bonus, lw, kd, kka, y_cv, at, sg = _inproj(
            xf.reshape(bsz, s, dm), row(norm1_g[l]), w_in[l].astype(BF16), row(rw_mu_prev[l]), row(rw_mu_next[l]),
            row(rw_w0[l]), _block_diag2(rw_w2[l]).astype(BF16), row(rw_a0[l]), _block_diag2(rw_a2[l]).astype(BF16),
            rw_g2[l].astype(BF16), row(rw_k_k[l]), row(rw_k_a[l]), row(rw_r_k[l]), jones, cv_w[l], row(cv_b[l]))
        yf, yb = _scan(r, v, kk, lw, kd, kka, masks)
        y_at = _attn(at, tabs, row(jnp.tile(at_q_g[l], AT_Q_HEADS)),
                     row(jnp.tile(at_k_g[l], AT_KV_HEADS)), jones)
        xf = _merge(yf.reshape(t, RW_WIDTH), yb.reshape(t, RW_WIDTH), g.reshape(t, RW_WIDTH),
                    bonus.reshape(t, RW_WIDTH),
                    row(rw_ln_g[l]), row(rw_ln_b[l]), jones, y_cv.reshape(t, CV_WIDTH),
                    y_at.reshape(t, AT_Q_HEADS * HEAD), sg.reshape(t, GATE_COLS), xf, w_branch[l].astype(BF16), w_out[l].astype(BF16))
        xf = _mlp(xf, row(norm2_g[l]), w_mlp1[l].astype(BF16), w_mlp2[l].astype(BF16))
    return xf.reshape(bsz, s, dm)
```

```python
import math

import numpy as np
import jax
import jax.numpy as jnp
from jax import lax
from jax.experimental import pallas as pl
from jax.experimental.pallas import tpu as pltpu

F32 = jnp.float32
BF16 = jnp.bfloat16

D_MODEL = 1024
DEPTH = 4
GRID_W = 64
NORM_EPS = 1e-6
HEAD = 64
RW_WIDTH = 512
RW_LORA = 128
RW_LN_EPS = 64e-5
RW_COLS = 3 * RW_WIDTH + 3 * RW_LORA
CV_WIDTH = 512
CV_COLS = 3 * CV_WIDTH
AT_Q_HEADS = 8
AT_KV_HEADS = 2
AT_COLS = (AT_Q_HEADS + 2 * AT_KV_HEADS) * HEAD
GATE_COLS = 3 * D_MODEL
ROPE_THETA = 10000.0
D_FF = 4 * D_MODEL

LANES = 128
SUBLANES = 8
VMEM_LIMIT_BYTES = 56 * 1024 * 1024
CHUNK = 64
TM_INPROJ = 256
TB_SCAN = 512
SCAN_SLOTS = 2
TQ_ATTN = 128
TK_ATTN = 2048
TK_PREP = 512
TM_MERGE = 512
TM_MLP = 512
FF_CHUNK = 1024


def _cparams(*sem):
    return pltpu.CompilerParams(dimension_semantics=sem, vmem_limit_bytes=VMEM_LIMIT_BYTES)


def _dot(a, b):
    return jnp.dot(a, b, preferred_element_type=F32)


def _dot_nt(a, b):
    return lax.dot_general(a, b, (((1,), (1,)), ((), ())), preferred_element_type=F32)


def _segsum(x, j):
    return _dot(x.astype(BF16), j)


def _segsum_split(x, j):
    hi = x.astype(BF16)
    return _dot(hi, j) + _dot((x - hi.astype(F32)).astype(BF16), j)


def _sigmoid(x):
    return 1.0 / (1.0 + jnp.exp(-x))


def _resident(shape):
    nd = len(shape)
    return pl.BlockSpec(shape, lambda *_: (0,) * nd, pipeline_mode=pl.Buffered(1))


def _shift_rows(c, halo_prev, halo_next, first, last):
    n = c.shape[0]
    row = lax.broadcasted_iota(jnp.int32, c.shape, 0)
    lane_row = lax.broadcasted_iota(jnp.int32, (1, c.shape[1]), 1)
    is_first = (lane_row * 0 + first.astype(jnp.int32)) > 0
    is_last = (lane_row * 0 + last.astype(jnp.int32)) > 0
    prev_row = jnp.where(is_first, 0.0, halo_prev[halo_prev.shape[0] - 1:, :])
    next_row = jnp.where(is_last, 0.0, halo_next[0:1, :])
    prev = jnp.where(row == 0, prev_row, pltpu.roll(c, 1, 0))
    nxt = jnp.where(row == n - 1, next_row, pltpu.roll(c, n - 1, 0))
    return prev, nxt


def _halo_specs(ts, width, n_seq_rows, rows):
    r = ts // rows
    last_blk = n_seq_rows // rows - 1
    prev = pl.BlockSpec((None, rows, width), lambda b, i: (b, jnp.maximum(i * r - 1, 0), 0))
    nxt = pl.BlockSpec((None, rows, width), lambda b, i: (b, jnp.minimum((i + 1) * r, last_blk), 0))
    return prev, nxt


def _inproj_kernel(x_ref, xp_ref, xn_ref, g_ref, w_ref, mup_ref, mun_ref, w0_ref, w2_ref, a0_ref, a2_ref, g2_ref,
                   kk_w_ref, ka_ref, rk_ref, j_ref, cvw_ref, cvb_ref,
                   r_ref, v_ref, kk_ref, gate_ref, bonus_ref, lw_ref, kd_ref, kka_ref, ycv_ref, at_ref, sg_ref):
    i = pl.program_id(1)

    def normed(x):
        return (x * lax.rsqrt(jnp.mean(x * x, axis=-1, keepdims=True) + NORM_EPS) * g_ref[...]).astype(BF16)

    h = normed(x_ref[...])
    o0, o1, o2 = RW_COLS, RW_COLS + CV_COLS, RW_COLS + CV_COLS + AT_COLS
    dm = D_MODEL

    def gate_cols(n):
        sg_ref[:, n * dm:(n + 1) * dm] = _sigmoid(_dot(h, w_ref[:, o2 + n * dm:o2 + (n + 1) * dm])).astype(BF16)

    c = _dot(h, w_ref[:, 0:o0])
    h_halo = normed(jnp.concatenate([xp_ref[...], xn_ref[...]], axis=0))
    c_halo = _dot(h_halo, w_ref[:, 0:o0])
    first, last = i == 0, i == pl.num_programs(1) - 1
    cw = CV_WIDTH
    cv = _dot(h, w_ref[:, o0:o1])
    cv_halo = _dot(h_halo, w_ref[:, o0 + cw:o1])
    zc = cv[:, cw:2 * cw] * cv[:, 2 * cw:3 * cw]
    zc_halo = cv_halo[:, 0:cw] * cv_halo[:, cw:2 * cw]
    zc_prev, zc_next = _shift_rows(zc, zc_halo[0:SUBLANES], zc_halo[SUBLANES:2 * SUBLANES], first, last)
    conv = cvb_ref[...] + cvw_ref[0:1, :] * zc_prev + cvw_ref[1:2, :] * zc + cvw_ref[2:3, :] * zc_next
    ycv_ref[...] = (cv[:, 0:cw] * conv).astype(BF16)
    prev, nxt = _shift_rows(c, c_halo[0:SUBLANES], c_halo[SUBLANES:2 * SUBLANES], first, last)
    z = c + mup_ref[...] * (prev - c) + mun_ref[...] * (nxt - c)
    w = RW_WIDTH
    r, k, v = z[:, 0:w], z[:, w:2 * w], z[:, 2 * w:3 * w]
    wd = z[:, 3 * w:3 * w + RW_LORA]
    ad = z[:, 3 * w + RW_LORA:3 * w + 2 * RW_LORA]
    gd = z[:, 3 * w + 2 * RW_LORA:3 * w + 3 * RW_LORA]
    j = j_ref[...]
    w_raw = w0_ref[...] + _dot(jnp.tanh(wd).astype(BF16), w2_ref[...])
    lw = -math.exp(-0.5) * _sigmoid(w_raw)
    a = _sigmoid(a0_ref[...] + _dot(ad.astype(BF16), a2_ref[...]))
    gate_ref[...] = _dot(_sigmoid(gd).astype(BF16), g2_ref[...]).astype(BF16)
    gate_cols(0)
    kkr = k * kk_w_ref[...]
    kk = kkr / jnp.maximum(jnp.sqrt(_segsum_split(kkr * kkr, j)), 1e-12)
    r_ref[...] = r.astype(BF16)
    v_ref[...] = v.astype(BF16)
    kk_ref[...] = kk.astype(BF16)
    at_ref[...] = _dot(h, w_ref[:, o1:o2]).astype(BF16)
    gate_cols(1)
    ksum = jnp.zeros_like(k)
    for d in range(2):
        a_d = a[:, d * w:(d + 1) * w]
        k_d = k * (1.0 + (a_d - 1.0) * ka_ref[...])
        ksum = ksum + k_d
        lw_ref[d] = lw[:, d * w:(d + 1) * w]
        kd_ref[d] = k_d.astype(BF16)
        kka_ref[d] = (kk * a_d).astype(BF16)
    gate_cols(2)
    bonus_ref[...] = (_segsum_split(r * (0.5 * ksum) * rk_ref[...], j) * v).astype(BF16)


def _inproj(x, g, w, mup, mun, w0, w2bd, a0, a2bd, g2, k_k, k_a, r_k, jones, cv_w, cv_b):
    bsz, s, _ = x.shape
    tm = min(TM_INPROJ, s)
    hp, hn = _halo_specs(tm, D_MODEL, s, SUBLANES)
    tok = lambda n: pl.BlockSpec((None, tm, n), lambda b_, i: (b_, i, 0))
    tok2 = pl.BlockSpec((2, None, tm, RW_WIDTH), lambda b_, i: (0, b_, i, 0))
    consts = (g, w, mup, mun, w0, w2bd, a0, a2bd, g2, k_k, k_a, r_k, jones, cv_w, cv_b)
    one = lambda n: jax.ShapeDtypeStruct((bsz, s, n), BF16)
    two = lambda dt: jax.ShapeDtypeStruct((2, bsz, s, RW_WIDTH), dt)
    return pl.pallas_call(
        _inproj_kernel,
        grid=(bsz, s // tm),
        in_specs=[tok(D_MODEL), hp, hn] + [_resident(a.shape) for a in consts],
        out_specs=[tok(RW_WIDTH)] * 5 + [tok2] * 3 + [tok(CV_WIDTH), tok(AT_COLS), tok(GATE_COLS)],
        out_shape=[one(RW_WIDTH)] * 5 + [two(F32), two(BF16), two(BF16)]
        + [one(CV_WIDTH), one(AT_COLS), one(GATE_COLS)],
        compiler_params=_cparams("parallel", "arbitrary"),
    )(x, x, x, *consts)


def _scan_masks():
    c = CHUNK
    t = np.arange(c)
    lower = (t[None, :] <= t[:, None]).astype(np.float32)
    strict_l = (t[None, :] < t[:, None]).astype(np.float32)
    ones = np.ones((c, c), np.float32)
    tri = np.stack([np.concatenate([lower, ones], 0), np.concatenate([lower.T, ones], 0)])
    strict = np.stack([np.tile(strict_l, (1, 2)), np.tile(strict_l.T, (1, 2))])
    incl = np.stack([np.tile(lower, (1, 2)), np.tile(lower.T, (1, 2))])
    return jnp.asarray(tri, BF16), jnp.asarray(strict, F32), jnp.asarray(incl, F32)


def _scan_kernel(rf_ref, vf_ref, kkf_ref, rb_ref, vb_ref, kkb_ref,
                 lwf_ref, kdf_ref, kkaf_ref, lwb_ref, kdb_ref, kkab_ref,
                 tri_ref, strict_ref, incl_ref, yf_ref, yb_ref, st_ref,
                 kq_ref, rq_ref, kdec_ref, bdec_ref, eend_ref, avk_ref, ark_ref, arb_ref, m_ref, tinv_ref):
    c = CHUNK
    n_chunks = rf_ref.shape[0] // c
    n_pairs = RW_WIDTH // LANES

    @pl.when(pl.program_id(1) == 0)
    def _():
        st_ref[...] = jnp.zeros_like(st_ref)

    lane = lax.broadcasted_iota(jnp.int32, (c, LANES), 1)
    row = lax.broadcasted_iota(jnp.int32, (c, LANES), 0)
    head0 = lane < HEAD
    eye = jnp.where((lane & (HEAD - 1)) == row, 1.0, 0.0).astype(F32)
    ri = lax.broadcasted_iota(jnp.int32, (LANES, LANES), 0)
    ci = lax.broadcasted_iota(jnp.int32, (LANES, LANES), 1)
    same_head = jnp.where((ri < HEAD) == (ci < HEAD), 1.0, 0.0).astype(F32)

    def stack(x):
        return jnp.concatenate([jnp.where(head0, x, 0.0), jnp.where(head0, 0.0, x)], axis=0)

    def stack_b(x):
        return stack(x).astype(BF16)

    dir_refs = ((rf_ref, vf_ref, kkf_ref, lwf_ref, kdf_ref, kkaf_ref, yf_ref),
                (rb_ref, vb_ref, kkb_ref, lwb_ref, kdb_ref, kkab_ref, yb_ref))

    def chunk_rows(d, step):
        cidx = step if d == 0 else n_chunks - 1 - step
        start = cidx * c
        return pl.ds(start if isinstance(start, int) else pl.multiple_of(start, c), c)

    def local_stages(step):
        slot = step % SCAN_SLOTS
        ch = []
        for d, (r_ref, v_ref, kk_ref, lw_ref, kd_ref, kka_ref, y_ref) in enumerate(dir_refs):
            rows = chunk_rows(d, step)
            lw = lw_ref[rows, :]
            hi = lw.astype(BF16)
            rem = lw - hi.astype(F32)
            mid = rem.astype(BF16)
            lo = (rem - mid.astype(F32)).astype(BF16)
            tri = tri_ref[d]
            cum = _dot(tri, hi) + _dot(tri, mid) + _dot(tri, lo)
            lp = cum[0:c]
            e_end = jnp.exp(cum[c:c + SUBLANES])
            e_n = jnp.exp(-lp)
            rq = (r_ref[rows, :] * jnp.exp(lp)).astype(BF16)
            kq = (kk_ref[rows, :] * jnp.exp(lp - lw)).astype(BF16)
            ks = kd_ref[rows, :] * e_n
            bs = kka_ref[rows, :] * e_n
            kq_ref[d, slot] = kq
            rq_ref[d, slot] = rq
            kdec_ref[d, slot] = (ks * e_end[0:1]).astype(BF16)
            bdec_ref[d, slot] = (bs * e_end[0:1]).astype(BF16)
            eend_ref[d, slot] = e_end
            for p in range(n_pairs):
                sl = slice(p * LANES, (p + 1) * LANES)
                ch.append(dict(d=d, sl=sl, kq=kq[:, sl], rq=rq[:, sl], ks=ks[:, sl], bs=bs[:, sl]))
        n = range(len(ch))
        strict = [strict_ref[x["d"]] for x in ch]
        incl = [incl_ref[x["d"]] for x in ch]
        yield
        lq = [jnp.concatenate([x["kq"], x["rq"]], axis=0) for x in ch]
        s_ks = [stack_b(x["ks"]) for x in ch]
        s_bs = [stack_b(x["bs"]) for x in ch]
        gkb = [_dot_nt(lq[i], jnp.concatenate([s_ks[i], s_bs[i]], axis=0)) for i in n]
        yield
        gk = [x[:, 0:LANES] for x in gkb]
        gb = [x[:, LANES:2 * LANES] for x in gkb]
        m = [-(gb[i][0:c] * strict[i]) for i in n]
        for i, x in enumerate(ch):
            avk_ref[x["d"], slot, :, x["sl"]] = (gk[i][0:c] * strict[i]).astype(BF16)
            ark_ref[x["d"], slot, :, x["sl"]] = (gk[i][c:2 * c] * incl[i]).astype(BF16)
            arb_ref[x["d"], slot, :, x["sl"]] = (gb[i][c:2 * c] * incl[i]).astype(BF16)
            m_ref[x["d"], slot, :, x["sl"]] = m[i].astype(BF16)
        t_inv = [eye + x for x in m]
        pw = [_dot(x.astype(BF16), stack_b(x)) for x in m]
        yield
        n_levels = int(math.log2(c)) - 1
        for lvl in range(n_levels):
            pw_bd = [stack_b(x) for x in pw]
            if lvl + 1 < n_levels:
                both = [_dot(jnp.concatenate([t_inv[i].astype(BF16), pw[i].astype(BF16)], axis=0), pw_bd[i])
                        for i in n]
                t_inv = [t_inv[i] + both[i][0:c] for i in n]
                pw = [both[i][c:2 * c] for i in n]
            else:
                t_inv = [t_inv[i] + _dot(t_inv[i].astype(BF16), pw_bd[i]) for i in n]
            yield
        for i, x in enumerate(ch):
            tinv_ref[x["d"], slot, :, x["sl"]] = t_inv[i].astype(BF16)

    def state_stages(step):
        slot = step % SCAN_SLOTS
        ch = []
        for d, (r_ref, v_ref, kk_ref, lw_ref, kd_ref, kka_ref, y_ref) in enumerate(dir_refs):
            rows = chunk_rows(d, step)
            for p in range(n_pairs):
                ch.append(dict(d=d, p=p, rows=rows, sl=slice(p * LANES, (p + 1) * LANES), y_ref=y_ref, v_ref=v_ref))
        n = range(len(ch))
        get = lambda ref: [ref[x["d"], slot, :, x["sl"]] for x in ch]
        v = [x["v_ref"][x["rows"], x["sl"]].astype(F32) for x in ch]
        s_vb = [stack_b(x) for x in v]
        st = [st_ref[x["d"], x["p"]] for x in ch]
        stb = [x.astype(BF16) for x in st]
        kq_b, rq_b, a_vk, a_rk, a_rb, t_b, m_b = (get(r) for r in (
            kq_ref, rq_ref, avk_ref, ark_ref, arb_ref, tinv_ref, m_ref))
        from_state = [_dot_nt(jnp.concatenate([kq_b[i], rq_b[i]], axis=0), stb[i]) for i in n]
        from_v = [_dot(jnp.concatenate([a_vk[i], a_rk[i]], axis=0), s_vb[i]) for i in n]
        yield
        rhs = [from_state[i][0:c] + from_v[i][0:c] for i in n]
        u0 = [-_dot(t_b[i], stack_b(rhs[i])) for i in n]
        yield
        res = [_dot(m_b[i], stack_b(u0[i])) - rhs[i] - u0[i] for i in n]
        yield
        u = [u0[i] + _dot(t_b[i], stack_b(res[i])) for i in n]
        yield
        y = [from_state[i][c:2 * c] + from_v[i][c:2 * c] + _dot(a_rb[i], stack_b(u[i])) for i in n]
        vu_t = [jnp.concatenate([v[i], u[i]], axis=0).T.astype(BF16) for i in n]
        kb = [jnp.concatenate([kdec_ref[x["d"], slot, :, x["sl"]], bdec_ref[x["d"], slot, :, x["sl"]]], axis=0)
              for x in ch]
        upd = [_dot(vu_t[i], kb[i]) * same_head for i in n]
        for i, x in enumerate(ch):
            x["y_ref"][x["rows"], x["sl"]] = y[i]
            st_ref[x["d"], x["p"]] = st[i] * eend_ref[x["d"], slot, 0:1, x["sl"]] + upd[i]

    def emit(*gens, order=None):
        live = {i: g for i, g in enumerate(gens)}
        for i in list(order or ()) + [None]:
            while i is None and live:
                for k in list(live):
                    if next(live[k], live) is live:
                        del live[k]
            if i in live and next(live[i], live) is live:
                del live[i]

    emit(local_stages(0))

    def body(i, carry):
        emit(local_stages(i + 1), state_stages(i), order=(0, 1, 0, 0, 1, 0, 0, 1, 0, 1, 0, 1, 0))
        return carry

    lax.fori_loop(0, n_chunks - 1, body, 0)
    emit(state_stages(n_chunks - 1))


def _scan(r, v, kk, lw, kd, kka, masks):
    bsz, s, w = r.shape
    tb = min(TB_SCAN, s)
    nblk = s // tb
    fwd = pl.BlockSpec((None, tb, w), lambda b_, j: (b_, j, 0))
    bwd = pl.BlockSpec((None, tb, w), lambda b_, j: (b_, nblk - 1 - j, 0))
    fwd2 = pl.BlockSpec((None, None, tb, w), lambda b_, j: (0, b_, j, 0))
    bwd2 = pl.BlockSpec((None, None, tb, w), lambda b_, j: (1, b_, nblk - 1 - j, 0))
    out = jax.ShapeDtypeStruct((bsz, s, w), F32)
    per_chunk = lambda dt: pltpu.VMEM((2, SCAN_SLOTS, CHUNK, w), dt)
    return pl.pallas_call(
        _scan_kernel,
        grid=(bsz, nblk),
        in_specs=[fwd, fwd, fwd, bwd, bwd, bwd, fwd2, fwd2, fwd2, bwd2, bwd2, bwd2]
        + [_resident(a.shape) for a in masks],
        out_specs=[fwd, bwd],
        out_shape=[out, out],
        scratch_shapes=[pltpu.VMEM((2, w // LANES, LANES, LANES), F32), per_chunk(BF16), per_chunk(BF16),
                        per_chunk(BF16), per_chunk(BF16), pltpu.VMEM((2, SCAN_SLOTS, SUBLANES, w), F32)]
        + [per_chunk(BF16)] * 5,
        compiler_params=_cparams("parallel", "arbitrary"),
    )(r, v, kk, r, v, kk, lw, kd, kka, lw, kd, kka, *masks)


def _rope_partner(x):
    n = x.shape[1]
    lane = lax.broadcasted_iota(jnp.int32, x.shape, 1)
    return jnp.where((lane & 31) < 16, pltpu.roll(x, n - 16, 1), pltpu.roll(x, 16, 1))


def _attn_kernel(q_ref, k_ref, v_ref, cq_ref, sq_ref, ck_ref, sk_ref, qg_ref, kg_ref, jq_ref, jk_ref,
                 o_ref, kd_ref, vd_ref):
    s_len = k_ref.shape[0]
    tq = q_ref.shape[0]
    tk = min(TK_PREP, s_len)

    @pl.when(pl.program_id(1) == 0)
    def _():
        def body(i, carry):
            rows = pl.ds(pl.multiple_of(i * tk, tk), tk)
            k = k_ref[rows, :].astype(F32)
            kn = k * lax.rsqrt(_segsum(k * k, jk_ref[...]) * (1.0 / HEAD) + NORM_EPS) * kg_ref[...]
            kr = kn * ck_ref[rows, :] + _rope_partner(kn) * sk_ref[rows, :]
            v = v_ref[rows, :].astype(F32)
            low = lax.broadcasted_iota(jnp.int32, kr.shape, 1) < HEAD
            ksw = pltpu.roll(kr, HEAD, 1)
            kd_ref[0, rows, :] = jnp.where(low, kr, ksw).astype(BF16)
            kd_ref[1, rows, :] = jnp.where(low, ksw, kr).astype(BF16)
            vd_ref[0, rows, :] = jnp.where(low, v, 1.0).astype(BF16)
            vd_ref[1, rows, :] = jnp.where(low, pltpu.roll(v, HEAD, 1), 1.0).astype(BF16)
            return carry
        lax.fori_loop(0, s_len // tk, body, 0)

    q = q_ref[...].astype(F32)
    qn = q * lax.rsqrt(_segsum(q * q, jq_ref[...]) * (1.0 / HEAD) + NORM_EPS) * qg_ref[...]
    qr = qn * cq_ref[...] + _rope_partner(qn) * sq_ref[...]
    low = lax.broadcasted_iota(jnp.int32, (tq, LANES), 1) < HEAD
    pairs_per_kv = AT_Q_HEADS // AT_KV_HEADS // 2
    tkc = min(TK_ATTN, s_len)
    n_kc = s_len // tkc
    qs = []
    for g in range(AT_KV_HEADS):
        parts = []
        for jj in range(pairs_per_kv):
            qp = qr[:, (g * pairs_per_kv + jj) * LANES:(g * pairs_per_kv + jj + 1) * LANES]
            parts += [jnp.where(low, qp, 0.0), jnp.where(low, 0.0, qp)]
        qs.append(jnp.concatenate(parts, axis=0).astype(BF16))
    streams = [(g, kc) for g in range(AT_KV_HEADS) for kc in range(n_kc)]
    scores = lambda g, kc: _dot_nt(qs[g], kd_ref[g, kc * tkc:(kc + 1) * tkc, :])
    partial, s_next = {}, scores(*streams[0])
    for i, (g, kc) in enumerate(streams):
        s = s_next
        if i + 1 < len(streams):
            s_next = scores(*streams[i + 1])
        m = jnp.max(s, axis=-1, keepdims=True)
        p = jnp.exp2(s - m).astype(BF16)
        partial[g, kc] = (m, _dot(p, vd_ref[g, kc * tkc:(kc + 1) * tkc, :]))
    for g in range(AT_KV_HEADS):
        m_all = partial[g, 0][0]
        for kc in range(1, n_kc):
            m_all = jnp.maximum(m_all, partial[g, kc][0])
        o = sum(jnp.exp2(partial[g, kc][0] - m_all) * partial[g, kc][1] for kc in range(n_kc))
        o = o / pltpu.roll(o, HEAD, 1)
        for jj in range(pairs_per_kv):
            even = o[(2 * jj) * tq:(2 * jj + 1) * tq]
            odd = pltpu.roll(o[(2 * jj + 1) * tq:(2 * jj + 2) * tq], HEAD, 1)
            pair = jnp.where(low, even, odd)
            col = (g * pairs_per_kv + jj) * LANES
            o_ref[:, col:col + LANES] = pair.astype(BF16)


def _attn(at, tabs, q_g, k_g, jones):
    bsz, s, _ = at.shape
    tq = min(TQ_ATTN, s)
    cq, sq, ck, sk = tabs
    qw = AT_Q_HEADS * HEAD
    kvw = AT_KV_HEADS * HEAD
    cst = lambda a: pl.BlockSpec(a.shape, lambda b_, i: (0,) * a.ndim)
    jk = jones[0:kvw, 0:kvw]
    return pl.pallas_call(
        _attn_kernel,
        grid=(bsz, s // tq),
        in_specs=[pl.BlockSpec((None, tq, qw), lambda b_, i: (b_, i, 0)),
                  pl.BlockSpec((None, s, kvw), lambda b_, i: (b_, 0, qw // kvw)),
                  pl.BlockSpec((None, s, kvw), lambda b_, i: (b_, 0, qw // kvw + 1)),
                  pl.BlockSpec((tq, qw), lambda b_, i: (i, 0)),
                  pl.BlockSpec((tq, qw), lambda b_, i: (i, 0)),
                  cst(ck), cst(sk), cst(q_g), cst(k_g), cst(jones), cst(jk)],
        out_specs=pl.BlockSpec((None, tq, qw), lambda b_, i: (b_, i, 0)),
        out_shape=jax.ShapeDtypeStruct((bsz, s, qw), BF16),
        scratch_shapes=[pltpu.VMEM((AT_KV_HEADS, s, LANES), BF16), pltpu.VMEM((AT_KV_HEADS, s, LANES), BF16)],
        compiler_params=_cparams("parallel", "arbitrary"),
    )(at, at, at, cq, sq, ck, sk, q_g, k_g, jones, jk)


def _rope_tables(s):
    rows = s // GRID_W
    pos_r = jnp.repeat(jnp.arange(rows), GRID_W).astype(F32)
    pos_c = jnp.tile(jnp.arange(GRID_W), rows).astype(F32)
    half = HEAD // 2
    inv = ROPE_THETA ** (-jnp.arange(0, half, 2, dtype=F32) / half)
    ang_r = pos_r[:, None] * inv
    ang_c = pos_c[:, None] * inv
    cos = jnp.concatenate([jnp.cos(ang_r)] * 2 + [jnp.cos(ang_c)] * 2, axis=-1)
    sin = jnp.concatenate([-jnp.sin(ang_r), jnp.sin(ang_r), -jnp.sin(ang_c), jnp.sin(ang_c)], axis=-1)
    scale = HEAD ** -0.5 * math.log2(math.e)
    return (jnp.tile(cos, (1, AT_Q_HEADS)) * scale, jnp.tile(sin, (1, AT_Q_HEADS)) * scale,
            jnp.tile(cos, (1, AT_KV_HEADS)), jnp.tile(sin, (1, AT_KV_HEADS)))


def _merge_kernel(y0_ref, y1_ref, g_ref, bonus_ref, lng_ref, lnb_ref, j_ref, ycv_ref, yat_ref, sg_ref, x_ref,
                  wb_ref, wo_ref, o_ref):
    j = j_ref[...]
    y = y0_ref[...] + y1_ref[...]
    mu = _segsum(y, j) * (1.0 / HEAD)
    dev = y - mu
    var = _segsum(dev * dev, j) * (1.0 / HEAD)
    yn = dev * lax.rsqrt(var + RW_LN_EPS) * lng_ref[...] + lnb_ref[...]
    y_rw = ((yn + bonus_ref[...]) * g_ref[...]).astype(BF16)
    dm = D_MODEL
    merged = (sg_ref[:, 0:dm].astype(F32) * _dot(y_rw, wb_ref[0])
              + sg_ref[:, dm:2 * dm].astype(F32) * _dot(ycv_ref[...], wb_ref[1])
              + sg_ref[:, 2 * dm:3 * dm].astype(F32) * _dot(yat_ref[...], wb_ref[2]))
    o_ref[...] = x_ref[...] + _dot(merged.astype(BF16), wo_ref[...])


def _merge(yf, yb, g, bonus, ln_g, ln_b, jones, y_cv, y_at, sg, x, w_branch, w_out):
    t = x.shape[0]
    tm = TM_MERGE
    row = lambda n: pl.BlockSpec((tm, n), lambda i: (i, 0))
    return pl.pallas_call(
        _merge_kernel,
        grid=(t // tm,),
        in_specs=[row(RW_WIDTH), row(RW_WIDTH), row(RW_WIDTH), row(RW_WIDTH),
                  _resident(ln_g.shape), _resident(ln_b.shape), _resident(jones.shape),
                  row(CV_WIDTH), row(AT_Q_HEADS * HEAD), row(GATE_COLS), row(D_MODEL),
                  _resident(w_branch.shape), _resident(w_out.shape)],
        out_specs=row(D_MODEL),
        out_shape=jax.ShapeDtypeStruct((t, D_MODEL), F32),
        compiler_params=_cparams("parallel"),
    )(yf, yb, g, bonus, ln_g, ln_b, jones, y_cv, y_at, sg, x, w_branch, w_out)


def _mlp_kernel(x_ref, g_ref, w1_ref, w2_ref, o_ref):
    x = x_ref[...]
    h = (x * lax.rsqrt(jnp.mean(x * x, axis=-1, keepdims=True) + NORM_EPS) * g_ref[...]).astype(BF16)
    acc = x
    for c in range(D_FF // FF_CHUNK):
        cols = slice(c * FF_CHUNK, (c + 1) * FF_CHUNK)
        u = jnp.maximum(_dot(h, w1_ref[:, cols]), 0.0)
        acc = acc + _dot((u * u).astype(BF16), w2_ref[cols, :])
    o_ref[...] = acc


def _mlp(x, g, w1, w2):
    t = x.shape[0]
    tm = TM_MLP
    row = pl.BlockSpec((tm, D_MODEL), lambda i: (i, 0))
    return pl.pallas_call(
        _mlp_kernel,
        grid=(t // tm,),
        in_specs=[row, _resident(g.shape), _resident(w1.shape), _resident(w2.shape)],
        out_specs=row,
        out_shape=jax.ShapeDtypeStruct((t, D_MODEL), F32),
        compiler_params=_cparams("parallel"),
    )(x, g, w1, w2)


def _block_diag2(w):
    z = jnp.zeros_like(w[0])
    return jnp.concatenate([jnp.concatenate([w[0], z], axis=1), jnp.concatenate([z, w[1]], axis=1)], axis=0)


def kernel(x, norm1_g, w_in, rw_mu_prev, rw_mu_next, rw_w0, rw_w2, rw_a0, rw_a2, rw_g2, rw_k_k, rw_k_a, rw_r_k,
           rw_ln_g, rw_ln_b, cv_w, cv_b, at_q_g, at_k_g, w_branch, w_out, norm2_g, w_mlp1, w_mlp2):
    bsz, s, dm = x.shape
    assert dm == D_MODEL and s % GRID_W == 0
    for tile in (TM_INPROJ, TB_SCAN, TQ_ATTN, TK_PREP):
        assert s % min(tile, s) == 0 and min(tile, s) % CHUNK == 0
    t = bsz * s
    tabs = _rope_tables(s)
    masks = _scan_masks()
    jones = jnp.asarray(np.kron(np.eye(RW_WIDTH // HEAD, dtype=np.float32), np.ones((HEAD, HEAD), np.float32)), BF16)
    row = lambda a: a.reshape(1, -1)
    xf = x.reshape(t, dm)
    for l in range(DEPTH):
        r, v, kk, g, bonus, lw, kd, kka, y_cv, at, sg = _inproj(
            xf.reshape(bsz, s, dm), row(norm1_g[l]), w_in[l].astype(BF16), row(rw_mu_prev[l]), row(rw_mu_next[l]),
            row(rw_w0[l]), _block_diag2(rw_w2[l]).astype(BF16), row(rw_a0[l]), _block_diag2(rw_a2[l]).astype(BF16),
            rw_g2[l].astype(BF16), row(rw_k_k[l]), row(rw_k_a[l]), row(rw_r_k[l]), jones, cv_w[l], row(cv_b[l]))
        yf, yb = _scan(r, v, kk, lw, kd, kka, masks)
        y_at = _attn(at, tabs, row(jnp.tile(at_q_g[l], AT_Q_HEADS)),
                     row(jnp.tile(at_k_g[l], AT_KV_HEADS)), jones)
        xf = _merge(yf.reshape(t, RW_WIDTH), yb.reshape(t, RW_WIDTH), g.reshape(t, RW_WIDTH),
                    bonus.reshape(t, RW_WIDTH),
                    row(rw_ln_g[l]), row(rw_ln_b[l]), jones, y_cv.reshape(t, CV_WIDTH),
                    y_at.reshape(t, AT_Q_HEADS * HEAD), sg.reshape(t, GATE_COLS), xf,
                    w_branch[l].astype(BF16), w_out[l].astype(BF16))
        xf = _mlp(xf, row(norm2_g[l]), w_mlp1[l].astype(BF16), w_mlp2[l].astype(BF16))
    return xf.reshape(bsz, s, dm)
```

```python
import math

import numpy as np
import jax
import jax.numpy as jnp
from jax import lax
from jax.experimental import pallas as pl
from jax.experimental.pallas import tpu as pltpu

F32 = jnp.float32
BF16 = jnp.bfloat16

D_MODEL = 1024
DEPTH = 4
GRID_W = 64
NORM_EPS = 1e-6
HEAD = 64
RW_WIDTH = 512
RW_LORA = 128
RW_LN_EPS = 64e-5
RW_COLS = 3 * RW_WIDTH + 3 * RW_LORA
CV_WIDTH = 512
CV_COLS = 3 * CV_WIDTH
AT_Q_HEADS = 8
AT_KV_HEADS = 2
AT_COLS = (AT_Q_HEADS + 2 * AT_KV_HEADS) * HEAD
GATE_COLS = 3 * D_MODEL
ROPE_THETA = 10000.0
D_FF = 4 * D_MODEL

LANES = 128
SUBLANES = 8
VMEM_LIMIT_BYTES = 56 * 1024 * 1024
CHUNK = 64
TM_INPROJ = 256
TB_SCAN = 512
TQ_ATTN = 128
TK_ATTN = 2048
TK_PREP = 512
TM_MERGE = 512
TM_MLP = 512
FF_CHUNK = 1024


def _cparams(*sem):
    return pltpu.CompilerParams(dimension_semantics=sem, vmem_limit_bytes=VMEM_LIMIT_BYTES)


def _dot(a, b):
    return jnp.dot(a, b, preferred_element_type=F32)


def _dot_nt(a, b):
    return lax.dot_general(a, b, (((1,), (1,)), ((), ())), preferred_element_type=F32)


def _segsum(x, j):
    return _dot(x.astype(BF16), j)


def _segsum_split(x, j):
    hi = x.astype(BF16)
    return _dot(hi, j) + _dot((x - hi.astype(F32)).astype(BF16), j)


def _sigmoid(x):
    return 1.0 / (1.0 + jnp.exp(-x))


def _resident(shape):
    nd = len(shape)
    return pl.BlockSpec(shape, lambda *_: (0,) * nd, pipeline_mode=pl.Buffered(1))


def _shift_rows(c, halo_prev, halo_next, first, last):
    n = c.shape[0]
    row = lax.broadcasted_iota(jnp.int32, c.shape, 0)
    lane_row = lax.broadcasted_iota(jnp.int32, (1, c.shape[1]), 1)
    is_first = (lane_row * 0 + first.astype(jnp.int32)) > 0
    is_last = (lane_row * 0 + last.astype(jnp.int32)) > 0
    prev_row = jnp.where(is_first, 0.0, halo_prev[halo_prev.shape[0] - 1:, :])
    next_row = jnp.where(is_last, 0.0, halo_next[0:1, :])
    prev = jnp.where(row == 0, prev_row, pltpu.roll(c, 1, 0))
    nxt = jnp.where(row == n - 1, next_row, pltpu.roll(c, n - 1, 0))
    return prev, nxt


def _halo_specs(ts, width, n_seq_rows, rows):
    r = ts // rows
    last_blk = n_seq_rows // rows - 1
    prev = pl.BlockSpec((None, rows, width), lambda b, i: (b, jnp.maximum(i * r - 1, 0), 0))
    nxt = pl.BlockSpec((None, rows, width), lambda b, i: (b, jnp.minimum((i + 1) * r, last_blk), 0))
    return prev, nxt


def _inproj_kernel(x_ref, xp_ref, xn_ref, g_ref, w_ref, mup_ref, mun_ref, w0_ref, w2_ref, a0_ref, a2_ref, g2_ref,
                   kk_w_ref, ka_ref, rk_ref, j_ref, cvw_ref, cvb_ref,
                   r_ref, v_ref, kk_ref, gate_ref, bonus_ref, lw_ref, kd_ref, kka_ref, ycv_ref, at_ref, sg_ref):
    i = pl.program_id(1)

    def normed(x):
        return (x * lax.rsqrt(jnp.mean(x * x, axis=-1, keepdims=True) + NORM_EPS) * g_ref[...]).astype(BF16)

    h = normed(x_ref[...])
    o0, o1, o2 = RW_COLS, RW_COLS + CV_COLS, RW_COLS + CV_COLS + AT_COLS
    dm = D_MODEL

    def gate_cols(n):
        sg_ref[:, n * dm:(n + 1) * dm] = _sigmoid(_dot(h, w_ref[:, o2 + n * dm:o2 + (n + 1) * dm])).astype(BF16)

    c = _dot(h, w_ref[:, 0:o0])
    h_halo = normed(jnp.concatenate([xp_ref[...], xn_ref[...]], axis=0))
    c_halo = _dot(h_halo, w_ref[:, 0:o0])
    first, last = i == 0, i == pl.num_programs(1) - 1
    cw = CV_WIDTH
    cv = _dot(h, w_ref[:, o0:o1])
    cv_halo = _dot(h_halo, w_ref[:, o0 + cw:o1])
    zc = cv[:, cw:2 * cw] * cv[:, 2 * cw:3 * cw]
    zc_halo = cv_halo[:, 0:cw] * cv_halo[:, cw:2 * cw]
    zc_prev, zc_next = _shift_rows(zc, zc_halo[0:SUBLANES], zc_halo[SUBLANES:2 * SUBLANES], first, last)
    conv = cvb_ref[...] + cvw_ref[0:1, :] * zc_prev + cvw_ref[1:2, :] * zc + cvw_ref[2:3, :] * zc_next
    ycv_ref[...] = (cv[:, 0:cw] * conv).astype(BF16)
    prev, nxt = _shift_rows(c, c_halo[0:SUBLANES], c_halo[SUBLANES:2 * SUBLANES], first, last)
    z = c + mup_ref[...] * (prev - c) + mun_ref[...] * (nxt - c)
    w = RW_WIDTH
    r, k, v = z[:, 0:w], z[:, w:2 * w], z[:, 2 * w:3 * w]
    wd = z[:, 3 * w:3 * w + RW_LORA]
    ad = z[:, 3 * w + RW_LORA:3 * w + 2 * RW_LORA]
    gd = z[:, 3 * w + 2 * RW_LORA:3 * w + 3 * RW_LORA]
    j = j_ref[...]
    w_raw = w0_ref[...] + _dot(jnp.tanh(wd).astype(BF16), w2_ref[...])
    lw = -math.exp(-0.5) * _sigmoid(w_raw)
    a = _sigmoid(a0_ref[...] + _dot(ad.astype(BF16), a2_ref[...]))
    gate_ref[...] = _dot(_sigmoid(gd).astype(BF16), g2_ref[...]).astype(BF16)
    gate_cols(0)
    kkr = k * kk_w_ref[...]
    kk = kkr / jnp.maximum(jnp.sqrt(_segsum_split(kkr * kkr, j)), 1e-12)
    r_ref[...] = r.astype(BF16)
    v_ref[...] = v.astype(BF16)
    kk_ref[...] = kk.astype(BF16)
    at_ref[...] = _dot(h, w_ref[:, o1:o2]).astype(BF16)
    gate_cols(1)
    ksum = jnp.zeros_like(k)
    for d in range(2):
        a_d = a[:, d * w:(d + 1) * w]
        k_d = k * (1.0 + (a_d - 1.0) * ka_ref[...])
        ksum = ksum + k_d
        lw_ref[d] = lw[:, d * w:(d + 1) * w]
        kd_ref[d] = k_d.astype(BF16)
        kka_ref[d] = (kk * a_d).astype(BF16)
    gate_cols(2)
    bonus_ref[...] = (_segsum_split(r * (0.5 * ksum) * rk_ref[...], j) * v).astype(BF16)


def _inproj(x, g, w, mup, mun, w0, w2bd, a0, a2bd, g2, k_k, k_a, r_k, jones, cv_w, cv_b):
    bsz, s, _ = x.shape
    tm = min(TM_INPROJ, s)
    hp, hn = _halo_specs(tm, D_MODEL, s, SUBLANES)
    tok = lambda n: pl.BlockSpec((None, tm, n), lambda b_, i: (b_, i, 0))
    tok2 = pl.BlockSpec((2, None, tm, RW_WIDTH), lambda b_, i: (0, b_, i, 0))
    consts = (g, w, mup, mun, w0, w2bd, a0, a2bd, g2, k_k, k_a, r_k, jones, cv_w, cv_b)
    one = lambda n: jax.ShapeDtypeStruct((bsz, s, n), BF16)
    two = lambda dt: jax.ShapeDtypeStruct((2, bsz, s, RW_WIDTH), dt)
    return pl.pallas_call(
        _inproj_kernel,
        grid=(bsz, s // tm),
        in_specs=[tok(D_MODEL), hp, hn] + [_resident(a.shape) for a in consts],
        out_specs=[tok(RW_WIDTH)] * 5 + [tok2] * 3 + [tok(CV_WIDTH), tok(AT_COLS), tok(GATE_COLS)],
        out_shape=[one(RW_WIDTH)] * 5 + [two(F32), two(BF16), two(BF16)]
        + [one(CV_WIDTH), one(AT_COLS), one(GATE_COLS)],
        compiler_params=_cparams("parallel", "arbitrary"),
    )(x, x, x, *consts)


def _scan_masks():
    c = CHUNK
    t = np.arange(c)
    lower = (t[None, :] <= t[:, None]).astype(np.float32)
    strict_l = (t[None, :] < t[:, None]).astype(np.float32)
    ones = np.ones((c, c), np.float32)
    tri = np.stack([np.concatenate([lower, ones], 0), np.concatenate([lower.T, ones], 0)])
    strict = np.stack([np.tile(strict_l, (1, 2)), np.tile(strict_l.T, (1, 2))])
    incl = np.stack([np.tile(lower, (1, 2)), np.tile(lower.T, (1, 2))])
    return jnp.asarray(tri, BF16), jnp.asarray(strict, F32), jnp.asarray(incl, F32)


def _scan_kernel(rf_ref, vf_ref, kkf_ref, rb_ref, vb_ref, kkb_ref,
                 lwf_ref, kdf_ref, kkaf_ref, lwb_ref, kdb_ref, kkab_ref,
                 tri_ref, strict_ref, incl_ref, yf_ref, yb_ref, st_ref,
                 kq_ref, rq_ref, kdec_ref, bdec_ref, eend_ref, avk_ref, ark_ref, arb_ref, m_ref, tinv_ref):
    c = CHUNK
    n_chunks = rf_ref.shape[0] // c
    n_pairs = RW_WIDTH // LANES

    @pl.when(pl.program_id(1) == 0)
    def _():
        st_ref[...] = jnp.zeros_like(st_ref)

    lane = lax.broadcasted_iota(jnp.int32, (c, LANES), 1)
    row = lax.broadcasted_iota(jnp.int32, (c, LANES), 0)
    head0 = lane < HEAD
    eye = jnp.where((lane & (HEAD - 1)) == row, 1.0, 0.0).astype(F32)
    ri = lax.broadcasted_iota(jnp.int32, (LANES, LANES), 0)
    ci = lax.broadcasted_iota(jnp.int32, (LANES, LANES), 1)
    same_head = jnp.where((ri < HEAD) == (ci < HEAD), 1.0, 0.0).astype(F32)

    def stack(x):
        return jnp.concatenate([jnp.where(head0, x, 0.0), jnp.where(head0, 0.0, x)], axis=0)

    def stack_b(x):
        return stack(x).astype(BF16)

    dir_refs = ((rf_ref, vf_ref, kkf_ref, lwf_ref, kdf_ref, kkaf_ref, yf_ref),
                (rb_ref, vb_ref, kkb_ref, lwb_ref, kdb_ref, kkab_ref, yb_ref))

    def chunk_rows(d, step):
        cidx = step if d == 0 else n_chunks - 1 - step
        start = cidx * c
        return pl.ds(start if isinstance(start, int) else pl.multiple_of(start, c), c)

    def local_stages(step):
        slot = step
        ch = []
        for d, (r_ref, v_ref, kk_ref, lw_ref, kd_ref, kka_ref, y_ref) in enumerate(dir_refs):
            rows = chunk_rows(d, step)
            lw = lw_ref[rows, :]
            hi = lw.astype(BF16)
            rem = lw - hi.astype(F32)
            mid = rem.astype(BF16)
            lo = (rem - mid.astype(F32)).astype(BF16)
            tri = tri_ref[d]
            cum = _dot(tri, hi) + _dot(tri, mid) + _dot(tri, lo)
            lp = cum[0:c]
            e_end = jnp.exp(cum[c:c + SUBLANES])
            e_n = jnp.exp(-lp)
            rq = (r_ref[rows, :] * jnp.exp(lp)).astype(BF16)
            kq = (kk_ref[rows, :] * jnp.exp(lp - lw)).astype(BF16)
            ks = kd_ref[rows, :] * e_n
            bs = kka_ref[rows, :] * e_n
            kq_ref[d, slot] = kq
            rq_ref[d, slot] = rq
            kdec_ref[d, slot] = (ks * e_end[0:1]).astype(BF16)
            bdec_ref[d, slot] = (bs * e_end[0:1]).astype(BF16)
            eend_ref[d, slot] = e_end
            for p in range(n_pairs):
                sl = slice(p * LANES, (p + 1) * LANES)
                ch.append(dict(d=d, sl=sl, kq=kq[:, sl], rq=rq[:, sl], ks=ks[:, sl], bs=bs[:, sl]))
        n = range(len(ch))
        strict = [strict_ref[x["d"]] for x in ch]
        incl = [incl_ref[x["d"]] for x in ch]
        yield
        lq = [jnp.concatenate([x["kq"], x["rq"]], axis=0) for x in ch]
        s_ks = [stack_b(x["ks"]) for x in ch]
        s_bs = [stack_b(x["bs"]) for x in ch]
        gkb = [_dot_nt(lq[i], jnp.concatenate([s_ks[i], s_bs[i]], axis=0)) for i in n]
        yield
        gk = [x[:, 0:LANES] for x in gkb]
        gb = [x[:, LANES:2 * LANES] for x in gkb]
        m = [-(gb[i][0:c] * strict[i]) for i in n]
        for i, x in enumerate(ch):
            avk_ref[x["d"], slot, :, x["sl"]] = (gk[i][0:c] * strict[i]).astype(BF16)
            ark_ref[x["d"], slot, :, x["sl"]] = (gk[i][c:2 * c] * incl[i]).astype(BF16)
            arb_ref[x["d"], slot, :, x["sl"]] = (gb[i][c:2 * c] * incl[i]).astype(BF16)
            m_ref[x["d"], slot, :, x["sl"]] = m[i].astype(BF16)
        t_inv = [eye + x for x in m]
        pw = [_dot(x.astype(BF16), stack_b(x)) for x in m]
        yield
        n_levels = int(math.log2(c)) - 1
        for lvl in range(n_levels):
            pw_bd = [stack_b(x) for x in pw]
            if lvl + 1 < n_levels:
                both = [_dot(jnp.concatenate([t_inv[i].astype(BF16), pw[i].astype(BF16)], axis=0), pw_bd[i])
                        for i in n]
                t_inv = [t_inv[i] + both[i][0:c] for i in n]
                pw = [both[i][c:2 * c] for i in n]
            else:
                t_inv = [t_inv[i] + _dot(t_inv[i].astype(BF16), pw_bd[i]) for i in n]
            yield
        for i, x in enumerate(ch):
            tinv_ref[x["d"], slot, :, x["sl"]] = t_inv[i].astype(BF16)

    def state_stages(step):
        slot = step
        ch = []
        for d, (r_ref, v_ref, kk_ref, lw_ref, kd_ref, kka_ref, y_ref) in enumerate(dir_refs):
            rows = chunk_rows(d, step)
            for p in range(n_pairs):
                ch.append(dict(d=d, p=p, rows=rows, sl=slice(p * LANES, (p + 1) * LANES), y_ref=y_ref, v_ref=v_ref))
        n = range(len(ch))
        get = lambda ref: [ref[x["d"], slot, :, x["sl"]] for x in ch]
        v = [x["v_ref"][x["rows"], x["sl"]].astype(F32) for x in ch]
        s_vb = [stack_b(x) for x in v]
        st = [st_ref[x["d"], x["p"]] for x in ch]
        stb = [x.astype(BF16) for x in st]
        kq_b, rq_b, a_vk, a_rk, a_rb, t_b, m_b = (get(r) for r in (
            kq_ref, rq_ref, avk_ref, ark_ref, arb_ref, tinv_ref, m_ref))
        from_state = [_dot_nt(jnp.concatenate([kq_b[i], rq_b[i]], axis=0), stb[i]) for i in n]
        from_v = [_dot(jnp.concatenate([a_vk[i], a_rk[i]], axis=0), s_vb[i]) for i in n]
        yield
        rhs = [from_state[i][0:c] + from_v[i][0:c] for i in n]
        u0 = [-_dot(t_b[i], stack_b(rhs[i])) for i in n]
        yield
        res = [_dot(m_b[i], stack_b(u0[i])) - rhs[i] - u0[i] for i in n]
        yield
        u = [u0[i] + _dot(t_b[i], stack_b(res[i])) for i in n]
        yield
        y = [from_state[i][c:2 * c] + from_v[i][c:2 * c] + _dot(a_rb[i], stack_b(u[i])) for i in n]
        vu_t = [jnp.concatenate([v[i], u[i]], axis=0).T.astype(BF16) for i in n]
        kb = [jnp.concatenate([kdec_ref[x["d"], slot, :, x["sl"]], bdec_ref[x["d"], slot, :, x["sl"]]], axis=0)
              for x in ch]
        upd = [_dot(vu_t[i], kb[i]) * same_head for i in n]
        for i, x in enumerate(ch):
            x["y_ref"][x["rows"], x["sl"]] = y[i]
            st_ref[x["d"], x["p"]] = st[i] * eend_ref[x["d"], slot, 0:1, x["sl"]] + upd[i]

    def emit(*gens, order=None):
        live = {i: g for i, g in enumerate(gens)}
        for i in list(order or ()) + [None]:
            while i is None and live:
                for k in list(live):
                    if next(live[k], live) is live:
                        del live[k]
            if i in live and next(live[i], live) is live:
                del live[i]

    emit(local_stages(0))

    def body(i, carry):
        emit(local_stages(i + 1), state_stages(i), order=(0, 1, 0, 0, 1, 0, 0, 1, 0, 1, 0, 1, 0))
        return carry

    lax.fori_loop(0, n_chunks - 1, body, 0)
    emit(state_stages(n_chunks - 1))


def _scan(r, v, kk, lw, kd, kka, masks):
    bsz, s, w = r.shape
    tb = min(TB_SCAN, s)
    nblk = s // tb
    fwd = pl.BlockSpec((None, tb, w), lambda b_, j: (b_, j, 0))
    bwd = pl.BlockSpec((None, tb, w), lambda b_, j: (b_, nblk - 1 - j, 0))
    fwd2 = pl.BlockSpec((None, None, tb, w), lambda b_, j: (0, b_, j, 0))
    bwd2 = pl.BlockSpec((None, None, tb, w), lambda b_, j: (1, b_, nblk - 1 - j, 0))
    out = jax.ShapeDtypeStruct((bsz, s, w), F32)
    per_chunk = lambda dt: pltpu.VMEM((2, tb // CHUNK, CHUNK, w), dt)
    return pl.pallas_call(
        _scan_kernel,
        grid=(bsz, nblk),
        in_specs=[fwd, fwd, fwd, bwd, bwd, bwd, fwd2, fwd2, fwd2, bwd2, bwd2, bwd2]
        + [_resident(a.shape) for a in masks],
        out_specs=[fwd, bwd],
        out_shape=[out, out],
        scratch_shapes=[pltpu.VMEM((2, w // LANES, LANES, LANES), F32), per_chunk(BF16), per_chunk(BF16),
                        per_chunk(BF16), per_chunk(BF16), pltpu.VMEM((2, tb // CHUNK, SUBLANES, w), F32)]
        + [per_chunk(BF16)] * 5,
        compiler_params=_cparams("parallel", "arbitrary"),
    )(r, v, kk, r, v, kk, lw, kd, kka, lw, kd, kka, *masks)


def _rope_partner(x):
    n = x.shape[1]
    lane = lax.broadcasted_iota(jnp.int32, x.shape, 1)
    return jnp.where((lane & 31) < 16, pltpu.roll(x, n - 16, 1), pltpu.roll(x, 16, 1))


def _attn_kernel(q_ref, k_ref, v_ref, cq_ref, sq_ref, ck_ref, sk_ref, qg_ref, kg_ref, jq_ref, jk_ref,
                 o_ref, kd_ref, vd_ref):
    s_len = k_ref.shape[0]
    tq = q_ref.shape[0]
    tk = min(TK_PREP, s_len)

    @pl.when(pl.program_id(1) == 0)
    def _():
        def body(i, carry):
            rows = pl.ds(pl.multiple_of(i * tk, tk), tk)
            k = k_ref[rows, :].astype(F32)
            kn = k * lax.rsqrt(_segsum(k * k, jk_ref[...]) * (1.0 / HEAD) + NORM_EPS) * kg_ref[...]
            kr = kn * ck_ref[rows, :] + _rope_partner(kn) * sk_ref[rows, :]
            v = v_ref[rows, :].astype(F32)
            low = lax.broadcasted_iota(jnp.int32, kr.shape, 1) < HEAD
            ksw = pltpu.roll(kr, HEAD, 1)
            kd_ref[0, rows, :] = jnp.where(low, kr, ksw).astype(BF16)
            kd_ref[1, rows, :] = jnp.where(low, ksw, kr).astype(BF16)
            vd_ref[0, rows, :] = jnp.where(low, v, 1.0).astype(BF16)
            vd_ref[1, rows, :] = jnp.where(low, pltpu.roll(v, HEAD, 1), 1.0).astype(BF16)
            return carry
        lax.fori_loop(0, s_len // tk, body, 0)

    q = q_ref[...].astype(F32)
    qn = q * lax.rsqrt(_segsum(q * q, jq_ref[...]) * (1.0 / HEAD) + NORM_EPS) * qg_ref[...]
    qr = qn * cq_ref[...] + _rope_partner(qn) * sq_ref[...]
    low = lax.broadcasted_iota(jnp.int32, (tq, LANES), 1) < HEAD
    pairs_per_kv = AT_Q_HEADS // AT_KV_HEADS // 2
    tkc = min(TK_ATTN, s_len)
    n_kc = s_len // tkc
    qs = []
    for g in range(AT_KV_HEADS):
        parts = []
        for jj in range(pairs_per_kv):
            qp = qr[:, (g * pairs_per_kv + jj) * LANES:(g * pairs_per_kv + jj + 1) * LANES]
            parts += [jnp.where(low, qp, 0.0), jnp.where(low, 0.0, qp)]
        qs.append(jnp.concatenate(parts, axis=0).astype(BF16))
    streams = [(g, kc) for g in range(AT_KV_HEADS) for kc in range(n_kc)]
    scores = lambda g, kc: _dot_nt(qs[g], kd_ref[g, kc * tkc:(kc + 1) * tkc, :])
    partial, s_next = {}, scores(*streams[0])
    for i, (g, kc) in enumerate(streams):
        s = s_next
        if i + 1 < len(streams):
            s_next = scores(*streams[i + 1])
        m = jnp.max(s, axis=-1, keepdims=True)
        p = jnp.exp2(s - m).astype(BF16)
        partial[g, kc] = (m, _dot(p, vd_ref[g, kc * tkc:(kc + 1) * tkc, :]))
    for g in range(AT_KV_HEADS):
        m_all = partial[g, 0][0]
        for kc in range(1, n_kc):
            m_all = jnp.maximum(m_all, partial[g, kc][0])
        o = sum(jnp.exp2(partial[g, kc][0] - m_all) * partial[g, kc][1] for kc in range(n_kc))
        o = o / pltpu.roll(o, HEAD, 1)
        for jj in range(pairs_per_kv):
            even = o[(2 * jj) * tq:(2 * jj + 1) * tq]
            odd = pltpu.roll(o[(2 * jj + 1) * tq:(2 * jj + 2) * tq], HEAD, 1)
            pair = jnp.where(low, even, odd)
            col = (g * pairs_per_kv + jj) * LANES
            o_ref[:, col:col + LANES] = pair.astype(BF16)


def _attn(at, tabs, q_g, k_g, jones):
    bsz, s, _ = at.shape
    tq = min(TQ_ATTN, s)
    cq, sq, ck, sk = tabs
    qw = AT_Q_HEADS * HEAD
    kvw = AT_KV_HEADS * HEAD
    cst = lambda a: pl.BlockSpec(a.shape, lambda b_, i: (0,) * a.ndim)
    jk = jones[0:kvw, 0:kvw]
    return pl.pallas_call(
        _attn_kernel,
        grid=(bsz, s // tq),
        in_specs=[pl.BlockSpec((None, tq, qw), lambda b_, i: (b_, i, 0)),
                  pl.BlockSpec((None, s, kvw), lambda b_, i: (b_, 0, qw // kvw)),
                  pl.BlockSpec((None, s, kvw), lambda b_, i: (b_, 0, qw // kvw + 1)),
                  pl.BlockSpec((tq, qw), lambda b_, i: (i, 0)),
                  pl.BlockSpec((tq, qw), lambda b_, i: (i, 0)),
                  cst(ck), cst(sk), cst(q_g), cst(k_g), cst(jones), cst(jk)],
        out_specs=pl.BlockSpec((None, tq, qw), lambda b_, i: (b_, i, 0)),
        out_shape=jax.ShapeDtypeStruct((bsz, s, qw), BF16),
        scratch_shapes=[pltpu.VMEM((AT_KV_HEADS, s, LANES), BF16), pltpu.VMEM((AT_KV_HEADS, s, LANES), BF16)],
        compiler_params=_cparams("parallel", "arbitrary"),
    )(at, at, at, cq, sq, ck, sk, q_g, k_g, jones, jk)


def _rope_tables(s):
    rows = s // GRID_W
    pos_r = jnp.repeat(jnp.arange(rows), GRID_W).astype(F32)
    pos_c = jnp.tile(jnp.arange(GRID_W), rows).astype(F32)
    half = HEAD // 2
    inv = ROPE_THETA ** (-jnp.arange(0, half, 2, dtype=F32) / half)
    ang_r = pos_r[:, None] * inv
    ang_c = pos_c[:, None] * inv
    cos = jnp.concatenate([jnp.cos(ang_r)] * 2 + [jnp.cos(ang_c)] * 2, axis=-1)
    sin = jnp.concatenate([-jnp.sin(ang_r), jnp.sin(ang_r), -jnp.sin(ang_c), jnp.sin(ang_c)], axis=-1)
    scale = HEAD ** -0.5 * math.log2(math.e)
    return (jnp.tile(cos, (1, AT_Q_HEADS)) * scale, jnp.tile(sin, (1, AT_Q_HEADS)) * scale,
            jnp.tile(cos, (1, AT_KV_HEADS)), jnp.tile(sin, (1, AT_KV_HEADS)))


def _merge_kernel(y0_ref, y1_ref, g_ref, bonus_ref, lng_ref, lnb_ref, j_ref, ycv_ref, yat_ref, sg_ref, x_ref,
                  wb_ref, wo_ref, o_ref):
    j = j_ref[...]
    y = y0_ref[...] + y1_ref[...]
    mu = _segsum(y, j) * (1.0 / HEAD)
    dev = y - mu
    var = _segsum(dev * dev, j) * (1.0 / HEAD)
    yn = dev * lax.rsqrt(var + RW_LN_EPS) * lng_ref[...] + lnb_ref[...]
    y_rw = ((yn + bonus_ref[...]) * g_ref[...]).astype(BF16)
    dm = D_MODEL
    merged = (sg_ref[:, 0:dm].astype(F32) * _dot(y_rw, wb_ref[0])
              + sg_ref[:, dm:2 * dm].astype(F32) * _dot(ycv_ref[...], wb_ref[1])
              + sg_ref[:, 2 * dm:3 * dm].astype(F32) * _dot(yat_ref[...], wb_ref[2]))
    o_ref[...] = x_ref[...] + _dot(merged.astype(BF16), wo_ref[...])


def _merge(yf, yb, g, bonus, ln_g, ln_b, jones, y_cv, y_at, sg, x, w_branch, w_out):
    t = x.shape[0]
    tm = TM_MERGE
    row = lambda n: pl.BlockSpec((tm, n), lambda i: (i, 0))
    return pl.pallas_call(
        _merge_kernel,
        grid=(t // tm,),
        in_specs=[row(RW_WIDTH), row(RW_WIDTH), row(RW_WIDTH), row(RW_WIDTH),
                  _resident(ln_g.shape), _resident(ln_b.shape), _resident(jones.shape),
                  row(CV_WIDTH), row(AT_Q_HEADS * HEAD), row(GATE_COLS), row(D_MODEL),
                  _resident(w_branch.shape), _resident(w_out.shape)],
        out_specs=row(D_MODEL),
        out_shape=jax.ShapeDtypeStruct((t, D_MODEL), F32),
        compiler_params=_cparams("parallel"),
    )(yf, yb, g, bonus, ln_g, ln_b, jones, y_cv, y_at, sg, x, w_branch, w_out)


def _mlp_kernel(x_ref, g_ref, w1_ref, w2_ref, o_ref):
    x = x_ref[...]
    h = (x * lax.rsqrt(jnp.mean(x * x, axis=-1, keepdims=True) + NORM_EPS) * g_ref[...]).astype(BF16)
    acc = x
    for c in range(D_FF // FF_CHUNK):
        cols = slice(c * FF_CHUNK, (c + 1) * FF_CHUNK)
        u = jnp.maximum(_dot(h, w1_ref[:, cols]), 0.0)
        acc = acc + _dot((u * u).astype(BF16), w2_ref[cols, :])
    o_ref[...] = acc


def _mlp(x, g, w1, w2):
    t = x.shape[0]
    tm = TM_MLP
    row = pl.BlockSpec((tm, D_MODEL), lambda i: (i, 0))
    return pl.pallas_call(
        _mlp_kernel,
        grid=(t // tm,),
        in_specs=[row, _resident(g.shape), _resident(w1.shape), _resident(w2.shape)],
        out_specs=row,
        out_shape=jax.ShapeDtypeStruct((t, D_MODEL), F32),
        compiler_params=_cparams("parallel"),
    )(x, g, w1, w2)


def _block_diag2(w):
    z = jnp.zeros_like(w[0])
    return jnp.concatenate([jnp.concatenate([w[0], z], axis=1), jnp.concatenate([z, w[1]], axis=1)], axis=0)


def kernel(x, norm1_g, w_in, rw_mu_prev, rw_mu_next, rw_w0, rw_w2, rw_a0, rw_a2, rw_g2, rw_k_k, rw_k_a, rw_r_k,
           rw_ln_g, rw_ln_b, cv_w, cv_b, at_q_g, at_k_g, w_branch, w_out, norm2_g, w_mlp1, w_mlp2):
    bsz, s, dm = x.shape
    assert dm == D_MODEL and s % GRID_W == 0
    for tile in (TM_INPROJ, TB_SCAN, TQ_ATTN, TK_PREP):
        assert s % min(tile, s) == 0 and min(tile, s) % CHUNK == 0
    t = bsz * s
    tabs = _rope_tables(s)
    masks = _scan_masks()
    jones = jnp.asarray(np.kron(np.eye(RW_WIDTH // HEAD, dtype=np.float32), np.ones((HEAD, HEAD), np.float32)), BF16)
    row = lambda a: a.reshape(1, -1)
    xf = x.reshape(t, dm)
    for l in range(DEPTH):
        r, v, kk, g, bonus, lw, kd, kka, y_cv, at, sg = _inproj(
            xf.reshape(bsz, s, dm), row(norm1_g[l]), w_in[l].astype(BF16), row(rw_mu_prev[l]), row(rw_mu_next[l]),
            row(rw_w0[l]), _block_diag2(rw_w2[l]).astype(BF16), row(rw_a0[l]), _block_diag2(rw_a2[l]).astype(BF16),
            rw_g2[l].astype(BF16), row(rw_k_k[l]), row(rw_k_a[l]), row(rw_r_k[l]), jones, cv_w[l], row(cv_b[l]))
        yf, yb = _scan(r, v, kk, lw, kd, kka, masks)
        y_at = _attn(at, tabs, row(jnp.tile(at_q_g[l], AT_Q_HEADS)),
                     row(jnp.tile(at_k_g[l], AT_KV_HEADS)), jones)
        xf = _merge(yf.reshape(t, RW_WIDTH), yb.reshape(t, RW_WIDTH), g.reshape(t, RW_WIDTH),
                    bonus.reshape(t, RW_WIDTH),
                    row(rw_ln_g[l]), row(rw_ln_b[l]), jones, y_cv.reshape(t, CV_WIDTH),
                    y_at.reshape(t, AT_Q_HEADS * HEAD), sg.reshape(t, GATE_COLS), xf,
                    w_branch[l].astype(BF16), w_out[l].astype(BF16))
        xf = _mlp(xf, row(norm2_g[l]), w_mlp1[l].astype(BF16), w_mlp2[l].astype(BF16))
    return xf.reshape(bsz, s, dm)
```

```python
import math

import numpy as np
import jax
import jax.numpy as jnp
from jax import lax
from jax.experimental import pallas as pl
from jax.experimental.pallas import tpu as pltpu

F32 = jnp.float32
BF16 = jnp.bfloat16

D_MODEL = 1024
DEPTH = 4
GRID_W = 64
NORM_EPS = 1e-6
HEAD = 64
RW_WIDTH = 512
RW_LORA = 128
RW_LN_EPS = 64e-5
RW_COLS = 3 * RW_WIDTH + 3 * RW_LORA
CV_WIDTH = 512
CV_COLS = 3 * CV_WIDTH
AT_Q_HEADS = 8
AT_KV_HEADS = 2
AT_COLS = (AT_Q_HEADS + 2 * AT_KV_HEADS) * HEAD
GATE_COLS = 3 * D_MODEL
ROPE_THETA = 10000.0
D_FF = 4 * D_MODEL

LANES = 128
SUBLANES = 8
VMEM_LIMIT_BYTES = 56 * 1024 * 1024
CHUNK = 64
TM_INPROJ = 256
TB_SCAN = 512
TQ_ATTN = 256
TK_ATTN = 2048
TK_PREP = 512
TM_MERGE = 512
TM_MLP = 512
FF_CHUNK = 1024


def _cparams(*sem):
    return pltpu.CompilerParams(dimension_semantics=sem, vmem_limit_bytes=VMEM_LIMIT_BYTES)


def _dot(a, b):
    return jnp.dot(a, b, preferred_element_type=F32)


def _dot_nt(a, b):
    return lax.dot_general(a, b, (((1,), (1,)), ((), ())), preferred_element_type=F32)


def _segsum(x, j):
    return _dot(x.astype(BF16), j)


def _segsum_split(x, j):
    hi = x.astype(BF16)
    return _dot(hi, j) + _dot((x - hi.astype(F32)).astype(BF16), j)


def _sigmoid(x):
    return 1.0 / (1.0 + jnp.exp(-x))


def _resident(shape):
    nd = len(shape)
    return pl.BlockSpec(shape, lambda *_: (0,) * nd, pipeline_mode=pl.Buffered(1))


def _shift_rows(c, halo_prev, halo_next, first, last):
    n = c.shape[0]
    row = lax.broadcasted_iota(jnp.int32, c.shape, 0)
    lane_row = lax.broadcasted_iota(jnp.int32, (1, c.shape[1]), 1)
    is_first = (lane_row * 0 + first.astype(jnp.int32)) > 0
    is_last = (lane_row * 0 + last.astype(jnp.int32)) > 0
    prev_row = jnp.where(is_first, 0.0, halo_prev[halo_prev.shape[0] - 1:, :])
    next_row = jnp.where(is_last, 0.0, halo_next[0:1, :])
    prev = jnp.where(row == 0, prev_row, pltpu.roll(c, 1, 0))
    nxt = jnp.where(row == n - 1, next_row, pltpu.roll(c, n - 1, 0))
    return prev, nxt


def _halo_specs(ts, width, n_seq_rows, rows):
    r = ts // rows
    last_blk = n_seq_rows // rows - 1
    prev = pl.BlockSpec((None, rows, width), lambda b, i: (b, jnp.maximum(i * r - 1, 0), 0))
    nxt = pl.BlockSpec((None, rows, width), lambda b, i: (b, jnp.minimum((i + 1) * r, last_blk), 0))
    return prev, nxt


def _inproj_kernel(x_ref, xp_ref, xn_ref, g_ref, w_ref, mup_ref, mun_ref, w0_ref, w2_ref, a0_ref, a2_ref, g2_ref,
                   kk_w_ref, ka_ref, rk_ref, j_ref, cvw_ref, cvb_ref,
                   r_ref, v_ref, kk_ref, gate_ref, bonus_ref, lw_ref, kd_ref, kka_ref, ycv_ref, at_ref, sg_ref):
    i = pl.program_id(1)

    def normed(x):
        return (x * lax.rsqrt(jnp.mean(x * x, axis=-1, keepdims=True) + NORM_EPS) * g_ref[...]).astype(BF16)

    h = normed(x_ref[...])
    o0, o1, o2 = RW_COLS, RW_COLS + CV_COLS, RW_COLS + CV_COLS + AT_COLS
    dm = D_MODEL

    def gate_cols(n):
        sg_ref[:, n * dm:(n + 1) * dm] = _sigmoid(_dot(h, w_ref[:, o2 + n * dm:o2 + (n + 1) * dm])).astype(BF16)

    c = _dot(h, w_ref[:, 0:o0])
    h_halo = normed(jnp.concatenate([xp_ref[...], xn_ref[...]], axis=0))
    c_halo = _dot(h_halo, w_ref[:, 0:o0])
    first, last = i == 0, i == pl.num_programs(1) - 1
    cw = CV_WIDTH
    cv = _dot(h, w_ref[:, o0:o1])
    cv_halo = _dot(h_halo, w_ref[:, o0 + cw:o1])
    zc = cv[:, cw:2 * cw] * cv[:, 2 * cw:3 * cw]
    zc_halo = cv_halo[:, 0:cw] * cv_halo[:, cw:2 * cw]
    zc_prev, zc_next = _shift_rows(zc, zc_halo[0:SUBLANES], zc_halo[SUBLANES:2 * SUBLANES], first, last)
    conv = cvb_ref[...] + cvw_ref[0:1, :] * zc_prev + cvw_ref[1:2, :] * zc + cvw_ref[2:3, :] * zc_next
    ycv_ref[...] = (cv[:, 0:cw] * conv).astype(BF16)
    prev, nxt = _shift_rows(c, c_halo[0:SUBLANES], c_halo[SUBLANES:2 * SUBLANES], first, last)
    z = c + mup_ref[...] * (prev - c) + mun_ref[...] * (nxt - c)
    w = RW_WIDTH
    r, k, v = z[:, 0:w], z[:, w:2 * w], z[:, 2 * w:3 * w]
    wd = z[:, 3 * w:3 * w + RW_LORA]
    ad = z[:, 3 * w + RW_LORA:3 * w + 2 * RW_LORA]
    gd = z[:, 3 * w + 2 * RW_LORA:3 * w + 3 * RW_LORA]
    j = j_ref[...]
    w_raw = w0_ref[...] + _dot(jnp.tanh(wd).astype(BF16), w2_ref[...])
    lw = -math.exp(-0.5) * _sigmoid(w_raw)
    a = _sigmoid(a0_ref[...] + _dot(ad.astype(BF16), a2_ref[...]))
    gate_ref[...] = _dot(_sigmoid(gd).astype(BF16), g2_ref[...]).astype(BF16)
    gate_cols(0)
    kkr = k * kk_w_ref[...]
    kk = kkr / jnp.maximum(jnp.sqrt(_segsum_split(kkr * kkr, j)), 1e-12)
    r_ref[...] = r.astype(BF16)
    v_ref[...] = v.astype(BF16)
    kk_ref[...] = kk.astype(BF16)
    at_ref[...] = _dot(h, w_ref[:, o1:o2]).astype(BF16)
    gate_cols(1)
    ksum = jnp.zeros_like(k)
    for d in range(2):
        a_d = a[:, d * w:(d + 1) * w]
        k_d = k * (1.0 + (a_d - 1.0) * ka_ref[...])
        ksum = ksum + k_d
        lw_ref[d] = lw[:, d * w:(d + 1) * w]
        kd_ref[d] = k_d.astype(BF16)
        kka_ref[d] = (kk * a_d).astype(BF16)
    gate_cols(2)
    bonus_ref[...] = (_segsum_split(r * (0.5 * ksum) * rk_ref[...], j) * v).astype(BF16)


def _inproj(x, g, w, mup, mun, w0, w2bd, a0, a2bd, g2, k_k, k_a, r_k, jones, cv_w, cv_b):
    bsz, s, _ = x.shape
    tm = min(TM_INPROJ, s)
    hp, hn = _halo_specs(tm, D_MODEL, s, SUBLANES)
    tok = lambda n: pl.BlockSpec((None, tm, n), lambda b_, i: (b_, i, 0))
    tok2 = pl.BlockSpec((2, None, tm, RW_WIDTH), lambda b_, i: (0, b_, i, 0))
    consts = (g, w, mup, mun, w0, w2bd, a0, a2bd, g2, k_k, k_a, r_k, jones, cv_w, cv_b)
    one = lambda n: jax.ShapeDtypeStruct((bsz, s, n), BF16)
    two = lambda dt: jax.ShapeDtypeStruct((2, bsz, s, RW_WIDTH), dt)
    return pl.pallas_call(
        _inproj_kernel,
        grid=(bsz, s // tm),
        in_specs=[tok(D_MODEL), hp, hn] + [_resident(a.shape) for a in consts],
        out_specs=[tok(RW_WIDTH)] * 5 + [tok2] * 3 + [tok(CV_WIDTH), tok(AT_COLS), tok(GATE_COLS)],
        out_shape=[one(RW_WIDTH)] * 5 + [two(F32), two(BF16), two(BF16)]
        + [one(CV_WIDTH), one(AT_COLS), one(GATE_COLS)],
        compiler_params=_cparams("parallel", "arbitrary"),
    )(x, x, x, *consts)


def _scan_masks():
    c = CHUNK
    t = np.arange(c)
    lower = (t[None, :] <= t[:, None]).astype(np.float32)
    strict_l = (t[None, :] < t[:, None]).astype(np.float32)
    ones = np.ones((c, c), np.float32)
    tri = np.stack([np.concatenate([lower, ones], 0), np.concatenate([lower.T, ones], 0)])
    strict = np.stack([np.tile(strict_l, (1, 2)), np.tile(strict_l.T, (1, 2))])
    incl = np.stack([np.tile(lower, (1, 2)), np.tile(lower.T, (1, 2))])
    return jnp.asarray(tri, BF16), jnp.asarray(strict, F32), jnp.asarray(incl, F32)


def _scan_kernel(rf_ref, vf_ref, kkf_ref, rb_ref, vb_ref, kkb_ref,
                 lwf_ref, kdf_ref, kkaf_ref, lwb_ref, kdb_ref, kkab_ref,
                 tri_ref, strict_ref, incl_ref, yf_ref, yb_ref, st_ref,
                 kq_ref, rq_ref, kdec_ref, bdec_ref, eend_ref, avk_ref, ark_ref, arb_ref, m_ref, tinv_ref):
    c = CHUNK
    n_chunks = rf_ref.shape[0] // c
    n_pairs = RW_WIDTH // LANES

    @pl.when(pl.program_id(1) == 0)
    def _():
        st_ref[...] = jnp.zeros_like(st_ref)

    lane = lax.broadcasted_iota(jnp.int32, (c, LANES), 1)
    row = lax.broadcasted_iota(jnp.int32, (c, LANES), 0)
    head0 = lane < HEAD
    eye = jnp.where((lane & (HEAD - 1)) == row, 1.0, 0.0).astype(F32)
    ri = lax.broadcasted_iota(jnp.int32, (LANES, LANES), 0)
    ci = lax.broadcasted_iota(jnp.int32, (LANES, LANES), 1)
    same_head = jnp.where((ri < HEAD) == (ci < HEAD), 1.0, 0.0).astype(F32)

    def stack(x):
        return jnp.concatenate([jnp.where(head0, x, 0.0), jnp.where(head0, 0.0, x)], axis=0)

    def stack_b(x):
        return stack(x).astype(BF16)

    dir_refs = ((rf_ref, vf_ref, kkf_ref, lwf_ref, kdf_ref, kkaf_ref, yf_ref),
                (rb_ref, vb_ref, kkb_ref, lwb_ref, kdb_ref, kkab_ref, yb_ref))

    def chunk_rows(d, step):
        cidx = step if d == 0 else n_chunks - 1 - step
        start = cidx * c
        return pl.ds(start if isinstance(start, int) else pl.multiple_of(start, c), c)

    def local_stages(step):
        slot = step
        ch = []
        for d, (r_ref, v_ref, kk_ref, lw_ref, kd_ref, kka_ref, y_ref) in enumerate(dir_refs):
            rows = chunk_rows(d, step)
            lw = lw_ref[rows, :]
            hi = lw.astype(BF16)
            rem = lw - hi.astype(F32)
            mid = rem.astype(BF16)
            lo = (rem - mid.astype(F32)).astype(BF16)
            tri = tri_ref[d]
            cum = _dot(tri, hi) + _dot(tri, mid) + _dot(tri, lo)
            lp = cum[0:c]
            e_end = jnp.exp(cum[c:c + SUBLANES])
            e_n = jnp.exp(-lp)
            rq = (r_ref[rows, :] * jnp.exp(lp)).astype(BF16)
            kq = (kk_ref[rows, :] * jnp.exp(lp - lw)).astype(BF16)
            ks = kd_ref[rows, :] * e_n
            bs = kka_ref[rows, :] * e_n
            kq_ref[d, slot] = kq
            rq_ref[d, slot] = rq
            kdec_ref[d, slot] = (ks * e_end[0:1]).astype(BF16)
            bdec_ref[d, slot] = (bs * e_end[0:1]).astype(BF16)
            eend_ref[d, slot] = e_end
            for p in range(n_pairs):
                sl = slice(p * LANES, (p + 1) * LANES)
                ch.append(dict(d=d, sl=sl, kq=kq[:, sl], rq=rq[:, sl], ks=ks[:, sl], bs=bs[:, sl]))
        n = range(len(ch))
        strict = [strict_ref[x["d"]] for x in ch]
        incl = [incl_ref[x["d"]] for x in ch]
        yield
        lq = [jnp.concatenate([x["kq"], x["rq"]], axis=0) for x in ch]
        s_ks = [stack_b(x["ks"]) for x in ch]
        s_bs = [stack_b(x["bs"]) for x in ch]
        gkb = [_dot_nt(lq[i], jnp.concatenate([s_ks[i], s_bs[i]], axis=0)) for i in n]
        yield
        gk = [x[:, 0:LANES] for x in gkb]
        gb = [x[:, LANES:2 * LANES] for x in gkb]
        m = [-(gb[i][0:c] * strict[i]) for i in n]
        for i, x in enumerate(ch):
            avk_ref[x["d"], slot, :, x["sl"]] = (gk[i][0:c] * strict[i]).astype(BF16)
            ark_ref[x["d"], slot, :, x["sl"]] = (gk[i][c:2 * c] * incl[i]).astype(BF16)
            arb_ref[x["d"], slot, :, x["sl"]] = (gb[i][c:2 * c] * incl[i]).astype(BF16)
            m_ref[x["d"], slot, :, x["sl"]] = m[i].astype(BF16)
        t_inv = [eye + x for x in m]
        pw = [_dot(x.astype(BF16), stack_b(x)) for x in m]
        yield
        n_levels = int(math.log2(c)) - 1
        for lvl in range(n_levels):
            pw_bd = [stack_b(x) for x in pw]
            if lvl + 1 < n_levels:
                both = [_dot(jnp.concatenate([t_inv[i].astype(BF16), pw[i].astype(BF16)], axis=0), pw_bd[i])
                        for i in n]
                t_inv = [t_inv[i] + both[i][0:c] for i in n]
                pw = [both[i][c:2 * c] for i in n]
            else:
                t_inv = [t_inv[i] + _dot(t_inv[i].astype(BF16), pw_bd[i]) for i in n]
            yield
        for i, x in enumerate(ch):
            tinv_ref[x["d"], slot, :, x["sl"]] = t_inv[i].astype(BF16)

    def state_stages(step):
        slot = step
        ch = []
        for d, (r_ref, v_ref, kk_ref, lw_ref, kd_ref, kka_ref, y_ref) in enumerate(dir_refs):
            rows = chunk_rows(d, step)
            for p in range(n_pairs):
                ch.append(dict(d=d, p=p, rows=rows, sl=slice(p * LANES, (p + 1) * LANES), y_ref=y_ref, v_ref=v_ref))
        n = range(len(ch))
        get = lambda ref: [ref[x["d"], slot, :, x["sl"]] for x in ch]
        v = [x["v_ref"][x["rows"], x["sl"]].astype(F32) for x in ch]
        s_vb = [stack_b(x) for x in v]
        st = [st_ref[x["d"], x["p"]] for x in ch]
        stb = [x.astype(BF16) for x in st]
        kq_b, rq_b, a_vk, a_rk, a_rb, t_b, m_b = (get(r) for r in (
            kq_ref, rq_ref, avk_ref, ark_ref, arb_ref, tinv_ref, m_ref))
        from_state = [_dot_nt(jnp.concatenate([kq_b[i], rq_b[i]], axis=0), stb[i]) for i in n]
        from_v = [_dot(jnp.concatenate([a_vk[i], a_rk[i]], axis=0), s_vb[i]) for i in n]
        yield
        rhs = [from_state[i][0:c] + from_v[i][0:c] for i in n]
        u0 = [-_dot(t_b[i], stack_b(rhs[i])) for i in n]
        yield
        res = [_dot(m_b[i], stack_b(u0[i])) - rhs[i] - u0[i] for i in n]
        yield
        u = [u0[i] + _dot(t_b[i], stack_b(res[i])) for i in n]
        yield
        y = [from_state[i][c:2 * c] + from_v[i][c:2 * c] + _dot(a_rb[i], stack_b(u[i])) for i in n]
        vu_t = [jnp.concatenate([v[i], u[i]], axis=0).T.astype(BF16) for i in n]
        kb = [jnp.concatenate([kdec_ref[x["d"], slot, :, x["sl"]], bdec_ref[x["d"], slot, :, x["sl"]]], axis=0)
              for x in ch]
        upd = [_dot(vu_t[i], kb[i]) * same_head for i in n]
        for i, x in enumerate(ch):
            x["y_ref"][x["rows"], x["sl"]] = y[i]
            st_ref[x["d"], x["p"]] = st[i] * eend_ref[x["d"], slot, 0:1, x["sl"]] + upd[i]

    def emit(*gens, order=None):
        live = {i: g for i, g in enumerate(gens)}
        for i in list(order or ()) + [None]:
            while i is None and live:
                for k in list(live):
                    if next(live[k], live) is live:
                        del live[k]
            if i in live and next(live[i], live) is live:
                del live[i]

    emit(local_stages(0))

    def body(i, carry):
        emit(local_stages(i + 1), state_stages(i), order=(0, 1, 0, 0, 1, 0, 0, 1, 0, 1, 0, 1, 0))
        return carry

    lax.fori_loop(0, n_chunks - 1, body, 0)
    emit(state_stages(n_chunks - 1))


def _scan(r, v, kk, lw, kd, kka, masks):
    bsz, s, w = r.shape
    tb = min(TB_SCAN, s)
    nblk = s // tb
    fwd = pl.BlockSpec((None, tb, w), lambda b_, j: (b_, j, 0))
    bwd = pl.BlockSpec((None, tb, w), lambda b_, j: (b_, nblk - 1 - j, 0))
    fwd2 = pl.BlockSpec((None, None, tb, w), lambda b_, j: (0, b_, j, 0))
    bwd2 = pl.BlockSpec((None, None, tb, w), lambda b_, j: (1, b_, nblk - 1 - j, 0))
    out = jax.ShapeDtypeStruct((bsz, s, w), F32)
    per_chunk = lambda dt: pltpu.VMEM((2, tb // CHUNK, CHUNK, w), dt)
    return pl.pallas_call(
        _scan_kernel,
        grid=(bsz, nblk),
        in_specs=[fwd, fwd, fwd, bwd, bwd, bwd, fwd2, fwd2, fwd2, bwd2, bwd2, bwd2]
        + [_resident(a.shape) for a in masks],
        out_specs=[fwd, bwd],
        out_shape=[out, out],
        scratch_shapes=[pltpu.VMEM((2, w // LANES, LANES, LANES), F32), per_chunk(BF16), per_chunk(BF16),
                        per_chunk(BF16), per_chunk(BF16), pltpu.VMEM((2, tb // CHUNK, SUBLANES, w), F32)]
        + [per_chunk(BF16)] * 5,
        compiler_params=_cparams("parallel", "arbitrary"),
    )(r, v, kk, r, v, kk, lw, kd, kka, lw, kd, kka, *masks)


def _rope_partner(x):
    n = x.shape[1]
    lane = lax.broadcasted_iota(jnp.int32, x.shape, 1)
    return jnp.where((lane & 31) < 16, pltpu.roll(x, n - 16, 1), pltpu.roll(x, 16, 1))


def _attn_kernel(q_ref, k_ref, v_ref, cq_ref, sq_ref, ck_ref, sk_ref, qg_ref, kg_ref, jq_ref, jk_ref,
                 o_ref, kd_ref, vd_ref):
    s_len = k_ref.shape[0]
    tq = q_ref.shape[0]
    tk = min(TK_PREP, s_len)

    @pl.when(pl.program_id(1) == 0)
    def _():
        def body(i, carry):
            rows = pl.ds(pl.multiple_of(i * tk, tk), tk)
            k = k_ref[rows, :].astype(F32)
            kn = k * lax.rsqrt(_segsum(k * k, jk_ref[...]) * (1.0 / HEAD) + NORM_EPS) * kg_ref[...]
            kr = kn * ck_ref[rows, :] + _rope_partner(kn) * sk_ref[rows, :]
            v = v_ref[rows, :].astype(F32)
            low = lax.broadcasted_iota(jnp.int32, kr.shape, 1) < HEAD
            ksw = pltpu.roll(kr, HEAD, 1)
            kd_ref[0, rows, :] = jnp.where(low, kr, ksw).astype(BF16)
            kd_ref[1, rows, :] = jnp.where(low, ksw, kr).astype(BF16)
            vd_ref[0, rows, :] = jnp.where(low, v, 1.0).astype(BF16)
            vd_ref[1, rows, :] = jnp.where(low, pltpu.roll(v, HEAD, 1), 1.0).astype(BF16)
            return carry
        lax.fori_loop(0, s_len // tk, body, 0)

    q = q_ref[...].astype(F32)
    qn = q * lax.rsqrt(_segsum(q * q, jq_ref[...]) * (1.0 / HEAD) + NORM_EPS) * qg_ref[...]
    qr = qn * cq_ref[...] + _rope_partner(qn) * sq_ref[...]
    low = lax.broadcasted_iota(jnp.int32, (tq, LANES), 1) < HEAD
    pairs_per_kv = AT_Q_HEADS // AT_KV_HEADS // 2
    tkc = min(TK_ATTN, s_len)
    n_kc = s_len // tkc
    qs = []
    for g in range(AT_KV_HEADS):
        parts = []
        for jj in range(pairs_per_kv):
            qp = qr[:, (g * pairs_per_kv + jj) * LANES:(g * pairs_per_kv + jj + 1) * LANES]
            parts += [jnp.where(low, qp, 0.0), jnp.where(low, 0.0, qp)]
        qs.append(jnp.concatenate(parts, axis=0).astype(BF16))
    streams = [(g, kc) for g in range(AT_KV_HEADS) for kc in range(n_kc)]
    scores = lambda g, kc: _dot_nt(qs[g], kd_ref[g, kc * tkc:(kc + 1) * tkc, :])
    partial, s_next = {}, scores(*streams[0])
    for i, (g, kc) in enumerate(streams):
        s = s_next
        if i + 1 < len(streams):
            s_next = scores(*streams[i + 1])
        m = jnp.max(s, axis=-1, keepdims=True)
        p = jnp.exp2(s - m).astype(BF16)
        partial[g, kc] = (m, _dot(p, vd_ref[g, kc * tkc:(kc + 1) * tkc, :]))
    for g in range(AT_KV_HEADS):
        m_all = partial[g, 0][0]
        for kc in range(1, n_kc):
            m_all = jnp.maximum(m_all, partial[g, kc][0])
        o = sum(jnp.exp2(partial[g, kc][0] - m_all) * partial[g, kc][1] for kc in range(n_kc))
        o = o / pltpu.roll(o, HEAD, 1)
        for jj in range(pairs_per_kv):
            even = o[(2 * jj) * tq:(2 * jj + 1) * tq]
            odd = pltpu.roll(o[(2 * jj + 1) * tq:(2 * jj + 2) * tq], HEAD, 1)
            pair = jnp.where(low, even, odd)
            col = (g * pairs_per_kv + jj) * LANES
            o_ref[:, col:col + LANES] = pair.astype(BF16)


def _attn(at, tabs, q_g, k_g, jones):
    bsz, s, _ = at.shape
    tq = min(TQ_ATTN, s)
    cq, sq, ck, sk = tabs
    qw = AT_Q_HEADS * HEAD
    kvw = AT_KV_HEADS * HEAD
    cst = lambda a: pl.BlockSpec(a.shape, lambda b_, i: (0,) * a.ndim)
    jk = jones[0:kvw, 0:kvw]
    return pl.pallas_call(
        _attn_kernel,
        grid=(bsz, s // tq),
        in_specs=[pl.BlockSpec((None, tq, qw), lambda b_, i: (b_, i, 0)),
                  pl.BlockSpec((None, s, kvw), lambda b_, i: (b_, 0, qw // kvw)),
                  pl.BlockSpec((None, s, kvw), lambda b_, i: (b_, 0, qw // kvw + 1)),
                  pl.BlockSpec((tq, qw), lambda b_, i: (i, 0)),
                  pl.BlockSpec((tq, qw), lambda b_, i: (i, 0)),
                  cst(ck), cst(sk), cst(q_g), cst(k_g), cst(jones), cst(jk)],
        out_specs=pl.BlockSpec((None, tq, qw), lambda b_, i: (b_, i, 0)),
        out_shape=jax.ShapeDtypeStruct((bsz, s, qw), BF16),
        scratch_shapes=[pltpu.VMEM((AT_KV_HEADS, s, LANES), BF16), pltpu.VMEM((AT_KV_HEADS, s, LANES), BF16)],
        compiler_params=_cparams("parallel", "arbitrary"),
    )(at, at, at, cq, sq, ck, sk, q_g, k_g, jones, jk)


def _rope_tables(s):
    rows = s // GRID_W
    pos_r = jnp.repeat(jnp.arange(rows), GRID_W).astype(F32)
    pos_c = jnp.tile(jnp.arange(GRID_W), rows).astype(F32)
    half = HEAD // 2
    inv = ROPE_THETA ** (-jnp.arange(0, half, 2, dtype=F32) / half)
    ang_r = pos_r[:, None] * inv
    ang_c = pos_c[:, None] * inv
    cos = jnp.concatenate([jnp.cos(ang_r)] * 2 + [jnp.cos(ang_c)] * 2, axis=-1)
    sin = jnp.concatenate([-jnp.sin(ang_r), jnp.sin(ang_r), -jnp.sin(ang_c), jnp.sin(ang_c)], axis=-1)
    scale = HEAD ** -0.5 * math.log2(math.e)
    return (jnp.tile(cos, (1, AT_Q_HEADS)) * scale, jnp.tile(sin, (1, AT_Q_HEADS)) * scale,
            jnp.tile(cos, (1, AT_KV_HEADS)), jnp.tile(sin, (1, AT_KV_HEADS)))


def _merge_kernel(y0_ref, y1_ref, g_ref, bonus_ref, lng_ref, lnb_ref, j_ref, ycv_ref, yat_ref, sg_ref, x_ref,
                  wb_ref, wo_ref, o_ref):
    j = j_ref[...]
    y = y0_ref[...] + y1_ref[...]
    mu = _segsum(y, j) * (1.0 / HEAD)
    dev = y - mu
    var = _segsum(dev * dev, j) * (1.0 / HEAD)
    yn = dev * lax.rsqrt(var + RW_LN_EPS) * lng_ref[...] + lnb_ref[...]
    y_rw = ((yn + bonus_ref[...]) * g_ref[...]).astype(BF16)
    dm = D_MODEL
    merged = (sg_ref[:, 0:dm].astype(F32) * _dot(y_rw, wb_ref[0])
              + sg_ref[:, dm:2 * dm].astype(F32) * _dot(ycv_ref[...], wb_ref[1])
              + sg_ref[:, 2 * dm:3 * dm].astype(F32) * _dot(yat_ref[...], wb_ref[2]))
    o_ref[...] = x_ref[...] + _dot(merged.astype(BF16), wo_ref[...])


def _merge(yf, yb, g, bonus, ln_g, ln_b, jones, y_cv, y_at, sg, x, w_branch, w_out):
    t = x.shape[0]
    tm = TM_MERGE
    row = lambda n: pl.BlockSpec((tm, n), lambda i: (i, 0))
    return pl.pallas_call(
        _merge_kernel,
        grid=(t // tm,),
        in_specs=[row(RW_WIDTH), row(RW_WIDTH), row(RW_WIDTH), row(RW_WIDTH),
                  _resident(ln_g.shape), _resident(ln_b.shape), _resident(jones.shape),
                  row(CV_WIDTH), row(AT_Q_HEADS * HEAD), row(GATE_COLS), row(D_MODEL),
                  _resident(w_branch.shape), _resident(w_out.shape)],
        out_specs=row(D_MODEL),
        out_shape=jax.ShapeDtypeStruct((t, D_MODEL), F32),
        compiler_params=_cparams("parallel"),
    )(yf, yb, g, bonus, ln_g, ln_b, jones, y_cv, y_at, sg, x, w_branch, w_out)


def _mlp_kernel(x_ref, g_ref, w1_ref, w2_ref, o_ref):
    x = x_ref[...]
    h = (x * lax.rsqrt(jnp.mean(x * x, axis=-1, keepdims=True) + NORM_EPS) * g_ref[...]).astype(BF16)
    acc = x
    for c in range(D_FF // FF_CHUNK):
        cols = slice(c * FF_CHUNK, (c + 1) * FF_CHUNK)
        u = jnp.maximum(_dot(h, w1_ref[:, cols]), 0.0)
        acc = acc + _dot((u * u).astype(BF16), w2_ref[cols, :])
    o_ref[...] = acc


def _mlp(x, g, w1, w2):
    t = x.shape[0]
    tm = TM_MLP
    row = pl.BlockSpec((tm, D_MODEL), lambda i: (i, 0))
    return pl.pallas_call(
        _mlp_kernel,
        grid=(t // tm,),
        in_specs=[row, _resident(g.shape), _resident(w1.shape), _resident(w2.shape)],
        out_specs=row,
        out_shape=jax.ShapeDtypeStruct((t, D_MODEL), F32),
        compiler_params=_cparams("parallel"),
    )(x, g, w1, w2)


def _block_diag2(w):
    z = jnp.zeros_like(w[0])
    return jnp.concatenate([jnp.concatenate([w[0], z], axis=1), jnp.concatenate([z, w[1]], axis=1)], axis=0)


def kernel(x, norm1_g, w_in, rw_mu_prev, rw_mu_next, rw_w0, rw_w2, rw_a0, rw_a2, rw_g2, rw_k_k, rw_k_a, rw_r_k,
           rw_ln_g, rw_ln_b, cv_w, cv_b, at_q_g, at_k_g, w_branch, w_out, norm2_g, w_mlp1, w_mlp2):
    bsz, s, dm = x.shape
    assert dm == D_MODEL and s % GRID_W == 0
    for tile in (TM_INPROJ, TB_SCAN, TQ_ATTN, TK_PREP):
        assert s % min(tile, s) == 0 and min(tile, s) % CHUNK == 0
    t = bsz * s
    tabs = _rope_tables(s)
    masks = _scan_masks()
    jones = jnp.asarray(np.kron(np.eye(RW_WIDTH // HEAD, dtype=np.float32), np.ones((HEAD, HEAD), np.float32)), BF16)
    row = lambda a: a.reshape(1, -1)
    xf = x.reshape(t, dm)
    for l in range(DEPTH):
        r, v, kk, g, bonus, lw, kd, kka, y_cv, at, sg = _inproj(
            xf.reshape(bsz, s, dm), row(norm1_g[l]), w_in[l].astype(BF16), row(rw_mu_prev[l]), row(rw_mu_next[l]),
            row(rw_w0[l]), _block_diag2(rw_w2[l]).astype(BF16), row(rw_a0[l]), _block_diag2(rw_a2[l]).astype(BF16),
            rw_g2[l].astype(BF16), row(rw_k_k[l]), row(rw_k_a[l]), row(rw_r_k[l]), jones, cv_w[l], row(cv_b[l]))
        yf, yb = _scan(r, v, kk, lw, kd, kka, masks)
        y_at = _attn(at, tabs, row(jnp.tile(at_q_g[l], AT_Q_HEADS)),
                     row(jnp.tile(at_k_g[l], AT_KV_HEADS)), jones)
        xf = _merge(yf.reshape(t, RW_WIDTH), yb.reshape(t, RW_WIDTH), g.reshape(t, RW_WIDTH),
                    bonus.reshape(t, RW_WIDTH),
                    row(rw_ln_g[l]), row(rw_ln_b[l]), jones, y_cv.reshape(t, CV_WIDTH),
                    y_at.reshape(t, AT_Q_HEADS * HEAD), sg.reshape(t, GATE_COLS), xf,
                    w_branch[l].astype(BF16), w_out[l].astype(BF16))
        xf = _mlp(xf, row(norm2_g[l]), w_mlp1[l].astype(BF16), w_mlp2[l].astype(BF16))
    return xf.reshape(bsz, s, dm)
```

```python
import math

import numpy as np
import jax
import jax.numpy as jnp
from jax import lax
from jax.experimental import pallas as pl
from jax.experimental.pallas import tpu as pltpu

F32 = jnp.float32
BF16 = jnp.bfloat16

D_MODEL = 1024
DEPTH = 4
GRID_W = 64
NORM_EPS = 1e-6
HEAD = 64
RW_WIDTH = 512
RW_LORA = 128
RW_LN_EPS = 64e-5
RW_COLS = 3 * RW_WIDTH + 3 * RW_LORA
CV_WIDTH = 512
CV_COLS = 3 * CV_WIDTH
AT_Q_HEADS = 8
AT_KV_HEADS = 2
AT_COLS = (AT_Q_HEADS + 2 * AT_KV_HEADS) * HEAD
GATE_COLS = 3 * D_MODEL
ROPE_THETA = 10000.0
D_FF = 4 * D_MODEL

LANES = 128
SUBLANES = 8
VMEM_LIMIT_BYTES = 56 * 1024 * 1024
CHUNK = 64
TM_INPROJ = 256
TB_SCAN = 512
TQ_ATTN = 512
TK_ATTN = 1024
TK_PREP = 512
TM_MERGE = 512
TM_MLP = 512
FF_CHUNK = 1024


def _cparams(*sem):
    return pltpu.CompilerParams(dimension_semantics=sem, vmem_limit_bytes=VMEM_LIMIT_BYTES)


def _dot(a, b):
    return jnp.dot(a, b, preferred_element_type=F32)


def _dot_nt(a, b):
    return lax.dot_general(a, b, (((1,), (1,)), ((), ())), preferred_element_type=F32)


def _segsum(x, j):
    return _dot(x.astype(BF16), j)


def _segsum_split(x, j):
    hi = x.astype(BF16)
    return _dot(hi, j) + _dot((x - hi.astype(F32)).astype(BF16), j)


def _sigmoid(x):
    return 1.0 / (1.0 + jnp.exp(-x))


def _resident(shape):
    nd = len(shape)
    return pl.BlockSpec(shape, lambda *_: (0,) * nd, pipeline_mode=pl.Buffered(1))


def _shift_rows(c, halo_prev, halo_next, first, last):
    n = c.shape[0]
    row = lax.broadcasted_iota(jnp.int32, c.shape, 0)
    lane_row = lax.broadcasted_iota(jnp.int32, (1, c.shape[1]), 1)
    is_first = (lane_row * 0 + first.astype(jnp.int32)) > 0
    is_last = (lane_row * 0 + last.astype(jnp.int32)) > 0
    prev_row = jnp.where(is_first, 0.0, halo_prev[halo_prev.shape[0] - 1:, :])
    next_row = jnp.where(is_last, 0.0, halo_next[0:1, :])
    prev = jnp.where(row == 0, prev_row, pltpu.roll(c, 1, 0))
    nxt = jnp.where(row == n - 1, next_row, pltpu.roll(c, n - 1, 0))
    return prev, nxt


def _halo_specs(ts, width, n_seq_rows, rows):
    r = ts // rows
    last_blk = n_seq_rows // rows - 1
    prev = pl.BlockSpec((None, rows, width), lambda b, i: (b, jnp.maximum(i * r - 1, 0), 0))
    nxt = pl.BlockSpec((None, rows, width), lambda b, i: (b, jnp.minimum((i + 1) * r, last_blk), 0))
    return prev, nxt


def _inproj_kernel(x_ref, xp_ref, xn_ref, g_ref, w_ref, mup_ref, mun_ref, w0_ref, w2_ref, a0_ref, a2_ref, g2_ref,
                   kk_w_ref, ka_ref, rk_ref, j_ref, cvw_ref, cvb_ref,
                   r_ref, v_ref, kk_ref, gate_ref, bonus_ref, lw_ref, kd_ref, kka_ref, ycv_ref, at_ref, sg_ref):
    i = pl.program_id(1)

    def normed(x):
        return (x * lax.rsqrt(jnp.mean(x * x, axis=-1, keepdims=True) + NORM_EPS) * g_ref[...]).astype(BF16)

    h = normed(x_ref[...])
    o0, o1, o2 = RW_COLS, RW_COLS + CV_COLS, RW_COLS + CV_COLS + AT_COLS
    dm = D_MODEL

    def gate_cols(n):
        sg_ref[:, n * dm:(n + 1) * dm] = _sigmoid(_dot(h, w_ref[:, o2 + n * dm:o2 + (n + 1) * dm])).astype(BF16)

    c = _dot(h, w_ref[:, 0:o0])
    h_halo = normed(jnp.concatenate([xp_ref[...], xn_ref[...]], axis=0))
    c_halo = _dot(h_halo, w_ref[:, 0:o0])
    first, last = i == 0, i == pl.num_programs(1) - 1
    cw = CV_WIDTH
    cv = _dot(h, w_ref[:, o0:o1])
    cv_halo = _dot(h_halo, w_ref[:, o0 + cw:o1])
    zc = cv[:, cw:2 * cw] * cv[:, 2 * cw:3 * cw]
    zc_halo = cv_halo[:, 0:cw] * cv_halo[:, cw:2 * cw]
    zc_prev, zc_next = _shift_rows(zc, zc_halo[0:SUBLANES], zc_halo[SUBLANES:2 * SUBLANES], first, last)
    conv = cvb_ref[...] + cvw_ref[0:1, :] * zc_prev + cvw_ref[1:2, :] * zc + cvw_ref[2:3, :] * zc_next
    ycv_ref[...] = (cv[:, 0:cw] * conv).astype(BF16)
    prev, nxt = _shift_rows(c, c_halo[0:SUBLANES], c_halo[SUBLANES:2 * SUBLANES], first, last)
    z = c + mup_ref[...] * (prev - c) + mun_ref[...] * (nxt - c)
    w = RW_WIDTH
    r, k, v = z[:, 0:w], z[:, w:2 * w], z[:, 2 * w:3 * w]
    wd = z[:, 3 * w:3 * w + RW_LORA]
    ad = z[:, 3 * w + RW_LORA:3 * w + 2 * RW_LORA]
    gd = z[:, 3 * w + 2 * RW_LORA:3 * w + 3 * RW_LORA]
    j = j_ref[...]
    w_raw = w0_ref[...] + _dot(jnp.tanh(wd).astype(BF16), w2_ref[...])
    lw = -math.exp(-0.5) * _sigmoid(w_raw)
    a = _sigmoid(a0_ref[...] + _dot(ad.astype(BF16), a2_ref[...]))
    gate_ref[...] = _dot(_sigmoid(gd).astype(BF16), g2_ref[...]).astype(BF16)
    gate_cols(0)
    kkr = k * kk_w_ref[...]
    kk = kkr / jnp.maximum(jnp.sqrt(_segsum_split(kkr * kkr, j)), 1e-12)
    r_ref[...] = r.astype(BF16)
    v_ref[...] = v.astype(BF16)
    kk_ref[...] = kk.astype(BF16)
    at_ref[...] = _dot(h, w_ref[:, o1:o2]).astype(BF16)
    gate_cols(1)
    ksum = jnp.zeros_like(k)
    for d in range(2):
        a_d = a[:, d * w:(d + 1) * w]
        k_d = k * (1.0 + (a_d - 1.0) * ka_ref[...])
        ksum = ksum + k_d
        lw_ref[d] = lw[:, d * w:(d + 1) * w]
        kd_ref[d] = k_d.astype(BF16)
        kka_ref[d] = (kk * a_d).astype(BF16)
    gate_cols(2)
    bonus_ref[...] = (_segsum_split(r * (0.5 * ksum) * rk_ref[...], j) * v).astype(BF16)


def _inproj(x, g, w, mup, mun, w0, w2bd, a0, a2bd, g2, k_k, k_a, r_k, jones, cv_w, cv_b):
    bsz, s, _ = x.shape
    tm = min(TM_INPROJ, s)
    hp, hn = _halo_specs(tm, D_MODEL, s, SUBLANES)
    tok = lambda n: pl.BlockSpec((None, tm, n), lambda b_, i: (b_, i, 0))
    tok2 = pl.BlockSpec((2, None, tm, RW_WIDTH), lambda b_, i: (0, b_, i, 0))
    consts = (g, w, mup, mun, w0, w2bd, a0, a2bd, g2, k_k, k_a, r_k, jones, cv_w, cv_b)
    one = lambda n: jax.ShapeDtypeStruct((bsz, s, n), BF16)
    two = lambda dt: jax.ShapeDtypeStruct((2, bsz, s, RW_WIDTH), dt)
    return pl.pallas_call(
        _inproj_kernel,
        grid=(bsz, s // tm),
        in_specs=[tok(D_MODEL), hp, hn] + [_resident(a.shape) for a in consts],
        out_specs=[tok(RW_WIDTH)] * 5 + [tok2] * 3 + [tok(CV_WIDTH), tok(AT_COLS), tok(GATE_COLS)],
        out_shape=[one(RW_WIDTH)] * 5 + [two(F32), two(BF16), two(BF16)]
        + [one(CV_WIDTH), one(AT_COLS), one(GATE_COLS)],
        compiler_params=_cparams("parallel", "arbitrary"),
    )(x, x, x, *consts)


def _scan_masks():
    c = CHUNK
    t = np.arange(c)
    lower = (t[None, :] <= t[:, None]).astype(np.float32)
    strict_l = (t[None, :] < t[:, None]).astype(np.float32)
    ones = np.ones((c, c), np.float32)
    tri = np.stack([np.concatenate([lower, ones], 0), np.concatenate([lower.T, ones], 0)])
    strict = np.stack([np.tile(strict_l, (1, 2)), np.tile(strict_l.T, (1, 2))])
    incl = np.stack([np.tile(lower, (1, 2)), np.tile(lower.T, (1, 2))])
    return jnp.asarray(tri, BF16), jnp.asarray(strict, F32), jnp.asarray(incl, F32)


def _scan_kernel(rf_ref, vf_ref, kkf_ref, rb_ref, vb_ref, kkb_ref,
                 lwf_ref, kdf_ref, kkaf_ref, lwb_ref, kdb_ref, kkab_ref,
                 tri_ref, strict_ref, incl_ref, yf_ref, yb_ref, st_ref,
                 kq_ref, rq_ref, kdec_ref, bdec_ref, eend_ref, avk_ref, ark_ref, arb_ref, m_ref, tinv_ref):
    c = CHUNK
    n_chunks = rf_ref.shape[0] // c
    n_pairs = RW_WIDTH // LANES

    @pl.when(pl.program_id(1) == 0)
    def _():
        st_ref[...] = jnp.zeros_like(st_ref)

    lane = lax.broadcasted_iota(jnp.int32, (c, LANES), 1)
    row = lax.broadcasted_iota(jnp.int32, (c, LANES), 0)
    head0 = lane < HEAD
    eye = jnp.where((lane & (HEAD - 1)) == row, 1.0, 0.0).astype(F32)
    ri = lax.broadcasted_iota(jnp.int32, (LANES, LANES), 0)
    ci = lax.broadcasted_iota(jnp.int32, (LANES, LANES), 1)
    same_head = jnp.where((ri < HEAD) == (ci < HEAD), 1.0, 0.0).astype(F32)

    def stack(x):
        return jnp.concatenate([jnp.where(head0, x, 0.0), jnp.where(head0, 0.0, x)], axis=0)

    def stack_b(x):
        return stack(x).astype(BF16)

    dir_refs = ((rf_ref, vf_ref, kkf_ref, lwf_ref, kdf_ref, kkaf_ref, yf_ref),
                (rb_ref, vb_ref, kkb_ref, lwb_ref, kdb_ref, kkab_ref, yb_ref))

    def chunk_rows(d, step):
        cidx = step if d == 0 else n_chunks - 1 - step
        start = cidx * c
        return pl.ds(start if isinstance(start, int) else pl.multiple_of(start, c), c)

    def local_stages(step):
        slot = step
        ch = []
        for d, (r_ref, v_ref, kk_ref, lw_ref, kd_ref, kka_ref, y_ref) in enumerate(dir_refs):
            rows = chunk_rows(d, step)
            lw = lw_ref[rows, :]
            hi = lw.astype(BF16)
            rem = lw - hi.astype(F32)
            mid = rem.astype(BF16)
            lo = (rem - mid.astype(F32)).astype(BF16)
            tri = tri_ref[d]
            cum = _dot(tri, hi) + _dot(tri, mid) + _dot(tri, lo)
            lp = cum[0:c]
            e_end = jnp.exp(cum[c:c + SUBLANES])
            e_n = jnp.exp(-lp)
            rq = (r_ref[rows, :] * jnp.exp(lp)).astype(BF16)
            kq = (kk_ref[rows, :] * jnp.exp(lp - lw)).astype(BF16)
            ks = kd_ref[rows, :] * e_n
            bs = kka_ref[rows, :] * e_n
            kq_ref[d, slot] = kq
            rq_ref[d, slot] = rq
            kdec_ref[d, slot] = (ks * e_end[0:1]).astype(BF16)
            bdec_ref[d, slot] = (bs * e_end[0:1]).astype(BF16)
            eend_ref[d, slot] = e_end
            for p in range(n_pairs):
                sl = slice(p * LANES, (p + 1) * LANES)
                ch.append(dict(d=d, sl=sl, kq=kq[:, sl], rq=rq[:, sl], ks=ks[:, sl], bs=bs[:, sl]))
        n = range(len(ch))
        strict = [strict_ref[x["d"]] for x in ch]
        incl = [incl_ref[x["d"]] for x in ch]
        yield
        lq = [jnp.concatenate([x["kq"], x["rq"]], axis=0) for x in ch]
        s_ks = [stack_b(x["ks"]) for x in ch]
        s_bs = [stack_b(x["bs"]) for x in ch]
        gkb = [_dot_nt(lq[i], jnp.concatenate([s_ks[i], s_bs[i]], axis=0)) for i in n]
        yield
        gk = [x[:, 0:LANES] for x in gkb]
        gb = [x[:, LANES:2 * LANES] for x in gkb]
        m = [-(gb[i][0:c] * strict[i]) for i in n]
        for i, x in enumerate(ch):
            avk_ref[x["d"], slot, :, x["sl"]] = (gk[i][0:c] * strict[i]).astype(BF16)
            ark_ref[x["d"], slot, :, x["sl"]] = (gk[i][c:2 * c] * incl[i]).astype(BF16)
            arb_ref[x["d"], slot, :, x["sl"]] = (gb[i][c:2 * c] * incl[i]).astype(BF16)
            m_ref[x["d"], slot, :, x["sl"]] = m[i].astype(BF16)
        t_inv = [eye + x for x in m]
        pw = [_dot(x.astype(BF16), stack_b(x)) for x in m]
        yield
        n_levels = int(math.log2(c)) - 1
        for lvl in range(n_levels):
            pw_bd = [stack_b(x) for x in pw]
            if lvl + 1 < n_levels:
                both = [_dot(jnp.concatenate([t_inv[i].astype(BF16), pw[i].astype(BF16)], axis=0), pw_bd[i])
                        for i in n]
                t_inv = [t_inv[i] + both[i][0:c] for i in n]
                pw = [both[i][c:2 * c] for i in n]
            else:
                t_inv = [t_inv[i] + _dot(t_inv[i].astype(BF16), pw_bd[i]) for i in n]
            yield
        for i, x in enumerate(ch):
            tinv_ref[x["d"], slot, :, x["sl"]] = t_inv[i].astype(BF16)

    def state_stages(step):
        slot = step
        ch = []
        for d, (r_ref, v_ref, kk_ref, lw_ref, kd_ref, kka_ref, y_ref) in enumerate(dir_refs):
            rows = chunk_rows(d, step)
            for p in range(n_pairs):
                ch.append(dict(d=d, p=p, rows=rows, sl=slice(p * LANES, (p + 1) * LANES), y_ref=y_ref, v_ref=v_ref))
        n = range(len(ch))
        get = lambda ref: [ref[x["d"], slot, :, x["sl"]] for x in ch]
        v = [x["v_ref"][x["rows"], x["sl"]].astype(F32) for x in ch]
        s_vb = [stack_b(x) for x in v]
        st = [st_ref[x["d"], x["p"]] for x in ch]
        stb = [x.astype(BF16) for x in st]
        kq_b, rq_b, a_vk, a_rk, a_rb, t_b, m_b = (get(r) for r in (
            kq_ref, rq_ref, avk_ref, ark_ref, arb_ref, tinv_ref, m_ref))
        from_state = [_dot_nt(jnp.concatenate([kq_b[i], rq_b[i]], axis=0), stb[i]) for i in n]
        from_v = [_dot(jnp.concatenate([a_vk[i], a_rk[i]], axis=0), s_vb[i]) for i in n]
        yield
        rhs = [from_state[i][0:c] + from_v[i][0:c] for i in n]
        u0 = [-_dot(t_b[i], stack_b(rhs[i])) for i in n]
        yield
        res = [_dot(m_b[i], stack_b(u0[i])) - rhs[i] - u0[i] for i in n]
        yield
        u = [u0[i] + _dot(t_b[i], stack_b(res[i])) for i in n]
        yield
        y = [from_state[i][c:2 * c] + from_v[i][c:2 * c] + _dot(a_rb[i], stack_b(u[i])) for i in n]
        vu_t = [jnp.concatenate([v[i], u[i]], axis=0).T.astype(BF16) for i in n]
        kb = [jnp.concatenate([kdec_ref[x["d"], slot, :, x["sl"]], bdec_ref[x["d"], slot, :, x["sl"]]], axis=0)
              for x in ch]
        upd = [_dot(vu_t[i], kb[i]) * same_head for i in n]
        for i, x in enumerate(ch):
            x["y_ref"][x["rows"], x["sl"]] = y[i]
            st_ref[x["d"], x["p"]] = st[i] * eend_ref[x["d"], slot, 0:1, x["sl"]] + upd[i]

    def emit(*gens, order=None):
        live = {i: g for i, g in enumerate(gens)}
        for i in list(order or ()) + [None]:
            while i is None and live:
                for k in list(live):
                    if next(live[k], live) is live:
                        del live[k]
            if i in live and next(live[i], live) is live:
                del live[i]

    emit(local_stages(0))

    def body(i, carry):
        emit(local_stages(i + 1), state_stages(i), order=(0, 1, 0, 0, 1, 0, 0, 1, 0, 1, 0, 1, 0))
        return carry

    lax.fori_loop(0, n_chunks - 1, body, 0)
    emit(state_stages(n_chunks - 1))


def _scan(r, v, kk, lw, kd, kka, masks):
    bsz, s, w = r.shape
    tb = min(TB_SCAN, s)
    nblk = s // tb
    fwd = pl.BlockSpec((None, tb, w), lambda b_, j: (b_, j, 0))
    bwd = pl.BlockSpec((None, tb, w), lambda b_, j: (b_, nblk - 1 - j, 0))
    fwd2 = pl.BlockSpec((None, None, tb, w), lambda b_, j: (0, b_, j, 0))
    bwd2 = pl.BlockSpec((None, None, tb, w), lambda b_, j: (1, b_, nblk - 1 - j, 0))
    out = jax.ShapeDtypeStruct((bsz, s, w), F32)
    per_chunk = lambda dt: pltpu.VMEM((2, tb // CHUNK, CHUNK, w), dt)
    return pl.pallas_call(
        _scan_kernel,
        grid=(bsz, nblk),
        in_specs=[fwd, fwd, fwd, bwd, bwd, bwd, fwd2, fwd2, fwd2, bwd2, bwd2, bwd2]
        + [_resident(a.shape) for a in masks],
        out_specs=[fwd, bwd],
        out_shape=[out, out],
        scratch_shapes=[pltpu.VMEM((2, w // LANES, LANES, LANES), F32), per_chunk(BF16), per_chunk(BF16),
                        per_chunk(BF16), per_chunk(BF16), pltpu.VMEM((2, tb // CHUNK, SUBLANES, w), F32)]
        + [per_chunk(BF16)] * 5,
        compiler_params=_cparams("parallel", "arbitrary"),
    )(r, v, kk, r, v, kk, lw, kd, kka, lw, kd, kka, *masks)


def _rope_partner(x):
    n = x.shape[1]
    lane = lax.broadcasted_iota(jnp.int32, x.shape, 1)
    return jnp.where((lane & 31) < 16, pltpu.roll(x, n - 16, 1), pltpu.roll(x, 16, 1))


def _attn_kernel(q_ref, k_ref, v_ref, cq_ref, sq_ref, ck_ref, sk_ref, qg_ref, kg_ref, jq_ref, jk_ref,
                 o_ref, kd_ref, vd_ref):
    s_len = k_ref.shape[0]
    tq = q_ref.shape[0]
    tk = min(TK_PREP, s_len)

    @pl.when(pl.program_id(1) == 0)
    def _():
        def body(i, carry):
            rows = pl.ds(pl.multiple_of(i * tk, tk), tk)
            k = k_ref[rows, :].astype(F32)
            kn = k * lax.rsqrt(_segsum(k * k, jk_ref[...]) * (1.0 / HEAD) + NORM_EPS) * kg_ref[...]
            kr = kn * ck_ref[rows, :] + _rope_partner(kn) * sk_ref[rows, :]
            v = v_ref[rows, :].astype(F32)
            low = lax.broadcasted_iota(jnp.int32, kr.shape, 1) < HEAD
            ksw = pltpu.roll(kr, HEAD, 1)
            kd_ref[0, rows, :] = jnp.where(low, kr, ksw).astype(BF16)
            kd_ref[1, rows, :] = jnp.where(low, ksw, kr).astype(BF16)
            vd_ref[0, rows, :] = jnp.where(low, v, 1.0).astype(BF16)
            vd_ref[1, rows, :] = jnp.where(low, pltpu.roll(v, HEAD, 1), 1.0).astype(BF16)
            return carry
        lax.fori_loop(0, s_len // tk, body, 0)

    q = q_ref[...].astype(F32)
    qn = q * lax.rsqrt(_segsum(q * q, jq_ref[...]) * (1.0 / HEAD) + NORM_EPS) * qg_ref[...]
    qr = qn * cq_ref[...] + _rope_partner(qn) * sq_ref[...]
    low = lax.broadcasted_iota(jnp.int32, (tq, LANES), 1) < HEAD
    pairs_per_kv = AT_Q_HEADS // AT_KV_HEADS // 2
    tkc = min(TK_ATTN, s_len)
    n_kc = s_len // tkc
    qs = []
    for g in range(AT_KV_HEADS):
        parts = []
        for jj in range(pairs_per_kv):
            qp = qr[:, (g * pairs_per_kv + jj) * LANES:(g * pairs_per_kv + jj + 1) * LANES]
            parts += [jnp.where(low, qp, 0.0), jnp.where(low, 0.0, qp)]
        qs.append(jnp.concatenate(parts, axis=0).astype(BF16))
    streams = [(g, kc) for g in range(AT_KV_HEADS) for kc in range(n_kc)]
    scores = lambda g, kc: _dot_nt(qs[g], kd_ref[g, kc * tkc:(kc + 1) * tkc, :])
    partial, s_next = {}, scores(*streams[0])
    for i, (g, kc) in enumerate(streams):
        s = s_next
        if i + 1 < len(streams):
            s_next = scores(*streams[i + 1])
        m = jnp.max(s, axis=-1, keepdims=True)
        p = jnp.exp2(s - m).astype(BF16)
        partial[g, kc] = (m, _dot(p, vd_ref[g, kc * tkc:(kc + 1) * tkc, :]))
    for g in range(AT_KV_HEADS):
        m_all = partial[g, 0][0]
        for kc in range(1, n_kc):
            m_all = jnp.maximum(m_all, partial[g, kc][0])
        o = sum(jnp.exp2(partial[g, kc][0] - m_all) * partial[g, kc][1] for kc in range(n_kc))
        o = o / pltpu.roll(o, HEAD, 1)
        for jj in range(pairs_per_kv):
            even = o[(2 * jj) * tq:(2 * jj + 1) * tq]
            odd = pltpu.roll(o[(2 * jj + 1) * tq:(2 * jj + 2) * tq], HEAD, 1)
            pair = jnp.where(low, even, odd)
            col = (g * pairs_per_kv + jj) * LANES
            o_ref[:, col:col + LANES] = pair.astype(BF16)


def _attn(at, tabs, q_g, k_g, jones):
    bsz, s, _ = at.shape
    tq = min(TQ_ATTN, s)
    cq, sq, ck, sk = tabs
    qw = AT_Q_HEADS * HEAD
    kvw = AT_KV_HEADS * HEAD
    cst = lambda a: pl.BlockSpec(a.shape, lambda b_, i: (0,) * a.ndim)
    jk = jones[0:kvw, 0:kvw]
    return pl.pallas_call(
        _attn_kernel,
        grid=(bsz, s // tq),
        in_specs=[pl.BlockSpec((None, tq, qw), lambda b_, i: (b_, i, 0)),
                  pl.BlockSpec((None, s, kvw), lambda b_, i: (b_, 0, qw // kvw)),
                  pl.BlockSpec((None, s, kvw), lambda b_, i: (b_, 0, qw // kvw + 1)),
                  pl.BlockSpec((tq, qw), lambda b_, i: (i, 0)),
                  pl.BlockSpec((tq, qw), lambda b_, i: (i, 0)),
                  cst(ck), cst(sk), cst(q_g), cst(k_g), cst(jones), cst(jk)],
        out_specs=pl.BlockSpec((None, tq, qw), lambda b_, i: (b_, i, 0)),
        out_shape=jax.ShapeDtypeStruct((bsz, s, qw), BF16),
        scratch_shapes=[pltpu.VMEM((AT_KV_HEADS, s, LANES), BF16), pltpu.VMEM((AT_KV_HEADS, s, LANES), BF16)],
        compiler_params=_cparams("parallel", "arbitrary"),
    )(at, at, at, cq, sq, ck, sk, q_g, k_g, jones, jk)


def _rope_tables(s):
    rows = s // GRID_W
    pos_r = jnp.repeat(jnp.arange(rows), GRID_W).astype(F32)
    pos_c = jnp.tile(jnp.arange(GRID_W), rows).astype(F32)
    half = HEAD // 2
    inv = ROPE_THETA ** (-jnp.arange(0, half, 2, dtype=F32) / half)
    ang_r = pos_r[:, None] * inv
    ang_c = pos_c[:, None] * inv
    cos = jnp.concatenate([jnp.cos(ang_r)] * 2 + [jnp.cos(ang_c)] * 2, axis=-1)
    sin = jnp.concatenate([-jnp.sin(ang_r), jnp.sin(ang_r), -jnp.sin(ang_c), jnp.sin(ang_c)], axis=-1)
    scale = HEAD ** -0.5 * math.log2(math.e)
    return (jnp.tile(cos, (1, AT_Q_HEADS)) * scale, jnp.tile(sin, (1, AT_Q_HEADS)) * scale,
            jnp.tile(cos, (1, AT_KV_HEADS)), jnp.tile(sin, (1, AT_KV_HEADS)))


def _merge_kernel(y0_ref, y1_ref, g_ref, bonus_ref, lng_ref, lnb_ref, j_ref, ycv_ref, yat_ref, sg_ref, x_ref,
                  wb_ref, wo_ref, o_ref):
    j = j_ref[...]
    y = y0_ref[...] + y1_ref[...]
    mu = _segsum(y, j) * (1.0 / HEAD)
    dev = y - mu
    var = _segsum(dev * dev, j) * (1.0 / HEAD)
    yn = dev * lax.rsqrt(var + RW_LN_EPS) * lng_ref[...] + lnb_ref[...]
    y_rw = ((yn + bonus_ref[...]) * g_ref[...]).astype(BF16)
    dm = D_MODEL
    merged = (sg_ref[:, 0:dm].astype(F32) * _dot(y_rw, wb_ref[0])
              + sg_ref[:, dm:2 * dm].astype(F32) * _dot(ycv_ref[...], wb_ref[1])
              + sg_ref[:, 2 * dm:3 * dm].astype(F32) * _dot(yat_ref[...], wb_ref[2]))
    o_ref[...] = x_ref[...] + _dot(merged.astype(BF16), wo_ref[...])


def _merge(yf, yb, g, bonus, ln_g, ln_b, jones, y_cv, y_at, sg, x, w_branch, w_out):
    t = x.shape[0]
    tm = TM_MERGE
    row = lambda n: pl.BlockSpec((tm, n), lambda i: (i, 0))
    return pl.pallas_call(
        _merge_kernel,
        grid=(t // tm,),
        in_specs=[row(RW_WIDTH), row(RW_WIDTH), row(RW_WIDTH), row(RW_WIDTH),
                  _resident(ln_g.shape), _resident(ln_b.shape), _resident(jones.shape),
                  row(CV_WIDTH), row(AT_Q_HEADS * HEAD), row(GATE_COLS), row(D_MODEL),
                  _resident(w_branch.shape), _resident(w_out.shape)],
        out_specs=row(D_MODEL),
        out_shape=jax.ShapeDtypeStruct((t, D_MODEL), F32),
        compiler_params=_cparams("parallel"),
    )(yf, yb, g, bonus, ln_g, ln_b, jones, y_cv, y_at, sg, x, w_branch, w_out)


def _mlp_kernel(x_ref, g_ref, w1_ref, w2_ref, o_ref):
    x = x_ref[...]
    h = (x * lax.rsqrt(jnp.mean(x * x, axis=-1, keepdims=True) + NORM_EPS) * g_ref[...]).astype(BF16)
    acc = x
    for c in range(D_FF // FF_CHUNK):
        cols = slice(c * FF_CHUNK, (c + 1) * FF_CHUNK)
        u = jnp.maximum(_dot(h, w1_ref[:, cols]), 0.0)
        acc = acc + _dot((u * u).astype(BF16), w2_ref[cols, :])
    o_ref[...] = acc


def _mlp(x, g, w1, w2):
    t = x.shape[0]
    tm = TM_MLP
    row = pl.BlockSpec((tm, D_MODEL), lambda i: (i, 0))
    return pl.pallas_call(
        _mlp_kernel,
        grid=(t // tm,),
        in_specs=[row, _resident(g.shape), _resident(w1.shape), _resident(w2.shape)],
        out_specs=row,
        out_shape=jax.ShapeDtypeStruct((t, D_MODEL), F32),
        compiler_params=_cparams("parallel"),
    )(x, g, w1, w2)


def _block_diag2(w):
    z = jnp.zeros_like(w[0])
    return jnp.concatenate([jnp.concatenate([w[0], z], axis=1), jnp.concatenate([z, w[1]], axis=1)], axis=0)


def kernel(x, norm1_g, w_in, rw_mu_prev, rw_mu_next, rw_w0, rw_w2, rw_a0, rw_a2, rw_g2, rw_k_k, rw_k_a, rw_r_k,
           rw_ln_g, rw_ln_b, cv_w, cv_b, at_q_g, at_k_g, w_branch, w_out, norm2_g, w_mlp1, w_mlp2):
    bsz, s, dm = x.shape
    assert dm == D_MODEL and s % GRID_W == 0
    for tile in (TM_INPROJ, TB_SCAN, TQ_ATTN, TK_PREP):
        assert s % min(tile, s) == 0 and min(tile, s) % CHUNK == 0
    t = bsz * s
    tabs = _rope_tables(s)
    masks = _scan_masks()
    jones = jnp.asarray(np.kron(np.eye(RW_WIDTH // HEAD, dtype=np.float32), np.ones((HEAD, HEAD), np.float32)), BF16)
    row = lambda a: a.reshape(1, -1)
    xf = x.reshape(t, dm)
    for l in range(DEPTH):
        r, v, kk, g, bonus, lw, kd, kka, y_cv, at, sg = _inproj(
            xf.reshape(bsz, s, dm), row(norm1_g[l]), w_in[l].astype(BF16), row(rw_mu_prev[l]), row(rw_mu_next[l]),
            row(rw_w0[l]), _block_diag2(rw_w2[l]).astype(BF16), row(rw_a0[l]), _block_diag2(rw_a2[l]).astype(BF16),
            rw_g2[l].astype(BF16), row(rw_k_k[l]), row(rw_k_a[l]), row(rw_r_k[l]), jones, cv_w[l], row(cv_b[l]))
        yf, yb = _scan(r, v, kk, lw, kd, kka, masks)
        y_at = _attn(at, tabs, row(jnp.tile(at_q_g[l], AT_Q_HEADS)),
                     row(jnp.tile(at_k_g[l], AT_KV_HEADS)), jones)
        xf = _merge(yf.reshape(t, RW_WIDTH), yb.reshape(t, RW_WIDTH), g.reshape(t, RW_WIDTH),
                    bonus.reshape(t, RW_WIDTH),
                    row(rw_ln_g[l]), row(rw_ln_b[l]), jones, y_cv.reshape(t, CV_WIDTH),
                    y_at.reshape(t, AT_Q_HEADS * HEAD), sg.reshape(t, GATE_COLS), xf,
                    w_branch[l].astype(BF16), w_out[l].astype(BF16))
        xf = _mlp(xf, row(norm2_g[l]), w_mlp1[l].astype(BF16), w_mlp2[l].astype(BF16))
    return xf.reshape(bsz, s, dm)
```

```python
import math

import numpy as np
import jax
import jax.numpy as jnp
from jax import lax
from jax.experimental import pallas as pl
from jax.experimental.pallas import tpu as pltpu

F32 = jnp.float32
BF16 = jnp.bfloat16

D_MODEL = 1024
DEPTH = 4
GRID_W = 64
NORM_EPS = 1e-6
HEAD = 64
RW_WIDTH = 512
RW_LORA = 128
RW_LN_EPS = 64e-5
RW_COLS = 3 * RW_WIDTH + 3 * RW_LORA
CV_WIDTH = 512
CV_COLS = 3 * CV_WIDTH
AT_Q_HEADS = 8
AT_KV_HEADS = 2
AT_COLS = (AT_Q_HEADS + 2 * AT_KV_HEADS) * HEAD
GATE_COLS = 3 * D_MODEL
ROPE_THETA = 10000.0
D_FF = 4 * D_MODEL

LANES = 128
SUBLANES = 8
VMEM_LIMIT_BYTES = 56 * 1024 * 1024
CHUNK = 64
TM_INPROJ = 256
TB_SCAN = 512
TQ_ATTN = 256
TK_ATTN = 2048
TK_PREP = 512
TM_MERGE = 512
TM_MLP = 512
FF_CHUNK = 1024


def _cparams(*sem):
    return pltpu.CompilerParams(dimension_semantics=sem, vmem_limit_bytes=VMEM_LIMIT_BYTES)


def _dot(a, b):
    return jnp.dot(a, b, preferred_element_type=F32)


def _dot_nt(a, b):
    return lax.dot_general(a, b, (((1,), (1,)), ((), ())), preferred_element_type=F32)


def _segsum(x, j):
    return _dot(x.astype(BF16), j)


def _segsum_split(x, j):
    hi = x.astype(BF16)
    return _dot(hi, j) + _dot((x - hi.astype(F32)).astype(BF16), j)


def _sigmoid(x):
    return 1.0 / (1.0 + jnp.exp(-x))


def _resident(shape):
    nd = len(shape)
    return pl.BlockSpec(shape, lambda *_: (0,) * nd, pipeline_mode=pl.Buffered(1))


def _shift_rows(c, halo_prev, halo_next, first, last):
    n = c.shape[0]
    row = lax.broadcasted_iota(jnp.int32, c.shape, 0)
    lane_row = lax.broadcasted_iota(jnp.int32, (1, c.shape[1]), 1)
    is_first = (lane_row * 0 + first.astype(jnp.int32)) > 0
    is_last = (lane_row * 0 + last.astype(jnp.int32)) > 0
    prev_row = jnp.where(is_first, 0.0, halo_prev[halo_prev.shape[0] - 1:, :])
    next_row = jnp.where(is_last, 0.0, halo_next[0:1, :])
    prev = jnp.where(row == 0, prev_row, pltpu.roll(c, 1, 0))
    nxt = jnp.where(row == n - 1, next_row, pltpu.roll(c, n - 1, 0))
    return prev, nxt


def _halo_specs(ts, width, n_seq_rows, rows):
    r = ts // rows
    last_blk = n_seq_rows // rows - 1
    prev = pl.BlockSpec((None, rows, width), lambda b, i: (b, jnp.maximum(i * r - 1, 0), 0))
    nxt = pl.BlockSpec((None, rows, width), lambda b, i: (b, jnp.minimum((i + 1) * r, last_blk), 0))
    return prev, nxt


def _inproj_kernel(x_ref, xp_ref, xn_ref, g_ref, w_ref, mup_ref, mun_ref, w0_ref, w2_ref, a0_ref, a2_ref, g2_ref,
                   kk_w_ref, ka_ref, rk_ref, j_ref, cvw_ref, cvb_ref,
                   r_ref, v_ref, kk_ref, gate_ref, bonus_ref, lw_ref, kd_ref, kka_ref, ycv_ref, at_ref, sg_ref):
    i = pl.program_id(1)

    def normed(x):
        return (x * lax.rsqrt(jnp.mean(x * x, axis=-1, keepdims=True) + NORM_EPS) * g_ref[...]).astype(BF16)

    h = normed(x_ref[...])
    o0, o1, o2 = RW_COLS, RW_COLS + CV_COLS, RW_COLS + CV_COLS + AT_COLS
    dm = D_MODEL

    def gate_cols(n):
        sg_ref[:, n * dm:(n + 1) * dm] = _sigmoid(_dot(h, w_ref[:, o2 + n * dm:o2 + (n + 1) * dm])).astype(BF16)

    c = _dot(h, w_ref[:, 0:o0])
    h_halo = normed(jnp.concatenate([xp_ref[...], xn_ref[...]], axis=0))
    c_halo = _dot(h_halo, w_ref[:, 0:o0])
    first, last = i == 0, i == pl.num_programs(1) - 1
    cw = CV_WIDTH
    cv = _dot(h, w_ref[:, o0:o1])
    cv_halo = _dot(h_halo, w_ref[:, o0 + cw:o1])
    zc = cv[:, cw:2 * cw] * cv[:, 2 * cw:3 * cw]
    zc_halo = cv_halo[:, 0:cw] * cv_halo[:, cw:2 * cw]
    zc_prev, zc_next = _shift_rows(zc, zc_halo[0:SUBLANES], zc_halo[SUBLANES:2 * SUBLANES], first, last)
    conv = cvb_ref[...] + cvw_ref[0:1, :] * zc_prev + cvw_ref[1:2, :] * zc + cvw_ref[2:3, :] * zc_next
    ycv_ref[...] = (cv[:, 0:cw] * conv).astype(BF16)
    prev, nxt = _shift_rows(c, c_halo[0:SUBLANES], c_halo[SUBLANES:2 * SUBLANES], first, last)
    z = c + mup_ref[...] * (prev - c) + mun_ref[...] * (nxt - c)
    w = RW_WIDTH
    r, k, v = z[:, 0:w], z[:, w:2 * w], z[:, 2 * w:3 * w]
    wd = z[:, 3 * w:3 * w + RW_LORA]
    ad = z[:, 3 * w + RW_LORA:3 * w + 2 * RW_LORA]
    gd = z[:, 3 * w + 2 * RW_LORA:3 * w + 3 * RW_LORA]
    j = j_ref[...]
    w_raw = w0_ref[...] + _dot(jnp.tanh(wd).astype(BF16), w2_ref[...])
    lw = -math.exp(-0.5) * _sigmoid(w_raw)
    a = _sigmoid(a0_ref[...] + _dot(ad.astype(BF16), a2_ref[...]))
    gate_ref[...] = _dot(_sigmoid(gd).astype(BF16), g2_ref[...]).astype(BF16)
    gate_cols(0)
    kkr = k * kk_w_ref[...]
    kk = kkr / jnp.maximum(jnp.sqrt(_segsum_split(kkr * kkr, j)), 1e-12)
    r_ref[...] = r.astype(BF16)
    v_ref[...] = v.astype(BF16)
    kk_ref[...] = kk.astype(BF16)
    at_ref[...] = _dot(h, w_ref[:, o1:o2]).astype(BF16)
    gate_cols(1)
    ksum = jnp.zeros_like(k)
    for d in range(2):
        a_d = a[:, d * w:(d + 1) * w]
        k_d = k * (1.0 + (a_d - 1.0) * ka_ref[...])
        ksum = ksum + k_d
        lw_ref[d] = lw[:, d * w:(d + 1) * w]
        kd_ref[d] = k_d.astype(BF16)
        kka_ref[d] = (kk * a_d).astype(BF16)
    gate_cols(2)
    bonus_ref[...] = (_segsum_split(r * (0.5 * ksum) * rk_ref[...], j) * v).astype(BF16)


def _inproj(x, g, w, mup, mun, w0, w2bd, a0, a2bd, g2, k_k, k_a, r_k, jones, cv_w, cv_b):
    bsz, s, _ = x.shape
    tm = min(TM_INPROJ, s)
    hp, hn = _halo_specs(tm, D_MODEL, s, SUBLANES)
    tok = lambda n: pl.BlockSpec((None, tm, n), lambda b_, i: (b_, i, 0))
    tok2 = pl.BlockSpec((2, None, tm, RW_WIDTH), lambda b_, i: (0, b_, i, 0))
    consts = (g, w, mup, mun, w0, w2bd, a0, a2bd, g2, k_k, k_a, r_k, jones, cv_w, cv_b)
    one = lambda n: jax.ShapeDtypeStruct((bsz, s, n), BF16)
    two = lambda dt: jax.ShapeDtypeStruct((2, bsz, s, RW_WIDTH), dt)
    return pl.pallas_call(
        _inproj_kernel,
        grid=(bsz, s // tm),
        in_specs=[tok(D_MODEL), hp, hn] + [_resident(a.shape) for a in consts],
        out_specs=[tok(RW_WIDTH)] * 5 + [tok2] * 3 + [tok(CV_WIDTH), tok(AT_COLS), tok(GATE_COLS)],
        out_shape=[one(RW_WIDTH)] * 5 + [two(F32), two(BF16), two(BF16)]
        + [one(CV_WIDTH), one(AT_COLS), one(GATE_COLS)],
        compiler_params=_cparams("parallel", "arbitrary"),
    )(x, x, x, *consts)


def _scan_masks():
    c = CHUNK
    t = np.arange(c)
    lower = (t[None, :] <= t[:, None]).astype(np.float32)
    strict_l = (t[None, :] < t[:, None]).astype(np.float32)
    ones = np.ones((c, c), np.float32)
    tri = np.stack([np.concatenate([lower, ones], 0), np.concatenate([lower.T, ones], 0)])
    strict = np.stack([np.tile(strict_l, (1, 2)), np.tile(strict_l.T, (1, 2))])
    incl = np.stack([np.tile(lower, (1, 2)), np.tile(lower.T, (1, 2))])
    return jnp.asarray(tri, BF16), jnp.asarray(strict, F32), jnp.asarray(incl, F32)


def _scan_kernel(rf_ref, vf_ref, kkf_ref, rb_ref, vb_ref, kkb_ref,
                 lwf_ref, kdf_ref, kkaf_ref, lwb_ref, kdb_ref, kkab_ref,
                 tri_ref, strict_ref, incl_ref, yf_ref, yb_ref, st_ref,
                 kq_ref, rq_ref, kdec_ref, bdec_ref, eend_ref, avk_ref, ark_ref, arb_ref, m_ref, tinv_ref):
    c = CHUNK
    n_chunks = rf_ref.shape[0] // c
    n_pairs = RW_WIDTH // LANES

    @pl.when(pl.program_id(1) == 0)
    def _():
        st_ref[...] = jnp.zeros_like(st_ref)

    lane = lax.broadcasted_iota(jnp.int32, (c, LANES), 1)
    row = lax.broadcasted_iota(jnp.int32, (c, LANES), 0)
    head0 = lane < HEAD
    eye = jnp.where((lane & (HEAD - 1)) == row, 1.0, 0.0).astype(F32)
    ri = lax.broadcasted_iota(jnp.int32, (LANES, LANES), 0)
    ci = lax.broadcasted_iota(jnp.int32, (LANES, LANES), 1)
    same_head = jnp.where((ri < HEAD) == (ci < HEAD), 1.0, 0.0).astype(F32)

    def stack(x):
        return jnp.concatenate([jnp.where(head0, x, 0.0), jnp.where(head0, 0.0, x)], axis=0)

    def stack_b(x):
        return stack(x).astype(BF16)

    dir_refs = ((rf_ref, vf_ref, kkf_ref, lwf_ref, kdf_ref, kkaf_ref, yf_ref),
                (rb_ref, vb_ref, kkb_ref, lwb_ref, kdb_ref, kkab_ref, yb_ref))

    def chunk_rows(d, step):
        cidx = step if d == 0 else n_chunks - 1 - step
        start = cidx * c
        return pl.ds(start if isinstance(start, int) else pl.multiple_of(start, c), c)

    def local_stages(step):
        slot = step
        ch = []
        for d, (r_ref, v_ref, kk_ref, lw_ref, kd_ref, kka_ref, y_ref) in enumerate(dir_refs):
            rows = chunk_rows(d, step)
            lw = lw_ref[rows, :]
            hi = lw.astype(BF16)
            rem = lw - hi.astype(F32)
            mid = rem.astype(BF16)
            lo = (rem - mid.astype(F32)).astype(BF16)
            tri = tri_ref[d]
            cum = _dot(tri, hi) + _dot(tri, mid) + _dot(tri, lo)
            lp = cum[0:c]
            e_end = jnp.exp(cum[c:c + SUBLANES])
            e_n = jnp.exp(-lp)
            rq = (r_ref[rows, :] * jnp.exp(lp)).astype(BF16)
            kq = (kk_ref[rows, :] * jnp.exp(lp - lw)).astype(BF16)
            ks = kd_ref[rows, :] * e_n
            bs = kka_ref[rows, :] * e_n
            kq_ref[d, slot] = kq
            rq_ref[d, slot] = rq
            kdec_ref[d, slot] = (ks * e_end[0:1]).astype(BF16)
            bdec_ref[d, slot] = (bs * e_end[0:1]).astype(BF16)
            eend_ref[d, slot] = e_end
            for p in range(n_pairs):
                sl = slice(p * LANES, (p + 1) * LANES)
                ch.append(dict(d=d, sl=sl, kq=kq[:, sl], rq=rq[:, sl], ks=ks[:, sl], bs=bs[:, sl]))
        n = range(len(ch))
        strict = [strict_ref[x["d"]] for x in ch]
        incl = [incl_ref[x["d"]] for x in ch]
        yield
        lq = [jnp.concatenate([x["kq"], x["rq"]], axis=0) for x in ch]
        s_ks = [stack_b(x["ks"]) for x in ch]
        s_bs = [stack_b(x["bs"]) for x in ch]
        gkb = [_dot_nt(lq[i], jnp.concatenate([s_ks[i], s_bs[i]], axis=0)) for i in n]
        yield
        gk = [x[:, 0:LANES] for x in gkb]
        gb = [x[:, LANES:2 * LANES] for x in gkb]
        m = [-(gb[i][0:c] * strict[i]) for i in n]
        for i, x in enumerate(ch):
            avk_ref[x["d"], slot, :, x["sl"]] = (gk[i][0:c] * strict[i]).astype(BF16)
            ark_ref[x["d"], slot, :, x["sl"]] = (gk[i][c:2 * c] * incl[i]).astype(BF16)
            arb_ref[x["d"], slot, :, x["sl"]] = (gb[i][c:2 * c] * incl[i]).astype(BF16)
            m_ref[x["d"], slot, :, x["sl"]] = m[i].astype(BF16)
        t_inv = [eye + x for x in m]
        pw = [_dot(x.astype(BF16), stack_b(x)) for x in m]
        yield
        n_levels = int(math.log2(c)) - 1
        for lvl in range(n_levels):
            pw_bd = [stack_b(x) for x in pw]
            if lvl + 1 < n_levels:
                both = [_dot(jnp.concatenate([t_inv[i].astype(BF16), pw[i].astype(BF16)], axis=0), pw_bd[i])
                        for i in n]
                t_inv = [t_inv[i] + both[i][0:c] for i in n]
                pw = [both[i][c:2 * c] for i in n]
            else:
                t_inv = [t_inv[i] + _dot(t_inv[i].astype(BF16), pw_bd[i]) for i in n]
            yield
        for i, x in enumerate(ch):
            tinv_ref[x["d"], slot, :, x["sl"]] = t_inv[i].astype(BF16)

    def state_stages(step):
        slot = step
        ch = []
        for d, (r_ref, v_ref, kk_ref, lw_ref, kd_ref, kka_ref, y_ref) in enumerate(dir_refs):
            rows = chunk_rows(d, step)
            for p in range(n_pairs):
                ch.append(dict(d=d, p=p, rows=rows, sl=slice(p * LANES, (p + 1) * LANES), y_ref=y_ref, v_ref=v_ref))
        n = range(len(ch))
        get = lambda ref: [ref[x["d"], slot, :, x["sl"]] for x in ch]
        v = [x["v_ref"][x["rows"], x["sl"]].astype(F32) for x in ch]
        s_vb = [stack_b(x) for x in v]
        st = [st_ref[x["d"], x["p"]] for x in ch]
        stb = [x.astype(BF16) for x in st]
        kq_b, rq_b, a_vk, a_rk, a_rb, t_b, m_b = (get(r) for r in (
            kq_ref, rq_ref, avk_ref, ark_ref, arb_ref, tinv_ref, m_ref))
        from_state = [_dot_nt(jnp.concatenate([kq_b[i], rq_b[i]], axis=0), stb[i]) for i in n]
        from_v = [_dot(jnp.concatenate([a_vk[i], a_rk[i]], axis=0), s_vb[i]) for i in n]
        yield
        rhs = [from_state[i][0:c] + from_v[i][0:c] for i in n]
        u0 = [-_dot(t_b[i], stack_b(rhs[i])) for i in n]
        yield
        res = [_dot(m_b[i], stack_b(u0[i])) - rhs[i] - u0[i] for i in n]
        yield
        u = [u0[i] + _dot(t_b[i], stack_b(res[i])) for i in n]
        yield
        y = [from_state[i][c:2 * c] + from_v[i][c:2 * c] + _dot(a_rb[i], stack_b(u[i])) for i in n]
        vu_t = [jnp.concatenate([v[i], u[i]], axis=0).T.astype(BF16) for i in n]
        kb = [jnp.concatenate([kdec_ref[x["d"], slot, :, x["sl"]], bdec_ref[x["d"], slot, :, x["sl"]]], axis=0)
              for x in ch]
        upd = [_dot(vu_t[i], kb[i]) * same_head for i in n]
        for i, x in enumerate(ch):
            x["y_ref"][x["rows"], x["sl"]] = y[i]
            st_ref[x["d"], x["p"]] = st[i] * eend_ref[x["d"], slot, 0:1, x["sl"]] + upd[i]

    def emit(*gens, order=None):
        live = {i: g for i, g in enumerate(gens)}
        for i in list(order or ()) + [None]:
            while i is None and live:
                for k in list(live):
                    if next(live[k], live) is live:
                        del live[k]
            if i in live and next(live[i], live) is live:
                del live[i]

    emit(local_stages(0))

    def body(i, carry):
        emit(local_stages(i + 1), state_stages(i), order=(0, 1, 0, 0, 1, 0, 0, 1, 0, 1, 0, 1, 0))
        return carry

    lax.fori_loop(0, n_chunks - 1, body, 0)
    emit(state_stages(n_chunks - 1))


def _scan(r, v, kk, lw, kd, kka, masks):
    bsz, s, w = r.shape
    tb = min(TB_SCAN, s)
    nblk = s // tb
    fwd = pl.BlockSpec((None, tb, w), lambda b_, j: (b_, j, 0))
    bwd = pl.BlockSpec((None, tb, w), lambda b_, j: (b_, nblk - 1 - j, 0))
    fwd2 = pl.BlockSpec((None, None, tb, w), lambda b_, j: (0, b_, j, 0))
    bwd2 = pl.BlockSpec((None, None, tb, w), lambda b_, j: (1, b_, nblk - 1 - j, 0))
    out = jax.ShapeDtypeStruct((bsz, s, w), F32)
    per_chunk = lambda dt: pltpu.VMEM((2, tb // CHUNK, CHUNK, w), dt)
    return pl.pallas_call(
        _scan_kernel,
        grid=(bsz, nblk),
        in_specs=[fwd, fwd, fwd, bwd, bwd, bwd, fwd2, fwd2, fwd2, bwd2, bwd2, bwd2]
        + [_resident(a.shape) for a in masks],
        out_specs=[fwd, bwd],
        out_shape=[out, out],
        scratch_shapes=[pltpu.VMEM((2, w // LANES, LANES, LANES), F32), per_chunk(BF16), per_chunk(BF16),
                        per_chunk(BF16), per_chunk(BF16), pltpu.VMEM((2, tb // CHUNK, SUBLANES, w), F32)]
        + [per_chunk(BF16)] * 5,
        compiler_params=_cparams("parallel", "arbitrary"),
    )(r, v, kk, r, v, kk, lw, kd, kka, lw, kd, kka, *masks)


def _rope_partner(x):
    n = x.shape[1]
    lane = lax.broadcasted_iota(jnp.int32, x.shape, 1)
    return jnp.where((lane & 31) < 16, pltpu.roll(x, n - 16, 1), pltpu.roll(x, 16, 1))


def _attn_kernel(q_ref, k_ref, v_ref, cq_ref, sq_ref, ck_ref, sk_ref, qg_ref, kg_ref, jq_ref, jk_ref,
                 o_ref, kd_ref, vd_ref):
    s_len = k_ref.shape[0]
    tq = q_ref.shape[0]
    tk = min(TK_PREP, s_len)

    @pl.when(pl.program_id(1) == 0)
    def _():
        def body(i, carry):
            rows = pl.ds(pl.multiple_of(i * tk, tk), tk)
            k = k_ref[rows, :].astype(F32)
            kn = k * lax.rsqrt(_segsum(k * k, jk_ref[...]) * (1.0 / HEAD) + NORM_EPS) * kg_ref[...]
            kr = kn * ck_ref[rows, :] + _rope_partner(kn) * sk_ref[rows, :]
            v = v_ref[rows, :].astype(F32)
            low = lax.broadcasted_iota(jnp.int32, kr.shape, 1) < HEAD
            ksw = pltpu.roll(kr, HEAD, 1)
            kd_ref[0, rows, :] = jnp.where(low, kr, ksw).astype(BF16)
            kd_ref[1, rows, :] = jnp.where(low, ksw, kr).astype(BF16)
            vd_ref[0, rows, :] = jnp.where(low, v, 1.0).astype(BF16)
            vd_ref[1, rows, :] = jnp.where(low, pltpu.roll(v, HEAD, 1), 1.0).astype(BF16)
            return carry
        lax.fori_loop(0, s_len // tk, body, 0)

    q = q_ref[...].astype(F32)
    qn = q * lax.rsqrt(_segsum(q * q, jq_ref[...]) * (1.0 / HEAD) + NORM_EPS) * qg_ref[...]
    qr = qn * cq_ref[...] + _rope_partner(qn) * sq_ref[...]
    low = lax.broadcasted_iota(jnp.int32, (tq, LANES), 1) < HEAD
    pairs_per_kv = AT_Q_HEADS // AT_KV_HEADS // 2
    tkc = min(TK_ATTN, s_len)
    n_kc = s_len // tkc
    qs = []
    for g in range(AT_KV_HEADS):
        parts = []
        for jj in range(pairs_per_kv):
            qp = qr[:, (g * pairs_per_kv + jj) * LANES:(g * pairs_per_kv + jj + 1) * LANES]
            parts += [jnp.where(low, qp, 0.0), jnp.where(low, 0.0, qp)]
        qs.append(jnp.concatenate(parts, axis=0).astype(BF16))
    streams = [(g, kc) for g in range(AT_KV_HEADS) for kc in range(n_kc)]
    scores = lambda g, kc: _dot_nt(qs[g], kd_ref[g, kc * tkc:(kc + 1) * tkc, :])
    partial, s_next = {}, scores(*streams[0])
    for i, (g, kc) in enumerate(streams):
        s = s_next
        if i + 1 < len(streams):
            s_next = scores(*streams[i + 1])
        m = jnp.max(s, axis=-1, keepdims=True)
        p = jnp.exp2(s.astype(BF16) - m.astype(BF16))
        partial[g, kc] = (m, _dot(p, vd_ref[g, kc * tkc:(kc + 1) * tkc, :]))
    for g in range(AT_KV_HEADS):
        m_all = partial[g, 0][0]
        for kc in range(1, n_kc):
            m_all = jnp.maximum(m_all, partial[g, kc][0])
        o = sum(jnp.exp2(partial[g, kc][0] - m_all) * partial[g, kc][1] for kc in range(n_kc))
        o = o / pltpu.roll(o, HEAD, 1)
        for jj in range(pairs_per_kv):
            even = o[(2 * jj) * tq:(2 * jj + 1) * tq]
            odd = pltpu.roll(o[(2 * jj + 1) * tq:(2 * jj + 2) * tq], HEAD, 1)
            pair = jnp.where(low, even, odd)
            col = (g * pairs_per_kv + jj) * LANES
            o_ref[:, col:col + LANES] = pair.astype(BF16)


def _attn(at, tabs, q_g, k_g, jones):
    bsz, s, _ = at.shape
    tq = min(TQ_ATTN, s)
    cq, sq, ck, sk = tabs
    qw = AT_Q_HEADS * HEAD
    kvw = AT_KV_HEADS * HEAD
    cst = lambda a: pl.BlockSpec(a.shape, lambda b_, i: (0,) * a.ndim)
    jk = jones[0:kvw, 0:kvw]
    return pl.pallas_call(
        _attn_kernel,
        grid=(bsz, s // tq),
        in_specs=[pl.BlockSpec((None, tq, qw), lambda b_, i: (b_, i, 0)),
                  pl.BlockSpec((None, s, kvw), lambda b_, i: (b_, 0, qw // kvw)),
                  pl.BlockSpec((None, s, kvw), lambda b_, i: (b_, 0, qw // kvw + 1)),
                  pl.BlockSpec((tq, qw), lambda b_, i: (i, 0)),
                  pl.BlockSpec((tq, qw), lambda b_, i: (i, 0)),
                  cst(ck), cst(sk), cst(q_g), cst(k_g), cst(jones), cst(jk)],
        out_specs=pl.BlockSpec((None, tq, qw), lambda b_, i: (b_, i, 0)),
        out_shape=jax.ShapeDtypeStruct((bsz, s, qw), BF16),
        scratch_shapes=[pltpu.VMEM((AT_KV_HEADS, s, LANES), BF16), pltpu.VMEM((AT_KV_HEADS, s, LANES), BF16)],
        compiler_params=_cparams("parallel", "arbitrary"),
    )(at, at, at, cq, sq, ck, sk, q_g, k_g, jones, jk)


def _rope_tables(s):
    rows = s // GRID_W
    pos_r = jnp.repeat(jnp.arange(rows), GRID_W).astype(F32)
    pos_c = jnp.tile(jnp.arange(GRID_W), rows).astype(F32)
    half = HEAD // 2
    inv = ROPE_THETA ** (-jnp.arange(0, half, 2, dtype=F32) / half)
    ang_r = pos_r[:, None] * inv
    ang_c = pos_c[:, None] * inv
    cos = jnp.concatenate([jnp.cos(ang_r)] * 2 + [jnp.cos(ang_c)] * 2, axis=-1)
    sin = jnp.concatenate([-jnp.sin(ang_r), jnp.sin(ang_r), -jnp.sin(ang_c), jnp.sin(ang_c)], axis=-1)
    scale = HEAD ** -0.5 * math.log2(math.e)
    return (jnp.tile(cos, (1, AT_Q_HEADS)) * scale, jnp.tile(sin, (1, AT_Q_HEADS)) * scale,
            jnp.tile(cos, (1, AT_KV_HEADS)), jnp.tile(sin, (1, AT_KV_HEADS)))


def _merge_kernel(y0_ref, y1_ref, g_ref, bonus_ref, lng_ref, lnb_ref, j_ref, ycv_ref, yat_ref, sg_ref, x_ref,
                  wb_ref, wo_ref, o_ref):
    j = j_ref[...]
    y = y0_ref[...] + y1_ref[...]
    mu = _segsum(y, j) * (1.0 / HEAD)
    dev = y - mu
    var = _segsum(dev * dev, j) * (1.0 / HEAD)
    yn = dev * lax.rsqrt(var + RW_LN_EPS) * lng_ref[...] + lnb_ref[...]
    y_rw = ((yn + bonus_ref[...]) * g_ref[...]).astype(BF16)
    dm = D_MODEL
    merged = (sg_ref[:, 0:dm].astype(F32) * _dot(y_rw, wb_ref[0])
              + sg_ref[:, dm:2 * dm].astype(F32) * _dot(ycv_ref[...], wb_ref[1])
              + sg_ref[:, 2 * dm:3 * dm].astype(F32) * _dot(yat_ref[...], wb_ref[2]))
    o_ref[...] = x_ref[...] + _dot(merged.astype(BF16), wo_ref[...])


def _merge(yf, yb, g, bonus, ln_g, ln_b, jones, y_cv, y_at, sg, x, w_branch, w_out):
    t = x.shape[0]
    tm = TM_MERGE
    row = lambda n: pl.BlockSpec((tm, n), lambda i: (i, 0))
    return pl.pallas_call(
        _merge_kernel,
        grid=(t // tm,),
        in_specs=[row(RW_WIDTH), row(RW_WIDTH), row(RW_WIDTH), row(RW_WIDTH),
                  _resident(ln_g.shape), _resident(ln_b.shape), _resident(jones.shape),
                  row(CV_WIDTH), row(AT_Q_HEADS * HEAD), row(GATE_COLS), row(D_MODEL),
                  _resident(w_branch.shape), _resident(w_out.shape)],
        out_specs=row(D_MODEL),
        out_shape=jax.ShapeDtypeStruct((t, D_MODEL), F32),
        compiler_params=_cparams("parallel"),
    )(yf, yb, g, bonus, ln_g, ln_b, jones, y_cv, y_at, sg, x, w_branch, w_out)


def _mlp_kernel(x_ref, g_ref, w1_ref, w2_ref, o_ref):
    x = x_ref[...]
    h = (x * lax.rsqrt(jnp.mean(x * x, axis=-1, keepdims=True) + NORM_EPS) * g_ref[...]).astype(BF16)
    acc = x
    for c in range(D_FF // FF_CHUNK):
        cols = slice(c * FF_CHUNK, (c + 1) * FF_CHUNK)
        u = jnp.maximum(_dot(h, w1_ref[:, cols]), 0.0)
        acc = acc + _dot((u * u).astype(BF16), w2_ref[cols, :])
    o_ref[...] = acc


def _mlp(x, g, w1, w2):
    t = x.shape[0]
    tm = TM_MLP
    row = pl.BlockSpec((tm, D_MODEL), lambda i: (i, 0))
    return pl.pallas_call(
        _mlp_kernel,
        grid=(t // tm,),
        in_specs=[row, _resident(g.shape), _resident(w1.shape), _resident(w2.shape)],
        out_specs=row,
        out_shape=jax.ShapeDtypeStruct((t, D_MODEL), F32),
        compiler_params=_cparams("parallel"),
    )(x, g, w1, w2)


def _block_diag2(w):
    z = jnp.zeros_like(w[0])
    return jnp.concatenate([jnp.concatenate([w[0], z], axis=1), jnp.concatenate([z, w[1]], axis=1)], axis=0)


def kernel(x, norm1_g, w_in, rw_mu_prev, rw_mu_next, rw_w0, rw_w2, rw_a0, rw_a2, rw_g2, rw_k_k, rw_k_a, rw_r_k,
           rw_ln_g, rw_ln_b, cv_w, cv_b, at_q_g, at_k_g, w_branch, w_out, norm2_g, w_mlp1, w_mlp2):
    bsz, s, dm = x.shape
    assert dm == D_MODEL and s % GRID_W == 0
    for tile in (TM_INPROJ, TB_SCAN, TQ_ATTN, TK_PREP):
        assert s % min(tile, s) == 0 and min(tile, s) % CHUNK == 0
    t = bsz * s
    tabs = _rope_tables(s)
    masks = _scan_masks()
    jones = jnp.asarray(np.kron(np.eye(RW_WIDTH // HEAD, dtype=np.float32), np.ones((HEAD, HEAD), np.float32)), BF16)
    row = lambda a: a.reshape(1, -1)
    xf = x.reshape(t, dm)
    for l in range(DEPTH):
        r, v, kk, g, bonus, lw, kd, kka, y_cv, at, sg = _inproj(
            xf.reshape(bsz, s, dm), row(norm1_g[l]), w_in[l].astype(BF16), row(rw_mu_prev[l]), row(rw_mu_next[l]),
            row(rw_w0[l]), _block_diag2(rw_w2[l]).astype(BF16), row(rw_a0[l]), _block_diag2(rw_a2[l]).astype(BF16),
            rw_g2[l].astype(BF16), row(rw_k_k[l]), row(rw_k_a[l]), row(rw_r_k[l]), jones, cv_w[l], row(cv_b[l]))
        yf, yb = _scan(r, v, kk, lw, kd, kka, masks)
        y_at = _attn(at, tabs, row(jnp.tile(at_q_g[l], AT_Q_HEADS)),
                     row(jnp.tile(at_k_g[l], AT_KV_HEADS)), jones)
        xf = _merge(yf.reshape(t, RW_WIDTH), yb.reshape(t, RW_WIDTH), g.reshape(t, RW_WIDTH),
                    bonus.reshape(t, RW_WIDTH),
                    row(rw_ln_g[l]), row(rw_ln_b[l]), jones, y_cv.reshape(t, CV_WIDTH),
                    y_at.reshape(t, AT_Q_HEADS * HEAD), sg.reshape(t, GATE_COLS), xf,
                    w_branch[l].astype(BF16), w_out[l].astype(BF16))
        xf = _mlp(xf, row(norm2_g[l]), w_mlp1[l].astype(BF16), w_mlp2[l].astype(BF16))
    return xf.reshape(bsz, s, dm)
```

```python
import math

import numpy as np
import jax
import jax.numpy as jnp
from jax import lax
from jax.experimental import pallas as pl
from jax.experimental.pallas import tpu as pltpu

F32 = jnp.float32
BF16 = jnp.bfloat16

D_MODEL = 1024
DEPTH = 4
GRID_W = 64
NORM_EPS = 1e-6
HEAD = 64
RW_WIDTH = 512
RW_LORA = 128
RW_LN_EPS = 64e-5
RW_COLS = 3 * RW_WIDTH + 3 * RW_LORA
CV_WIDTH = 512
CV_COLS = 3 * CV_WIDTH
AT_Q_HEADS = 8
AT_KV_HEADS = 2
AT_COLS = (AT_Q_HEADS + 2 * AT_KV_HEADS) * HEAD
GATE_COLS = 3 * D_MODEL
ROPE_THETA = 10000.0
D_FF = 4 * D_MODEL

LANES = 128
SUBLANES = 8
VMEM_LIMIT_BYTES = 56 * 1024 * 1024
CHUNK = 64
TM_INPROJ = 256
TB_SCAN = 512
TQ_ATTN = 256
TK_ATTN = 2048
TK_PREP = 512
TM_MERGE = 512
TM_MLP = 512
FF_CHUNK = 1024


def _cparams(*sem):
    return pltpu.CompilerParams(dimension_semantics=sem, vmem_limit_bytes=VMEM_LIMIT_BYTES)


def _dot(a, b):
    return jnp.dot(a, b, preferred_element_type=F32)


def _dot_nt(a, b):
    return lax.dot_general(a, b, (((1,), (1,)), ((), ())), preferred_element_type=F32)


def _segsum(x, j):
    return _dot(x.astype(BF16), j)


def _segsum_split(x, j):
    hi = x.astype(BF16)
    return _dot(hi, j) + _dot((x - hi.astype(F32)).astype(BF16), j)


def _sigmoid(x):
    return 1.0 / (1.0 + jnp.exp(-x))


def _resident(shape):
    nd = len(shape)
    return pl.BlockSpec(shape, lambda *_: (0,) * nd, pipeline_mode=pl.Buffered(1))


def _shift_rows(c, halo_prev, halo_next, first, last):
    n = c.shape[0]
    row = lax.broadcasted_iota(jnp.int32, c.shape, 0)
    lane_row = lax.broadcasted_iota(jnp.int32, (1, c.shape[1]), 1)
    is_first = (lane_row * 0 + first.astype(jnp.int32)) > 0
    is_last = (lane_row * 0 + last.astype(jnp.int32)) > 0
    prev_row = jnp.where(is_first, 0.0, halo_prev[halo_prev.shape[0] - 1:, :])
    next_row = jnp.where(is_last, 0.0, halo_next[0:1, :])
    prev = jnp.where(row == 0, prev_row, pltpu.roll(c, 1, 0))
    nxt = jnp.where(row == n - 1, next_row, pltpu.roll(c, n - 1, 0))
    return prev, nxt


def _halo_specs(ts, width, n_seq_rows, rows):
    r = ts // rows
    last_blk = n_seq_rows // rows - 1
    prev = pl.BlockSpec((None, rows, width), lambda b, i: (b, jnp.maximum(i * r - 1, 0), 0))
    nxt = pl.BlockSpec((None, rows, width), lambda b, i: (b, jnp.minimum((i + 1) * r, last_blk), 0))
    return prev, nxt


def _inproj_kernel(x_ref, xp_ref, xn_ref, g_ref, w_ref, mup_ref, mun_ref, w0_ref, w2_ref, a0_ref, a2_ref, g2_ref,
                   kk_w_ref, ka_ref, rk_ref, j_ref, cvw_ref, cvb_ref,
                   r_ref, v_ref, kk_ref, gate_ref, bonus_ref, lw_ref, kd_ref, kka_ref, ycv_ref, at_ref, sg_ref):
    i = pl.program_id(1)

    def normed(x):
        return (x * lax.rsqrt(jnp.mean(x * x, axis=-1, keepdims=True) + NORM_EPS) * g_ref[...]).astype(BF16)

    h = normed(x_ref[...])
    o0, o1, o2 = RW_COLS, RW_COLS + CV_COLS, RW_COLS + CV_COLS + AT_COLS
    dm = D_MODEL

    def gate_cols(n):
        sg_ref[:, n * dm:(n + 1) * dm] = _sigmoid(_dot(h, w_ref[:, o2 + n * dm:o2 + (n + 1) * dm])).astype(BF16)

    c = _dot(h, w_ref[:, 0:o0])
    h_halo = normed(jnp.concatenate([xp_ref[...], xn_ref[...]], axis=0))
    c_halo = _dot(h_halo, w_ref[:, 0:o0])
    first, last = i == 0, i == pl.num_programs(1) - 1
    cw = CV_WIDTH
    cv = _dot(h, w_ref[:, o0:o1])
    cv_halo = _dot(h_halo, w_ref[:, o0 + cw:o1])
    zc = cv[:, cw:2 * cw] * cv[:, 2 * cw:3 * cw]
    zc_halo = cv_halo[:, 0:cw] * cv_halo[:, cw:2 * cw]
    zc_prev, zc_next = _shift_rows(zc, zc_halo[0:SUBLANES], zc_halo[SUBLANES:2 * SUBLANES], first, last)
    conv = cvb_ref[...] + cvw_ref[0:1, :] * zc_prev + cvw_ref[1:2, :] * zc + cvw_ref[2:3, :] * zc_next
    ycv_ref[...] = (cv[:, 0:cw] * conv).astype(BF16)
    prev, nxt = _shift_rows(c, c_halo[0:SUBLANES], c_halo[SUBLANES:2 * SUBLANES], first, last)
    z = c + mup_ref[...] * (prev - c) + mun_ref[...] * (nxt - c)
    w = RW_WIDTH
    r, k, v = z[:, 0:w], z[:, w:2 * w], z[:, 2 * w:3 * w]
    wd = z[:, 3 * w:3 * w + RW_LORA]
    ad = z[:, 3 * w + RW_LORA:3 * w + 2 * RW_LORA]
    gd = z[:, 3 * w + 2 * RW_LORA:3 * w + 3 * RW_LORA]
    j = j_ref[...]
    w_raw = w0_ref[...] + _dot(jnp.tanh(wd).astype(BF16), w2_ref[...])
    lw = -math.exp(-0.5) * _sigmoid(w_raw)
    a = _sigmoid(a0_ref[...] + _dot(ad.astype(BF16), a2_ref[...]))
    gate_ref[...] = _dot(_sigmoid(gd).astype(BF16), g2_ref[...]).astype(BF16)
    gate_cols(0)
    kkr = k * kk_w_ref[...]
    kk = kkr / jnp.maximum(jnp.sqrt(_segsum_split(kkr * kkr, j)), 1e-12)
    r_ref[...] = r.astype(BF16)
    v_ref[...] = v.astype(BF16)
    kk_ref[...] = kk.astype(BF16)
    at_ref[...] = _dot(h, w_ref[:, o1:o2]).astype(BF16)
    gate_cols(1)
    ksum = jnp.zeros_like(k)
    for d in range(2):
        a_d = a[:, d * w:(d + 1) * w]
        k_d = k * (1.0 + (a_d - 1.0) * ka_ref[...])
        ksum = ksum + k_d
        lw_ref[d] = lw[:, d * w:(d + 1) * w]
        kd_ref[d] = k_d.astype(BF16)
        kka_ref[d] = (kk * a_d).astype(BF16)
    gate_cols(2)
    bonus_ref[...] = (_segsum_split(r * (0.5 * ksum) * rk_ref[...], j) * v).astype(BF16)


def _inproj(x, g, w, mup, mun, w0, w2bd, a0, a2bd, g2, k_k, k_a, r_k, jones, cv_w, cv_b):
    bsz, s, _ = x.shape
    tm = min(TM_INPROJ, s)
    hp, hn = _halo_specs(tm, D_MODEL, s, SUBLANES)
    tok = lambda n: pl.BlockSpec((None, tm, n), lambda b_, i: (b_, i, 0))
    tok2 = pl.BlockSpec((2, None, tm, RW_WIDTH), lambda b_, i: (0, b_, i, 0))
    consts = (g, w, mup, mun, w0, w2bd, a0, a2bd, g2, k_k, k_a, r_k, jones, cv_w, cv_b)
    one = lambda n: jax.ShapeDtypeStruct((bsz, s, n), BF16)
    two = lambda dt: jax.ShapeDtypeStruct((2, bsz, s, RW_WIDTH), dt)
    return pl.pallas_call(
        _inproj_kernel,
        grid=(bsz, s // tm),
        in_specs=[tok(D_MODEL), hp, hn] + [_resident(a.shape) for a in consts],
        out_specs=[tok(RW_WIDTH)] * 5 + [tok2] * 3 + [tok(CV_WIDTH), tok(AT_COLS), tok(GATE_COLS)],
        out_shape=[one(RW_WIDTH)] * 5 + [two(F32), two(BF16), two(BF16)]
        + [one(CV_WIDTH), one(AT_COLS), one(GATE_COLS)],
        compiler_params=_cparams("parallel", "arbitrary"),
    )(x, x, x, *consts)


def _scan_masks():
    c = CHUNK
    t = np.arange(c)
    lower = (t[None, :] <= t[:, None]).astype(np.float32)
    strict_l = (t[None, :] < t[:, None]).astype(np.float32)
    ones = np.ones((2 * SUBLANES, c), np.float32)
    tri = np.stack([np.concatenate([lower, ones], 0), np.concatenate([lower.T, ones], 0)])
    strict = np.stack([np.tile(strict_l, (1, 2)), np.tile(strict_l.T, (1, 2))])
    incl = np.stack([np.tile(lower, (1, 2)), np.tile(lower.T, (1, 2))])
    return jnp.asarray(tri, BF16), jnp.asarray(strict, F32), jnp.asarray(incl, F32)


def _scan_kernel(rf_ref, vf_ref, kkf_ref, rb_ref, vb_ref, kkb_ref,
                 lwf_ref, kdf_ref, kkaf_ref, lwb_ref, kdb_ref, kkab_ref,
                 tri_ref, strict_ref, incl_ref, yf_ref, yb_ref, st_ref,
                 kq_ref, rq_ref, kdec_ref, bdec_ref, eend_ref, avk_ref, ark_ref, arb_ref, m_ref, tinv_ref):
    c = CHUNK
    n_chunks = rf_ref.shape[0] // c
    n_pairs = RW_WIDTH // LANES

    @pl.when(pl.program_id(1) == 0)
    def _():
        st_ref[...] = jnp.zeros_like(st_ref)

    lane = lax.broadcasted_iota(jnp.int32, (c, LANES), 1)
    row = lax.broadcasted_iota(jnp.int32, (c, LANES), 0)
    head0 = lane < HEAD
    eye = jnp.where((lane & (HEAD - 1)) == row, 1.0, 0.0).astype(F32)
    ri = lax.broadcasted_iota(jnp.int32, (LANES, LANES), 0)
    ci = lax.broadcasted_iota(jnp.int32, (LANES, LANES), 1)
    same_head = jnp.where((ri < HEAD) == (ci < HEAD), 1.0, 0.0).astype(F32)

    def stack(x):
        return jnp.concatenate([jnp.where(head0, x, 0.0), jnp.where(head0, 0.0, x)], axis=0)

    def stack_b(x):
        return stack(x).astype(BF16)

    dir_refs = ((rf_ref, vf_ref, kkf_ref, lwf_ref, kdf_ref, kkaf_ref, yf_ref),
                (rb_ref, vb_ref, kkb_ref, lwb_ref, kdb_ref, kkab_ref, yb_ref))

    def chunk_rows(d, step):
        cidx = step if d == 0 else n_chunks - 1 - step
        start = cidx * c
        return pl.ds(start if isinstance(start, int) else pl.multiple_of(start, c), c)

    def local_stages(step):
        slot = step
        ch = []
        for d, (r_ref, v_ref, kk_ref, lw_ref, kd_ref, kka_ref, y_ref) in enumerate(dir_refs):
            rows = chunk_rows(d, step)
            lw = lw_ref[rows, :]
            hi = lw.astype(BF16)
            lo = (lw - hi.astype(F32)).astype(BF16)
            tri = tri_ref[d]
            cum = _dot(tri, hi) + _dot(tri, lo)
            lp = cum[0:c]
            e_end = jnp.exp(cum[c:c + SUBLANES])
            e_n = jnp.exp(-lp)
            rq = (r_ref[rows, :] * jnp.exp(lp)).astype(BF16)
            kq = (kk_ref[rows, :] * jnp.exp(lp - lw)).astype(BF16)
            ks = kd_ref[rows, :] * e_n
            bs = kka_ref[rows, :] * e_n
            kq_ref[d, slot] = kq
            rq_ref[d, slot] = rq
            kdec_ref[d, slot] = (ks * e_end[0:1]).astype(BF16)
            bdec_ref[d, slot] = (bs * e_end[0:1]).astype(BF16)
            eend_ref[d, slot] = e_end
            for p in range(n_pairs):
                sl = slice(p * LANES, (p + 1) * LANES)
                ch.append(dict(d=d, sl=sl, kq=kq[:, sl], rq=rq[:, sl], ks=ks[:, sl], bs=bs[:, sl]))
        n = range(len(ch))
        strict = [strict_ref[x["d"]] for x in ch]
        incl = [incl_ref[x["d"]] for x in ch]
        yield
        lq = [jnp.concatenate([x["kq"], x["rq"]], axis=0) for x in ch]
        s_ks = [stack_b(x["ks"]) for x in ch]
        s_bs = [stack_b(x["bs"]) for x in ch]
        gkb = [_dot_nt(lq[i], jnp.concatenate([s_ks[i], s_bs[i]], axis=0)) for i in n]
        yield
        gk = [x[:, 0:LANES] for x in gkb]
        gb = [x[:, LANES:2 * LANES] for x in gkb]
        m = [-(gb[i][0:c] * strict[i]) for i in n]
        for i, x in enumerate(ch):
            avk_ref[x["d"], slot, :, x["sl"]] = (gk[i][0:c] * strict[i]).astype(BF16)
            ark_ref[x["d"], slot, :, x["sl"]] = (gk[i][c:2 * c] * incl[i]).astype(BF16)
            arb_ref[x["d"], slot, :, x["sl"]] = (gb[i][c:2 * c] * incl[i]).astype(BF16)
            m_ref[x["d"], slot, :, x["sl"]] = m[i].astype(BF16)
        t_inv = [eye + x for x in m]
        pw = [_dot(x.astype(BF16), stack_b(x)) for x in m]
        yield
        n_levels = int(math.log2(c)) - 1
        for lvl in range(n_levels):
            pw_bd = [stack_b(x) for x in pw]
            if lvl + 1 < n_levels:
                both = [_dot(jnp.concatenate([t_inv[i].astype(BF16), pw[i].astype(BF16)], axis=0), pw_bd[i])
                        for i in n]
                t_inv = [t_inv[i] + both[i][0:c] for i in n]
                pw = [both[i][c:2 * c] for i in n]
            else:
                t_inv = [t_inv[i] + _dot(t_inv[i].astype(BF16), pw_bd[i]) for i in n]
            yield
        for i, x in enumerate(ch):
            tinv_ref[x["d"], slot, :, x["sl"]] = t_inv[i].astype(BF16)

    def state_stages(step):
        slot = step
        ch = []
        for d, (r_ref, v_ref, kk_ref, lw_ref, kd_ref, kka_ref, y_ref) in enumerate(dir_refs):
            rows = chunk_rows(d, step)
            for p in range(n_pairs):
                ch.append(dict(d=d, p=p, rows=rows, sl=slice(p * LANES, (p + 1) * LANES), y_ref=y_ref, v_ref=v_ref))
        n = range(len(ch))
        get = lambda ref: [ref[x["d"], slot, :, x["sl"]] for x in ch]
        v = [x["v_ref"][x["rows"], x["sl"]].astype(F32) for x in ch]
        s_vb = [stack_b(x) for x in v]
        st = [st_ref[x["d"], x["p"]] for x in ch]
        stb = [x.astype(BF16) for x in st]
        kq_b, rq_b, a_vk, a_rk, a_rb, t_b, m_b = (get(r) for r in (
            kq_ref, rq_ref, avk_ref, ark_ref, arb_ref, tinv_ref, m_ref))
        from_state = [_dot_nt(jnp.concatenate([kq_b[i], rq_b[i]], axis=0), stb[i]) for i in n]
        from_v = [_dot(jnp.concatenate([a_vk[i], a_rk[i]], axis=0), s_vb[i]) for i in n]
        yield
        rhs = [from_state[i][0:c] + from_v[i][0:c] for i in n]
        u0 = [-_dot(t_b[i], stack_b(rhs[i])) for i in n]
        yield
        res = [_dot(m_b[i], stack_b(u0[i])) - rhs[i] - u0[i] for i in n]
        yield
        u = [u0[i] + _dot(t_b[i], stack_b(res[i])) for i in n]
        yield
        y = [from_state[i][c:2 * c] + from_v[i][c:2 * c] + _dot(a_rb[i], stack_b(u[i])) for i in n]
        vu_t = [jnp.concatenate([v[i], u[i]], axis=0).T.astype(BF16) for i in n]
        kb = [jnp.concatenate([kdec_ref[x["d"], slot, :, x["sl"]], bdec_ref[x["d"], slot, :, x["sl"]]], axis=0)
              for x in ch]
        upd = [_dot(vu_t[i], kb[i]) * same_head for i in n]
        for i, x in enumerate(ch):
            x["y_ref"][x["rows"], x["sl"]] = y[i]
            st_ref[x["d"], x["p"]] = st[i] * eend_ref[x["d"], slot, 0:1, x["sl"]] + upd[i]

    def emit(*gens, order=None):
        live = {i: g for i, g in enumerate(gens)}
        for i in list(order or ()) + [None]:
            while i is None and live:
                for k in list(live):
                    if next(live[k], live) is live:
                        del live[k]
            if i in live and next(live[i], live) is live:
                del live[i]

    emit(local_stages(0))

    def body(i, carry):
        emit(local_stages(i + 1), state_stages(i), order=(0, 1, 0, 0, 1, 0, 0, 1, 0, 1, 0, 1, 0))
        return carry

    lax.fori_loop(0, n_chunks - 1, body, 0)
    emit(state_stages(n_chunks - 1))


def _scan(r, v, kk, lw, kd, kka, masks):
    bsz, s, w = r.shape
    tb = min(TB_SCAN, s)
    nblk = s // tb
    fwd = pl.BlockSpec((None, tb, w), lambda b_, j: (b_, j, 0))
    bwd = pl.BlockSpec((None, tb, w), lambda b_, j: (b_, nblk - 1 - j, 0))
    fwd2 = pl.BlockSpec((None, None, tb, w), lambda b_, j: (0, b_, j, 0))
    bwd2 = pl.BlockSpec((None, None, tb, w), lambda b_, j: (1, b_, nblk - 1 - j, 0))
    out = jax.ShapeDtypeStruct((bsz, s, w), F32)
    per_chunk = lambda dt: pltpu.VMEM((2, tb // CHUNK, CHUNK, w), dt)
    return pl.pallas_call(
        _scan_kernel,
        grid=(bsz, nblk),
        in_specs=[fwd, fwd, fwd, bwd, bwd, bwd, fwd2, fwd2, fwd2, bwd2, bwd2, bwd2]
        + [_resident(a.shape) for a in masks],
        out_specs=[fwd, bwd],
        out_shape=[out, out],
        scratch_shapes=[pltpu.VMEM((2, w // LANES, LANES, LANES), F32), per_chunk(BF16), per_chunk(BF16),
                        per_chunk(BF16), per_chunk(BF16), pltpu.VMEM((2, tb // CHUNK, SUBLANES, w), F32)]
        + [per_chunk(BF16)] * 5,
        compiler_params=_cparams("parallel", "arbitrary"),
    )(r, v, kk, r, v, kk, lw, kd, kka, lw, kd, kka, *masks)


def _rope_partner(x):
    n = x.shape[1]
    lane = lax.broadcasted_iota(jnp.int32, x.shape, 1)
    return jnp.where((lane & 31) < 16, pltpu.roll(x, n - 16, 1), pltpu.roll(x, 16, 1))


def _attn_kernel(q_ref, k_ref, v_ref, cq_ref, sq_ref, ck_ref, sk_ref, qg_ref, kg_ref, jq_ref, jk_ref,
                 o_ref, kd_ref, vd_ref):
    s_len = k_ref.shape[0]
    tq = q_ref.shape[0]
    tk = min(TK_PREP, s_len)

    @pl.when(pl.program_id(1) == 0)
    def _():
        def body(i, carry):
            rows = pl.ds(pl.multiple_of(i * tk, tk), tk)
            k = k_ref[rows, :].astype(F32)
            kn = k * lax.rsqrt(_segsum(k * k, jk_ref[...]) * (1.0 / HEAD) + NORM_EPS) * kg_ref[...]
            kr = kn * ck_ref[rows, :] + _rope_partner(kn) * sk_ref[rows, :]
            v = v_ref[rows, :].astype(F32)
            low = lax.broadcasted_iota(jnp.int32, kr.shape, 1) < HEAD
            ksw = pltpu.roll(kr, HEAD, 1)
            kd_ref[0, rows, :] = jnp.where(low, kr, ksw).astype(BF16)
            kd_ref[1, rows, :] = jnp.where(low, ksw, kr).astype(BF16)
            vd_ref[0, rows, :] = jnp.where(low, v, 1.0).astype(BF16)
            vd_ref[1, rows, :] = jnp.where(low, pltpu.roll(v, HEAD, 1), 1.0).astype(BF16)
            return carry
        lax.fori_loop(0, s_len // tk, body, 0)

    q = q_ref[...].astype(F32)
    qn = q * lax.rsqrt(_segsum(q * q, jq_ref[...]) * (1.0 / HEAD) + NORM_EPS) * qg_ref[...]
    qr = qn * cq_ref[...] + _rope_partner(qn) * sq_ref[...]
    low = lax.broadcasted_iota(jnp.int32, (tq, LANES), 1) < HEAD
    pairs_per_kv = AT_Q_HEADS // AT_KV_HEADS // 2
    tkc = min(TK_ATTN, s_len)
    n_kc = s_len // tkc
    qs = []
    for g in range(AT_KV_HEADS):
        parts = []
        for jj in range(pairs_per_kv):
            qp = qr[:, (g * pairs_per_kv + jj) * LANES:(g * pairs_per_kv + jj + 1) * LANES]
            parts += [jnp.where(low, qp, 0.0), jnp.where(low, 0.0, qp)]
        qs.append(jnp.concatenate(parts, axis=0).astype(BF16))
    streams = [(g, kc) for g in range(AT_KV_HEADS) for kc in range(n_kc)]
    scores = lambda g, kc: _dot_nt(qs[g], kd_ref[g, kc * tkc:(kc + 1) * tkc, :])
    partial, s_next = {}, scores(*streams[0])
    for i, (g, kc) in enumerate(streams):
        s = s_next
        if i + 1 < len(streams):
            s_next = scores(*streams[i + 1])
        m = jnp.max(s, axis=-1, keepdims=True)
        p = jnp.exp2(s - m).astype(BF16)
        partial[g, kc] = (m, _dot(p, vd_ref[g, kc * tkc:(kc + 1) * tkc, :]))
    for g in range(AT_KV_HEADS):
        m_all = partial[g, 0][0]
        for kc in range(1, n_kc):
            m_all = jnp.maximum(m_all, partial[g, kc][0])
        o = sum(jnp.exp2(partial[g, kc][0] - m_all) * partial[g, kc][1] for kc in range(n_kc))
        o = o / pltpu.roll(o, HEAD, 1)
        for jj in range(pairs_per_kv):
            even = o[(2 * jj) * tq:(2 * jj + 1) * tq]
            odd = pltpu.roll(o[(2 * jj + 1) * tq:(2 * jj + 2) * tq], HEAD, 1)
            pair = jnp.where(low, even, odd)
            col = (g * pairs_per_kv + jj) * LANES
            o_ref[:, col:col + LANES] = pair.astype(BF16)


def _attn(at, tabs, q_g, k_g, jones):
    bsz, s, _ = at.shape
    tq = min(TQ_ATTN, s)
    cq, sq, ck, sk = tabs
    qw = AT_Q_HEADS * HEAD
    kvw = AT_KV_HEADS * HEAD
    cst = lambda a: pl.BlockSpec(a.shape, lambda b_, i: (0,) * a.ndim)
    jk = jones[0:kvw, 0:kvw]
    return pl.pallas_call(
        _attn_kernel,
        grid=(bsz, s // tq),
        in_specs=[pl.BlockSpec((None, tq, qw), lambda b_, i: (b_, i, 0)),
                  pl.BlockSpec((None, s, kvw), lambda b_, i: (b_, 0, qw // kvw)),
                  pl.BlockSpec((None, s, kvw), lambda b_, i: (b_, 0, qw // kvw + 1)),
                  pl.BlockSpec((tq, qw), lambda b_, i: (i, 0)),
                  pl.BlockSpec((tq, qw), lambda b_, i: (i, 0)),
                  cst(ck), cst(sk), cst(q_g), cst(k_g), cst(jones), cst(jk)],
        out_specs=pl.BlockSpec((None, tq, qw), lambda b_, i: (b_, i, 0)),
        out_shape=jax.ShapeDtypeStruct((bsz, s, qw), BF16),
        scratch_shapes=[pltpu.VMEM((AT_KV_HEADS, s, LANES), BF16), pltpu.VMEM((AT_KV_HEADS, s, LANES), BF16)],
        compiler_params=_cparams("parallel", "arbitrary"),
    )(at, at, at, cq, sq, ck, sk, q_g, k_g, jones, jk)


def _rope_tables(s):
    rows = s // GRID_W
    pos_r = jnp.repeat(jnp.arange(rows), GRID_W).astype(F32)
    pos_c = jnp.tile(jnp.arange(GRID_W), rows).astype(F32)
    half = HEAD // 2
    inv = ROPE_THETA ** (-jnp.arange(0, half, 2, dtype=F32) / half)
    ang_r = pos_r[:, None] * inv
    ang_c = pos_c[:, None] * inv
    cos = jnp.concatenate([jnp.cos(ang_r)] * 2 + [jnp.cos(ang_c)] * 2, axis=-1)
    sin = jnp.concatenate([-jnp.sin(ang_r), jnp.sin(ang_r), -jnp.sin(ang_c), jnp.sin(ang_c)], axis=-1)
    scale = HEAD ** -0.5 * math.log2(math.e)
    return (jnp.tile(cos, (1, AT_Q_HEADS)) * scale, jnp.tile(sin, (1, AT_Q_HEADS)) * scale,
            jnp.tile(cos, (1, AT_KV_HEADS)), jnp.tile(sin, (1, AT_KV_HEADS)))


def _merge_kernel(y0_ref, y1_ref, g_ref, bonus_ref, lng_ref, lnb_ref, j_ref, ycv_ref, yat_ref, sg_ref, x_ref,
                  wb_ref, wo_ref, o_ref):
    j = j_ref[...]
    y = y0_ref[...] + y1_ref[...]
    mu = _segsum(y, j) * (1.0 / HEAD)
    dev = y - mu
    var = _segsum(dev * dev, j) * (1.0 / HEAD)
    yn = dev * lax.rsqrt(var + RW_LN_EPS) * lng_ref[...] + lnb_ref[...]
    y_rw = ((yn + bonus_ref[...]) * g_ref[...]).astype(BF16)
    dm = D_MODEL
    merged = (sg_ref[:, 0:dm].astype(F32) * _dot(y_rw, wb_ref[0])
              + sg_ref[:, dm:2 * dm].astype(F32) * _dot(ycv_ref[...], wb_ref[1])
              + sg_ref[:, 2 * dm:3 * dm].astype(F32) * _dot(yat_ref[...], wb_ref[2]))
    o_ref[...] = x_ref[...] + _dot(merged.astype(BF16), wo_ref[...])


def _merge(yf, yb, g, bonus, ln_g, ln_b, jones, y_cv, y_at, sg, x, w_branch, w_out):
    t = x.shape[0]
    tm = TM_MERGE
    row = lambda n: pl.BlockSpec((tm, n), lambda i: (i, 0))
    return pl.pallas_call(
        _merge_kernel,
        grid=(t // tm,),
        in_specs=[row(RW_WIDTH), row(RW_WIDTH), row(RW_WIDTH), row(RW_WIDTH),
                  _resident(ln_g.shape), _resident(ln_b.shape), _resident(jones.shape),
                  row(CV_WIDTH), row(AT_Q_HEADS * HEAD), row(GATE_COLS), row(D_MODEL),
                  _resident(w_branch.shape), _resident(w_out.shape)],
        out_specs=row(D_MODEL),
        out_shape=jax.ShapeDtypeStruct((t, D_MODEL), F32),
        compiler_params=_cparams("parallel"),
    )(yf, yb, g, bonus, ln_g, ln_b, jones, y_cv, y_at, sg, x, w_branch, w_out)


def _mlp_kernel(x_ref, g_ref, w1_ref, w2_ref, o_ref):
    x = x_ref[...]
    h = (x * lax.rsqrt(jnp.mean(x * x, axis=-1, keepdims=True) + NORM_EPS) * g_ref[...]).astype(BF16)
    acc = x
    for c in range(D_FF // FF_CHUNK):
        cols = slice(c * FF_CHUNK, (c + 1) * FF_CHUNK)
        u = jnp.maximum(_dot(h, w1_ref[:, cols]), 0.0)
        acc = acc + _dot((u * u).astype(BF16), w2_ref[cols, :])
    o_ref[...] = acc


def _mlp(x, g, w1, w2):
    t = x.shape[0]
    tm = TM_MLP
    row = pl.BlockSpec((tm, D_MODEL), lambda i: (i, 0))
    return pl.pallas_call(
        _mlp_kernel,
        grid=(t // tm,),
        in_specs=[row, _resident(g.shape), _resident(w1.shape), _resident(w2.shape)],
        out_specs=row,
        out_shape=jax.ShapeDtypeStruct((t, D_MODEL), F32),
        compiler_params=_cparams("parallel"),
    )(x, g, w1, w2)


def _block_diag2(w):
    z = jnp.zeros_like(w[0])
    return jnp.concatenate([jnp.concatenate([w[0], z], axis=1), jnp.concatenate([z, w[1]], axis=1)], axis=0)


def kernel(x, norm1_g, w_in, rw_mu_prev, rw_mu_next, rw_w0, rw_w2, rw_a0, rw_a2, rw_g2, rw_k_k, rw_k_a, rw_r_k,
           rw_ln_g, rw_ln_b, cv_w, cv_b, at_q_g, at_k_g, w_branch, w_out, norm2_g, w_mlp1, w_mlp2):
    bsz, s, dm = x.shape
    assert dm == D_MODEL and s % GRID_W == 0
    for tile in (TM_INPROJ, TB_SCAN, TQ_ATTN, TK_PREP):
        assert s % min(tile, s) == 0 and min(tile, s) % CHUNK == 0
    t = bsz * s
    tabs = _rope_tables(s)
    masks = _scan_masks()
    jones = jnp.asarray(np.kron(np.eye(RW_WIDTH // HEAD, dtype=np.float32), np.ones((HEAD, HEAD), np.float32)), BF16)
    row = lambda a: a.reshape(1, -1)
    xf = x.reshape(t, dm)
    for l in range(DEPTH):
        r, v, kk, g, bonus, lw, kd, kka, y_cv, at, sg = _inproj(
            xf.reshape(bsz, s, dm), row(norm1_g[l]), w_in[l].astype(BF16), row(rw_mu_prev[l]), row(rw_mu_next[l]),
            row(rw_w0[l]), _block_diag2(rw_w2[l]).astype(BF16), row(rw_a0[l]), _block_diag2(rw_a2[l]).astype(BF16),
            rw_g2[l].astype(BF16), row(rw_k_k[l]), row(rw_k_a[l]), row(rw_r_k[l]), jones, cv_w[l], row(cv_b[l]))
        yf, yb = _scan(r, v, kk, lw, kd, kka, masks)
        y_at = _attn(at, tabs, row(jnp.tile(at_q_g[l], AT_Q_HEADS)),
                     row(jnp.tile(at_k_g[l], AT_KV_HEADS)), jones)
        xf = _merge(yf.reshape(t, RW_WIDTH), yb.reshape(t, RW_WIDTH), g.reshape(t, RW_WIDTH),
                    bonus.reshape(t, RW_WIDTH),
                    row(rw_ln_g[l]), row(rw_ln_b[l]), jones, y_cv.reshape(t, CV_WIDTH),
                    y_at.reshape(t, AT_Q_HEADS * HEAD), sg.reshape(t, GATE_COLS), xf,
                    w_branch[l].astype(BF16), w_out[l].astype(BF16))
        xf = _mlp(xf, row(norm2_g[l]), w_mlp1[l].astype(BF16), w_mlp2[l].astype(BF16))
    return xf.reshape(bsz, s, dm)
```

```python
import math

import numpy as np
import jax
import jax.numpy as jnp
from jax import lax
from jax.experimental import pallas as pl
from jax.experimental.pallas import tpu as pltpu

F32 = jnp.float32
BF16 = jnp.bfloat16

D_MODEL = 1024
DEPTH = 4
GRID_W = 64
NORM_EPS = 1e-6
HEAD = 64
RW_WIDTH = 512
RW_LORA = 128
RW_LN_EPS = 64e-5
RW_COLS = 3 * RW_WIDTH + 3 * RW_LORA
CV_WIDTH = 512
CV_COLS = 3 * CV_WIDTH
AT_Q_HEADS = 8
AT_KV_HEADS = 2
AT_COLS = (AT_Q_HEADS + 2 * AT_KV_HEADS) * HEAD
GATE_COLS = 3 * D_MODEL
ROPE_THETA = 10000.0
D_FF = 4 * D_MODEL

LANES = 128
SUBLANES = 8
VMEM_LIMIT_BYTES = 56 * 1024 * 1024
CHUNK = 64
TM_INPROJ = 256
TB_SCAN = 512
TQ_ATTN = 256
TK_ATTN = 2048
TK_PREP = 512
TM_MERGE = 512
TM_MLP = 512
FF_CHUNK = 1024


def _cparams(*sem):
    return pltpu.CompilerParams(dimension_semantics=sem, vmem_limit_bytes=VMEM_LIMIT_BYTES)


def _dot(a, b):
    return jnp.dot(a, b, preferred_element_type=F32)


def _dot_nt(a, b):
    return lax.dot_general(a, b, (((1,), (1,)), ((), ())), preferred_element_type=F32)


def _segsum(x, j):
    return _dot(x.astype(BF16), j)


def _segsum_split(x, j):
    hi = x.astype(BF16)
    return _dot(hi, j) + _dot((x - hi.astype(F32)).astype(BF16), j)


def _sigmoid(x):
    return 1.0 / (1.0 + jnp.exp(-x))


def _resident(shape):
    nd = len(shape)
    return pl.BlockSpec(shape, lambda *_: (0,) * nd, pipeline_mode=pl.Buffered(1))


def _shift_rows(c, halo_prev, halo_next, first, last):
    n = c.shape[0]
    row = lax.broadcasted_iota(jnp.int32, c.shape, 0)
    lane_row = lax.broadcasted_iota(jnp.int32, (1, c.shape[1]), 1)
    is_first = (lane_row * 0 + first.astype(jnp.int32)) > 0
    is_last = (lane_row * 0 + last.astype(jnp.int32)) > 0
    prev_row = jnp.where(is_first, 0.0, halo_prev[halo_prev.shape[0] - 1:, :])
    next_row = jnp.where(is_last, 0.0, halo_next[0:1, :])
    prev = jnp.where(row == 0, prev_row, pltpu.roll(c, 1, 0))
    nxt = jnp.where(row == n - 1, next_row, pltpu.roll(c, n - 1, 0))
    return prev, nxt


def _halo_specs(ts, width, n_seq_rows, rows):
    r = ts // rows
    last_blk = n_seq_rows // rows - 1
    prev = pl.BlockSpec((None, rows, width), lambda b, i: (b, jnp.maximum(i * r - 1, 0), 0))
    nxt = pl.BlockSpec((None, rows, width), lambda b, i: (b, jnp.minimum((i + 1) * r, last_blk), 0))
    return prev, nxt


def _inproj_kernel(x_ref, xp_ref, xn_ref, g_ref, w_ref, mup_ref, mun_ref, w0_ref, w2_ref, a0_ref, a2_ref, g2_ref,
                   kk_w_ref, ka_ref, rk_ref, j_ref, cvw_ref, cvb_ref,
                   r_ref, v_ref, kk_ref, gate_ref, bonus_ref, lw_ref, kd_ref, kka_ref, ycv_ref, at_ref, sg_ref):
    i = pl.program_id(1)

    def normed(x):
        return (x * lax.rsqrt(jnp.mean(x * x, axis=-1, keepdims=True) + NORM_EPS) * g_ref[...]).astype(BF16)

    h = normed(x_ref[...])
    o0, o1, o2 = RW_COLS, RW_COLS + CV_COLS, RW_COLS + CV_COLS + AT_COLS
    dm = D_MODEL

    def gate_cols(n):
        sg_ref[:, n * dm:(n + 1) * dm] = _sigmoid(_dot(h, w_ref[:, o2 + n * dm:o2 + (n + 1) * dm])).astype(BF16)

    tm = x_ref.shape[0]
    h_ext = jnp.concatenate([h, normed(jnp.concatenate([xp_ref[...], xn_ref[...]], axis=0))], axis=0)
    c_ext = _dot(h_ext, w_ref[:, 0:o0])
    c, c_halo = c_ext[0:tm], c_ext[tm:tm + 2 * SUBLANES]
    first, last = i == 0, i == pl.num_programs(1) - 1
    cw = CV_WIDTH
    cv_ext = _dot(h_ext, w_ref[:, o0:o1])
    cv, cv_halo = cv_ext[0:tm], cv_ext[tm:tm + 2 * SUBLANES]
    zc = cv[:, cw:2 * cw] * cv[:, 2 * cw:3 * cw]
    zc_halo = cv_halo[:, cw:2 * cw] * cv_halo[:, 2 * cw:3 * cw]
    zc_prev, zc_next = _shift_rows(zc, zc_halo[0:SUBLANES], zc_halo[SUBLANES:2 * SUBLANES], first, last)
    conv = cvb_ref[...] + cvw_ref[0:1, :] * zc_prev + cvw_ref[1:2, :] * zc + cvw_ref[2:3, :] * zc_next
    ycv_ref[...] = (cv[:, 0:cw] * conv).astype(BF16)
    prev, nxt = _shift_rows(c, c_halo[0:SUBLANES], c_halo[SUBLANES:2 * SUBLANES], first, last)
    z = c + mup_ref[...] * (prev - c) + mun_ref[...] * (nxt - c)
    w = RW_WIDTH
    r, k, v = z[:, 0:w], z[:, w:2 * w], z[:, 2 * w:3 * w]
    wd = z[:, 3 * w:3 * w + RW_LORA]
    ad = z[:, 3 * w + RW_LORA:3 * w + 2 * RW_LORA]
    gd = z[:, 3 * w + 2 * RW_LORA:3 * w + 3 * RW_LORA]
    j = j_ref[...]
    w_raw = w0_ref[...] + _dot(jnp.tanh(wd).astype(BF16), w2_ref[...])
    lw = -math.exp(-0.5) * _sigmoid(w_raw)
    a = _sigmoid(a0_ref[...] + _dot(ad.astype(BF16), a2_ref[...]))
    gate_ref[...] = _dot(_sigmoid(gd).astype(BF16), g2_ref[...]).astype(BF16)
    gate_cols(0)
    kkr = k * kk_w_ref[...]
    kk = kkr / jnp.maximum(jnp.sqrt(_segsum_split(kkr * kkr, j)), 1e-12)
    r_ref[...] = r.astype(BF16)
    v_ref[...] = v.astype(BF16)
    kk_ref[...] = kk.astype(BF16)
    at_ref[...] = _dot(h, w_ref[:, o1:o2]).astype(BF16)
    gate_cols(1)
    ksum = jnp.zeros_like(k)
    for d in range(2):
        a_d = a[:, d * w:(d + 1) * w]
        k_d = k * (1.0 + (a_d - 1.0) * ka_ref[...])
        ksum = ksum + k_d
        lw_ref[d] = lw[:, d * w:(d + 1) * w]
        kd_ref[d] = k_d.astype(BF16)
        kka_ref[d] = (kk * a_d).astype(BF16)
    gate_cols(2)
    bonus_ref[...] = (_segsum_split(r * (0.5 * ksum) * rk_ref[...], j) * v).astype(BF16)


def _inproj(x, g, w, mup, mun, w0, w2bd, a0, a2bd, g2, k_k, k_a, r_k, jones, cv_w, cv_b):
    bsz, s, _ = x.shape
    tm = min(TM_INPROJ, s)
    hp, hn = _halo_specs(tm, D_MODEL, s, SUBLANES)
    tok = lambda n: pl.BlockSpec((None, tm, n), lambda b_, i: (b_, i, 0))
    tok2 = pl.BlockSpec((2, None, tm, RW_WIDTH), lambda b_, i: (0, b_, i, 0))
    consts = (g, w, mup, mun, w0, w2bd, a0, a2bd, g2, k_k, k_a, r_k, jones, cv_w, cv_b)
    one = lambda n: jax.ShapeDtypeStruct((bsz, s, n), BF16)
    two = lambda dt: jax.ShapeDtypeStruct((2, bsz, s, RW_WIDTH), dt)
    return pl.pallas_call(
        _inproj_kernel,
        grid=(bsz, s // tm),
        in_specs=[tok(D_MODEL), hp, hn] + [_resident(a.shape) for a in consts],
        out_specs=[tok(RW_WIDTH)] * 5 + [tok2] * 3 + [tok(CV_WIDTH), tok(AT_COLS), tok(GATE_COLS)],
        out_shape=[one(RW_WIDTH)] * 5 + [two(F32), two(BF16), two(BF16)]
        + [one(CV_WIDTH), one(AT_COLS), one(GATE_COLS)],
        compiler_params=_cparams("parallel", "arbitrary"),
    )(x, x, x, *consts)


def _scan_masks():
    c = CHUNK
    t = np.arange(c)
    lower = (t[None, :] <= t[:, None]).astype(np.float32)
    strict_l = (t[None, :] < t[:, None]).astype(np.float32)
    ones = np.ones((2 * SUBLANES, c), np.float32)
    tri = np.stack([np.concatenate([lower, ones], 0), np.concatenate([lower.T, ones], 0)])
    strict = np.stack([np.tile(strict_l, (1, 2)), np.tile(strict_l.T, (1, 2))])
    incl = np.stack([np.tile(lower, (1, 2)), np.tile(lower.T, (1, 2))])
    return jnp.asarray(tri, BF16), jnp.asarray(strict, F32), jnp.asarray(incl, F32)


def _scan_kernel(rf_ref, vf_ref, kkf_ref, rb_ref, vb_ref, kkb_ref,
                 lwf_ref, kdf_ref, kkaf_ref, lwb_ref, kdb_ref, kkab_ref,
                 tri_ref, strict_ref, incl_ref, yf_ref, yb_ref, st_ref,
                 kq_ref, rq_ref, kdec_ref, bdec_ref, eend_ref, avk_ref, ark_ref, arb_ref, m_ref, tinv_ref):
    c = CHUNK
    n_chunks = rf_ref.shape[0] // c
    n_pairs = RW_WIDTH // LANES

    @pl.when(pl.program_id(1) == 0)
    def _():
        st_ref[...] = jnp.zeros_like(st_ref)

    lane = lax.broadcasted_iota(jnp.int32, (c, LANES), 1)
    row = lax.broadcasted_iota(jnp.int32, (c, LANES), 0)
    head0 = lane < HEAD
    eye = jnp.where((lane & (HEAD - 1)) == row, 1.0, 0.0).astype(F32)
    ri = lax.broadcasted_iota(jnp.int32, (LANES, LANES), 0)
    ci = lax.broadcasted_iota(jnp.int32, (LANES, LANES), 1)
    same_head = jnp.where((ri < HEAD) == (ci < HEAD), 1.0, 0.0).astype(F32)

    def stack(x):
        return jnp.concatenate([jnp.where(head0, x, 0.0), jnp.where(head0, 0.0, x)], axis=0)

    def stack_b(x):
        return stack(x).astype(BF16)

    dir_refs = ((rf_ref, vf_ref, kkf_ref, lwf_ref, kdf_ref, kkaf_ref, yf_ref),
                (rb_ref, vb_ref, kkb_ref, lwb_ref, kdb_ref, kkab_ref, yb_ref))

    def chunk_rows(d, step):
        cidx = step if d == 0 else n_chunks - 1 - step
        start = cidx * c
        return pl.ds(start if isinstance(start, int) else pl.multiple_of(start, c), c)

    def local_stages(step):
        slot = step
        ch = []
        for d, (r_ref, v_ref, kk_ref, lw_ref, kd_ref, kka_ref, y_ref) in enumerate(dir_refs):
            rows = chunk_rows(d, step)
            lw = lw_ref[rows, :]
            hi = lw.astype(BF16)
            lo = (lw - hi.astype(F32)).astype(BF16)
            tri = tri_ref[d]
            cum = _dot(tri, hi) + _dot(tri, lo)
            lp = cum[0:c]
            e_end = jnp.exp(cum[c:c + SUBLANES])
            e_n = jnp.exp(-lp)
            rq = (r_ref[rows, :] * jnp.exp(lp)).astype(BF16)
            kq = (kk_ref[rows, :] * jnp.exp(lp - lw)).astype(BF16)
            ks = kd_ref[rows, :] * e_n
            bs = kka_ref[rows, :] * e_n
            kq_ref[d, slot] = kq
            rq_ref[d, slot] = rq
            kdec_ref[d, slot] = (ks * e_end[0:1]).astype(BF16)
            bdec_ref[d, slot] = (bs * e_end[0:1]).astype(BF16)
            eend_ref[d, slot] = e_end
            for p in range(n_pairs):
                sl = slice(p * LANES, (p + 1) * LANES)
                ch.append(dict(d=d, sl=sl, kq=kq[:, sl], rq=rq[:, sl], ks=ks[:, sl], bs=bs[:, sl]))
        n = range(len(ch))
        strict = [strict_ref[x["d"]] for x in ch]
        incl = [incl_ref[x["d"]] for x in ch]
        yield
        lq = [jnp.concatenate([x["kq"], x["rq"]], axis=0) for x in ch]
        s_ks = [stack_b(x["ks"]) for x in ch]
        s_bs = [stack_b(x["bs"]) for x in ch]
        gkb = [_dot_nt(lq[i], jnp.concatenate([s_ks[i], s_bs[i]], axis=0)) for i in n]
        yield
        gk = [x[:, 0:LANES] for x in gkb]
        gb = [x[:, LANES:2 * LANES] for x in gkb]
        m = [-(gb[i][0:c] * strict[i]) for i in n]
        for i, x in enumerate(ch):
            avk_ref[x["d"], slot, :, x["sl"]] = (gk[i][0:c] * strict[i]).astype(BF16)
            ark_ref[x["d"], slot, :, x["sl"]] = (gk[i][c:2 * c] * incl[i]).astype(BF16)
            arb_ref[x["d"], slot, :, x["sl"]] = (gb[i][c:2 * c] * incl[i]).astype(BF16)
            m_ref[x["d"], slot, :, x["sl"]] = m[i].astype(BF16)
        t_inv = [eye + x for x in m]
        pw = [_dot(x.astype(BF16), stack_b(x)) for x in m]
        yield
        n_levels = int(math.log2(c)) - 1
        for lvl in range(n_levels):
            pw_bd = [stack_b(x) for x in pw]
            if lvl + 1 < n_levels:
                both = [_dot(jnp.concatenate([t_inv[i].astype(BF16), pw[i].astype(BF16)], axis=0), pw_bd[i])
                        for i in n]
                t_inv = [t_inv[i] + both[i][0:c] for i in n]
                pw = [both[i][c:2 * c] for i in n]
            else:
                t_inv = [t_inv[i] + _dot(t_inv[i].astype(BF16), pw_bd[i]) for i in n]
            yield
        for i, x in enumerate(ch):
            tinv_ref[x["d"], slot, :, x["sl"]] = t_inv[i].astype(BF16)

    def state_stages(step):
        slot = step
        ch = []
        for d, (r_ref, v_ref, kk_ref, lw_ref, kd_ref, kka_ref, y_ref) in enumerate(dir_refs):
            rows = chunk_rows(d, step)
            for p in range(n_pairs):
                ch.append(dict(d=d, p=p, rows=rows, sl=slice(p * LANES, (p + 1) * LANES), y_ref=y_ref, v_ref=v_ref))
        n = range(len(ch))
        get = lambda ref: [ref[x["d"], slot, :, x["sl"]] for x in ch]
        v = [x["v_ref"][x["rows"], x["sl"]].astype(F32) for x in ch]
        s_vb = [stack_b(x) for x in v]
        st = [st_ref[x["d"], x["p"]] for x in ch]
        stb = [x.astype(BF16) for x in st]
        kq_b, rq_b, a_vk, a_rk, a_rb, t_b, m_b = (get(r) for r in (
            kq_ref, rq_ref, avk_ref, ark_ref, arb_ref, tinv_ref, m_ref))
        from_state = [_dot_nt(jnp.concatenate([kq_b[i], rq_b[i]], axis=0), stb[i]) for i in n]
        from_v = [_dot(jnp.concatenate([a_vk[i], a_rk[i]], axis=0), s_vb[i]) for i in n]
        yield
        rhs = [from_state[i][0:c] + from_v[i][0:c] for i in n]
        u0 = [-_dot(t_b[i], stack_b(rhs[i])) for i in n]
        yield
        res = [_dot(m_b[i], stack_b(u0[i])) - rhs[i] - u0[i] for i in n]
        yield
        u = [u0[i] + _dot(t_b[i], stack_b(res[i])) for i in n]
        yield
        y = [from_state[i][c:2 * c] + from_v[i][c:2 * c] + _dot(a_rb[i], stack_b(u[i])) for i in n]
        vu_t = [jnp.concatenate([v[i], u[i]], axis=0).T.astype(BF16) for i in n]
        kb = [jnp.concatenate([kdec_ref[x["d"], slot, :, x["sl"]], bdec_ref[x["d"], slot, :, x["sl"]]], axis=0)
              for x in ch]
        upd = [_dot(vu_t[i], kb[i]) * same_head for i in n]
        for i, x in enumerate(ch):
            x["y_ref"][x["rows"], x["sl"]] = y[i]
            st_ref[x["d"], x["p"]] = st[i] * eend_ref[x["d"], slot, 0:1, x["sl"]] + upd[i]

    def emit(*gens, order=None):
        live = {i: g for i, g in enumerate(gens)}
        for i in list(order or ()) + [None]:
            while i is None and live:
                for k in list(live):
                    if next(live[k], live) is live:
                        del live[k]
            if i in live and next(live[i], live) is live:
                del live[i]

    emit(local_stages(0))

    def body(i, carry):
        emit(local_stages(i + 1), state_stages(i), order=(0, 1, 0, 0, 1, 0, 0, 1, 0, 1, 0, 1, 0))
        return carry

    lax.fori_loop(0, n_chunks - 1, body, 0)
    emit(state_stages(n_chunks - 1))


def _scan(r, v, kk, lw, kd, kka, masks):
    bsz, s, w = r.shape
    tb = min(TB_SCAN, s)
    nblk = s // tb
    fwd = pl.BlockSpec((None, tb, w), lambda b_, j: (b_, j, 0))
    bwd = pl.BlockSpec((None, tb, w), lambda b_, j: (b_, nblk - 1 - j, 0))
    fwd2 = pl.BlockSpec((None, None, tb, w), lambda b_, j: (0, b_, j, 0))
    bwd2 = pl.BlockSpec((None, None, tb, w), lambda b_, j: (1, b_, nblk - 1 - j, 0))
    out = jax.ShapeDtypeStruct((bsz, s, w), F32)
    per_chunk = lambda dt: pltpu.VMEM((2, tb // CHUNK, CHUNK, w), dt)
    return pl.pallas_call(
        _scan_kernel,
        grid=(bsz, nblk),
        in_specs=[fwd, fwd, fwd, bwd, bwd, bwd, fwd2, fwd2, fwd2, bwd2, bwd2, bwd2]
        + [_resident(a.shape) for a in masks],
        out_specs=[fwd, bwd],
        out_shape=[out, out],
        scratch_shapes=[pltpu.VMEM((2, w // LANES, LANES, LANES), F32), per_chunk(BF16), per_chunk(BF16),
                        per_chunk(BF16), per_chunk(BF16), pltpu.VMEM((2, tb // CHUNK, SUBLANES, w), F32)]
        + [per_chunk(BF16)] * 5,
        compiler_params=_cparams("parallel", "arbitrary"),
    )(r, v, kk, r, v, kk, lw, kd, kka, lw, kd, kka, *masks)


def _rope_partner(x):
    n = x.shape[1]
    lane = lax.broadcasted_iota(jnp.int32, x.shape, 1)
    return jnp.where((lane & 31) < 16, pltpu.roll(x, n - 16, 1), pltpu.roll(x, 16, 1))


def _attn_kernel(q_ref, k_ref, v_ref, cq_ref, sq_ref, ck_ref, sk_ref, qg_ref, kg_ref, jq_ref, jk_ref,
                 o_ref, kd_ref, vd_ref):
    s_len = k_ref.shape[0]
    tq = q_ref.shape[0]
    tk = min(TK_PREP, s_len)

    @pl.when(pl.program_id(1) == 0)
    def _():
        def body(i, carry):
            rows = pl.ds(pl.multiple_of(i * tk, tk), tk)
            k = k_ref[rows, :].astype(F32)
            kn = k * lax.rsqrt(_segsum(k * k, jk_ref[...]) * (1.0 / HEAD) + NORM_EPS) * kg_ref[...]
            kr = kn * ck_ref[rows, :] + _rope_partner(kn) * sk_ref[rows, :]
            v = v_ref[rows, :].astype(F32)
            low = lax.broadcasted_iota(jnp.int32, kr.shape, 1) < HEAD
            ksw = pltpu.roll(kr, HEAD, 1)
            kd_ref[0, rows, :] = jnp.where(low, kr, ksw).astype(BF16)
            kd_ref[1, rows, :] = jnp.where(low, ksw, kr).astype(BF16)
            vd_ref[0, rows, :] = jnp.where(low, v, 1.0).astype(BF16)
            vd_ref[1, rows, :] = jnp.where(low, pltpu.roll(v, HEAD, 1), 1.0).astype(BF16)
            return carry
        lax.fori_loop(0, s_len // tk, body, 0)

    q = q_ref[...].astype(F32)
    qn = q * lax.rsqrt(_segsum(q * q, jq_ref[...]) * (1.0 / HEAD) + NORM_EPS) * qg_ref[...]
    qr = qn * cq_ref[...] + _rope_partner(qn) * sq_ref[...]
    low = lax.broadcasted_iota(jnp.int32, (tq, LANES), 1) < HEAD
    pairs_per_kv = AT_Q_HEADS // AT_KV_HEADS // 2
    tkc = min(TK_ATTN, s_len)
    n_kc = s_len // tkc
    qs = []
    for g in range(AT_KV_HEADS):
        parts = []
        for jj in range(pairs_per_kv):
            qp = qr[:, (g * pairs_per_kv + jj) * LANES:(g * pairs_per_kv + jj + 1) * LANES]
            parts += [jnp.where(low, qp, 0.0), jnp.where(low, 0.0, qp)]
        qs.append(jnp.concatenate(parts, axis=0).astype(BF16))
    streams = [(g, kc) for g in range(AT_KV_HEADS) for kc in range(n_kc)]
    scores = lambda g, kc: _dot_nt(qs[g], kd_ref[g, kc * tkc:(kc + 1) * tkc, :])
    partial, s_next = {}, scores(*streams[0])
    for i, (g, kc) in enumerate(streams):
        s = s_next
        if i + 1 < len(streams):
            s_next = scores(*streams[i + 1])
        m = jnp.max(s, axis=-1, keepdims=True)
        p = jnp.exp2(s - m).astype(BF16)
        partial[g, kc] = (m, _dot(p, vd_ref[g, kc * tkc:(kc + 1) * tkc, :]))
    for g in range(AT_KV_HEADS):
        m_all = partial[g, 0][0]
        for kc in range(1, n_kc):
            m_all = jnp.maximum(m_all, partial[g, kc][0])
        o = sum(jnp.exp2(partial[g, kc][0] - m_all) * partial[g, kc][1] for kc in range(n_kc))
        o = o / pltpu.roll(o, HEAD, 1)
        for jj in range(pairs_per_kv):
            even = o[(2 * jj) * tq:(2 * jj + 1) * tq]
            odd = pltpu.roll(o[(2 * jj + 1) * tq:(2 * jj + 2) * tq], HEAD, 1)
            pair = jnp.where(low, even, odd)
            col = (g * pairs_per_kv + jj) * LANES
            o_ref[:, col:col + LANES] = pair.astype(BF16)


def _attn(at, tabs, q_g, k_g, jones):
    bsz, s, _ = at.shape
    tq = min(TQ_ATTN, s)
    cq, sq, ck, sk = tabs
    qw = AT_Q_HEADS * HEAD
    kvw = AT_KV_HEADS * HEAD
    cst = lambda a: pl.BlockSpec(a.shape, lambda b_, i: (0,) * a.ndim)
    jk = jones[0:kvw, 0:kvw]
    return pl.pallas_call(
        _attn_kernel,
        grid=(bsz, s // tq),
        in_specs=[pl.BlockSpec((None, tq, qw), lambda b_, i: (b_, i, 0)),
                  pl.BlockSpec((None, s, kvw), lambda b_, i: (b_, 0, qw // kvw)),
                  pl.BlockSpec((None, s, kvw), lambda b_, i: (b_, 0, qw // kvw + 1)),
                  pl.BlockSpec((tq, qw), lambda b_, i: (i, 0)),
                  pl.BlockSpec((tq, qw), lambda b_, i: (i, 0)),
                  cst(ck), cst(sk), cst(q_g), cst(k_g), cst(jones), cst(jk)],
        out_specs=pl.BlockSpec((None, tq, qw), lambda b_, i: (b_, i, 0)),
        out_shape=jax.ShapeDtypeStruct((bsz, s, qw), BF16),
        scratch_shapes=[pltpu.VMEM((AT_KV_HEADS, s, LANES), BF16), pltpu.VMEM((AT_KV_HEADS, s, LANES), BF16)],
        compiler_params=_cparams("parallel", "arbitrary"),
    )(at, at, at, cq, sq, ck, sk, q_g, k_g, jones, jk)


def _rope_tables(s):
    rows = s // GRID_W
    pos_r = jnp.repeat(jnp.arange(rows), GRID_W).astype(F32)
    pos_c = jnp.tile(jnp.arange(GRID_W), rows).astype(F32)
    half = HEAD // 2
    inv = ROPE_THETA ** (-jnp.arange(0, half, 2, dtype=F32) / half)
    ang_r = pos_r[:, None] * inv
    ang_c = pos_c[:, None] * inv
    cos = jnp.concatenate([jnp.cos(ang_r)] * 2 + [jnp.cos(ang_c)] * 2, axis=-1)
    sin = jnp.concatenate([-jnp.sin(ang_r), jnp.sin(ang_r), -jnp.sin(ang_c), jnp.sin(ang_c)], axis=-1)
    scale = HEAD ** -0.5 * math.log2(math.e)
    return (jnp.tile(cos, (1, AT_Q_HEADS)) * scale, jnp.tile(sin, (1, AT_Q_HEADS)) * scale,
            jnp.tile(cos, (1, AT_KV_HEADS)), jnp.tile(sin, (1, AT_KV_HEADS)))


def _merge_kernel(y0_ref, y1_ref, g_ref, bonus_ref, lng_ref, lnb_ref, j_ref, ycv_ref, yat_ref, sg_ref, x_ref,
                  wb_ref, wo_ref, o_ref):
    j = j_ref[...]
    y = y0_ref[...] + y1_ref[...]
    mu = _segsum(y, j) * (1.0 / HEAD)
    dev = y - mu
    var = _segsum(dev * dev, j) * (1.0 / HEAD)
    yn = dev * lax.rsqrt(var + RW_LN_EPS) * lng_ref[...] + lnb_ref[...]
    y_rw = ((yn + bonus_ref[...]) * g_ref[...]).astype(BF16)
    dm = D_MODEL
    merged = (sg_ref[:, 0:dm].astype(F32) * _dot(y_rw, wb_ref[0])
              + sg_ref[:, dm:2 * dm].astype(F32) * _dot(ycv_ref[...], wb_ref[1])
              + sg_ref[:, 2 * dm:3 * dm].astype(F32) * _dot(yat_ref[...], wb_ref[2]))
    o_ref[...] = x_ref[...] + _dot(merged.astype(BF16), wo_ref[...])


def _merge(yf, yb, g, bonus, ln_g, ln_b, jones, y_cv, y_at, sg, x, w_branch, w_out):
    t = x.shape[0]
    tm = TM_MERGE
    row = lambda n: pl.BlockSpec((tm, n), lambda i: (i, 0))
    return pl.pallas_call(
        _merge_kernel,
        grid=(t // tm,),
        in_specs=[row(RW_WIDTH), row(RW_WIDTH), row(RW_WIDTH), row(RW_WIDTH),
                  _resident(ln_g.shape), _resident(ln_b.shape), _resident(jones.shape),
                  row(CV_WIDTH), row(AT_Q_HEADS * HEAD), row(GATE_COLS), row(D_MODEL),
                  _resident(w_branch.shape), _resident(w_out.shape)],
        out_specs=row(D_MODEL),
        out_shape=jax.ShapeDtypeStruct((t, D_MODEL), F32),
        compiler_params=_cparams("parallel"),
    )(yf, yb, g, bonus, ln_g, ln_b, jones, y_cv, y_at, sg, x, w_branch, w_out)


def _mlp_kernel(x_ref, g_ref, w1_ref, w2_ref, o_ref):
    x = x_ref[...]
    h = (x * lax.rsqrt(jnp.mean(x * x, axis=-1, keepdims=True) + NORM_EPS) * g_ref[...]).astype(BF16)
    acc = x
    for c in range(D_FF // FF_CHUNK):
        cols = slice(c * FF_CHUNK, (c + 1) * FF_CHUNK)
        u = jnp.maximum(_dot(h, w1_ref[:, cols]), 0.0)
        acc = acc + _dot((u * u).astype(BF16), w2_ref[cols, :])
    o_ref[...] = acc


def _mlp(x, g, w1, w2):
    t = x.shape[0]
    tm = TM_MLP
    row = pl.BlockSpec((tm, D_MODEL), lambda i: (i, 0))
    return pl.pallas_call(
        _mlp_kernel,
        grid=(t // tm,),
        in_specs=[row, _resident(g.shape), _resident(w1.shape), _resident(w2.shape)],
        out_specs=row,
        out_shape=jax.ShapeDtypeStruct((t, D_MODEL), F32),
        compiler_params=_cparams("parallel"),
    )(x, g, w1, w2)


def _block_diag2(w):
    z = jnp.zeros_like(w[0])
    return jnp.concatenate([jnp.concatenate([w[0], z], axis=1), jnp.concatenate([z, w[1]], axis=1)], axis=0)


def kernel(x, norm1_g, w_in, rw_mu_prev, rw_mu_next, rw_w0, rw_w2, rw_a0, rw_a2, rw_g2, rw_k_k, rw_k_a, rw_r_k,
           rw_ln_g, rw_ln_b, cv_w, cv_b, at_q_g, at_k_g, w_branch, w_out, norm2_g, w_mlp1, w_mlp2):
    bsz, s, dm = x.shape
    assert dm == D_MODEL and s % GRID_W == 0
    for tile in (TM_INPROJ, TB_SCAN, TQ_ATTN, TK_PREP):
        assert s % min(tile, s) == 0 and min(tile, s) % CHUNK == 0
    t = bsz * s
    tabs = _rope_tables(s)
    masks = _scan_masks()
    jones = jnp.asarray(np.kron(np.eye(RW_WIDTH // HEAD, dtype=np.float32), np.ones((HEAD, HEAD), np.float32)), BF16)
    row = lambda a: a.reshape(1, -1)
    xf = x.reshape(t, dm)
    for l in range(DEPTH):
        r, v, kk, g, bonus, lw, kd, kka, y_cv, at, sg = _inproj(
            xf.reshape(bsz, s, dm), row(norm1_g[l]), w_in[l].astype(BF16), row(rw_mu_prev[l]), row(rw_mu_next[l]),
            row(rw_w0[l]), _block_diag2(rw_w2[l]).astype(BF16), row(rw_a0[l]), _block_diag2(rw_a2[l]).astype(BF16),
            rw_g2[l].astype(BF16), row(rw_k_k[l]), row(rw_k_a[l]), row(rw_r_k[l]), jones, cv_w[l], row(cv_b[l]))
        yf, yb = _scan(r, v, kk, lw, kd, kka, masks)
        y_at = _attn(at, tabs, row(jnp.tile(at_q_g[l], AT_Q_HEADS)),
                     row(jnp.tile(at_k_g[l], AT_KV_HEADS)), jones)
        xf = _merge(yf.reshape(t, RW_WIDTH), yb.reshape(t, RW_WIDTH), g.reshape(t, RW_WIDTH),
                    bonus.reshape(t, RW_WIDTH),
                    row(rw_ln_g[l]), row(rw_ln_b[l]), jones, y_cv.reshape(t, CV_WIDTH),
                    y_at.reshape(t, AT_Q_HEADS * HEAD), sg.reshape(t, GATE_COLS), xf,
                    w_branch[l].astype(BF16), w_out[l].astype(BF16))
        xf = _mlp(xf, row(norm2_g[l]), w_mlp1[l].astype(BF16), w_mlp2[l].astype(BF16))
    return xf.reshape(bsz, s, dm)
```

```python
import math

import numpy as np
import jax
import jax.numpy as jnp
from jax import lax
from jax.experimental import pallas as pl
from jax.experimental.pallas import tpu as pltpu

F32 = jnp.float32
BF16 = jnp.bfloat16

D_MODEL = 1024
DEPTH = 4
GRID_W = 64
NORM_EPS = 1e-6
HEAD = 64
RW_WIDTH = 512
RW_LORA = 128
RW_LN_EPS = 64e-5
RW_COLS = 3 * RW_WIDTH + 3 * RW_LORA
CV_WIDTH = 512
CV_COLS = 3 * CV_WIDTH
AT_Q_HEADS = 8
AT_KV_HEADS = 2
AT_COLS = (AT_Q_HEADS + 2 * AT_KV_HEADS) * HEAD
GATE_COLS = 3 * D_MODEL
ROPE_THETA = 10000.0
D_FF = 4 * D_MODEL

LANES = 128
SUBLANES = 8
VMEM_LIMIT_BYTES = 56 * 1024 * 1024
CHUNK = 64
TM_INPROJ = 256
TB_SCAN = 512
TQ_ATTN = 256
TK_ATTN = 2048
TK_PREP = 512
TM_MERGE = 512
TM_MLP = 512
FF_CHUNK = 1024


def _cparams(*sem):
    return pltpu.CompilerParams(dimension_semantics=sem, vmem_limit_bytes=VMEM_LIMIT_BYTES)


def _dot(a, b):
    return jnp.dot(a, b, preferred_element_type=F32)


def _dot_nt(a, b):
    return lax.dot_general(a, b, (((1,), (1,)), ((), ())), preferred_element_type=F32)


def _segsum(x, j):
    xb, jt = x.astype(BF16), j[0:LANES, 0:LANES]
    return jnp.concatenate([_dot(xb[:, t:t + LANES], jt) for t in range(0, x.shape[1], LANES)], axis=1)


def _segsum_split(x, j):
    hi = x.astype(BF16)
    return _dot(hi, j) + _dot((x - hi.astype(F32)).astype(BF16), j)


def _sigmoid(x):
    return 1.0 / (1.0 + jnp.exp(-x))


def _resident(shape):
    nd = len(shape)
    return pl.BlockSpec(shape, lambda *_: (0,) * nd, pipeline_mode=pl.Buffered(1))


def _shift_rows(c, halo_prev, halo_next, first, last):
    n = c.shape[0]
    row = lax.broadcasted_iota(jnp.int32, c.shape, 0)
    lane_row = lax.broadcasted_iota(jnp.int32, (1, c.shape[1]), 1)
    is_first = (lane_row * 0 + first.astype(jnp.int32)) > 0
    is_last = (lane_row * 0 + last.astype(jnp.int32)) > 0
    prev_row = jnp.where(is_first, 0.0, halo_prev[halo_prev.shape[0] - 1:, :])
    next_row = jnp.where(is_last, 0.0, halo_next[0:1, :])
    prev = jnp.where(row == 0, prev_row, pltpu.roll(c, 1, 0))
    nxt = jnp.where(row == n - 1, next_row, pltpu.roll(c, n - 1, 0))
    return prev, nxt


def _halo_specs(ts, width, n_seq_rows, rows):
    r = ts // rows
    last_blk = n_seq_rows // rows - 1
    prev = pl.BlockSpec((None, rows, width), lambda b, i: (b, jnp.maximum(i * r - 1, 0), 0))
    nxt = pl.BlockSpec((None, rows, width), lambda b, i: (b, jnp.minimum((i + 1) * r, last_blk), 0))
    return prev, nxt


def _inproj_kernel(x_ref, xp_ref, xn_ref, g_ref, w_ref, mup_ref, mun_ref, w0_ref, w2_ref, a0_ref, a2_ref, g2_ref,
                   kk_w_ref, ka_ref, rk_ref, j_ref, cvw_ref, cvb_ref,
                   r_ref, v_ref, kk_ref, gate_ref, bonus_ref, lw_ref, kd_ref, kka_ref, ycv_ref, at_ref, sg_ref):
    i = pl.program_id(1)

    def normed(x):
        return (x * lax.rsqrt(jnp.mean(x * x, axis=-1, keepdims=True) + NORM_EPS) * g_ref[...]).astype(BF16)

    h = normed(x_ref[...])
    o0, o1, o2 = RW_COLS, RW_COLS + CV_COLS, RW_COLS + CV_COLS + AT_COLS
    dm = D_MODEL

    def gate_cols(n):
        sg_ref[:, n * dm:(n + 1) * dm] = _sigmoid(_dot(h, w_ref[:, o2 + n * dm:o2 + (n + 1) * dm])).astype(BF16)

    tm = x_ref.shape[0]
    h_ext = jnp.concatenate([h, normed(jnp.concatenate([xp_ref[...], xn_ref[...]], axis=0))], axis=0)
    c_ext = _dot(h_ext, w_ref[:, 0:o0])
    c, c_halo = c_ext[0:tm], c_ext[tm:tm + 2 * SUBLANES]
    first, last = i == 0, i == pl.num_programs(1) - 1
    cw = CV_WIDTH
    cv_ext = _dot(h_ext, w_ref[:, o0:o1])
    cv, cv_halo = cv_ext[0:tm], cv_ext[tm:tm + 2 * SUBLANES]
    zc = cv[:, cw:2 * cw] * cv[:, 2 * cw:3 * cw]
    zc_halo = cv_halo[:, cw:2 * cw] * cv_halo[:, 2 * cw:3 * cw]
    zc_prev, zc_next = _shift_rows(zc, zc_halo[0:SUBLANES], zc_halo[SUBLANES:2 * SUBLANES], first, last)
    conv = cvb_ref[...] + cvw_ref[0:1, :] * zc_prev + cvw_ref[1:2, :] * zc + cvw_ref[2:3, :] * zc_next
    ycv_ref[...] = (cv[:, 0:cw] * conv).astype(BF16)
    prev, nxt = _shift_rows(c, c_halo[0:SUBLANES], c_halo[SUBLANES:2 * SUBLANES], first, last)
    z = c + mup_ref[...] * (prev - c) + mun_ref[...] * (nxt - c)
    w = RW_WIDTH
    r, k, v = z[:, 0:w], z[:, w:2 * w], z[:, 2 * w:3 * w]
    wd = z[:, 3 * w:3 * w + RW_LORA]
    ad = z[:, 3 * w + RW_LORA:3 * w + 2 * RW_LORA]
    gd = z[:, 3 * w + 2 * RW_LORA:3 * w + 3 * RW_LORA]
    j = j_ref[...]
    w_raw = w0_ref[...] + _dot(jnp.tanh(wd).astype(BF16), w2_ref[...])
    lw = -math.exp(-0.5) * _sigmoid(w_raw)
    a = _sigmoid(a0_ref[...] + _dot(ad.astype(BF16), a2_ref[...]))
    gate_ref[...] = _dot(_sigmoid(gd).astype(BF16), g2_ref[...]).astype(BF16)
    gate_cols(0)
    kkr = k * kk_w_ref[...]
    kk = kkr / jnp.maximum(jnp.sqrt(_segsum_split(kkr * kkr, j)), 1e-12)
    r_ref[...] = r.astype(BF16)
    v_ref[...] = v.astype(BF16)
    kk_ref[...] = kk.astype(BF16)
    at_ref[...] = _dot(h, w_ref[:, o1:o2]).astype(BF16)
    gate_cols(1)
    ksum = jnp.zeros_like(k)
    for d in range(2):
        a_d = a[:, d * w:(d + 1) * w]
        k_d = k * (1.0 + (a_d - 1.0) * ka_ref[...])
        ksum = ksum + k_d
        lw_ref[d] = lw[:, d * w:(d + 1) * w]
        kd_ref[d] = k_d.astype(BF16)
        kka_ref[d] = (kk * a_d).astype(BF16)
    gate_cols(2)
    bonus_ref[...] = (_segsum_split(r * (0.5 * ksum) * rk_ref[...], j) * v).astype(BF16)


def _inproj(x, g, w, mup, mun, w0, w2bd, a0, a2bd, g2, k_k, k_a, r_k, jones, cv_w, cv_b):
    bsz, s, _ = x.shape
    tm = min(TM_INPROJ, s)
    hp, hn = _halo_specs(tm, D_MODEL, s, SUBLANES)
    tok = lambda n: pl.BlockSpec((None, tm, n), lambda b_, i: (b_, i, 0))
    tok2 = pl.BlockSpec((2, None, tm, RW_WIDTH), lambda b_, i: (0, b_, i, 0))
    consts = (g, w, mup, mun, w0, w2bd, a0, a2bd, g2, k_k, k_a, r_k, jones, cv_w, cv_b)
    one = lambda n: jax.ShapeDtypeStruct((bsz, s, n), BF16)
    two = lambda dt: jax.ShapeDtypeStruct((2, bsz, s, RW_WIDTH), dt)
    return pl.pallas_call(
        _inproj_kernel,
        grid=(bsz, s // tm),
        in_specs=[tok(D_MODEL), hp, hn] + [_resident(a.shape) for a in consts],
        out_specs=[tok(RW_WIDTH)] * 5 + [tok2] * 3 + [tok(CV_WIDTH), tok(AT_COLS), tok(GATE_COLS)],
        out_shape=[one(RW_WIDTH)] * 5 + [two(F32), two(BF16), two(BF16)]
        + [one(CV_WIDTH), one(AT_COLS), one(GATE_COLS)],
        compiler_params=_cparams("parallel", "arbitrary"),
    )(x, x, x, *consts)


def _scan_masks():
    c = CHUNK
    t = np.arange(c)
    lower = (t[None, :] <= t[:, None]).astype(np.float32)
    strict_l = (t[None, :] < t[:, None]).astype(np.float32)
    ones = np.ones((2 * SUBLANES, c), np.float32)
    tri = np.stack([np.concatenate([lower, ones], 0), np.concatenate([lower.T, ones], 0)])
    strict = np.stack([np.tile(strict_l, (1, 2)), np.tile(strict_l.T, (1, 2))])
    incl = np.stack([np.tile(lower, (1, 2)), np.tile(lower.T, (1, 2))])
    return jnp.asarray(tri, BF16), jnp.asarray(strict, F32), jnp.asarray(incl, F32)


def _scan_kernel(rf_ref, vf_ref, kkf_ref, rb_ref, vb_ref, kkb_ref,
                 lwf_ref, kdf_ref, kkaf_ref, lwb_ref, kdb_ref, kkab_ref,
                 tri_ref, strict_ref, incl_ref, yf_ref, yb_ref, st_ref,
                 kq_ref, rq_ref, kdec_ref, bdec_ref, eend_ref, avk_ref, ark_ref, arb_ref, m_ref, tinv_ref):
    c = CHUNK
    n_chunks = rf_ref.shape[0] // c
    n_pairs = RW_WIDTH // LANES

    @pl.when(pl.program_id(1) == 0)
    def _():
        st_ref[...] = jnp.zeros_like(st_ref)

    lane = lax.broadcasted_iota(jnp.int32, (c, LANES), 1)
    row = lax.broadcasted_iota(jnp.int32, (c, LANES), 0)
    head0 = lane < HEAD
    eye = jnp.where((lane & (HEAD - 1)) == row, 1.0, 0.0).astype(F32)
    ri = lax.broadcasted_iota(jnp.int32, (LANES, LANES), 0)
    ci = lax.broadcasted_iota(jnp.int32, (LANES, LANES), 1)
    same_head = jnp.where((ri < HEAD) == (ci < HEAD), 1.0, 0.0).astype(F32)

    def stack(x):
        return jnp.concatenate([jnp.where(head0, x, 0.0), jnp.where(head0, 0.0, x)], axis=0)

    def stack_b(x):
        return stack(x).astype(BF16)

    dir_refs = ((rf_ref, vf_ref, kkf_ref, lwf_ref, kdf_ref, kkaf_ref, yf_ref),
                (rb_ref, vb_ref, kkb_ref, lwb_ref, kdb_ref, kkab_ref, yb_ref))

    def chunk_rows(d, step):
        cidx = step if d == 0 else n_chunks - 1 - step
        start = cidx * c
        return pl.ds(start if isinstance(start, int) else pl.multiple_of(start, c), c)

    def local_stages(step):
        slot = step
        ch = []
        for d, (r_ref, v_ref, kk_ref, lw_ref, kd_ref, kka_ref, y_ref) in enumerate(dir_refs):
            rows = chunk_rows(d, step)
            lw = lw_ref[rows, :]
            hi = lw.astype(BF16)
            lo = (lw - hi.astype(F32)).astype(BF16)
            tri = tri_ref[d]
            cum = _dot(tri, hi) + _dot(tri, lo)
            lp = cum[0:c]
            e_end = jnp.exp(cum[c:c + SUBLANES])
            e_n = jnp.exp(-lp)
            rq = (r_ref[rows, :] * jnp.exp(lp)).astype(BF16)
            kq = (kk_ref[rows, :] * jnp.exp(lp - lw)).astype(BF16)
            ks = kd_ref[rows, :] * e_n
            bs = kka_ref[rows, :] * e_n
            kq_ref[d, slot] = kq
            rq_ref[d, slot] = rq
            kdec_ref[d, slot] = (ks * e_end[0:1]).astype(BF16)
            bdec_ref[d, slot] = (bs * e_end[0:1]).astype(BF16)
            eend_ref[d, slot] = e_end
            for p in range(n_pairs):
                sl = slice(p * LANES, (p + 1) * LANES)
                ch.append(dict(d=d, sl=sl, kq=kq[:, sl], rq=rq[:, sl], ks=ks[:, sl], bs=bs[:, sl]))
        n = range(len(ch))
        strict = [strict_ref[x["d"]] for x in ch]
        incl = [incl_ref[x["d"]] for x in ch]
        yield
        lq = [jnp.concatenate([x["kq"], x["rq"]], axis=0) for x in ch]
        s_ks = [stack_b(x["ks"]) for x in ch]
        s_bs = [stack_b(x["bs"]) for x in ch]
        gkb = [_dot_nt(lq[i], jnp.concatenate([s_ks[i], s_bs[i]], axis=0)) for i in n]
        yield
        gk = [x[:, 0:LANES] for x in gkb]
        gb = [x[:, LANES:2 * LANES] for x in gkb]
        m = [-(gb[i][0:c] * strict[i]) for i in n]
        for i, x in enumerate(ch):
            avk_ref[x["d"], slot, :, x["sl"]] = (gk[i][0:c] * strict[i]).astype(BF16)
            ark_ref[x["d"], slot, :, x["sl"]] = (gk[i][c:2 * c] * incl[i]).astype(BF16)
            arb_ref[x["d"], slot, :, x["sl"]] = (gb[i][c:2 * c] * incl[i]).astype(BF16)
            m_ref[x["d"], slot, :, x["sl"]] = m[i].astype(BF16)
        t_inv = [eye + x for x in m]
        pw = [_dot(x.astype(BF16), stack_b(x)) for x in m]
        yield
        n_levels = int(math.log2(c)) - 1
        for lvl in range(n_levels):
            pw_bd = [stack_b(x) for x in pw]
            if lvl + 1 < n_levels:
                both = [_dot(jnp.concatenate([t_inv[i].astype(BF16), pw[i].astype(BF16)], axis=0), pw_bd[i])
                        for i in n]
                t_inv = [t_inv[i] + both[i][0:c] for i in n]
                pw = [both[i][c:2 * c] for i in n]
            else:
                t_inv = [t_inv[i] + _dot(t_inv[i].astype(BF16), pw_bd[i]) for i in n]
            yield
        for i, x in enumerate(ch):
            tinv_ref[x["d"], slot, :, x["sl"]] = t_inv[i].astype(BF16)

    def state_stages(step):
        slot = step
        ch = []
        for d, (r_ref, v_ref, kk_ref, lw_ref, kd_ref, kka_ref, y_ref) in enumerate(dir_refs):
            rows = chunk_rows(d, step)
            for p in range(n_pairs):
                ch.append(dict(d=d, p=p, rows=rows, sl=slice(p * LANES, (p + 1) * LANES), y_ref=y_ref, v_ref=v_ref))
        n = range(len(ch))
        get = lambda ref: [ref[x["d"], slot, :, x["sl"]] for x in ch]
        v = [x["v_ref"][x["rows"], x["sl"]].astype(F32) for x in ch]
        s_vb = [stack_b(x) for x in v]
        st = [st_ref[x["d"], x["p"]] for x in ch]
        stb = [x.astype(BF16) for x in st]
        kq_b, rq_b, a_vk, a_rk, a_rb, t_b, m_b = (get(r) for r in (
            kq_ref, rq_ref, avk_ref, ark_ref, arb_ref, tinv_ref, m_ref))
        from_state = [_dot_nt(jnp.concatenate([kq_b[i], rq_b[i]], axis=0), stb[i]) for i in n]
        from_v = [_dot(jnp.concatenate([a_vk[i], a_rk[i]], axis=0), s_vb[i]) for i in n]
        yield
        rhs = [from_state[i][0:c] + from_v[i][0:c] for i in n]
        u0 = [-_dot(t_b[i], stack_b(rhs[i])) for i in n]
        yield
        res = [_dot(m_b[i], stack_b(u0[i])) - rhs[i] - u0[i] for i in n]
        yield
        u = [u0[i] + _dot(t_b[i], stack_b(res[i])) for i in n]
        yield
        y = [from_state[i][c:2 * c] + from_v[i][c:2 * c] + _dot(a_rb[i], stack_b(u[i])) for i in n]
        vu_t = [jnp.concatenate([v[i], u[i]], axis=0).T.astype(BF16) for i in n]
        kb = [jnp.concatenate([kdec_ref[x["d"], slot, :, x["sl"]], bdec_ref[x["d"], slot, :, x["sl"]]], axis=0)
              for x in ch]
        upd = [_dot(vu_t[i], kb[i]) * same_head for i in n]
        for i, x in enumerate(ch):
            x["y_ref"][x["rows"], x["sl"]] = y[i]
            st_ref[x["d"], x["p"]] = st[i] * eend_ref[x["d"], slot, 0:1, x["sl"]] + upd[i]

    def emit(*gens, order=None):
        live = {i: g for i, g in enumerate(gens)}
        for i in list(order or ()) + [None]:
            while i is None and live:
                for k in list(live):
                    if next(live[k], live) is live:
                        del live[k]
            if i in live and next(live[i], live) is live:
                del live[i]

    emit(local_stages(0))

    def body(i, carry):
        emit(local_stages(i + 1), state_stages(i), order=(0, 1, 0, 0, 1, 0, 0, 1, 0, 1, 0, 1, 0))
        return carry

    lax.fori_loop(0, n_chunks - 1, body, 0)
    emit(state_stages(n_chunks - 1))


def _scan(r, v, kk, lw, kd, kka, masks):
    bsz, s, w = r.shape
    tb = min(TB_SCAN, s)
    nblk = s // tb
    fwd = pl.BlockSpec((None, tb, w), lambda b_, j: (b_, j, 0))
    bwd = pl.BlockSpec((None, tb, w), lambda b_, j: (b_, nblk - 1 - j, 0))
    fwd2 = pl.BlockSpec((None, None, tb, w), lambda b_, j: (0, b_, j, 0))
    bwd2 = pl.BlockSpec((None, None, tb, w), lambda b_, j: (1, b_, nblk - 1 - j, 0))
    out = jax.ShapeDtypeStruct((bsz, s, w), F32)
    per_chunk = lambda dt: pltpu.VMEM((2, tb // CHUNK, CHUNK, w), dt)
    return pl.pallas_call(
        _scan_kernel,
        grid=(bsz, nblk),
        in_specs=[fwd, fwd, fwd, bwd, bwd, bwd, fwd2, fwd2, fwd2, bwd2, bwd2, bwd2]
        + [_resident(a.shape) for a in masks],
        out_specs=[fwd, bwd],
        out_shape=[out, out],
        scratch_shapes=[pltpu.VMEM((2, w // LANES, LANES, LANES), F32), per_chunk(BF16), per_chunk(BF16),
                        per_chunk(BF16), per_chunk(BF16), pltpu.VMEM((2, tb // CHUNK, SUBLANES, w), F32)]
        + [per_chunk(BF16)] * 5,
        compiler_params=_cparams("parallel", "arbitrary"),
    )(r, v, kk, r, v, kk, lw, kd, kka, lw, kd, kka, *masks)


def _rope_partner(x):
    n = x.shape[1]
    lane = lax.broadcasted_iota(jnp.int32, x.shape, 1)
    return jnp.where((lane & 31) < 16, pltpu.roll(x, n - 16, 1), pltpu.roll(x, 16, 1))


def _attn_kernel(q_ref, k_ref, v_ref, cq_ref, sq_ref, ck_ref, sk_ref, qg_ref, kg_ref, jq_ref, jk_ref,
                 o_ref, kd_ref, vd_ref):
    s_len = k_ref.shape[0]
    tq = q_ref.shape[0]
    tk = min(TK_PREP, s_len)

    @pl.when(pl.program_id(1) == 0)
    def _():
        def body(i, carry):
            rows = pl.ds(pl.multiple_of(i * tk, tk), tk)
            k = k_ref[rows, :].astype(F32)
            kn = k * lax.rsqrt(_segsum(k * k, jk_ref[...]) * (1.0 / HEAD) + NORM_EPS) * kg_ref[...]
            kr = kn * ck_ref[rows, :] + _rope_partner(kn) * sk_ref[rows, :]
            v = v_ref[rows, :].astype(F32)
            low = lax.broadcasted_iota(jnp.int32, kr.shape, 1) < HEAD
            ksw = pltpu.roll(kr, HEAD, 1)
            kd_ref[0, rows, :] = jnp.where(low, kr, ksw).astype(BF16)
            kd_ref[1, rows, :] = jnp.where(low, ksw, kr).astype(BF16)
            vd_ref[0, rows, :] = jnp.where(low, v, 1.0).astype(BF16)
            vd_ref[1, rows, :] = jnp.where(low, pltpu.roll(v, HEAD, 1), 1.0).astype(BF16)
            return carry
        lax.fori_loop(0, s_len // tk, body, 0)

    q = q_ref[...].astype(F32)
    qn = q * lax.rsqrt(_segsum(q * q, jq_ref[...]) * (1.0 / HEAD) + NORM_EPS) * qg_ref[...]
    qr = qn * cq_ref[...] + _rope_partner(qn) * sq_ref[...]
    low = lax.broadcasted_iota(jnp.int32, (tq, LANES), 1) < HEAD
    pairs_per_kv = AT_Q_HEADS // AT_KV_HEADS // 2
    tkc = min(TK_ATTN, s_len)
    n_kc = s_len // tkc
    qs = []
    for g in range(AT_KV_HEADS):
        parts = []
        for jj in range(pairs_per_kv):
            qp = qr[:, (g * pairs_per_kv + jj) * LANES:(g * pairs_per_kv + jj + 1) * LANES]
            parts += [jnp.where(low, qp, 0.0), jnp.where(low, 0.0, qp)]
        qs.append(jnp.concatenate(parts, axis=0).astype(BF16))
    streams = [(g, kc) for g in range(AT_KV_HEADS) for kc in range(n_kc)]
    scores = lambda g, kc: _dot_nt(qs[g], kd_ref[g, kc * tkc:(kc + 1) * tkc, :])
    partial, s_next = {}, scores(*streams[0])
    for i, (g, kc) in enumerate(streams):
        s = s_next
        if i + 1 < len(streams):
            s_next = scores(*streams[i + 1])
        m = jnp.max(s, axis=-1, keepdims=True)
        p = jnp.exp2(s - m).astype(BF16)
        partial[g, kc] = (m, _dot(p, vd_ref[g, kc * tkc:(kc + 1) * tkc, :]))
    for g in range(AT_KV_HEADS):
        m_all = partial[g, 0][0]
        for kc in range(1, n_kc):
            m_all = jnp.maximum(m_all, partial[g, kc][0])
        o = sum(jnp.exp2(partial[g, kc][0] - m_all) * partial[g, kc][1] for kc in range(n_kc))
        o = o / pltpu.roll(o, HEAD, 1)
        for jj in range(pairs_per_kv):
            even = o[(2 * jj) * tq:(2 * jj + 1) * tq]
            odd = pltpu.roll(o[(2 * jj + 1) * tq:(2 * jj + 2) * tq], HEAD, 1)
            pair = jnp.where(low, even, odd)
            col = (g * pairs_per_kv + jj) * LANES
            o_ref[:, col:col + LANES] = pair.astype(BF16)


def _attn(at, tabs, q_g, k_g, jones):
    bsz, s, _ = at.shape
    tq = min(TQ_ATTN, s)
    cq, sq, ck, sk = tabs
    qw = AT_Q_HEADS * HEAD
    kvw = AT_KV_HEADS * HEAD
    cst = lambda a: pl.BlockSpec(a.shape, lambda b_, i: (0,) * a.ndim)
    jk = jones[0:kvw, 0:kvw]
    return pl.pallas_call(
        _attn_kernel,
        grid=(bsz, s // tq),
        in_specs=[pl.BlockSpec((None, tq, qw), lambda b_, i: (b_, i, 0)),
                  pl.BlockSpec((None, s, kvw), lambda b_, i: (b_, 0, qw // kvw)),
                  pl.BlockSpec((None, s, kvw), lambda b_, i: (b_, 0, qw // kvw + 1)),
                  pl.BlockSpec((tq, qw), lambda b_, i: (i, 0)),
                  pl.BlockSpec((tq, qw), lambda b_, i: (i, 0)),
                  cst(ck), cst(sk), cst(q_g), cst(k_g), cst(jones), cst(jk)],
        out_specs=pl.BlockSpec((None, tq, qw), lambda b_, i: (b_, i, 0)),
        out_shape=jax.ShapeDtypeStruct((bsz, s, qw), BF16),
        scratch_shapes=[pltpu.VMEM((AT_KV_HEADS, s, LANES), BF16), pltpu.VMEM((AT_KV_HEADS, s, LANES), BF16)],
        compiler_params=_cparams("parallel", "arbitrary"),
    )(at, at, at, cq, sq, ck, sk, q_g, k_g, jones, jk)


def _rope_tables(s):
    rows = s // GRID_W
    pos_r = jnp.repeat(jnp.arange(rows), GRID_W).astype(F32)
    pos_c = jnp.tile(jnp.arange(GRID_W), rows).astype(F32)
    half = HEAD // 2
    inv = ROPE_THETA ** (-jnp.arange(0, half, 2, dtype=F32) / half)
    ang_r = pos_r[:, None] * inv
    ang_c = pos_c[:, None] * inv
    cos = jnp.concatenate([jnp.cos(ang_r)] * 2 + [jnp.cos(ang_c)] * 2, axis=-1)
    sin = jnp.concatenate([-jnp.sin(ang_r), jnp.sin(ang_r), -jnp.sin(ang_c), jnp.sin(ang_c)], axis=-1)
    scale = HEAD ** -0.5 * math.log2(math.e)
    return (jnp.tile(cos, (1, AT_Q_HEADS)) * scale, jnp.tile(sin, (1, AT_Q_HEADS)) * scale,
            jnp.tile(cos, (1, AT_KV_HEADS)), jnp.tile(sin, (1, AT_KV_HEADS)))


def _merge_kernel(y0_ref, y1_ref, g_ref, bonus_ref, lng_ref, lnb_ref, j_ref, ycv_ref, yat_ref, sg_ref, x_ref,
                  wb_ref, wo_ref, o_ref):
    j = j_ref[...]
    y = y0_ref[...] + y1_ref[...]
    mu = _segsum(y, j) * (1.0 / HEAD)
    dev = y - mu
    var = _segsum(dev * dev, j) * (1.0 / HEAD)
    yn = dev * lax.rsqrt(var + RW_LN_EPS) * lng_ref[...] + lnb_ref[...]
    y_rw = ((yn + bonus_ref[...]) * g_ref[...]).astype(BF16)
    dm = D_MODEL
    merged = (sg_ref[:, 0:dm].astype(F32) * _dot(y_rw, wb_ref[0])
              + sg_ref[:, dm:2 * dm].astype(F32) * _dot(ycv_ref[...], wb_ref[1])
              + sg_ref[:, 2 * dm:3 * dm].astype(F32) * _dot(yat_ref[...], wb_ref[2]))
    o_ref[...] = x_ref[...] + _dot(merged.astype(BF16), wo_ref[...])


def _merge(yf, yb, g, bonus, ln_g, ln_b, jones, y_cv, y_at, sg, x, w_branch, w_out):
    t = x.shape[0]
    tm = TM_MERGE
    row = lambda n: pl.BlockSpec((tm, n), lambda i: (i, 0))
    return pl.pallas_call(
        _merge_kernel,
        grid=(t // tm,),
        in_specs=[row(RW_WIDTH), row(RW_WIDTH), row(RW_WIDTH), row(RW_WIDTH),
                  _resident(ln_g.shape), _resident(ln_b.shape), _resident(jones.shape),
                  row(CV_WIDTH), row(AT_Q_HEADS * HEAD), row(GATE_COLS), row(D_MODEL),
                  _resident(w_branch.shape), _resident(w_out.shape)],
        out_specs=row(D_MODEL),
        out_shape=jax.ShapeDtypeStruct((t, D_MODEL), F32),
        compiler_params=_cparams("parallel"),
    )(yf, yb, g, bonus, ln_g, ln_b, jones, y_cv, y_at, sg, x, w_branch, w_out)


def _mlp_kernel(x_ref, g_ref, w1_ref, w2_ref, o_ref):
    x = x_ref[...]
    h = (x * lax.rsqrt(jnp.mean(x * x, axis=-1, keepdims=True) + NORM_EPS) * g_ref[...]).astype(BF16)
    acc = x
    for c in range(D_FF // FF_CHUNK):
        cols = slice(c * FF_CHUNK, (c + 1) * FF_CHUNK)
        u = jnp.maximum(_dot(h, w1_ref[:, cols]), 0.0)
        acc = acc + _dot((u * u).astype(BF16), w2_ref[cols, :])
    o_ref[...] = acc


def _mlp(x, g, w1, w2):
    t = x.shape[0]
    tm = TM_MLP
    row = pl.BlockSpec((tm, D_MODEL), lambda i: (i, 0))
    return pl.pallas_call(
        _mlp_kernel,
        grid=(t // tm,),
        in_specs=[row, _resident(g.shape), _resident(w1.shape), _resident(w2.shape)],
        out_specs=row,
        out_shape=jax.ShapeDtypeStruct((t, D_MODEL), F32),
        compiler_params=_cparams("parallel"),
    )(x, g, w1, w2)


def _block_diag2(w):
    z = jnp.zeros_like(w[0])
    return jnp.concatenate([jnp.concatenate([w[0], z], axis=1), jnp.concatenate([z, w[1]], axis=1)], axis=0)


def kernel(x, norm1_g, w_in, rw_mu_prev, rw_mu_next, rw_w0, rw_w2, rw_a0, rw_a2, rw_g2, rw_k_k, rw_k_a, rw_r_k,
           rw_ln_g, rw_ln_b, cv_w, cv_b, at_q_g, at_k_g, w_branch, w_out, norm2_g, w_mlp1, w_mlp2):
    bsz, s, dm = x.shape
    assert dm == D_MODEL and s % GRID_W == 0
    for tile in (TM_INPROJ, TB_SCAN, TQ_ATTN, TK_PREP):
        assert s % min(tile, s) == 0 and min(tile, s) % CHUNK == 0
    t = bsz * s
    tabs = _rope_tables(s)
    masks = _scan_masks()
    jones = jnp.asarray(np.kron(np.eye(RW_WIDTH // HEAD, dtype=np.float32), np.ones((HEAD, HEAD), np.float32)), BF16)
    row = lambda a: a.reshape(1, -1)
    xf = x.reshape(t, dm)
    for l in range(DEPTH):
        r, v, kk, g, bonus, lw, kd, kka, y_cv, at, sg = _inproj(
            xf.reshape(bsz, s, dm), row(norm1_g[l]), w_in[l].astype(BF16), row(rw_mu_prev[l]), row(rw_mu_next[l]),
            row(rw_w0[l]), _block_diag2(rw_w2[l]).astype(BF16), row(rw_a0[l]), _block_diag2(rw_a2[l]).astype(BF16),
            rw_g2[l].astype(BF16), row(rw_k_k[l]), row(rw_k_a[l]), row(rw_r_k[l]), jones, cv_w[l], row(cv_b[l]))
        yf, yb = _scan(r, v, kk, lw, kd, kka, masks)
        y_at = _attn(at, tabs, row(jnp.tile(at_q_g[l], AT_Q_HEADS)),
                     row(jnp.tile(at_k_g[l], AT_KV_HEADS)), jones)
        xf = _merge(yf.reshape(t, RW_WIDTH), yb.reshape(t, RW_WIDTH), g.reshape(t, RW_WIDTH),
                    bonus.reshape(t, RW_WIDTH),
                    row(rw_ln_g[l]), row(rw_ln_b[l]), jones, y_cv.reshape(t, CV_WIDTH),
                    y_at.reshape(t, AT_Q_HEADS * HEAD), sg.reshape(t, GATE_COLS), xf,
                    w_branch[l].astype(BF16), w_out[l].astype(BF16))
        xf = _mlp(xf, row(norm2_g[l]), w_mlp1[l].astype(BF16), w_mlp2[l].astype(BF16))
    return xf.reshape(bsz, s, dm)
```

```python
import math

import numpy as np
import jax
import jax.numpy as jnp
from jax import lax
from jax.experimental import pallas as pl
from jax.experimental.pallas import tpu as pltpu

F32 = jnp.float32
BF16 = jnp.bfloat16

D_MODEL = 1024
DEPTH = 4
GRID_W = 64
NORM_EPS = 1e-6
HEAD = 64
RW_WIDTH = 512
RW_LORA = 128
RW_LN_EPS = 64e-5
RW_COLS = 3 * RW_WIDTH + 3 * RW_LORA
CV_WIDTH = 512
CV_COLS = 3 * CV_WIDTH
AT_Q_HEADS = 8
AT_KV_HEADS = 2
AT_COLS = (AT_Q_HEADS + 2 * AT_KV_HEADS) * HEAD
GATE_COLS = 3 * D_MODEL
ROPE_THETA = 10000.0
D_FF = 4 * D_MODEL

LANES = 128
SUBLANES = 8
VMEM_LIMIT_BYTES = 56 * 1024 * 1024
CHUNK = 64
TM_INPROJ = 256
TB_SCAN = 512
TQ_ATTN = 256
TK_ATTN = 2048
TK_PREP = 512
TM_MERGE = 512
TM_MLP = 512
FF_CHUNK = 1024


def _cparams(*sem):
    return pltpu.CompilerParams(dimension_semantics=sem, vmem_limit_bytes=VMEM_LIMIT_BYTES)


def _dot(a, b):
    return jnp.dot(a, b, preferred_element_type=F32)


def _dot_nt(a, b):
    return lax.dot_general(a, b, (((1,), (1,)), ((), ())), preferred_element_type=F32)


def _segsum(x, j):
    xb, jt = x.astype(BF16), j[0:LANES, 0:LANES]
    return jnp.concatenate([_dot(xb[:, t:t + LANES], jt) for t in range(0, x.shape[1], LANES)], axis=1)


def _sigmoid(x):
    return 1.0 / (1.0 + jnp.exp(-x))


def _resident(shape):
    nd = len(shape)
    return pl.BlockSpec(shape, lambda *_: (0,) * nd, pipeline_mode=pl.Buffered(1))


def _shift_rows(c, halo_prev, halo_next, first, last):
    n = c.shape[0]
    row = lax.broadcasted_iota(jnp.int32, c.shape, 0)
    lane_row = lax.broadcasted_iota(jnp.int32, (1, c.shape[1]), 1)
    is_first = (lane_row * 0 + first.astype(jnp.int32)) > 0
    is_last = (lane_row * 0 + last.astype(jnp.int32)) > 0
    prev_row = jnp.where(is_first, 0.0, halo_prev[halo_prev.shape[0] - 1:, :])
    next_row = jnp.where(is_last, 0.0, halo_next[0:1, :])
    prev = jnp.where(row == 0, prev_row, pltpu.roll(c, 1, 0))
    nxt = jnp.where(row == n - 1, next_row, pltpu.roll(c, n - 1, 0))
    return prev, nxt


def _halo_specs(ts, width, n_seq_rows, rows):
    r = ts // rows
    last_blk = n_seq_rows // rows - 1
    prev = pl.BlockSpec((None, rows, width), lambda b, i: (b, jnp.maximum(i * r - 1, 0), 0))
    nxt = pl.BlockSpec((None, rows, width), lambda b, i: (b, jnp.minimum((i + 1) * r, last_blk), 0))
    return prev, nxt


def _inproj_kernel(x_ref, xp_ref, xn_ref, g_ref, w_ref, mup_ref, mun_ref, w0_ref, w2_ref, a0_ref, a2_ref, g2_ref,
                   kk_w_ref, ka_ref, rk_ref, j_ref, cvw_ref, cvb_ref,
                   r_ref, v_ref, kk_ref, gate_ref, bonus_ref, lw_ref, kd_ref, kka_ref, ycv_ref, at_ref, sg_ref):
    i = pl.program_id(1)

    def normed(x):
        return (x * lax.rsqrt(jnp.mean(x * x, axis=-1, keepdims=True) + NORM_EPS) * g_ref[...]).astype(BF16)

    h = normed(x_ref[...])
    o0, o1, o2 = RW_COLS, RW_COLS + CV_COLS, RW_COLS + CV_COLS + AT_COLS
    dm = D_MODEL

    def gate_cols(n):
        sg_ref[:, n * dm:(n + 1) * dm] = _sigmoid(_dot(h, w_ref[:, o2 + n * dm:o2 + (n + 1) * dm])).astype(BF16)

    tm = x_ref.shape[0]
    h_ext = jnp.concatenate([h, normed(jnp.concatenate([xp_ref[...], xn_ref[...]], axis=0))], axis=0)
    c_ext = _dot(h_ext, w_ref[:, 0:o0])
    c, c_halo = c_ext[0:tm], c_ext[tm:tm + 2 * SUBLANES]
    first, last = i == 0, i == pl.num_programs(1) - 1
    cw = CV_WIDTH
    cv_ext = _dot(h_ext, w_ref[:, o0:o1])
    cv, cv_halo = cv_ext[0:tm], cv_ext[tm:tm + 2 * SUBLANES]
    zc = cv[:, cw:2 * cw] * cv[:, 2 * cw:3 * cw]
    zc_halo = cv_halo[:, cw:2 * cw] * cv_halo[:, 2 * cw:3 * cw]
    zc_prev, zc_next = _shift_rows(zc, zc_halo[0:SUBLANES], zc_halo[SUBLANES:2 * SUBLANES], first, last)
    conv = cvb_ref[...] + cvw_ref[0:1, :] * zc_prev + cvw_ref[1:2, :] * zc + cvw_ref[2:3, :] * zc_next
    ycv_ref[...] = (cv[:, 0:cw] * conv).astype(BF16)
    prev, nxt = _shift_rows(c, c_halo[0:SUBLANES], c_halo[SUBLANES:2 * SUBLANES], first, last)
    z = c + mup_ref[...] * (prev - c) + mun_ref[...] * (nxt - c)
    w = RW_WIDTH
    r, k, v = z[:, 0:w], z[:, w:2 * w], z[:, 2 * w:3 * w]
    wd = z[:, 3 * w:3 * w + RW_LORA]
    ad = z[:, 3 * w + RW_LORA:3 * w + 2 * RW_LORA]
    gd = z[:, 3 * w + 2 * RW_LORA:3 * w + 3 * RW_LORA]
    j = j_ref[...]
    w_raw = w0_ref[...] + _dot(jnp.tanh(wd).astype(BF16), w2_ref[...])
    lw = -math.exp(-0.5) * _sigmoid(w_raw)
    a = _sigmoid(a0_ref[...] + _dot(ad.astype(BF16), a2_ref[...]))
    gate_ref[...] = _dot(_sigmoid(gd).astype(BF16), g2_ref[...]).astype(BF16)
    gate_cols(0)
    kkr = k * kk_w_ref[...]
    kk = kkr / jnp.maximum(jnp.sqrt(_segsum(kkr * kkr, j)), 1e-12)
    r_ref[...] = r.astype(BF16)
    v_ref[...] = v.astype(BF16)
    kk_ref[...] = kk.astype(BF16)
    at_ref[...] = _dot(h, w_ref[:, o1:o2]).astype(BF16)
    gate_cols(1)
    ksum = jnp.zeros_like(k)
    for d in range(2):
        a_d = a[:, d * w:(d + 1) * w]
        k_d = k * (1.0 + (a_d - 1.0) * ka_ref[...])
        ksum = ksum + k_d
        lw_ref[d] = lw[:, d * w:(d + 1) * w]
        kd_ref[d] = k_d.astype(BF16)
        kka_ref[d] = (kk * a_d).astype(BF16)
    gate_cols(2)
    bonus_ref[...] = (_segsum(r * (0.5 * ksum) * rk_ref[...], j) * v).astype(BF16)


def _inproj(x, g, w, mup, mun, w0, w2bd, a0, a2bd, g2, k_k, k_a, r_k, jones, cv_w, cv_b):
    bsz, s, _ = x.shape
    tm = min(TM_INPROJ, s)
    hp, hn = _halo_specs(tm, D_MODEL, s, SUBLANES)
    tok = lambda n: pl.BlockSpec((None, tm, n), lambda b_, i: (b_, i, 0))
    tok2 = pl.BlockSpec((2, None, tm, RW_WIDTH), lambda b_, i: (0, b_, i, 0))
    consts = (g, w, mup, mun, w0, w2bd, a0, a2bd, g2, k_k, k_a, r_k, jones, cv_w, cv_b)
    one = lambda n: jax.ShapeDtypeStruct((bsz, s, n), BF16)
    two = lambda dt: jax.ShapeDtypeStruct((2, bsz, s, RW_WIDTH), dt)
    return pl.pallas_call(
        _inproj_kernel,
        grid=(bsz, s // tm),
        in_specs=[tok(D_MODEL), hp, hn] + [_resident(a.shape) for a in consts],
        out_specs=[tok(RW_WIDTH)] * 5 + [tok2] * 3 + [tok(CV_WIDTH), tok(AT_COLS), tok(GATE_COLS)],
        out_shape=[one(RW_WIDTH)] * 5 + [two(F32), two(BF16), two(BF16)]
        + [one(CV_WIDTH), one(AT_COLS), one(GATE_COLS)],
        compiler_params=_cparams("parallel", "arbitrary"),
    )(x, x, x, *consts)


def _scan_masks():
    c = CHUNK
    t = np.arange(c)
    lower = (t[None, :] <= t[:, None]).astype(np.float32)
    strict_l = (t[None, :] < t[:, None]).astype(np.float32)
    ones = np.ones((2 * SUBLANES, c), np.float32)
    tri = np.stack([np.concatenate([lower, ones], 0), np.concatenate([lower.T, ones], 0)])
    strict = np.stack([np.tile(strict_l, (1, 2)), np.tile(strict_l.T, (1, 2))])
    incl = np.stack([np.tile(lower, (1, 2)), np.tile(lower.T, (1, 2))])
    return jnp.asarray(tri, BF16), jnp.asarray(strict, F32), jnp.asarray(incl, F32)


def _scan_kernel(rf_ref, vf_ref, kkf_ref, rb_ref, vb_ref, kkb_ref,
                 lwf_ref, kdf_ref, kkaf_ref, lwb_ref, kdb_ref, kkab_ref,
                 tri_ref, strict_ref, incl_ref, yf_ref, yb_ref, st_ref,
                 kq_ref, rq_ref, kdec_ref, bdec_ref, eend_ref, avk_ref, ark_ref, arb_ref, m_ref, tinv_ref):
    c = CHUNK
    n_chunks = rf_ref.shape[0] // c
    n_pairs = RW_WIDTH // LANES

    @pl.when(pl.program_id(1) == 0)
    def _():
        st_ref[...] = jnp.zeros_like(st_ref)

    lane = lax.broadcasted_iota(jnp.int32, (c, LANES), 1)
    row = lax.broadcasted_iota(jnp.int32, (c, LANES), 0)
    head0 = lane < HEAD
    eye = jnp.where((lane & (HEAD - 1)) == row, 1.0, 0.0).astype(F32)
    ri = lax.broadcasted_iota(jnp.int32, (LANES, LANES), 0)
    ci = lax.broadcasted_iota(jnp.int32, (LANES, LANES), 1)
    same_head = jnp.where((ri < HEAD) == (ci < HEAD), 1.0, 0.0).astype(F32)

    def stack(x):
        return jnp.concatenate([jnp.where(head0, x, 0.0), jnp.where(head0, 0.0, x)], axis=0)

    def stack_b(x):
        return stack(x).astype(BF16)

    dir_refs = ((rf_ref, vf_ref, kkf_ref, lwf_ref, kdf_ref, kkaf_ref, yf_ref),
                (rb_ref, vb_ref, kkb_ref, lwb_ref, kdb_ref, kkab_ref, yb_ref))

    def chunk_rows(d, step):
        cidx = step if d == 0 else n_chunks - 1 - step
        start = cidx * c
        return pl.ds(start if isinstance(start, int) else pl.multiple_of(start, c), c)

    def local_stages(step):
        slot = step
        ch = []
        for d, (r_ref, v_ref, kk_ref, lw_ref, kd_ref, kka_ref, y_ref) in enumerate(dir_refs):
            rows = chunk_rows(d, step)
            lw = lw_ref[rows, :]
            hi = lw.astype(BF16)
            lo = (lw - hi.astype(F32)).astype(BF16)
            tri = tri_ref[d]
            cum = _dot(tri, hi) + _dot(tri, lo)
            lp = cum[0:c]
            e_end = jnp.exp(cum[c:c + SUBLANES])
            e_n = jnp.exp(-lp)
            rq = (r_ref[rows, :] * jnp.exp(lp)).astype(BF16)
            kq = (kk_ref[rows, :] * jnp.exp(lp - lw)).astype(BF16)
            ks = kd_ref[rows, :] * e_n
            bs = kka_ref[rows, :] * e_n
            kq_ref[d, slot] = kq
            rq_ref[d, slot] = rq
            kdec_ref[d, slot] = (ks * e_end[0:1]).astype(BF16)
            bdec_ref[d, slot] = (bs * e_end[0:1]).astype(BF16)
            eend_ref[d, slot] = e_end
            for p in range(n_pairs):
                sl = slice(p * LANES, (p + 1) * LANES)
                ch.append(dict(d=d, sl=sl, kq=kq[:, sl], rq=rq[:, sl], ks=ks[:, sl], bs=bs[:, sl]))
        n = range(len(ch))
        strict = [strict_ref[x["d"]] for x in ch]
        incl = [incl_ref[x["d"]] for x in ch]
        yield
        lq = [jnp.concatenate([x["kq"], x["rq"]], axis=0) for x in ch]
        s_ks = [stack_b(x["ks"]) for x in ch]
        s_bs = [stack_b(x["bs"]) for x in ch]
        gkb = [_dot_nt(lq[i], jnp.concatenate([s_ks[i], s_bs[i]], axis=0)) for i in n]
        yield
        gk = [x[:, 0:LANES] for x in gkb]
        gb = [x[:, LANES:2 * LANES] for x in gkb]
        m = [-(gb[i][0:c] * strict[i]) for i in n]
        for i, x in enumerate(ch):
            avk_ref[x["d"], slot, :, x["sl"]] = (gk[i][0:c] * strict[i]).astype(BF16)
            ark_ref[x["d"], slot, :, x["sl"]] = (gk[i][c:2 * c] * incl[i]).astype(BF16)
            arb_ref[x["d"], slot, :, x["sl"]] = (gb[i][c:2 * c] * incl[i]).astype(BF16)
            m_ref[x["d"], slot, :, x["sl"]] = m[i].astype(BF16)
        t_inv = [eye + x for x in m]
        pw = [_dot(x.astype(BF16), stack_b(x)) for x in m]
        yield
        n_levels = int(math.log2(c)) - 1
        for lvl in range(n_levels):
            pw_bd = [stack_b(x) for x in pw]
            if lvl + 1 < n_levels:
                both = [_dot(jnp.concatenate([t_inv[i].astype(BF16), pw[i].astype(BF16)], axis=0), pw_bd[i])
                        for i in n]
                t_inv = [t_inv[i] + both[i][0:c] for i in n]
                pw = [both[i][c:2 * c] for i in n]
            else:
                t_inv = [t_inv[i] + _dot(t_inv[i].astype(BF16), pw_bd[i]) for i in n]
            yield
        for i, x in enumerate(ch):
            tinv_ref[x["d"], slot, :, x["sl"]] = t_inv[i].astype(BF16)

    def state_stages(step):
        slot = step
        ch = []
        for d, (r_ref, v_ref, kk_ref, lw_ref, kd_ref, kka_ref, y_ref) in enumerate(dir_refs):
            rows = chunk_rows(d, step)
            for p in range(n_pairs):
                ch.append(dict(d=d, p=p, rows=rows, sl=slice(p * LANES, (p + 1) * LANES), y_ref=y_ref, v_ref=v_ref))
        n = range(len(ch))
        get = lambda ref: [ref[x["d"], slot, :, x["sl"]] for x in ch]
        v = [x["v_ref"][x["rows"], x["sl"]].astype(F32) for x in ch]
        s_vb = [stack_b(x) for x in v]
        st = [st_ref[x["d"], x["p"]] for x in ch]
        stb = [x.astype(BF16) for x in st]
        kq_b, rq_b, a_vk, a_rk, a_rb, t_b, m_b = (get(r) for r in (
            kq_ref, rq_ref, avk_ref, ark_ref, arb_ref, tinv_ref, m_ref))
        from_state = [_dot_nt(jnp.concatenate([kq_b[i], rq_b[i]], axis=0), stb[i]) for i in n]
        from_v = [_dot(jnp.concatenate([a_vk[i], a_rk[i]], axis=0), s_vb[i]) for i in n]
        yield
        rhs = [from_state[i][0:c] + from_v[i][0:c] for i in n]
        u0 = [-_dot(t_b[i], stack_b(rhs[i])) for i in n]
        yield
        res = [_dot(m_b[i], stack_b(u0[i])) - rhs[i] - u0[i] for i in n]
        yield
        u = [u0[i] + _dot(t_b[i], stack_b(res[i])) for i in n]
        yield
        y = [from_state[i][c:2 * c] + from_v[i][c:2 * c] + _dot(a_rb[i], stack_b(u[i])) for i in n]
        vu_t = [jnp.concatenate([v[i], u[i]], axis=0).T.astype(BF16) for i in n]
        kb = [jnp.concatenate([kdec_ref[x["d"], slot, :, x["sl"]], bdec_ref[x["d"], slot, :, x["sl"]]], axis=0)
              for x in ch]
        upd = [_dot(vu_t[i], kb[i]) * same_head for i in n]
        for i, x in enumerate(ch):
            x["y_ref"][x["rows"], x["sl"]] = y[i]
            st_ref[x["d"], x["p"]] = st[i] * eend_ref[x["d"], slot, 0:1, x["sl"]] + upd[i]

    def emit(*gens, order=None):
        live = {i: g for i, g in enumerate(gens)}
        for i in list(order or ()) + [None]:
            while i is None and live:
                for k in list(live):
                    if next(live[k], live) is live:
                        del live[k]
            if i in live and next(live[i], live) is live:
                del live[i]

    emit(local_stages(0))

    def body(i, carry):
        emit(local_stages(i + 1), state_stages(i), order=(0, 1, 0, 0, 1, 0, 0, 1, 0, 1, 0, 1, 0))
        return carry

    lax.fori_loop(0, n_chunks - 1, body, 0)
    emit(state_stages(n_chunks - 1))


def _scan(r, v, kk, lw, kd, kka, masks):
    bsz, s, w = r.shape
    tb = min(TB_SCAN, s)
    nblk = s // tb
    fwd = pl.BlockSpec((None, tb, w), lambda b_, j: (b_, j, 0))
    bwd = pl.BlockSpec((None, tb, w), lambda b_, j: (b_, nblk - 1 - j, 0))
    fwd2 = pl.BlockSpec((None, None, tb, w), lambda b_, j: (0, b_, j, 0))
    bwd2 = pl.BlockSpec((None, None, tb, w), lambda b_, j: (1, b_, nblk - 1 - j, 0))
    out = jax.ShapeDtypeStruct((bsz, s, w), F32)
    per_chunk = lambda dt: pltpu.VMEM((2, tb // CHUNK, CHUNK, w), dt)
    return pl.pallas_call(
        _scan_kernel,
        grid=(bsz, nblk),
        in_specs=[fwd, fwd, fwd, bwd, bwd, bwd, fwd2, fwd2, fwd2, bwd2, bwd2, bwd2]
        + [_resident(a.shape) for a in masks],
        out_specs=[fwd, bwd],
        out_shape=[out, out],
        scratch_shapes=[pltpu.VMEM((2, w // LANES, LANES, LANES), F32), per_chunk(BF16), per_chunk(BF16),
                        per_chunk(BF16), per_chunk(BF16), pltpu.VMEM((2, tb // CHUNK, SUBLANES, w), F32)]
        + [per_chunk(BF16)] * 5,
        compiler_params=_cparams("parallel", "arbitrary"),
    )(r, v, kk, r, v, kk, lw, kd, kka, lw, kd, kka, *masks)


def _rope_partner(x):
    n = x.shape[1]
    lane = lax.broadcasted_iota(jnp.int32, x.shape, 1)
    return jnp.where((lane & 31) < 16, pltpu.roll(x, n - 16, 1), pltpu.roll(x, 16, 1))


def _attn_kernel(q_ref, k_ref, v_ref, cq_ref, sq_ref, ck_ref, sk_ref, qg_ref, kg_ref, jq_ref, jk_ref,
                 o_ref, kd_ref, vd_ref):
    s_len = k_ref.shape[0]
    tq = q_ref.shape[0]
    tk = min(TK_PREP, s_len)

    @pl.when(pl.program_id(1) == 0)
    def _():
        def body(i, carry):
            rows = pl.ds(pl.multiple_of(i * tk, tk), tk)
            k = k_ref[rows, :].astype(F32)
            kn = k * lax.rsqrt(_segsum(k * k, jk_ref[...]) * (1.0 / HEAD) + NORM_EPS) * kg_ref[...]
            kr = kn * ck_ref[rows, :] + _rope_partner(kn) * sk_ref[rows, :]
            v = v_ref[rows, :].astype(F32)
            low = lax.broadcasted_iota(jnp.int32, kr.shape, 1) < HEAD
            ksw = pltpu.roll(kr, HEAD, 1)
            kd_ref[0, rows, :] = jnp.where(low, kr, ksw).astype(BF16)
            kd_ref[1, rows, :] = jnp.where(low, ksw, kr).astype(BF16)
            vd_ref[0, rows, :] = jnp.where(low, v, 1.0).astype(BF16)
            vd_ref[1, rows, :] = jnp.where(low, pltpu.roll(v, HEAD, 1), 1.0).astype(BF16)
            return carry
        lax.fori_loop(0, s_len // tk, body, 0)

    q = q_ref[...].astype(F32)
    qn = q * lax.rsqrt(_segsum(q * q, jq_ref[...]) * (1.0 / HEAD) + NORM_EPS) * qg_ref[...]
    qr = qn * cq_ref[...] + _rope_partner(qn) * sq_ref[...]
    low = lax.broadcasted_iota(jnp.int32, (tq, LANES), 1) < HEAD
    pairs_per_kv = AT_Q_HEADS // AT_KV_HEADS // 2
    tkc = min(TK_ATTN, s_len)
    n_kc = s_len // tkc
    qs = []
    for g in range(AT_KV_HEADS):
        parts = []
        for jj in range(pairs_per_kv):
            qp = qr[:, (g * pairs_per_kv + jj) * LANES:(g * pairs_per_kv + jj + 1) * LANES]
            parts += [jnp.where(low, qp, 0.0), jnp.where(low, 0.0, qp)]
        qs.append(jnp.concatenate(parts, axis=0).astype(BF16))
    streams = [(g, kc) for g in range(AT_KV_HEADS) for kc in range(n_kc)]
    scores = lambda g, kc: _dot_nt(qs[g], kd_ref[g, kc * tkc:(kc + 1) * tkc, :])
    partial, s_next = {}, scores(*streams[0])
    for i, (g, kc) in enumerate(streams):
        s = s_next
        if i + 1 < len(streams):
            s_next = scores(*streams[i + 1])
        m = jnp.max(s, axis=-1, keepdims=True)
        p = jnp.exp2(s - m).astype(BF16)
        partial[g, kc] = (m, _dot(p, vd_ref[g, kc * tkc:(kc + 1) * tkc, :]))
    for g in range(AT_KV_HEADS):
        m_all = partial[g, 0][0]
        for kc in range(1, n_kc):
            m_all = jnp.maximum(m_all, partial[g, kc][0])
        o = sum(jnp.exp2(partial[g, kc][0] - m_all) * partial[g, kc][1] for kc in range(n_kc))
        o = o / pltpu.roll(o, HEAD, 1)
        for jj in range(pairs_per_kv):
            even = o[(2 * jj) * tq:(2 * jj + 1) * tq]
            odd = pltpu.roll(o[(2 * jj + 1) * tq:(2 * jj + 2) * tq], HEAD, 1)
            pair = jnp.where(low, even, odd)
            col = (g * pairs_per_kv + jj) * LANES
            o_ref[:, col:col + LANES] = pair.astype(BF16)


def _attn(at, tabs, q_g, k_g, jones):
    bsz, s, _ = at.shape
    tq = min(TQ_ATTN, s)
    cq, sq, ck, sk = tabs
    qw = AT_Q_HEADS * HEAD
    kvw = AT_KV_HEADS * HEAD
    cst = lambda a: pl.BlockSpec(a.shape, lambda b_, i: (0,) * a.ndim)
    jk = jones[0:kvw, 0:kvw]
    return pl.pallas_call(
        _attn_kernel,
        grid=(bsz, s // tq),
        in_specs=[pl.BlockSpec((None, tq, qw), lambda b_, i: (b_, i, 0)),
                  pl.BlockSpec((None, s, kvw), lambda b_, i: (b_, 0, qw // kvw)),
                  pl.BlockSpec((None, s, kvw), lambda b_, i: (b_, 0, qw // kvw + 1)),
                  pl.BlockSpec((tq, qw), lambda b_, i: (i, 0)),
                  pl.BlockSpec((tq, qw), lambda b_, i: (i, 0)),
                  cst(ck), cst(sk), cst(q_g), cst(k_g), cst(jones), cst(jk)],
        out_specs=pl.BlockSpec((None, tq, qw), lambda b_, i: (b_, i, 0)),
        out_shape=jax.ShapeDtypeStruct((bsz, s, qw), BF16),
        scratch_shapes=[pltpu.VMEM((AT_KV_HEADS, s, LANES), BF16), pltpu.VMEM((AT_KV_HEADS, s, LANES), BF16)],
        compiler_params=_cparams("parallel", "arbitrary"),
    )(at, at, at, cq, sq, ck, sk, q_g, k_g, jones, jk)


def _rope_tables(s):
    rows = s // GRID_W
    pos_r = jnp.repeat(jnp.arange(rows), GRID_W).astype(F32)
    pos_c = jnp.tile(jnp.arange(GRID_W), rows).astype(F32)
    half = HEAD // 2
    inv = ROPE_THETA ** (-jnp.arange(0, half, 2, dtype=F32) / half)
    ang_r = pos_r[:, None] * inv
    ang_c = pos_c[:, None] * inv
    cos = jnp.concatenate([jnp.cos(ang_r)] * 2 + [jnp.cos(ang_c)] * 2, axis=-1)
    sin = jnp.concatenate([-jnp.sin(ang_r), jnp.sin(ang_r), -jnp.sin(ang_c), jnp.sin(ang_c)], axis=-1)
    scale = HEAD ** -0.5 * math.log2(math.e)
    return (jnp.tile(cos, (1, AT_Q_HEADS)) * scale, jnp.tile(sin, (1, AT_Q_HEADS)) * scale,
            jnp.tile(cos, (1, AT_KV_HEADS)), jnp.tile(sin, (1, AT_KV_HEADS)))


def _merge_kernel(y0_ref, y1_ref, g_ref, bonus_ref, lng_ref, lnb_ref, j_ref, ycv_ref, yat_ref, sg_ref, x_ref,
                  wb_ref, wo_ref, o_ref):
    j = j_ref[...]
    y = y0_ref[...] + y1_ref[...]
    mu = _segsum(y, j) * (1.0 / HEAD)
    dev = y - mu
    var = _segsum(dev * dev, j) * (1.0 / HEAD)
    yn = dev * lax.rsqrt(var + RW_LN_EPS) * lng_ref[...] + lnb_ref[...]
    y_rw = ((yn + bonus_ref[...]) * g_ref[...]).astype(BF16)
    dm = D_MODEL
    merged = (sg_ref[:, 0:dm].astype(F32) * _dot(y_rw, wb_ref[0])
              + sg_ref[:, dm:2 * dm].astype(F32) * _dot(ycv_ref[...], wb_ref[1])
              + sg_ref[:, 2 * dm:3 * dm].astype(F32) * _dot(yat_ref[...], wb_ref[2]))
    o_ref[...] = x_ref[...] + _dot(merged.astype(BF16), wo_ref[...])


def _merge(yf, yb, g, bonus, ln_g, ln_b, jones, y_cv, y_at, sg, x, w_branch, w_out):
    t = x.shape[0]
    tm = TM_MERGE
    row = lambda n: pl.BlockSpec((tm, n), lambda i: (i, 0))
    return pl.pallas_call(
        _merge_kernel,
        grid=(t // tm,),
        in_specs=[row(RW_WIDTH), row(RW_WIDTH), row(RW_WIDTH), row(RW_WIDTH),
                  _resident(ln_g.shape), _resident(ln_b.shape), _resident(jones.shape),
                  row(CV_WIDTH), row(AT_Q_HEADS * HEAD), row(GATE_COLS), row(D_MODEL),
                  _resident(w_branch.shape), _resident(w_out.shape)],
        out_specs=row(D_MODEL),
        out_shape=jax.ShapeDtypeStruct((t, D_MODEL), F32),
        compiler_params=_cparams("parallel"),
    )(yf, yb, g, bonus, ln_g, ln_b, jones, y_cv, y_at, sg, x, w_branch, w_out)


def _mlp_kernel(x_ref, g_ref, w1_ref, w2_ref, o_ref):
    x = x_ref[...]
    h = (x * lax.rsqrt(jnp.mean(x * x, axis=-1, keepdims=True) + NORM_EPS) * g_ref[...]).astype(BF16)
    acc = x
    for c in range(D_FF // FF_CHUNK):
        cols = slice(c * FF_CHUNK, (c + 1) * FF_CHUNK)
        u = jnp.maximum(_dot(h, w1_ref[:, cols]), 0.0)
        acc = acc + _dot((u * u).astype(BF16), w2_ref[cols, :])
    o_ref[...] = acc


def _mlp(x, g, w1, w2):
    t = x.shape[0]
    tm = TM_MLP
    row = pl.BlockSpec((tm, D_MODEL), lambda i: (i, 0))
    return pl.pallas_call(
        _mlp_kernel,
        grid=(t // tm,),
        in_specs=[row, _resident(g.shape), _resident(w1.shape), _resident(w2.shape)],
        out_specs=row,
        out_shape=jax.ShapeDtypeStruct((t, D_MODEL), F32),
        compiler_params=_cparams("parallel"),
    )(x, g, w1, w2)


def _block_diag2(w):
    z = jnp.zeros_like(w[0])
    return jnp.concatenate([jnp.concatenate([w[0], z], axis=1), jnp.concatenate([z, w[1]], axis=1)], axis=0)


def kernel(x, norm1_g, w_in, rw_mu_prev, rw_mu_next, rw_w0, rw_w2, rw_a0, rw_a2, rw_g2, rw_k_k, rw_k_a, rw_r_k,
           rw_ln_g, rw_ln_b, cv_w, cv_b, at_q_g, at_k_g, w_branch, w_out, norm2_g, w_mlp1, w_mlp2):
    bsz, s, dm = x.shape
    assert dm == D_MODEL and s % GRID_W == 0
    for tile in (TM_INPROJ, TB_SCAN, TQ_ATTN, TK_PREP):
        assert s % min(tile, s) == 0 and min(tile, s) % CHUNK == 0
    t = bsz * s
    tabs = _rope_tables(s)
    masks = _scan_masks()
    jones = jnp.asarray(np.kron(np.eye(RW_WIDTH // HEAD, dtype=np.float32), np.ones((HEAD, HEAD), np.float32)), BF16)
    row = lambda a: a.reshape(1, -1)
    xf = x.reshape(t, dm)
    for l in range(DEPTH):
        r, v, kk, g, bonus, lw, kd, kka, y_cv, at, sg = _inproj(
            xf.reshape(bsz, s, dm), row(norm1_g[l]), w_in[l].astype(BF16), row(rw_mu_prev[l]), row(rw_mu_next[l]),
            row(rw_w0[l]), _block_diag2(rw_w2[l]).astype(BF16), row(rw_a0[l]), _block_diag2(rw_a2[l]).astype(BF16),
            rw_g2[l].astype(BF16), row(rw_k_k[l]), row(rw_k_a[l]), row(rw_r_k[l]), jones, cv_w[l], row(cv_b[l]))
        yf, yb = _scan(r, v, kk, lw, kd, kka, masks)
        y_at = _attn(at, tabs, row(jnp.tile(at_q_g[l], AT_Q_HEADS)),
                     row(jnp.tile(at_k_g[l], AT_KV_HEADS)), jones)
        xf = _merge(yf.reshape(t, RW_WIDTH), yb.reshape(t, RW_WIDTH), g.reshape(t, RW_WIDTH),
                    bonus.reshape(t, RW_WIDTH),
                    row(rw_ln_g[l]), row(rw_ln_b[l]), jones, y_cv.reshape(t, CV_WIDTH),
                    y_at.reshape(t, AT_Q_HEADS * HEAD), sg.reshape(t, GATE_COLS), xf,
                    w_branch[l].astype(BF16), w_out[l].astype(BF16))
        xf = _mlp(xf, row(norm2_g[l]), w_mlp1[l].astype(BF16), w_mlp2[l].astype(BF16))
    return xf.reshape(bsz, s, dm)
```
